```python
import math
import jax
import jax.numpy as jnp
from jax import lax
import numpy as np

D_MODEL = 1024
BATCH = 2
SEQ = 8192
DEPTH = 2

GRID_W = 64
CTX_LEN = 256
HEAD_DIM = 64
N_MOD = 6
NORM_EPS = 1e-6
ROPE_BASE = 10000.0
BLOCK = 128

HY_GROUPS = 4
HY_WIDTH = HY_GROUPS * HEAD_DIM
HY_ORDER = 2
HY_BANDS = 16
HY_EMB_DIM = 1 + 2 * HY_BANDS
HY_FILTER_WIDTH = 64
HY_SHORT_CONV = 3
HY_DECAY_MIN = 3.07
HY_DECAY_MAX = 15.35

WA_Q_HEADS = 8
WA_KV_HEADS = 2
WA_GROUP = WA_Q_HEADS // WA_KV_HEADS
WINDOW = 128

DA_HEADS = 4
DA_QK_DIM = 32
DA_V_DIM = 2 * DA_QK_DIM
DA_LAMBDA_BASE = 0.8
DA_LAMBDA_AMP = 0.6
DA_LAMBDA_RATE = 0.3

HY_COLS = (HY_ORDER + 1) * HY_WIDTH
WA_Q_COLS = WA_Q_HEADS * HEAD_DIM
WA_KV_COLS = WA_KV_HEADS * HEAD_DIM
WA_COLS = WA_Q_COLS + 2 * WA_KV_COLS
DA_QK_COLS = DA_HEADS * 2 * DA_QK_DIM
DA_COLS = 2 * DA_QK_COLS + DA_HEADS * DA_V_DIM
IN_COLS = HY_COLS + WA_COLS + DA_COLS
MIX_WIDTH = HY_WIDTH + WA_Q_COLS + DA_HEADS * DA_V_DIM

PEER_HEADS = 8
N_KEYS = 128
N_EXPERTS = N_KEYS * N_KEYS
PEER_TOPK = 16
PEER_QDIM = 256
TOKEN_CHUNK = 128

kernel_name = 'hybrid_hyena_swa_diffattn_peer_dit'


def rmsnorm(x, g):
    xf = x.astype(jnp.float32)
    y = xf * lax.rsqrt(jnp.mean(xf * xf, axis=-1, keepdims=True) + NORM_EPS)
    return (y * g.astype(jnp.float32)).astype(x.dtype)


def modulate(h, shift, scale):
    return h * (1.0 + scale) + shift


def axial_rope_tables(n_tokens, dim):
    n_rows = n_tokens // GRID_W
    row = jnp.broadcast_to(jnp.arange(n_rows)[:, None], (n_rows, GRID_W)).reshape(-1).astype(jnp.float32)
    col = jnp.broadcast_to(jnp.arange(GRID_W)[None, :], (n_rows, GRID_W)).reshape(-1).astype(jnp.float32)
    axis_dim = dim // 2
    inv_freq = ROPE_BASE ** (-jnp.arange(0, axis_dim, 2, dtype=jnp.float32) / axis_dim)
    ang_r = row[:, None] * inv_freq
    ang_c = col[:, None] * inv_freq
    return (jnp.cos(ang_r), jnp.sin(ang_r), jnp.cos(ang_c), jnp.sin(ang_c))


def _rotate_half(x, cos, sin):
    m = x.shape[-1] // 2
    x1, x2 = x[..., :m], x[..., m:]
    cos = cos[None, :, None, :]
    sin = sin[None, :, None, :]
    return jnp.concatenate([x1 * cos - x2 * sin, x2 * cos + x1 * sin], axis=-1)


def apply_axial_rope(x, tables):
    cr, sr, cc, sc = tables
    half = x.shape[-1] // 2
    xf = x.astype(jnp.float32)
    out = jnp.concatenate([_rotate_half(xf[..., :half], cr, sr), _rotate_half(xf[..., half:], cc, sc)], axis=-1)
    return out.astype(x.dtype)


def short_conv(u, w, b):
    n = u.shape[1]
    pad = HY_SHORT_CONV // 2
    up = jnp.pad(u, ((0, 0), (pad, pad), (0, 0)))
    out = b
    for j in range(HY_SHORT_CONV):
        out = out + up[:, j:j + n] * w[j]
    return out


def hyena_filters(n, lp):
    f32 = jnp.float32
    t = jnp.linspace(0.0, 1.0, n, dtype=f32)[:, None]
    bands = jnp.linspace(1e-4, HY_BANDS - 1, HY_BANDS, dtype=f32)
    phase = (2.0 * math.pi / n) * jnp.arange(n, dtype=f32)[:, None] * bands
    feat = jnp.concatenate([t, jnp.cos(phase), -jnp.sin(phase)], axis=-1)
    freq = lp['hy_freq'].astype(f32)
    z = jnp.sin(freq * (feat @ lp['hy_w1'].astype(f32) + lp['hy_b1'].astype(f32)))
    z = jnp.sin(freq * (z @ lp['hy_w2'].astype(f32) + lp['hy_b2'].astype(f32)))
    h = (z @ lp['hy_w3'].astype(f32)).reshape(n, HY_ORDER, 2, HY_WIDTH)
    window = jnp.exp(-t[:, :, None, None] * jnp.abs(lp['hy_decay'].astype(f32)))
    return h * window


def two_sided_spectrum(h):
    fwd = h[:, :, 0]
    bwd = h[:0:-1, :, 1]
    full = jnp.concatenate([fwd, jnp.zeros_like(fwd[:1]), bwd], axis=0)
    return jnp.fft.rfft(full, axis=0)


def hyena_mixer(u, lp):
    n = u.shape[1]
    parts = jnp.split(short_conv(u, lp['hy_conv_w'], lp['hy_conv_b']).astype(jnp.float32), HY_ORDER + 1, axis=-1)
    z = parts[-1]
    spec = two_sided_spectrum(hyena_filters(n, lp))
    bias = lp['hy_bias'].astype(jnp.float32)
    for o in range(HY_ORDER):
        zf = jnp.fft.rfft(z, n=2 * n, axis=1)
        conv = jnp.fft.irfft(zf * spec[None, :, o], n=2 * n, axis=1)[:, :n]
        z = parts[o] * (conv + bias[o] * z)
    return z.astype(u.dtype)


def window_gqa_latent(q, k, v, kc, vc, sink):
    B, S = q.shape[:2]
    nb = S // BLOCK
    scale = HEAD_DIM ** -0.5
    qb = q.reshape(B, nb, BLOCK, WA_KV_HEADS, WA_GROUP, HEAD_DIM)

    def band(t):
        tp = jnp.pad(t, ((0, 0), (BLOCK, BLOCK), (0, 0), (0, 0))).reshape(B, nb + 2, BLOCK, WA_KV_HEADS, HEAD_DIM)
        return jnp.concatenate([tp[:, :-2], tp[:, 1:-1], tp[:, 2:]], axis=2)

    kw, vw = band(k), band(v)
    qpos = jnp.arange(nb)[:, None] * BLOCK + jnp.arange(BLOCK)[None, :]
    kpos = jnp.arange(nb)[:, None] * BLOCK - BLOCK + jnp.arange(3 * BLOCK)[None, :]
    valid = ((jnp.abs(qpos[:, :, None] - kpos[:, None, :]) <= WINDOW)
             & (kpos[:, None, :] >= 0) & (kpos[:, None, :] < S))
    s_loc = jnp.einsum('bnqhgd,bnkhd->bnhgqk', qb, kw).astype(jnp.float32) * scale
    s_loc = jnp.where(valid[None, :, None, None], s_loc, -jnp.inf)
    s_ctx = jnp.einsum('bnqhgd,bchd->bnhgqc', qb, kc).astype(jnp.float32) * scale
    s_sink = jnp.broadcast_to(sink.astype(jnp.float32).reshape(WA_KV_HEADS, WA_GROUP)[None, None, :, :, None, None],
                              s_ctx.shape[:-1] + (1,))
    p = jax.nn.softmax(jnp.concatenate([s_loc, s_ctx, s_sink], axis=-1), axis=-1)
    n_loc = 3 * BLOCK
    n_ctx = kc.shape[1]
    out = (jnp.einsum('bnhgqk,bnkhd->bnqhgd', p[..., :n_loc].astype(v.dtype), vw)
           + jnp.einsum('bnhgqc,bchd->bnqhgd', p[..., n_loc:n_loc + n_ctx].astype(v.dtype), vc))
    return out.reshape(B, S, WA_Q_COLS)


def gqa_context(qc, kc, vc, sink):
    B, C = qc.shape[:2]
    q = qc.reshape(B, C, WA_KV_HEADS, WA_GROUP, HEAD_DIM)
    s = jnp.einsum('bqhgd,bkhd->bhgqk', q, kc).astype(jnp.float32) * HEAD_DIM ** -0.5
    s_sink = jnp.broadcast_to(sink.astype(jnp.float32).reshape(WA_KV_HEADS, WA_GROUP)[None, :, :, None, None],
                              s.shape[:-1] + (1,))
    p = jax.nn.softmax(jnp.concatenate([s, s_sink], axis=-1), axis=-1)[..., :C]
    return jnp.einsum('bhgqk,bkhd->bqhgd', p.astype(vc.dtype), vc).reshape(B, C, WA_Q_COLS)


def diff_lambda(lp, lam_init):
    f32 = jnp.float32
    return (jnp.exp(jnp.sum(lp['da_lq1'].astype(f32) * lp['da_lk1'].astype(f32)))
            - jnp.exp(jnp.sum(lp['da_lq2'].astype(f32) * lp['da_lk2'].astype(f32))) + lam_init)


def diff_attend(qb, k, v, lam):
    s = jnp.einsum('bqhmd,bkhmd->bhmqk', qb, k).astype(jnp.float32) * DA_QK_DIM ** -0.5
    p = jax.nn.softmax(s, axis=-1)
    a = p[:, :, 0] - lam * p[:, :, 1]
    return jnp.einsum('bhqk,bkhd->bqhd', a, v.astype(jnp.float32))


def diff_latent(q, k, v, kc, vc, lam):
    B, S = q.shape[:2]
    nb = S // BLOCK
    k_all = jnp.concatenate([k, kc], axis=1)
    v_all = jnp.concatenate([v, vc], axis=1)
    qblocks = jnp.moveaxis(q.reshape(B, nb, BLOCK, DA_HEADS, 2, DA_QK_DIM), 1, 0)
    out = lax.map(lambda qb: diff_attend(qb, k_all, v_all, lam), qblocks)
    return jnp.moveaxis(out, 0, 1).reshape(B, S, DA_HEADS, DA_V_DIM)


def diff_out(o, g, lam_init):
    B, L = o.shape[:2]
    return (rmsnorm(o, g) * (1.0 - lam_init)).reshape(B, L, DA_HEADS * DA_V_DIM)


def merge_heads(y_hy, y_wa, y_da, w_out, dtype):
    y = jnp.concatenate([y_hy.astype(dtype), y_wa.astype(dtype), y_da.astype(dtype)], axis=-1)
    return y @ w_out


def token_mixer(h, hc, lp, rope_wa, rope_da, lam_init, need_ctx_out):
    B, S, _ = h.shape
    C = hc.shape[1]
    cuts = (HY_COLS, HY_COLS + WA_COLS)
    hy, wa, da = jnp.split(h @ lp['w_in'], cuts, axis=-1)
    pc = hc @ lp['w_in']
    wa_c = pc[..., HY_COLS:HY_COLS + WA_COLS]
    da_c = pc[..., HY_COLS + WA_COLS:]
    y_hy = hyena_mixer(hy, lp)
    q, k, v = jnp.split(wa, (WA_Q_COLS, WA_Q_COLS + WA_KV_COLS), axis=-1)
    q = apply_axial_rope(q.reshape(B, S, WA_Q_HEADS, HEAD_DIM), rope_wa)
    k = apply_axial_rope(k.reshape(B, S, WA_KV_HEADS, HEAD_DIM), rope_wa)
    v = v.reshape(B, S, WA_KV_HEADS, HEAD_DIM)
    kc = wa_c[..., WA_Q_COLS:WA_Q_COLS + WA_KV_COLS].reshape(B, C, WA_KV_HEADS, HEAD_DIM)
    vc = wa_c[..., WA_Q_COLS + WA_KV_COLS:].reshape(B, C, WA_KV_HEADS, HEAD_DIM)
    y_wa = window_gqa_latent(q, k, v, kc, vc, lp['wa_sink'])
    lam = diff_lambda(lp, lam_init)
    q2, k2, v2 = jnp.split(da, (DA_QK_COLS, 2 * DA_QK_COLS), axis=-1)
    q2 = apply_axial_rope(q2.reshape(B, S, DA_HEADS * 2, DA_QK_DIM), rope_da).reshape(B, S, DA_HEADS, 2, DA_QK_DIM)
    k2 = apply_axial_rope(k2.reshape(B, S, DA_HEADS * 2, DA_QK_DIM), rope_da).reshape(B, S, DA_HEADS, 2, DA_QK_DIM)
    v2 = v2.reshape(B, S, DA_HEADS, DA_V_DIM)
    k2c = da_c[..., DA_QK_COLS:2 * DA_QK_COLS].reshape(B, C, DA_HEADS, 2, DA_QK_DIM)
    v2c = da_c[..., 2 * DA_QK_COLS:].reshape(B, C, DA_HEADS, DA_V_DIM)
    y_da = diff_out(diff_latent(q2, k2, v2, k2c, v2c, lam), lp['da_norm_g'], lam_init)
    y = merge_heads(y_hy, y_wa, y_da, lp['w_out'], h.dtype)
    if not need_ctx_out:
        return y, None
    yc_hy = hyena_mixer(pc[..., :HY_COLS], lp)
    qc = wa_c[..., :WA_Q_COLS].reshape(B, C, WA_Q_HEADS, HEAD_DIM)
    yc_wa = gqa_context(qc, kc, vc, lp['wa_sink'])
    q2c = da_c[..., :DA_QK_COLS].reshape(B, C, DA_HEADS, 2, DA_QK_DIM)
    yc_da = diff_out(diff_attend(q2c, k2c, v2c, lam), lp['da_norm_g'], lam_init)
    yc = merge_heads(yc_hy, yc_wa, yc_da, lp['w_out'], hc.dtype)
    return y, yc


def peer_ffn(h, wq, keys, u, v):
    T, D = h.shape
    half = PEER_QDIM // 2
    chunks = h.reshape(T // TOKEN_CHUNK, TOKEN_CHUNK, D)

    def one_chunk(xc):
        q = (xc @ wq).reshape(TOKEN_CHUNK, PEER_HEADS, 2, half)
        s = jnp.einsum('thpd,hpkd->thpk', q, keys).astype(jnp.float32)
        s_top, i_top = lax.top_k(s, PEER_TOPK)
        cand_s = (s_top[:, :, 0, :, None] + s_top[:, :, 1, None, :]).reshape(TOKEN_CHUNK, PEER_HEADS, PEER_TOPK * PEER_TOPK)
        cand_i = (i_top[:, :, 0, :, None] * N_KEYS + i_top[:, :, 1, None, :]).reshape(TOKEN_CHUNK, PEER_HEADS, PEER_TOPK * PEER_TOPK)
        best_s, best_j = lax.top_k(cand_s, PEER_TOPK)
        idx = jnp.take_along_axis(cand_i, best_j, axis=-1)
        gate = jax.nn.softmax(best_s, axis=-1)
        act = jax.nn.gelu(jnp.einsum('thkd,td->thk', u[idx], xc).astype(jnp.float32), approximate=False)
        return jnp.einsum('thk,thkd->td', (gate * act).astype(v.dtype), v[idx])

    return lax.map(one_chunk, chunks).reshape(T, D)


def setup_inputs(seed: int = 0) -> dict:
    key = jax.random.key(seed)
    ks = jax.random.split(key, 32)
    f32 = jnp.float32
    D = D_MODEL
    L = DEPTH
    FW = HY_FILTER_WIDTH

    def nrm(k, shape, s):
        return jax.random.normal(k, shape, f32) * s

    decay0 = jnp.broadcast_to(jnp.linspace(HY_DECAY_MIN, HY_DECAY_MAX, HY_WIDTH, dtype=f32), (L, HY_ORDER, 2, HY_WIDTH))
    return {
        'x': nrm(ks[0], (BATCH, SEQ, D), 1.0),
        'c': nrm(ks[1], (BATCH, D), 1.0),
        'ctx': nrm(ks[2], (BATCH, CTX_LEN, D), 1.0),
        'c_ctx': nrm(ks[3], (D,), 1.0),
        'w_mod': nrm(ks[4], (L, D, N_MOD * D), D ** -0.5),
        'b_mod': nrm(ks[5], (L, N_MOD * D), 0.02),
        'norm1_g': 1.0 + nrm(ks[6], (L, D), 0.02),
        'w_in': nrm(ks[7], (L, D, IN_COLS), D ** -0.5),
        'hy_conv_w': nrm(ks[8], (L, HY_SHORT_CONV, HY_COLS), HY_SHORT_CONV ** -0.5),
        'hy_conv_b': nrm(ks[9], (L, HY_COLS), 0.02),
        'hy_w1': nrm(ks[10], (L, HY_EMB_DIM, FW), HY_EMB_DIM ** -0.5),
        'hy_b1': nrm(ks[11], (L, FW), 0.02),
        'hy_freq': 1.0 + nrm(ks[12], (L, FW), 0.02),
        'hy_w2': nrm(ks[13], (L, FW, FW), FW ** -0.5),
        'hy_b2': nrm(ks[14], (L, FW), 0.02),
        'hy_w3': nrm(ks[15], (L, FW, HY_ORDER * 2 * HY_WIDTH), 0.05 * FW ** -0.5),
        'hy_decay': decay0 * (1.0 + nrm(ks[16], (L, HY_ORDER, 2, HY_WIDTH), 0.05)),
        'hy_bias': nrm(ks[17], (L, HY_ORDER, HY_WIDTH), 1.0),
        'wa_sink': nrm(ks[18], (L, WA_Q_HEADS), 1.0),
        'da_lq1': nrm(ks[19], (L, DA_QK_DIM), 0.1),
        'da_lk1': nrm(ks[20], (L, DA_QK_DIM), 0.1),
        'da_lq2': nrm(ks[21], (L, DA_QK_DIM), 0.1),
        'da_lk2': nrm(ks[22], (L, DA_QK_DIM), 0.1),
        'da_norm_g': 1.0 + nrm(ks[23], (L, DA_V_DIM), 0.02),
        'w_out': nrm(ks[24], (L, MIX_WIDTH, D), MIX_WIDTH ** -0.5),
        'norm2_g': 1.0 + nrm(ks[25], (L, D), 0.02),
        'peer_wq': nrm(ks[26], (L, D, PEER_HEADS * PEER_QDIM), D ** -0.5),
        'peer_keys': nrm(ks[27], (L, PEER_HEADS, 2, N_KEYS, PEER_QDIM // 2), (PEER_QDIM // 2) ** -0.5),
        'peer_u': nrm(ks[28], (L, N_EXPERTS, D), D ** -0.5),
        'peer_v': nrm(ks[29], (L, N_EXPERTS, D), 0.5),
        'final_g': 1.0 + nrm(ks[30], (D,), 0.02),
    }


def reference(x, c, ctx, c_ctx, w_mod, b_mod, norm1_g, w_in, hy_conv_w, hy_conv_b, hy_w1, hy_b1, hy_freq,
              hy_w2, hy_b2, hy_w3, hy_decay, hy_bias, wa_sink, da_lq1, da_lk1, da_lq2, da_lk2, da_norm_g,
              w_out, norm2_g, peer_wq, peer_keys, peer_u, peer_v, final_g):
    batch, n_lat, d = x.shape
    n_ctx = ctx.shape[1]
    rope_wa = axial_rope_tables(n_lat, HEAD_DIM)
    rope_da = axial_rope_tables(n_lat, DA_QK_DIM)
    c_act = jax.nn.silu(c)
    cc_act = jax.nn.silu(c_ctx)
    xc = ctx
    for li in range(DEPTH):
        need_ctx = li < DEPTH - 1
        lam_init = DA_LAMBDA_BASE - DA_LAMBDA_AMP * math.exp(-DA_LAMBDA_RATE * li)
        lp = {
            'w_in': w_in[li], 'w_out': w_out[li],
            'hy_conv_w': hy_conv_w[li], 'hy_conv_b': hy_conv_b[li],
            'hy_w1': hy_w1[li], 'hy_b1': hy_b1[li], 'hy_freq': hy_freq[li],
            'hy_w2': hy_w2[li], 'hy_b2': hy_b2[li], 'hy_w3': hy_w3[li],
            'hy_decay': hy_decay[li], 'hy_bias': hy_bias[li],
            'wa_sink': wa_sink[li],
            'da_lq1': da_lq1[li], 'da_lk1': da_lk1[li], 'da_lq2': da_lq2[li], 'da_lk2': da_lk2[li],
            'da_norm_g': da_norm_g[li],
        }
        mod = jnp.split((c_act @ w_mod[li] + b_mod[li])[:, None, :], N_MOD, axis=-1)
        modc = jnp.split((cc_act @ w_mod[li] + b_mod[li])[None, None, :], N_MOD, axis=-1)
        h = modulate(rmsnorm(x, norm1_g[li]), mod[0], mod[1])
        hc = modulate(rmsnorm(xc, norm1_g[li]), modc[0], modc[1])
        y, yc = token_mixer(h, hc, lp, rope_wa, rope_da, lam_init, need_ctx)
        x = x + mod[2] * y
        h2 = modulate(rmsnorm(x, norm2_g[li]), mod[3], mod[4])
        if need_ctx:
            xc = xc + modc[2] * yc
            h2c = modulate(rmsnorm(xc, norm2_g[li]), modc[3], modc[4])
            tok = jnp.concatenate([h2.reshape(-1, d), h2c.reshape(-1, d)], axis=0)
            out = peer_ffn(tok, peer_wq[li], peer_keys[li], peer_u[li], peer_v[li])
            x = x + mod[5] * out[:batch * n_lat].reshape(batch, n_lat, d)
            xc = xc + modc[5] * out[batch * n_lat:].reshape(batch, n_ctx, d)
        else:
            out = peer_ffn(h2.reshape(-1, d), peer_wq[li], peer_keys[li], peer_u[li], peer_v[li])
            x = x + mod[5] * out.reshape(batch, n_lat, d)
    return rmsnorm(x, final_g)
```

```python
import functools
import math

import numpy as np
import jax
import jax.numpy as jnp
from jax import lax
from jax.experimental import pallas as pl
from jax.experimental.pallas import tpu as pltpu

F32 = jnp.float32
BF16 = jnp.bfloat16
HI = lax.Precision.HIGHEST

LANES = 128
SUBLANES = 8
VMEM_LIMIT_BYTES = 56 * 1024 * 1024

D_MODEL = 1024
N_MOD = 6
NORM_EPS = 1e-6
ROPE_BASE = 10000.0
GRID_W = 64
BLOCK = 128
HEAD_DIM = 64
HY_WIDTH = 256
HY_ORDER = 2
HY_BANDS = 16
HY_EMB_DIM = 1 + 2 * HY_BANDS
HY_FILTER_WIDTH = 64
HY_COLS = 3 * HY_WIDTH
WA_Q_HEADS = 8
WA_KV_HEADS = 2
WA_GROUP = 4
WINDOW = 128
WA_Q_COLS = WA_Q_HEADS * HEAD_DIM
WA_KV_COLS = WA_KV_HEADS * HEAD_DIM
WA_COLS = WA_Q_COLS + 2 * WA_KV_COLS
DA_HEADS = 4
DA_QK_DIM = 32
DA_V_DIM = 64
DA_QK_COLS = DA_HEADS * 2 * DA_QK_DIM
DA_V_COLS = DA_HEADS * DA_V_DIM
DA_COLS = 2 * DA_QK_COLS + DA_V_COLS
DA_LAMBDA_BASE = 0.8
DA_LAMBDA_AMP = 0.6
DA_LAMBDA_RATE = 0.3
IN_COLS = HY_COLS + WA_COLS + DA_COLS
PEER_HEADS = 8
N_KEYS = 128
N_EXPERTS = N_KEYS * N_KEYS
PEER_TOPK = 16
PEER_QDIM = 256
PEER_QCOLS = PEER_HEADS * PEER_QDIM
NEG_BIG = -1e30


def _cparams(*sem):
    return pltpu.CompilerParams(dimension_semantics=sem, vmem_limit_bytes=VMEM_LIMIT_BYTES)


def _rmsnorm_mod(x, g, shift, scale):
    ms = jnp.mean(x * x, axis=-1, keepdims=True)
    return (x * lax.rsqrt(ms + NORM_EPS) * g) * (1.0 + scale) + shift


def _mod_kernel(c_ref, w_ref, b_ref, o_ref):
    c = c_ref[...]
    act = c * jax.nn.sigmoid(c)
    o_ref[...] = jnp.dot(act, w_ref[...], precision=HI, preferred_element_type=F32) + b_ref[...]


def mod_vectors(cpad, w_mod, b_mod):
    n_layers, d, n = w_mod.shape
    tn = 1536
    return pl.pallas_call(
        _mod_kernel,
        grid=(n_layers, n // tn),
        in_specs=[
            pl.BlockSpec((SUBLANES, d), lambda l, j: (0, 0)),
            pl.BlockSpec((None, d, tn), lambda l, j: (l, 0, j)),
            pl.BlockSpec((None, 1, tn), lambda l, j: (l, 0, j)),
        ],
        out_specs=pl.BlockSpec((None, SUBLANES, tn), lambda l, j: (l, 0, j)),
        out_shape=jax.ShapeDtypeStruct((n_layers, SUBLANES, n), F32),
        compiler_params=_cparams("arbitrary", "arbitrary"),
        name="mod_vectors",
    )(cpad, w_mod, b_mod.reshape(n_layers, 1, n))


def _rope128(x, cos, sin_a, sin_b, m):
    return x * cos + pltpu.roll(x, LANES - m, 1) * sin_a + pltpu.roll(x, m, 1) * sin_b


def _proj_in_kernel(x_ref, mod_ref, g_ref, w_ref, cw_ref, saw_ref, sbw_ref, cd_ref, sad_ref, sbd_ref,
                    hy_ref, qw_ref, kw_ref, vw_ref, qd_ref, kd_ref, vd_ref):
    h = _rmsnorm_mod(x_ref[...], g_ref[...], mod_ref[0:1, :], mod_ref[1:2, :]).astype(BF16)
    hy_ref[...] = jnp.dot(h, w_ref[:, 0:HY_COLS], preferred_element_type=F32)
    wa = jnp.dot(h, w_ref[:, HY_COLS:HY_COLS + WA_COLS], preferred_element_type=F32)
    da = jnp.dot(h, w_ref[:, HY_COLS + WA_COLS:IN_COLS], preferred_element_type=F32)
    cw, saw, sbw = cw_ref[...], saw_ref[...], sbw_ref[...]
    cd, sad, sbd = cd_ref[...], sad_ref[...], sbd_ref[...]
    wa_scale = HEAD_DIM ** -0.5
    da_scale = DA_QK_DIM ** -0.5
    for kb in range(WA_Q_COLS // LANES):
        blk = wa[:, kb * LANES:(kb + 1) * LANES]
        qw_ref[:, kb * LANES:(kb + 1) * LANES] = (_rope128(blk, cw, saw, sbw, 16) * wa_scale).astype(BF16)
    kw_ref[...] = _rope128(wa[:, WA_Q_COLS:WA_Q_COLS + WA_KV_COLS], cw, saw, sbw, 16).astype(BF16)
    vw_ref[...] = wa[:, WA_Q_COLS + WA_KV_COLS:WA_COLS].astype(BF16)
    for kb in range(DA_QK_COLS // LANES):
        blk = da[:, kb * LANES:(kb + 1) * LANES]
        qd_ref[:, kb * LANES:(kb + 1) * LANES] = (_rope128(blk, cd, sad, sbd, 8) * da_scale).astype(BF16)
        blk = da[:, DA_QK_COLS + kb * LANES:DA_QK_COLS + (kb + 1) * LANES]
        kd_ref[:, kb * LANES:(kb + 1) * LANES] = _rope128(blk, cd, sad, sbd, 8).astype(BF16)
    vd_ref[...] = da[:, 2 * DA_QK_COLS:DA_COLS].astype(BF16)


def _group_of_tile(i, tiles_per_batch, n_batch):
    return jnp.minimum(i // tiles_per_batch, n_batch)


def proj_in(x_all, mod, g, w_bf, rope_wa, rope_da, *, n_batch, seq, tm):
    t_all, d = x_all.shape
    tpb = seq // tm
    grp = functools.partial(_group_of_tile, tiles_per_batch=tpb, n_batch=n_batch)
    pos = lambda i: (jnp.where(i < n_batch * tpb, i % tpb, tpb), 0)
    row = lambda i: (i, 0)
    const = lambda i: (0, 0)
    tab = pl.BlockSpec((tm, LANES), pos)
    outs = [(HY_COLS, F32), (WA_Q_COLS, BF16), (WA_KV_COLS, BF16), (WA_KV_COLS, BF16),
            (DA_QK_COLS, BF16), (DA_QK_COLS, BF16), (DA_V_COLS, BF16)]
    return pl.pallas_call(
        _proj_in_kernel,
        grid=(t_all // tm,),
        in_specs=[
            pl.BlockSpec((tm, d), row),
            pl.BlockSpec((None, N_MOD, d), lambda i: (grp(i), 0, 0)),
            pl.BlockSpec((1, d), const),
            pl.BlockSpec((d, IN_COLS), const),
            tab, tab, tab, tab, tab, tab,
        ],
        out_specs=[pl.BlockSpec((tm, n), row) for n, _ in outs],
        out_shape=[jax.ShapeDtypeStruct((t_all, n), dt) for n, dt in outs],
        compiler_params=_cparams("arbitrary"),
        name="proj_in",
    )(x_all, mod, g, w_bf, *rope_wa, *rope_da)


def rope_tables(seq, head_dim, pad_rows):
    half = head_dim // 2
    axis_dim = half
    m = axis_dim // 2
    t = np.arange(seq)
    rowcol = np.stack([t // GRID_W, t % GRID_W], axis=0).astype(np.float32)
    inv_freq = (ROPE_BASE ** (-jnp.arange(0, axis_dim, 2, dtype=F32) / axis_dim))
    ang = jnp.asarray(rowcol)[:, :, None] * inv_freq[None, None, :]
    cos, sin = jnp.cos(ang), jnp.sin(ang)
    lane = np.arange(LANES)
    dd = lane % head_dim
    axis = dd // half
    sub = dd % half
    idx = sub % m
    first = sub < m
    cos_t = cos[axis, :, idx].T
    sin_t = sin[axis, :, idx].T
    sin_a = jnp.where(first[None, :], -sin_t, 0.0)
    sin_b = jnp.where(first[None, :], 0.0, sin_t)
    pad1 = jnp.ones((pad_rows, LANES), F32)
    pad0 = jnp.zeros((pad_rows, LANES), F32)
    return (jnp.concatenate([cos_t, pad1], 0), jnp.concatenate([sin_a, pad0], 0),
            jnp.concatenate([sin_b, pad0], 0))


def _softmax_sink_attend(q, k, v, sink, valid):
    s = lax.dot_general(q, k, (((1,), (1,)), ((), ())), preferred_element_type=F32)
    if valid is not None:
        nm = valid.shape[1]
        s = jnp.concatenate([jnp.where(valid, s[:, :nm], NEG_BIG), s[:, nm:]], axis=1)
    m = jnp.maximum(jnp.max(s, axis=1, keepdims=True), sink)
    p = jnp.exp(s - m)
    den = jnp.sum(p, axis=1, keepdims=True) + jnp.exp(sink - m)
    return jnp.dot(p.astype(BF16), v, preferred_element_type=F32) / den


def _wattn_kernel(sink_ref, q_ref, kp_ref, kc_ref, kn_ref, vp_ref, vc_ref, vn_ref, kx_ref, vx_ref, o_ref,
                  *, n_blocks):
    n = pl.program_id(1)
    r = lax.broadcasted_iota(jnp.int32, (BLOCK, 3 * BLOCK), 0)
    j = lax.broadcasted_iota(jnp.int32, (BLOCK, 3 * BLOCK), 1)
    lo = jnp.where(n > 0, 0, BLOCK)
    hi = jnp.where(n < n_blocks - 1, 3 * BLOCK, 2 * BLOCK)
    valid = (j >= jnp.maximum(r, lo)) & (j <= r + 2 * WINDOW) & (j < hi)
    outs = []
    for hk in range(WA_KV_HEADS):
        sl = slice(hk * HEAD_DIM, (hk + 1) * HEAD_DIM)
        k = jnp.concatenate([kp_ref[:, sl], kc_ref[:, sl], kn_ref[:, sl], kx_ref[:, sl]], axis=0)
        v = jnp.concatenate([vp_ref[:, sl], vc_ref[:, sl], vn_ref[:, sl], vx_ref[:, sl]], axis=0)
        for g in range(WA_GROUP):
            h = hk * WA_GROUP + g
            q = q_ref[:, h * HEAD_DIM:(h + 1) * HEAD_DIM]
            outs.append(_softmax_sink_attend(q, k, v, sink_ref[h], valid))
    o_ref[...] = jnp.concatenate(outs, axis=1).astype(BF16)


def window_attention(sink, qw, kw, vw, *, n_batch, seq, n_ctx):
    nb = seq // BLOCK
    ctx0 = (n_batch * seq) // n_ctx
    cur = lambda b, n: (b * nb + n, 0)
    prev = lambda b, n: (b * nb + jnp.maximum(n - 1, 0), 0)
    nxt = lambda b, n: (b * nb + jnp.minimum(n + 1, nb - 1), 0)
    ctx = lambda b, n: (ctx0 + b, 0)
    kv = lambda f: pl.BlockSpec((BLOCK, WA_KV_COLS), f)
    return pl.pallas_call(
        functools.partial(_wattn_kernel, n_blocks=nb),
        grid=(n_batch, nb),
        in_specs=[
            pl.BlockSpec(memory_space=pltpu.SMEM),
            pl.BlockSpec((BLOCK, WA_Q_COLS), cur),
            kv(prev), kv(cur), kv(nxt), kv(prev), kv(cur), kv(nxt),
            pl.BlockSpec((n_ctx, WA_KV_COLS), ctx),
            pl.BlockSpec((n_ctx, WA_KV_COLS), ctx),
        ],
        out_specs=pl.BlockSpec((BLOCK, WA_Q_COLS), cur),
        out_shape=jax.ShapeDtypeStruct((n_batch * seq, WA_Q_COLS), BF16),
        compiler_params=_cparams("arbitrary", "arbitrary"),
        name="window_attention",
    )(sink, qw, kw, kw, kw, vw, vw, vw, kw, vw)


def _diff_lambda(lq_ref, lam_init):
    a = jnp.sum(lq_ref[0:1, :] * lq_ref[1:2, :], axis=1, keepdims=True)
    b = jnp.sum(lq_ref[2:3, :] * lq_ref[3:4, :], axis=1, keepdims=True)
    return jnp.exp(a) - jnp.exp(b) + lam_init


def _diff_finish(acc1, l1, acc2, l2, lam, g128, lam_init):
    lane = lax.broadcasted_iota(jnp.int32, acc1[0].shape, 1)
    lo = lane < DA_V_DIM
    blocks = []
    for vb in range(DA_HEADS // 2):
        o = []
        for h in (2 * vb, 2 * vb + 1):
            o.append(acc1[h] / l1[h] - lam * (acc2[h] / l2[h]))
        blk = jnp.where(lo, o[0], o[1])
        sq = blk * blk
        s_lo = jnp.sum(jnp.where(lo, sq, 0.0), axis=1, keepdims=True)
        s_hi = jnp.sum(jnp.where(lo, 0.0, sq), axis=1, keepdims=True)
        ms = jnp.where(lo, s_lo, s_hi) * (1.0 / DA_V_DIM)
        blocks.append(blk * lax.rsqrt(ms + NORM_EPS) * g128 * (1.0 - lam_init))
    return jnp.concatenate(blocks, axis=1)


def _masked_q(q_ref, hm):
    kb, sub = divmod(hm, LANES // DA_QK_DIM)
    blk = q_ref[:, kb * LANES:(kb + 1) * LANES]
    lane = lax.broadcasted_iota(jnp.int32, blk.shape, 1)
    keep = (lane >= sub * DA_QK_DIM) & (lane < (sub + 1) * DA_QK_DIM)
    return jnp.where(keep, blk, jnp.zeros_like(blk))


def _dattn_kernel(lq_ref, g_ref, q_ref, k_ref, v_ref, kx_ref, vx_ref, o_ref, qm_scr, m_scr, l_scr, acc_scr,
                  *, lam_init, tk):
    n_hm = 2 * DA_HEADS
    per_blk = LANES // DA_QK_DIM
    for hm in range(n_hm):
        qm_scr[hm] = _masked_q(q_ref, hm)
    m_scr[...] = jnp.full(m_scr.shape, NEG_BIG, F32)
    l_scr[...] = jnp.zeros(l_scr.shape, F32)
    acc_scr[...] = jnp.zeros(acc_scr.shape, F32)

    def update(kc, vc):
        for hm in range(n_hm):
            h = hm // 2
            kb = hm // per_blk
            vb = h // 2
            s = lax.dot_general(qm_scr[hm], kc[:, kb * LANES:(kb + 1) * LANES], (((1,), (1,)), ((), ())),
                                preferred_element_type=F32)
            m_old = m_scr[hm]
            m_new = jnp.maximum(m_old, jnp.max(s, axis=1, keepdims=True))
            alpha = jnp.exp(m_old - m_new)
            p = jnp.exp(s - m_new)
            l_scr[hm] = alpha * l_scr[hm] + jnp.sum(p, axis=1, keepdims=True)
            acc_scr[hm] = alpha * acc_scr[hm] + jnp.dot(p.astype(BF16), vc[:, vb * LANES:(vb + 1) * LANES],
                                                        preferred_element_type=F32)
            m_scr[hm] = m_new

    def body(j, carry):
        start = pl.multiple_of(j * tk, tk)
        update(k_ref[pl.ds(start, tk), :], v_ref[pl.ds(start, tk), :])
        return carry

    lax.fori_loop(0, k_ref.shape[0] // tk, body, 0)
    update(kx_ref[...], vx_ref[...])
    lam = _diff_lambda(lq_ref, lam_init)
    acc1 = [acc_scr[2 * h] for h in range(DA_HEADS)]
    acc2 = [acc_scr[2 * h + 1] for h in range(DA_HEADS)]
    l1 = [l_scr[2 * h] for h in range(DA_HEADS)]
    l2 = [l_scr[2 * h + 1] for h in range(DA_HEADS)]
    o_ref[...] = _diff_finish(acc1, l1, acc2, l2, lam, g_ref[...], lam_init).astype(BF16)


def diff_attention(lq, g128, qd, kd, vd, *, n_batch, seq, n_ctx, lam_init, tq, tk):
    nq = seq // tq
    ctx0 = (n_batch * seq) // n_ctx
    n_hm = 2 * DA_HEADS
    return pl.pallas_call(
        functools.partial(_dattn_kernel, lam_init=lam_init, tk=tk),
        grid=(n_batch, nq),
        in_specs=[
            pl.BlockSpec((4, DA_QK_DIM), lambda b, i: (0, 0)),
            pl.BlockSpec((1, LANES), lambda b, i: (0, 0)),
            pl.BlockSpec((tq, DA_QK_COLS), lambda b, i: (b * nq + i, 0)),
            pl.BlockSpec((seq, DA_QK_COLS), lambda b, i: (b, 0)),
            pl.BlockSpec((seq, DA_V_COLS), lambda b, i: (b, 0)),
            pl.BlockSpec((n_ctx, DA_QK_COLS), lambda b, i: (ctx0 + b, 0)),
            pl.BlockSpec((n_ctx, DA_V_COLS), lambda b, i: (ctx0 + b, 0)),
        ],
        out_specs=pl.BlockSpec((tq, DA_V_COLS), lambda b, i: (b * nq + i, 0)),
        out_shape=jax.ShapeDtypeStruct((n_batch * seq, DA_V_COLS), BF16),
        scratch_shapes=[
            pltpu.VMEM((n_hm, tq, LANES), BF16),
            pltpu.VMEM((n_hm, tq, 1), F32),
            pltpu.VMEM((n_hm, tq, 1), F32),
            pltpu.VMEM((n_hm, tq, LANES), F32),
        ],
        compiler_params=_cparams("arbitrary", "arbitrary"),
        name="diff_attention",
    )(lq, g128, qd, kd, vd, kd, vd)


def _ctx_attn_kernel(sink_ref, lq_ref, g_ref, qw_ref, kw_ref, vw_ref, qd_ref, kd_ref, vd_ref, ow_ref, od_ref,
                     *, lam_init):
    outs = []
    for hk in range(WA_KV_HEADS):
        sl = slice(hk * HEAD_DIM, (hk + 1) * HEAD_DIM)
        k, v = kw_ref[:, sl], vw_ref[:, sl]
        for g in range(WA_GROUP):
            h = hk * WA_GROUP + g
            outs.append(_softmax_sink_attend(qw_ref[:, h * HEAD_DIM:(h + 1) * HEAD_DIM], k, v, sink_ref[h], None))
    ow_ref[...] = jnp.concatenate(outs, axis=1).astype(BF16)
    per_blk = LANES // DA_QK_DIM
    acc, den = [], []
    for hm in range(2 * DA_HEADS):
        kb, vb = hm // per_blk, hm // 4
        s = lax.dot_general(_masked_q(qd_ref, hm), kd_ref[:, kb * LANES:(kb + 1) * LANES],
                            (((1,), (1,)), ((), ())), preferred_element_type=F32)
        p = jnp.exp(s - jnp.max(s, axis=1, keepdims=True))
        den.append(jnp.sum(p, axis=1, keepdims=True))
        acc.append(jnp.dot(p.astype(BF16), vd_ref[:, vb * LANES:(vb + 1) * LANES], preferred_element_type=F32))
    lam = _diff_lambda(lq_ref, lam_init)
    od_ref[...] = _diff_finish(acc[0::2], den[0::2], acc[1::2], den[1::2], lam, g_ref[...], lam_init).astype(BF16)


def ctx_attention(sink, lq, g128, qw, kw, vw, qd, kd, vd, *, n_batch, seq, n_ctx, lam_init):
    ctx0 = (n_batch * seq) // n_ctx
    ctx = lambda b: (ctx0 + b, 0)
    out = lambda b: (b, 0)
    spec = lambda n, f: pl.BlockSpec((n_ctx, n), f)
    return pl.pallas_call(
        functools.partial(_ctx_attn_kernel, lam_init=lam_init),
        grid=(n_batch,),
        in_specs=[
            pl.BlockSpec(memory_space=pltpu.SMEM),
            pl.BlockSpec((4, DA_QK_DIM), lambda b: (0, 0)),
            pl.BlockSpec((1, LANES), lambda b: (0, 0)),
            spec(WA_Q_COLS, ctx), spec(WA_KV_COLS, ctx), spec(WA_KV_COLS, ctx),
            spec(DA_QK_COLS, ctx), spec(DA_QK_COLS, ctx), spec(DA_V_COLS, ctx),
        ],
        out_specs=[spec(WA_Q_COLS, out), spec(DA_V_COLS, out)],
        out_shape=[jax.ShapeDtypeStruct((n_batch * n_ctx, WA_Q_COLS), BF16),
                   jax.ShapeDtypeStruct((n_batch * n_ctx, DA_V_COLS), BF16)],
        compiler_params=_cparams("arbitrary"),
        name="ctx_attention",
    )(sink, lq, g128, qw, kw, vw, qd, kd, vd)


def _filter_kernel(feat_ref, w1_ref, b1_ref, fr_ref, w2_ref, b2_ref, w3_ref, dec_ref, o_ref, *, transposed):
    feat = feat_ref[...]
    fr = fr_ref[...]
    z = jnp.sin(fr * (jnp.dot(feat, w1_ref[...], precision=HI, preferred_element_type=F32) + b1_ref[...]))
    z = jnp.sin(fr * (jnp.dot(z, w2_ref[...], precision=HI, preferred_element_type=F32) + b2_ref[...]))
    h = jnp.dot(z, w3_ref[...], precision=HI, preferred_element_type=F32)
    h = h * jnp.exp(-feat[:, 0:1] * jnp.abs(dec_ref[...]))
    rows = lax.broadcasted_iota(jnp.int32, h.shape, 0) + pl.program_id(0) * h.shape[0]
    cols = lax.broadcasted_iota(jnp.int32, h.shape, 1)
    h = jnp.where((rows == 0) & (cols % (2 * HY_WIDTH) >= HY_WIDTH), 0.0, h)
    if transposed:
        for r in range(h.shape[0] // LANES):
            o_ref[r] = h[r * LANES:(r + 1) * LANES, :].T
    else:
        o_ref[...] = h


def hyena_filter_features(n):
    t = jnp.linspace(0.0, 1.0, n, dtype=F32)[:, None]
    bands = jnp.linspace(1e-4, HY_BANDS - 1, HY_BANDS, dtype=F32)
    phase = (2.0 * math.pi / n) * jnp.arange(n, dtype=F32)[:, None] * bands
    feat = jnp.concatenate([t, jnp.cos(phase), -jnp.sin(phase)], axis=-1)
    return jnp.pad(feat, ((0, 0), (0, LANES - HY_EMB_DIM)))


def hyena_filters(feat, w1p, b1, freq, w2, b2, w3, decay, *, transposed):
    n = feat.shape[0]
    tr = min(n, 512)
    ncol = w3.shape[1]
    fw = w2.shape[0]
    const = lambda i: (0, 0)
    if transposed:
        out_spec = pl.BlockSpec((tr // LANES, ncol, LANES), lambda i: (i, 0, 0))
        out_shape = jax.ShapeDtypeStruct((n // LANES, ncol, LANES), F32)
    else:
        out_spec = pl.BlockSpec((tr, ncol), lambda i: (i, 0))
        out_shape = jax.ShapeDtypeStruct((n, ncol), F32)
    return pl.pallas_call(
        functools.partial(_filter_kernel, transposed=transposed),
        grid=(n // tr,),
        in_specs=[
            pl.BlockSpec((tr, LANES), lambda i: (i, 0)),
            pl.BlockSpec((LANES, fw), const), pl.BlockSpec((1, fw), const), pl.BlockSpec((1, fw), const),
            pl.BlockSpec((fw, fw), const), pl.BlockSpec((1, fw), const),
            pl.BlockSpec((fw, ncol), const), pl.BlockSpec((1, ncol), const),
        ],
        out_specs=out_spec,
        out_shape=out_shape,
        compiler_params=_cparams("arbitrary"),
        name="hyena_filters",
    )(feat, w1p, b1, freq, w2, b2, w3, decay)


def dft_constants(seq):
    n = 2 * seq
    n1 = n // LANES
    a_rows = n1 // 2
    k1 = np.arange(n1, dtype=np.float64)[:, None]
    a = np.arange(a_rows, dtype=np.float64)[None, :]
    th = 2.0 * np.pi * k1 * a / n1
    c1, s1 = np.cos(th), np.sin(th)
    m1_data = np.block([[c1, s1], [-s1, c1]])
    m1_real = np.concatenate([c1, -s1], axis=0)
    m3 = np.block([[c1.T, -s1.T], [s1.T, c1.T]]) / n
    b = np.arange(LANES, dtype=np.float64)[None, :]
    psi = 2.0 * np.pi * k1 * b / n
    twc, tws = np.cos(psi), np.sin(psi)
    bb = np.arange(LANES, dtype=np.float64)
    phi = 2.0 * np.pi * np.outer(bb, bb) / LANES
    c2, s2 = np.cos(phi), np.sin(phi)
    w2f = np.block([[c2, -s2], [s2, c2]])
    w2i = np.block([[c2, s2], [-s2, c2]])
    f = lambda x: jnp.asarray(x, dtype=F32)
    return dict(m1_data=f(m1_data), m1_real=f(m1_real), m3=f(m3), twc=f(twc[:, None, :]), tws=f(tws[:, None, :]),
                w2f=f(w2f), w2i=f(w2i), n1=n1, a_rows=a_rows)


def _leftmm_kernel(m_ref, x_ref, o_ref):
    o_ref[...] = jnp.dot(m_ref[...], x_ref[...], precision=HI, preferred_element_type=F32)


def left_matmul(m, x, ct):
    r, k = m.shape
    cols = x.shape[1]
    return pl.pallas_call(
        _leftmm_kernel,
        grid=(cols // ct,),
        in_specs=[pl.BlockSpec((r, k), lambda j: (0, 0)), pl.BlockSpec((k, ct), lambda j: (0, j))],
        out_specs=pl.BlockSpec((r, ct), lambda j: (0, j)),
        out_shape=jax.ShapeDtypeStruct((r, cols), F32),
        compiler_params=_cparams("arbitrary"),
        name="dft_rows",
    )(m, x)


def _leftmm_gate_kernel(m_ref, z_ref, x_ref, zold_ref, bias_ref, o_ref):
    y = jnp.dot(m_ref[...], z_ref[...], precision=HI, preferred_element_type=F32)
    zold = zold_ref[...]
    o_ref[...] = x_ref[...] * (y + bias_ref[...] * zold)


def left_matmul_gate(m, z, xg, zold, bias_cols, ct):
    r, k = m.shape
    cols = z.shape[1]
    col = lambda j: (0, j)
    return pl.pallas_call(
        _leftmm_gate_kernel,
        grid=(cols // ct,),
        in_specs=[pl.BlockSpec((r, k), lambda j: (0, 0)), pl.BlockSpec((k, ct), col), pl.BlockSpec((r, ct), col),
                  pl.BlockSpec((r, ct), col), pl.BlockSpec((1, ct), col)],
        out_specs=pl.BlockSpec((r, ct), col),
        out_shape=jax.ShapeDtypeStruct((r, cols), F32),
        compiler_params=_cparams("arbitrary"),
        name="idft_rows_gate",
    )(m, z, xg, zold, bias_cols)


def _twiddle_fwd(yr, yi, c, s):
    return yr * c + yi * s, yi * c - yr * s


def _spectrum_kernel(yr_ref, yi_ref, twc_ref, tws_ref, w2f_ref, hr_ref, hi_ref, *, kb):
    for t in range(kb):
        ypr, ypi = _twiddle_fwd(yr_ref[t], yi_ref[t], twc_ref[t], tws_ref[t])
        x = jnp.dot(jnp.concatenate([ypr, ypi], axis=1), w2f_ref[...], precision=HI, preferred_element_type=F32)
        xr, xi = x[:, :LANES], x[:, LANES:]
        for o in range(HY_ORDER):
            f0 = slice((2 * o) * HY_WIDTH, (2 * o + 1) * HY_WIDTH)
            f1 = slice((2 * o + 1) * HY_WIDTH, (2 * o + 2) * HY_WIDTH)
            hr_ref[o, t] = xr[f0] + xr[f1]
            hi_ref[o, t] = xi[f0] - xi[f1]


def filter_spectrum(y, consts, kb):
    n1 = consts["n1"]
    nc = 2 * HY_ORDER * HY_WIDTH
    y4 = y.reshape(2, n1, nc, LANES)
    tw = pl.BlockSpec((kb, 1, LANES), lambda i: (i, 0, 0))
    out = pl.BlockSpec((HY_ORDER, kb, HY_WIDTH, LANES), lambda i: (0, i, 0, 0))
    shp = jax.ShapeDtypeStruct((HY_ORDER, n1, HY_WIDTH, LANES), F32)
    return pl.pallas_call(
        functools.partial(_spectrum_kernel, kb=kb),
        grid=(n1 // kb,),
        in_specs=[
            pl.BlockSpec((None, kb, nc, LANES), lambda i: (0, i, 0, 0)),
            pl.BlockSpec((None, kb, nc, LANES), lambda i: (1, i, 0, 0)),
            tw, tw,
            pl.BlockSpec((2 * LANES, 2 * LANES), lambda i: (0, 0)),
        ],
        out_specs=[out, out],
        out_shape=[shp, shp],
        compiler_params=_cparams("arbitrary"),
        name="filter_spectrum",
    )(y4, y4, consts["twc"], consts["tws"], consts["w2f"])


def _freq_kernel(yr_ref, yi_ref, twc_ref, tws_ref, w2f_ref, w2i_ref, hr_ref, hi_ref, zr_ref, zi_ref, *, kb):
    for t in range(kb):
        c, s = twc_ref[t], tws_ref[t]
        ypr, ypi = _twiddle_fwd(yr_ref[t], yi_ref[t], c, s)
        x = jnp.dot(jnp.concatenate([ypr, ypi], axis=1), w2f_ref[...], precision=HI, preferred_element_type=F32)
        xr, xi = x[:, :LANES], x[:, LANES:]
        hr, hi = hr_ref[t], hi_ref[t]
        gr = xr * hr - xi * hi
        gi = xr * hi + xi * hr
        z = jnp.dot(jnp.concatenate([gr, gi], axis=1), w2i_ref[...], precision=HI, preferred_element_type=F32)
        zr, zi = z[:, :LANES], z[:, LANES:]
        zr_ref[t] = zr * c - zi * s
        zi_ref[t] = zi * c + zr * s


def freq_multiply(y, hr, hi, order, consts, kb):
    n1 = consts["n1"]
    y4 = y.reshape(2, n1, HY_WIDTH, LANES)
    tw = pl.BlockSpec((kb, 1, LANES), lambda i: (i, 0, 0))
    blk = lambda p: pl.BlockSpec((None, kb, HY_WIDTH, LANES), lambda i: (p, i, 0, 0))
    wspec = pl.BlockSpec((2 * LANES, 2 * LANES), lambda i: (0, 0))
    out = pl.BlockSpec((kb, HY_WIDTH, LANES), lambda i: (i, 0, 0))
    shp = jax.ShapeDtypeStruct((n1, HY_WIDTH, LANES), F32)
    return pl.pallas_call(
        functools.partial(_freq_kernel, kb=kb),
        grid=(n1 // kb,),
        in_specs=[blk(0), blk(1), tw, tw, wspec, wspec, blk(order), blk(order)],
        out_specs=[out, out],
        out_shape=[shp, shp],
        compiler_params=_cparams("arbitrary"),
        name="freq_multiply",
    )(y4, y4, consts["twc"], consts["tws"], consts["w2f"], consts["w2i"], hr, hi)


def _short_conv_rows(u, prev_row, next_row, w_ref, b_ref):
    rows = lax.broadcasted_iota(jnp.int32, u.shape, 0)
    um1 = jnp.where(rows == 0, prev_row, pltpu.roll(u, 1, 0))
    up1 = jnp.where(rows == u.shape[0] - 1, next_row, pltpu.roll(u, u.shape[0] - 1, 0))
    return b_ref[...] + um1 * w_ref[0:1, :] + u * w_ref[1:2, :] + up1 * w_ref[2:3, :]


def _shortconv_t_kernel(u_ref, up_ref, un_ref, w_ref, b_ref, o_ref, *, n_tiles):
    i = pl.program_id(1)
    prev_row = jnp.where(i > 0, up_ref[SUBLANES - 1:SUBLANES, :], 0.0)
    next_row = jnp.where(i < n_tiles - 1, un_ref[0:1, :], 0.0)
    v = _short_conv_rows(u_ref[...], prev_row, next_row, w_ref, b_ref)
    for part in range(HY_ORDER + 1):
        for r in range(v.shape[0] // LANES):
            o_ref[part, r] = v[r * LANES:(r + 1) * LANES, part * HY_WIDTH:(part + 1) * HY_WIDTH].T


def short_conv_transposed(hy, conv_w, conv_b, *, n_batch, seq, tm):
    nt = seq // tm
    per8 = tm // SUBLANES
    a_rows = seq // LANES
    prev = lambda b, i: (jnp.maximum((b * nt + i) * per8 - 1, 0), 0)
    nxt = lambda b, i: (jnp.minimum((b * nt + i + 1) * per8, n_batch * nt * per8 - 1), 0)
    return pl.pallas_call(
        functools.partial(_shortconv_t_kernel, n_tiles=nt),
        grid=(n_batch, nt),
        in_specs=[
            pl.BlockSpec((tm, HY_COLS), lambda b, i: (b * nt + i, 0)),
            pl.BlockSpec((SUBLANES, HY_COLS), prev),
            pl.BlockSpec((SUBLANES, HY_COLS), nxt),
            pl.BlockSpec((3, HY_COLS), lambda b, i: (0, 0)),
            pl.BlockSpec((1, HY_COLS), lambda b, i: (0, 0)),
        ],
        out_specs=pl.BlockSpec((HY_ORDER + 1, None, tm // LANES, HY_WIDTH, LANES), lambda b, i: (0, b, i, 0, 0)),
        out_shape=jax.ShapeDtypeStruct((HY_ORDER + 1, n_batch, a_rows, HY_WIDTH, LANES), F32),
        compiler_params=_cparams("arbitrary", "arbitrary"),
        name="short_conv_transposed",
    )(hy, hy, hy, conv_w, conv_b)


def _untranspose_kernel(z_ref, o_ref):
    for r in range(z_ref.shape[0]):
        o_ref[r * LANES:(r + 1) * LANES, :] = z_ref[r].T.astype(BF16)


def untranspose(zt, *, n_batch, seq):
    a_rows = seq // LANES
    ta = min(SUBLANES, a_rows)
    nt = a_rows // ta
    return pl.pallas_call(
        _untranspose_kernel,
        grid=(n_batch, nt),
        in_specs=[pl.BlockSpec((None, ta, HY_WIDTH, LANES), lambda b, i: (b, i, 0, 0))],
        out_specs=pl.BlockSpec((ta * LANES, HY_WIDTH), lambda b, i: (b * nt + i, 0)),
        out_shape=jax.ShapeDtypeStruct((n_batch * seq, HY_WIDTH), BF16),
        compiler_params=_cparams("arbitrary", "arbitrary"),
        name="untranspose",
    )(zt)


def hyena_latent(hy, conv_w, conv_b, hy_bias, filt_t, consts, *, n_batch, seq):
    assert n_batch == 2, "the two batches are packed as real / imaginary parts of one DFT"
    n1, a_rows = consts["n1"], consts["a_rows"]
    cols = HY_WIDTH * LANES
    ct = min(cols, 2048)
    kb = min(n1, 4)
    parts = short_conv_transposed(hy, conv_w, conv_b, n_batch=n_batch, seq=seq, tm=min(seq, 1024))
    parts = parts.reshape(HY_ORDER + 1, n_batch * a_rows, cols)
    yf = left_matmul(consts["m1_real"], filt_t.reshape(a_rows, 2 * HY_ORDER * cols), ct)
    hr, hi = filter_spectrum(yf, consts, 1)
    z = parts[HY_ORDER]
    for o in range(HY_ORDER):
        y = left_matmul(consts["m1_data"], z, ct)
        zr, zi = freq_multiply(y, hr, hi, o, consts, kb)
        zcat = jnp.concatenate([zr.reshape(n1, cols), zi.reshape(n1, cols)], axis=0)
        bias_cols = jnp.repeat(hy_bias[o], LANES)[None, :]
        z = left_matmul_gate(consts["m3"], zcat, parts[o], z, bias_cols, ct)
    return untranspose(z.reshape(n_batch, a_rows, HY_WIDTH, LANES), n_batch=n_batch, seq=seq)


def _ctx_hyena_kernel(u_ref, w_ref, b_ref, h_ref, bias_ref, fc_ref, fs_ref, fct_ref, fst_ref, o_ref):
    u = u_ref[...]
    zero = jnp.zeros((1, u.shape[1]), F32)
    v = _short_conv_rows(u, zero, zero, w_ref, b_ref)
    z = v[:, HY_ORDER * HY_WIDTH:]
    mm = lambda a, b: jnp.dot(a, b, precision=HI, preferred_element_type=F32)
    fc, fs = fc_ref[...], fs_ref[...]
    inv_n = 1.0 / fc.shape[0]
    for o in range(HY_ORDER):
        h0 = h_ref[:, (2 * o) * HY_WIDTH:(2 * o + 1) * HY_WIDTH]
        h1 = h_ref[:, (2 * o + 1) * HY_WIDTH:(2 * o + 2) * HY_WIDTH]
        sr, si = mm(fc, h0 + h1), mm(fs, h1 - h0)
        zr, zi = mm(fc, z), -mm(fs, z)
        gr = zr * sr - zi * si
        gi = zr * si + zi * sr
        y = (mm(fct_ref[...], gr) - mm(fst_ref[...], gi)) * inv_n
        z = v[:, o * HY_WIDTH:(o + 1) * HY_WIDTH] * (y + bias_ref[o:o + 1, :] * z)
    o_ref[...] = z.astype(BF16)


def ctx_hyena(hy, conv_w, conv_b, filt, hy_bias, *, n_batch, seq, n_ctx):
    n = 2 * n_ctx
    k = np.arange(n, dtype=np.float64)[:, None]
    m = np.arange(n_ctx, dtype=np.float64)[None, :]
    ang = 2.0 * np.pi * k * m / n
    fc, fs = jnp.asarray(np.cos(ang), F32), jnp.asarray(np.sin(ang), F32)
    ctx0 = (n_batch * seq) // n_ctx
    const = lambda b: (0, 0)
    return pl.pallas_call(
        _ctx_hyena_kernel,
        grid=(n_batch,),
        in_specs=[
            pl.BlockSpec((n_ctx, HY_COLS), lambda b: (ctx0 + b, 0)),
            pl.BlockSpec((3, HY_COLS), const), pl.BlockSpec((1, HY_COLS), const),
            pl.BlockSpec((n_ctx, 2 * HY_ORDER * HY_WIDTH), const),
            pl.BlockSpec((HY_ORDER, HY_WIDTH), const),
            pl.BlockSpec((n, n_ctx), const), pl.BlockSpec((n, n_ctx), const),
            pl.BlockSpec((n_ctx, n), const), pl.BlockSpec((n_ctx, n), const),
        ],
        out_specs=pl.BlockSpec((n_ctx, HY_WIDTH), lambda b: (b, 0)),
        out_shape=jax.ShapeDtypeStruct((n_batch * n_ctx, HY_WIDTH), BF16),
        compiler_params=_cparams("arbitrary"),
        name="ctx_hyena",
    )(hy, conv_w, conv_b, filt, hy_bias, fc, fs, fc.T, fs.T)


def _outproj_kernel(x_ref, yh_ref, yw_ref, yd_ref, mod_ref, wo_ref, g_ref, wq_ref, xo_ref, h2_ref, qp_ref):
    y = jnp.dot(yh_ref[...], wo_ref[0:HY_WIDTH, :], preferred_element_type=F32)
    y = y + jnp.dot(yw_ref[...], wo_ref[HY_WIDTH:HY_WIDTH + WA_Q_COLS, :], preferred_element_type=F32)
    y = y + jnp.dot(yd_ref[...], wo_ref[HY_WIDTH + WA_Q_COLS:, :], preferred_element_type=F32)
    xn = x_ref[...] + mod_ref[2:3, :] * y
    xo_ref[...] = xn
    h2 = _rmsnorm_mod(xn, g_ref[...], mod_ref[3:4, :], mod_ref[4:5, :]).astype(BF16)
    h2_ref[...] = h2
    qp_ref[...] = jnp.dot(h2, wq_ref[...], preferred_element_type=F32)


def out_proj(x_all, y_hy, y_wa, y_da, mod, wo_bf, g2, wq_bf, *, n_rows, n_batch, seq, tm):
    d = x_all.shape[1]
    tpb = seq // tm
    grp = functools.partial(_group_of_tile, tiles_per_batch=tpb, n_batch=n_batch)
    row = lambda i: (i, 0)
    const = lambda i: (0, 0)
    mix = HY_WIDTH + WA_Q_COLS + DA_V_COLS
    return pl.pallas_call(
        _outproj_kernel,
        grid=(n_rows // tm,),
        in_specs=[
            pl.BlockSpec((tm, d), row), pl.BlockSpec((tm, HY_WIDTH), row), pl.BlockSpec((tm, WA_Q_COLS), row),
            pl.BlockSpec((tm, DA_V_COLS), row),
            pl.BlockSpec((None, N_MOD, d), lambda i: (grp(i), 0, 0)),
            pl.BlockSpec((mix, d), const), pl.BlockSpec((1, d), const), pl.BlockSpec((d, PEER_QCOLS), const),
        ],
        out_specs=[pl.BlockSpec((tm, d), row), pl.BlockSpec((tm, d), row), pl.BlockSpec((tm, PEER_QCOLS), row)],
        out_shape=[jax.ShapeDtypeStruct((n_rows, d), F32), jax.ShapeDtypeStruct((n_rows, d), BF16),
                   jax.ShapeDtypeStruct((n_rows, PEER_QCOLS), F32)],
        compiler_params=_cparams("arbitrary"),
        name="out_proj",
    )(x_all, y_hy, y_wa, y_da, mod, wo_bf, g2, wq_bf)


def _sort_pairs(n):
    pairs = []

    def merge(lo, cnt, r):
        step = r * 2
        if step < cnt:
            merge(lo, cnt, step)
            merge(lo + r, cnt, step)
            for i in range(lo + r, lo + cnt - r, step):
                pairs.append((i, i + r))
        else:
            pairs.append((lo, lo + r))

    def sort(lo, cnt):
        if cnt > 1:
            half = cnt // 2
            sort(lo, half)
            sort(lo + half, half)
            merge(lo, cnt, 1)

    sort(0, n)
    return pairs


def _sort_desc(vals):
    n = 1
    while n < len(vals):
        n *= 2
    v = list(vals) + [None] * (n - len(vals))
    for i, j in _sort_pairs(n):
        a, b = v[i], v[j]
        if b is None:
            continue
        if a is None:
            v[i], v[j] = b, None
        else:
            v[i], v[j] = jnp.maximum(a, b), jnp.minimum(a, b)
    return v[:len(vals)]


def _top16_rows(s):
    k = PEER_TOPK
    v = _sort_desc([s[SUBLANES * r:SUBLANES * (r + 1), :] for r in range(s.shape[0] // SUBLANES)])
    for shift in (4, 2, 1):
        other = [pltpu.roll(x, shift, 0) for x in v]
        v = [jnp.maximum(v[i], other[k - 1 - i]) for i in range(k)]
        d = k // 2
        while d >= 1:
            for i in range(k):
                if (i & d) == 0:
                    a, b = v[i], v[i + d]
                    v[i], v[i + d] = jnp.maximum(a, b), jnp.minimum(a, b)
            d //= 2
    return v


def _peer_topk_kernel(q_ref, keys_ref, thr_ref, e1_ref, s2_ref, e2_ref):
    k = PEER_TOPK
    tmk = q_ref.shape[0]
    sub = lax.broadcasted_iota(jnp.int32, (SUBLANES, tmk), 0)
    scores, tops = [], []
    for hp in range(2 * PEER_HEADS):
        s = lax.dot_general(keys_ref[hp], q_ref[:, hp * N_KEYS:(hp + 1) * N_KEYS], (((1,), (1,)), ((), ())),
                            precision=HI, preferred_element_type=F32)
        scores.append(s)
        top = _top16_rows(s)
        top.append(jnp.max(jnp.where(s < top[k - 1][0:1, :], s, NEG_BIG), axis=0, keepdims=True))
        tops.append(top)
    packed = []
    for p in range(2):
        lst = []
        for r in range(k + 1):
            acc = jnp.broadcast_to(tops[p][r], (SUBLANES, tmk))
            for h in range(1, PEER_HEADS):
                acc = jnp.where(sub == h, tops[2 * h + p][r], acc)
            lst.append(acc)
        packed.append(lst)
    cand = [packed[0][i] + packed[1][j] for i in range(k) for j in range(k) if (i + 1) * (j + 1) <= k]
    cs = _sort_desc(cand)
    c17 = jnp.maximum(cs[k], jnp.maximum(packed[0][k] + packed[1][0], packed[0][0] + packed[1][k]))
    tau = 0.5 * (cs[k - 1] + c17)
    zsum = jnp.zeros_like(tau)
    for r in range(k):
        zsum = zsum + jnp.exp(cs[r] - cs[0])
    inv_z = 1.0 / zsum
    for h in range(PEER_HEADS):
        s1, s2 = scores[2 * h], scores[2 * h + 1]
        row = lambda x: x[h:h + 1, :]
        thr_ref[h] = row(tau) - s1
        e1_ref[h] = jnp.exp(s1 - row(packed[0][0]))
        s2_ref[h] = s2
        e2_ref[h] = jnp.exp(s2 - row(packed[1][0])) * row(inv_z)


def peer_topk(qp, keys, *, tmk):
    t = qp.shape[0]
    out = pl.BlockSpec((PEER_HEADS, N_KEYS, tmk), lambda i: (0, 0, i))
    shp = jax.ShapeDtypeStruct((PEER_HEADS, N_KEYS, t), F32)
    return pl.pallas_call(
        _peer_topk_kernel,
        grid=(t // tmk,),
        in_specs=[pl.BlockSpec((tmk, PEER_QCOLS), lambda i: (i, 0)),
                  pl.BlockSpec((2 * PEER_HEADS, N_KEYS, PEER_QDIM // 2), lambda i: (0, 0, 0))],
        out_specs=[out, out, out, out],
        out_shape=[shp, shp, shp, shp],
        compiler_params=_cparams("arbitrary"),
        name="peer_topk",
    )(qp, keys)


def _peer_dense_kernel(x_ref, h_ref, mod_ref, u_ref, vt_ref, thr_ref, e1_ref, s2_ref, e2_ref, g_ref, o_ref,
                       a_scr, w_scr, acc_scr, *, final_norm):
    e = pl.program_id(1)

    @pl.when(e == 0)
    def _():
        acc_scr[...] = jnp.zeros(acc_scr.shape, F32)

    a_scr[...] = lax.dot_general(u_ref[...], h_ref[...], (((1,), (1,)), ((), ())), preferred_element_type=F32)
    n_i = u_ref.shape[0] // N_KEYS
    sqrt_half = math.sqrt(0.5)
    for ii in range(n_i):
        a = a_scr[ii * N_KEYS:(ii + 1) * N_KEYS, :]
        act = 0.5 * a * (1.0 + lax.erf(a * sqrt_half))
        gate = jnp.zeros_like(a)
        for h in range(PEER_HEADS):
            sel = s2_ref[h] >= thr_ref[h, ii:ii + 1, :]
            gate = gate + jnp.where(sel, e2_ref[h] * e1_ref[h, ii:ii + 1, :], 0.0)
        w_scr[ii * N_KEYS:(ii + 1) * N_KEYS, :] = (gate * act).astype(BF16)
    acc_scr[...] += jnp.dot(vt_ref[...], w_scr[...], preferred_element_type=F32)

    @pl.when(e == pl.num_programs(1) - 1)
    def _():
        xn = x_ref[...] + mod_ref[5:6, :] * acc_scr[...].T
        if final_norm:
            ms = jnp.mean(xn * xn, axis=-1, keepdims=True)
            xn = xn * lax.rsqrt(ms + NORM_EPS) * g_ref[...]
        o_ref[...] = xn


def peer_dense(x, h2, mod, u_bf, vt_bf, thr, e1, s2, e2, g_final, *, n_batch, seq, tm, te, final_norm):
    n_rows, d = x.shape
    tpb = seq // tm
    grp = functools.partial(_group_of_tile, tiles_per_batch=tpb, n_batch=n_batch)
    n_i = te // N_KEYS
    tok = lambda i, e: (i, 0)
    rows_i = pl.BlockSpec((PEER_HEADS, n_i, tm), lambda i, e: (0, e, i))
    full = pl.BlockSpec((PEER_HEADS, N_KEYS, tm), lambda i, e: (0, 0, i))
    return pl.pallas_call(
        functools.partial(_peer_dense_kernel, final_norm=final_norm),
        grid=(n_rows // tm, N_EXPERTS // te),
        in_specs=[
            pl.BlockSpec((tm, d), tok), pl.BlockSpec((tm, d), tok),
            pl.BlockSpec((None, N_MOD, d), lambda i, e: (grp(i), 0, 0)),
            pl.BlockSpec((te, d), lambda i, e: (e, 0)),
            pl.BlockSpec((d, te), lambda i, e: (0, e)),
            rows_i, rows_i, full, full,
            pl.BlockSpec((1, d), lambda i, e: (0, 0)),
        ],
        out_specs=pl.BlockSpec((tm, d), tok),
        out_shape=jax.ShapeDtypeStruct((n_rows, d), F32),
        scratch_shapes=[pltpu.VMEM((te, tm), F32), pltpu.VMEM((te, tm), BF16), pltpu.VMEM((d, tm), F32)],
        compiler_params=_cparams("arbitrary", "arbitrary"),
        name="peer_dense",
    )(x, h2, mod, u_bf, vt_bf, thr, e1, s2, e2, g_final)


def kernel(x, c, ctx, c_ctx, w_mod, b_mod, norm1_g, w_in, hy_conv_w, hy_conv_b, hy_w1, hy_b1, hy_freq, hy_w2,
           hy_b2, hy_w3, hy_decay, hy_bias, wa_sink, da_lq1, da_lk1, da_lq2, da_lk2, da_norm_g, w_out, norm2_g,
           peer_wq, peer_keys, peer_u, peer_v, final_g):
    n_batch, seq, d = x.shape
    n_ctx = ctx.shape[1]
    depth = w_in.shape[0]
    n_lat = n_batch * seq
    tm = 512 if seq % 512 == 0 else 256
    assert seq % tm == 0 and (n_batch * n_ctx) % tm == 0 and n_lat % n_ctx == 0 and n_batch + 1 <= SUBLANES

    cpad = jnp.zeros((SUBLANES, d), F32).at[:n_batch].set(c).at[n_batch].set(c_ctx)
    mod_all = mod_vectors(cpad, w_mod, b_mod).reshape(depth, SUBLANES, N_MOD, d)
    rope_wa = rope_tables(seq, HEAD_DIM, tm)
    rope_da = rope_tables(seq, DA_QK_DIM, tm)
    consts = dft_constants(seq)
    feat_lat = hyena_filter_features(seq)
    feat_ctx = hyena_filter_features(n_ctx)
    x_all = jnp.concatenate([x.reshape(n_lat, d), ctx.reshape(n_batch * n_ctx, d)], axis=0)

    out = None
    for li in range(depth):
        last = li == depth - 1
        lam_init = DA_LAMBDA_BASE - DA_LAMBDA_AMP * math.exp(-DA_LAMBDA_RATE * li)
        mod = mod_all[li]
        hy, qw, kw, vw, qd, kd, vd = proj_in(x_all, mod, norm1_g[li][None, :], w_in[li].astype(BF16), rope_wa,
                                             rope_da, n_batch=n_batch, seq=seq, tm=tm)
        w1p = jnp.pad(hy_w1[li], ((0, LANES - HY_EMB_DIM), (0, 0)))
        fargs = (w1p, hy_b1[li][None, :], hy_freq[li][None, :], hy_w2[li], hy_b2[li][None, :], hy_w3[li],
                 hy_decay[li].reshape(1, -1))
        conv_b = hy_conv_b[li][None, :]
        filt_t = hyena_filters(feat_lat, *fargs, transposed=True)
        y_hy = hyena_latent(hy, hy_conv_w[li], conv_b, hy_bias[li], filt_t, consts, n_batch=n_batch, seq=seq)
        y_wa = window_attention(wa_sink[li], qw, kw, vw, n_batch=n_batch, seq=seq, n_ctx=n_ctx)
        lq = jnp.stack([da_lq1[li], da_lk1[li], da_lq2[li], da_lk2[li]], axis=0)
        g128 = jnp.tile(da_norm_g[li], LANES // DA_V_DIM)[None, :]
        y_da = diff_attention(lq, g128, qd, kd, vd, n_batch=n_batch, seq=seq, n_ctx=n_ctx, lam_init=lam_init,
                              tq=min(seq, 1024), tk=min(seq, 512))
        if not last:
            filt_c = hyena_filters(feat_ctx, *fargs, transposed=False)
            yc_hy = ctx_hyena(hy, hy_conv_w[li], conv_b, filt_c, hy_bias[li], n_batch=n_batch, seq=seq, n_ctx=n_ctx)
            yc_wa, yc_da = ctx_attention(wa_sink[li], lq, g128, qw, kw, vw, qd, kd, vd, n_batch=n_batch, seq=seq,
                                         n_ctx=n_ctx, lam_init=lam_init)
            y_hy = jnp.concatenate([y_hy, yc_hy], axis=0)
            y_wa = jnp.concatenate([y_wa, yc_wa], axis=0)
            y_da = jnp.concatenate([y_da, yc_da], axis=0)
        n_rows = n_lat if last else x_all.shape[0]
        x_mid, h2, qp = out_proj(x_all, y_hy, y_wa, y_da, mod, w_out[li].astype(BF16), norm2_g[li][None, :],
                                 peer_wq[li].astype(BF16), n_rows=n_rows, n_batch=n_batch, seq=seq, tm=tm)
        keys = peer_keys[li].reshape(2 * PEER_HEADS, N_KEYS, PEER_QDIM // 2)
        thr, e1, s2, e2 = peer_topk(qp, keys, tmk=256)
        x_all = peer_dense(x_mid, h2, mod, peer_u[li].astype(BF16), peer_v[li].T.astype(BF16), thr, e1, s2, e2,
                           final_g[None, :], n_batch=n_batch, seq=seq, tm=tm, te=1024, final_norm=last)
        out = x_all
    return out.reshape(n_batch, seq, d)
```

```python
import functools
import math

import numpy as np
import jax
import jax.numpy as jnp
from jax import lax
from jax.experimental import pallas as pl
from jax.experimental.pallas import tpu as pltpu

F32 = jnp.float32
BF16 = jnp.bfloat16
HI = lax.Precision.HIGHEST

LANES = 128
SUBLANES = 8
VMEM_LIMIT_BYTES = 56 * 1024 * 1024

D_MODEL = 1024
N_MOD = 6
NORM_EPS = 1e-6
ROPE_BASE = 10000.0
GRID_W = 64
BLOCK = 128
HEAD_DIM = 64
HY_WIDTH = 256
HY_ORDER = 2
HY_BANDS = 16
HY_EMB_DIM = 1 + 2 * HY_BANDS
HY_FILTER_WIDTH = 64
HY_COLS = 3 * HY_WIDTH
WA_Q_HEADS = 8
WA_KV_HEADS = 2
WA_GROUP = 4
WINDOW = 128
WA_Q_COLS = WA_Q_HEADS * HEAD_DIM
WA_KV_COLS = WA_KV_HEADS * HEAD_DIM
WA_COLS = WA_Q_COLS + 2 * WA_KV_COLS
DA_HEADS = 4
DA_QK_DIM = 32
DA_V_DIM = 64
DA_QK_COLS = DA_HEADS * 2 * DA_QK_DIM
DA_V_COLS = DA_HEADS * DA_V_DIM
DA_COLS = 2 * DA_QK_COLS + DA_V_COLS
DA_LAMBDA_BASE = 0.8
DA_LAMBDA_AMP = 0.6
DA_LAMBDA_RATE = 0.3
IN_COLS = HY_COLS + WA_COLS + DA_COLS
PEER_HEADS = 8
N_KEYS = 128
N_EXPERTS = N_KEYS * N_KEYS
PEER_TOPK = 16
PEER_QDIM = 256
PEER_QCOLS = PEER_HEADS * PEER_QDIM
NEG_BIG = -1e30
LOG2E = math.log2(math.e)


def _cparams(*sem):
    return pltpu.CompilerParams(dimension_semantics=sem, vmem_limit_bytes=VMEM_LIMIT_BYTES)


def _rmsnorm_mod(x, g, shift, scale):
    ms = jnp.mean(x * x, axis=-1, keepdims=True)
    return (x * lax.rsqrt(ms + NORM_EPS) * g) * (1.0 + scale) + shift


def _mod_kernel(c_ref, w_ref, b_ref, o_ref):
    c = c_ref[...]
    act = c * jax.nn.sigmoid(c)
    o_ref[...] = jnp.dot(act, w_ref[...], precision=HI, preferred_element_type=F32) + b_ref[...]


def mod_vectors(cpad, w_mod, b_mod):
    n_layers, d, n = w_mod.shape
    tn = 1536
    return pl.pallas_call(
        _mod_kernel,
        grid=(n_layers, n // tn),
        in_specs=[
            pl.BlockSpec((SUBLANES, d), lambda l, j: (0, 0)),
            pl.BlockSpec((None, d, tn), lambda l, j: (l, 0, j)),
            pl.BlockSpec((None, 1, tn), lambda l, j: (l, 0, j)),
        ],
        out_specs=pl.BlockSpec((None, SUBLANES, tn), lambda l, j: (l, 0, j)),
        out_shape=jax.ShapeDtypeStruct((n_layers, SUBLANES, n), F32),
        compiler_params=_cparams("arbitrary", "arbitrary"),
        name="mod_vectors",
    )(cpad, w_mod, b_mod.reshape(n_layers, 1, n))


def _rope128(x, cos, sin_a, sin_b, m):
    return x * cos + pltpu.roll(x, LANES - m, 1) * sin_a + pltpu.roll(x, m, 1) * sin_b


def _proj_in_kernel(x_ref, mod_ref, g_ref, w_ref, cw_ref, saw_ref, sbw_ref, cd_ref, sad_ref, sbd_ref,
                    hy_ref, qw_ref, kw_ref, vw_ref, qd_ref, kd_ref, vd_ref, vdt_ref):
    h = _rmsnorm_mod(x_ref[...], g_ref[...], mod_ref[0:1, :], mod_ref[1:2, :]).astype(BF16)
    hy_ref[...] = jnp.dot(h, w_ref[:, 0:HY_COLS], preferred_element_type=F32)
    wa = jnp.dot(h, w_ref[:, HY_COLS:HY_COLS + WA_COLS], preferred_element_type=F32)
    da = jnp.dot(h, w_ref[:, HY_COLS + WA_COLS:IN_COLS], preferred_element_type=F32)
    cw, saw, sbw = cw_ref[...], saw_ref[...], sbw_ref[...]
    cd, sad, sbd = cd_ref[...], sad_ref[...], sbd_ref[...]
    wa_scale = HEAD_DIM ** -0.5
    da_scale = DA_QK_DIM ** -0.5 * LOG2E
    for kb in range(WA_Q_COLS // LANES):
        blk = wa[:, kb * LANES:(kb + 1) * LANES]
        qw_ref[:, kb * LANES:(kb + 1) * LANES] = (_rope128(blk, cw, saw, sbw, 16) * wa_scale).astype(BF16)
    kw_ref[...] = _rope128(wa[:, WA_Q_COLS:WA_Q_COLS + WA_KV_COLS], cw, saw, sbw, 16).astype(BF16)
    vw_ref[...] = wa[:, WA_Q_COLS + WA_KV_COLS:WA_COLS].astype(BF16)
    for kb in range(DA_QK_COLS // LANES):
        blk = da[:, kb * LANES:(kb + 1) * LANES]
        qd_ref[:, kb * LANES:(kb + 1) * LANES] = (_rope128(blk, cd, sad, sbd, 8) * da_scale).astype(BF16)
        blk = da[:, DA_QK_COLS + kb * LANES:DA_QK_COLS + (kb + 1) * LANES]
        kd_ref[:, kb * LANES:(kb + 1) * LANES] = _rope128(blk, cd, sad, sbd, 8).astype(BF16)
    vd = da[:, 2 * DA_QK_COLS:DA_COLS]
    vd_ref[...] = vd.astype(BF16)
    vdt_ref[...] = vd.T.astype(BF16)


def _group_of_tile(i, tiles_per_batch, n_batch):
    return jnp.minimum(i // tiles_per_batch, n_batch)


def proj_in(x_all, mod, g, w_bf, rope_wa, rope_da, *, n_batch, seq, tm):
    t_all, d = x_all.shape
    tpb = seq // tm
    grp = functools.partial(_group_of_tile, tiles_per_batch=tpb, n_batch=n_batch)
    pos = lambda i: (jnp.where(i < n_batch * tpb, i % tpb, tpb), 0)
    row = lambda i: (i, 0)
    const = lambda i: (0, 0)
    tab = pl.BlockSpec((tm, LANES), pos)
    outs = [(HY_COLS, F32), (WA_Q_COLS, BF16), (WA_KV_COLS, BF16), (WA_KV_COLS, BF16),
            (DA_QK_COLS, BF16), (DA_QK_COLS, BF16), (DA_V_COLS, BF16)]
    return pl.pallas_call(
        _proj_in_kernel,
        grid=(t_all // tm,),
        in_specs=[
            pl.BlockSpec((tm, d), row),
            pl.BlockSpec((None, N_MOD, d), lambda i: (grp(i), 0, 0)),
            pl.BlockSpec((1, d), const),
            pl.BlockSpec((d, IN_COLS), const),
            tab, tab, tab, tab, tab, tab,
        ],
        out_specs=[pl.BlockSpec((tm, n), row) for n, _ in outs] + [pl.BlockSpec((DA_V_COLS, tm), lambda i: (0, i))],
        out_shape=[jax.ShapeDtypeStruct((t_all, n), dt) for n, dt in outs]
        + [jax.ShapeDtypeStruct((DA_V_COLS, t_all), BF16)],
        compiler_params=_cparams("arbitrary"),
        name="proj_in",
    )(x_all, mod, g, w_bf, *rope_wa, *rope_da)


def rope_tables(seq, head_dim, pad_rows):
    half = head_dim // 2
    axis_dim = half
    m = axis_dim // 2
    t = np.arange(seq)
    rowcol = np.stack([t // GRID_W, t % GRID_W], axis=0).astype(np.float32)
    inv_freq = (ROPE_BASE ** (-jnp.arange(0, axis_dim, 2, dtype=F32) / axis_dim))
    ang = jnp.asarray(rowcol)[:, :, None] * inv_freq[None, None, :]
    cos, sin = jnp.cos(ang), jnp.sin(ang)
    lane = np.arange(LANES)
    dd = lane % head_dim
    axis = dd // half
    sub = dd % half
    idx = sub % m
    first = sub < m
    cos_t = cos[axis, :, idx].T
    sin_t = sin[axis, :, idx].T
    sin_a = jnp.where(first[None, :], -sin_t, 0.0)
    sin_b = jnp.where(first[None, :], 0.0, sin_t)
    pad1 = jnp.ones((pad_rows, LANES), F32)
    pad0 = jnp.zeros((pad_rows, LANES), F32)
    return (jnp.concatenate([cos_t, pad1], 0), jnp.concatenate([sin_a, pad0], 0),
            jnp.concatenate([sin_b, pad0], 0))


def _softmax_sink_attend(q, k, v, sink, valid):
    s = lax.dot_general(q, k, (((1,), (1,)), ((), ())), preferred_element_type=F32)
    if valid is not None:
        nm = valid.shape[1]
        s = jnp.concatenate([jnp.where(valid, s[:, :nm], NEG_BIG), s[:, nm:]], axis=1)
    m = jnp.maximum(jnp.max(s, axis=1, keepdims=True), sink)
    p = jnp.exp(s - m)
    den = jnp.sum(p, axis=1, keepdims=True) + jnp.exp(sink - m)
    return jnp.dot(p.astype(BF16), v, preferred_element_type=F32) / den


def _wattn_kernel(sink_ref, q_ref, kp_ref, kc_ref, kn_ref, vp_ref, vc_ref, vn_ref, kx_ref, vx_ref, o_ref,
                  *, n_blocks):
    n = pl.program_id(1)
    r = lax.broadcasted_iota(jnp.int32, (BLOCK, 3 * BLOCK), 0)
    j = lax.broadcasted_iota(jnp.int32, (BLOCK, 3 * BLOCK), 1)
    lo = jnp.where(n > 0, 0, BLOCK)
    hi = jnp.where(n < n_blocks - 1, 3 * BLOCK, 2 * BLOCK)
    valid = (j >= jnp.maximum(r, lo)) & (j <= r + 2 * WINDOW) & (j < hi)
    outs = []
    for hk in range(WA_KV_HEADS):
        sl = slice(hk * HEAD_DIM, (hk + 1) * HEAD_DIM)
        k = jnp.concatenate([kp_ref[:, sl], kc_ref[:, sl], kn_ref[:, sl], kx_ref[:, sl]], axis=0)
        v = jnp.concatenate([vp_ref[:, sl], vc_ref[:, sl], vn_ref[:, sl], vx_ref[:, sl]], axis=0)
        for g in range(WA_GROUP):
            h = hk * WA_GROUP + g
            q = q_ref[:, h * HEAD_DIM:(h + 1) * HEAD_DIM]
            outs.append(_softmax_sink_attend(q, k, v, sink_ref[h], valid))
    o_ref[...] = jnp.concatenate(outs, axis=1).astype(BF16)


def window_attention(sink, qw, kw, vw, *, n_batch, seq, n_ctx):
    nb = seq // BLOCK
    ctx0 = (n_batch * seq) // n_ctx
    cur = lambda b, n: (b * nb + n, 0)
    prev = lambda b, n: (b * nb + jnp.maximum(n - 1, 0), 0)
    nxt = lambda b, n: (b * nb + jnp.minimum(n + 1, nb - 1), 0)
    ctx = lambda b, n: (ctx0 + b, 0)
    kv = lambda f: pl.BlockSpec((BLOCK, WA_KV_COLS), f)
    return pl.pallas_call(
        functools.partial(_wattn_kernel, n_blocks=nb),
        grid=(n_batch, nb),
        in_specs=[
            pl.BlockSpec(memory_space=pltpu.SMEM),
            pl.BlockSpec((BLOCK, WA_Q_COLS), cur),
            kv(prev), kv(cur), kv(nxt), kv(prev), kv(cur), kv(nxt),
            pl.BlockSpec((n_ctx, WA_KV_COLS), ctx),
            pl.BlockSpec((n_ctx, WA_KV_COLS), ctx),
        ],
        out_specs=pl.BlockSpec((BLOCK, WA_Q_COLS), cur),
        out_shape=jax.ShapeDtypeStruct((n_batch * seq, WA_Q_COLS), BF16),
        compiler_params=_cparams("arbitrary", "arbitrary"),
        name="window_attention",
    )(sink, qw, kw, kw, kw, vw, vw, vw, kw, vw)


def _diff_lambda(lq_ref, lam_init):
    a = jnp.sum(lq_ref[0:1, :] * lq_ref[1:2, :], axis=1, keepdims=True)
    b = jnp.sum(lq_ref[2:3, :] * lq_ref[3:4, :], axis=1, keepdims=True)
    return jnp.exp(a) - jnp.exp(b) + lam_init


def _diff_finish(acc1, l1, acc2, l2, lam, g128, lam_init):
    lane = lax.broadcasted_iota(jnp.int32, acc1[0].shape, 1)
    lo = lane < DA_V_DIM
    blocks = []
    for vb in range(DA_HEADS // 2):
        o = []
        for h in (2 * vb, 2 * vb + 1):
            o.append(acc1[h] / l1[h] - lam * (acc2[h] / l2[h]))
        blk = jnp.where(lo, o[0], o[1])
        sq = blk * blk
        s_lo = jnp.sum(jnp.where(lo, sq, 0.0), axis=1, keepdims=True)
        s_hi = jnp.sum(jnp.where(lo, 0.0, sq), axis=1, keepdims=True)
        ms = jnp.where(lo, s_lo, s_hi) * (1.0 / DA_V_DIM)
        blocks.append(blk * lax.rsqrt(ms + NORM_EPS) * g128 * (1.0 - lam_init))
    return jnp.concatenate(blocks, axis=1)


def _masked_q(q_ref, hm):
    kb, sub = divmod(hm, LANES // DA_QK_DIM)
    blk = q_ref[:, kb * LANES:(kb + 1) * LANES]
    lane = lax.broadcasted_iota(jnp.int32, blk.shape, 1)
    keep = (lane >= sub * DA_QK_DIM) & (lane < (sub + 1) * DA_QK_DIM)
    return jnp.where(keep, blk, jnp.zeros_like(blk))


def _dattn_kernel(lq_ref, g_ref, q_ref, k_ref, vt_ref, kx_ref, vxt_ref, o_ref, qm_scr, m_scr, l_scr, acc_scr,
                  *, lam_init, tk, ahead):
    n_hm = 2 * DA_HEADS
    per_blk = LANES // DA_QK_DIM
    for hm in range(n_hm):
        qm_scr[hm] = _masked_q(q_ref, hm)
    m_scr[...] = jnp.full(m_scr.shape, NEG_BIG, F32)
    l_scr[...] = jnp.zeros(l_scr.shape, F32)
    acc_scr[...] = jnp.zeros(acc_scr.shape, F32)

    def update(kc, vtc):
        def scores(hm):
            kb = hm // per_blk
            return lax.dot_general(kc[:, kb * LANES:(kb + 1) * LANES], qm_scr[hm], (((1,), (1,)), ((), ())),
                                   preferred_element_type=F32)

        sts = [scores(hm) for hm in range(ahead)]
        for hm in range(n_hm):
            h = hm // 2
            if hm + ahead < n_hm:
                sts.append(scores(hm + ahead))
            st = sts[hm]
            m_old = m_scr[hm]
            m_new = jnp.maximum(m_old, jnp.max(st, axis=0, keepdims=True))
            alpha = jnp.exp2(m_old - m_new)
            pt = jnp.exp2(st - m_new)
            l_scr[hm] = alpha * l_scr[hm] + jnp.sum(pt, axis=0, keepdims=True)
            acc_scr[hm] = alpha * acc_scr[hm] + jnp.dot(vtc[h * DA_V_DIM:(h + 1) * DA_V_DIM, :], pt.astype(BF16),
                                                        preferred_element_type=F32)
            m_scr[hm] = m_new

    def body(j, carry):
        start = pl.multiple_of(j * tk, tk)
        update(k_ref[pl.ds(start, tk), :], vt_ref[:, pl.ds(start, tk)])
        return carry

    lax.fori_loop(0, k_ref.shape[0] // tk, body, 0)
    update(kx_ref[...], vxt_ref[...])
    lam = _diff_lambda(lq_ref, lam_init)
    heads = []
    for h in range(DA_HEADS):
        o = acc_scr[2 * h] / l_scr[2 * h] - lam * (acc_scr[2 * h + 1] / l_scr[2 * h + 1])
        ms = jnp.mean(o * o, axis=0, keepdims=True)
        heads.append(o * lax.rsqrt(ms + NORM_EPS) * g_ref[...] * (1.0 - lam_init))
    o_ref[...] = jnp.concatenate(heads, axis=0).T.astype(BF16)


def diff_attention(lq, g_rows, qd, kd, vdt, *, n_batch, seq, n_ctx, lam_init, tq, tk, ahead=2):
    nq = seq // tq
    ctx0 = (n_batch * seq) // n_ctx
    n_hm = 2 * DA_HEADS
    return pl.pallas_call(
        functools.partial(_dattn_kernel, lam_init=lam_init, tk=tk, ahead=ahead),
        grid=(n_batch, nq),
        in_specs=[
            pl.BlockSpec((4, DA_QK_DIM), lambda b, i: (0, 0)),
            pl.BlockSpec((DA_V_DIM, tq), lambda b, i: (0, 0)),
            pl.BlockSpec((tq, DA_QK_COLS), lambda b, i: (b * nq + i, 0)),
            pl.BlockSpec((seq, DA_QK_COLS), lambda b, i: (b, 0)),
            pl.BlockSpec((DA_V_COLS, seq), lambda b, i: (0, b)),
            pl.BlockSpec((n_ctx, DA_QK_COLS), lambda b, i: (ctx0 + b, 0)),
            pl.BlockSpec((DA_V_COLS, n_ctx), lambda b, i: (0, ctx0 + b)),
        ],
        out_specs=pl.BlockSpec((tq, DA_V_COLS), lambda b, i: (b * nq + i, 0)),
        out_shape=jax.ShapeDtypeStruct((n_batch * seq, DA_V_COLS), BF16),
        scratch_shapes=[
            pltpu.VMEM((n_hm, tq, LANES), BF16),
            pltpu.VMEM((n_hm, 1, tq), F32),
            pltpu.VMEM((n_hm, 1, tq), F32),
            pltpu.VMEM((n_hm, DA_V_DIM, tq), F32),
        ],
        compiler_params=_cparams("arbitrary", "arbitrary"),
        name="diff_attention",
    )(lq, g_rows, qd, kd, vdt, kd, vdt)


def _ctx_attn_kernel(sink_ref, lq_ref, g_ref, qw_ref, kw_ref, vw_ref, qd_ref, kd_ref, vd_ref, ow_ref, od_ref,
                     *, lam_init):
    outs = []
    for hk in range(WA_KV_HEADS):
        sl = slice(hk * HEAD_DIM, (hk + 1) * HEAD_DIM)
        k, v = kw_ref[:, sl], vw_ref[:, sl]
        for g in range(WA_GROUP):
            h = hk * WA_GROUP + g
            outs.append(_softmax_sink_attend(qw_ref[:, h * HEAD_DIM:(h + 1) * HEAD_DIM], k, v, sink_ref[h], None))
    ow_ref[...] = jnp.concatenate(outs, axis=1).astype(BF16)
    per_blk = LANES // DA_QK_DIM
    acc, den = [], []
    for hm in range(2 * DA_HEADS):
        kb, vb = hm // per_blk, hm // 4
        s = lax.dot_general(_masked_q(qd_ref, hm), kd_ref[:, kb * LANES:(kb + 1) * LANES],
                            (((1,), (1,)), ((), ())), preferred_element_type=F32)
        p = jnp.exp2(s - jnp.max(s, axis=1, keepdims=True))
        den.append(jnp.sum(p, axis=1, keepdims=True))
        acc.append(jnp.dot(p.astype(BF16), vd_ref[:, vb * LANES:(vb + 1) * LANES], preferred_element_type=F32))
    lam = _diff_lambda(lq_ref, lam_init)
    od_ref[...] = _diff_finish(acc[0::2], den[0::2], acc[1::2], den[1::2], lam, g_ref[...], lam_init).astype(BF16)


def ctx_attention(sink, lq, g128, qw, kw, vw, qd, kd, vd, *, n_batch, seq, n_ctx, lam_init):
    ctx0 = (n_batch * seq) // n_ctx
    ctx = lambda b: (ctx0 + b, 0)
    out = lambda b: (b, 0)
    spec = lambda n, f: pl.BlockSpec((n_ctx, n), f)
    return pl.pallas_call(
        functools.partial(_ctx_attn_kernel, lam_init=lam_init),
        grid=(n_batch,),
        in_specs=[
            pl.BlockSpec(memory_space=pltpu.SMEM),
            pl.BlockSpec((4, DA_QK_DIM), lambda b: (0, 0)),
            pl.BlockSpec((1, LANES), lambda b: (0, 0)),
            spec(WA_Q_COLS, ctx), spec(WA_KV_COLS, ctx), spec(WA_KV_COLS, ctx),
            spec(DA_QK_COLS, ctx), spec(DA_QK_COLS, ctx), spec(DA_V_COLS, ctx),
        ],
        out_specs=[spec(WA_Q_COLS, out), spec(DA_V_COLS, out)],
        out_shape=[jax.ShapeDtypeStruct((n_batch * n_ctx, WA_Q_COLS), BF16),
                   jax.ShapeDtypeStruct((n_batch * n_ctx, DA_V_COLS), BF16)],
        compiler_params=_cparams("arbitrary"),
        name="ctx_attention",
    )(sink, lq, g128, qw, kw, vw, qd, kd, vd)


def _filter_kernel(feat_ref, w1_ref, b1_ref, fr_ref, w2_ref, b2_ref, w3_ref, dec_ref, o_ref, *, transposed):
    feat = feat_ref[...]
    fr = fr_ref[...]
    z = jnp.sin(fr * (jnp.dot(feat, w1_ref[...], precision=HI, preferred_element_type=F32) + b1_ref[...]))
    z = jnp.sin(fr * (jnp.dot(z, w2_ref[...], precision=HI, preferred_element_type=F32) + b2_ref[...]))
    h = jnp.dot(z, w3_ref[...], precision=HI, preferred_element_type=F32)
    h = h * jnp.exp(-feat[:, 0:1] * jnp.abs(dec_ref[...]))
    rows = lax.broadcasted_iota(jnp.int32, h.shape, 0) + pl.program_id(0) * h.shape[0]
    cols = lax.broadcasted_iota(jnp.int32, h.shape, 1)
    h = jnp.where((rows == 0) & (cols % (2 * HY_WIDTH) >= HY_WIDTH), 0.0, h)
    if transposed:
        for r in range(h.shape[0] // LANES):
            o_ref[r] = h[r * LANES:(r + 1) * LANES, :].T
    else:
        o_ref[...] = h


def hyena_filter_features(n):
    t = jnp.linspace(0.0, 1.0, n, dtype=F32)[:, None]
    bands = jnp.linspace(1e-4, HY_BANDS - 1, HY_BANDS, dtype=F32)
    phase = (2.0 * math.pi / n) * jnp.arange(n, dtype=F32)[:, None] * bands
    feat = jnp.concatenate([t, jnp.cos(phase), -jnp.sin(phase)], axis=-1)
    return jnp.pad(feat, ((0, 0), (0, LANES - HY_EMB_DIM)))


def hyena_filters(feat, w1p, b1, freq, w2, b2, w3, decay, *, transposed):
    n = feat.shape[0]
    tr = min(n, 512)
    ncol = w3.shape[1]
    fw = w2.shape[0]
    const = lambda i: (0, 0)
    if transposed:
        out_spec = pl.BlockSpec((tr // LANES, ncol, LANES), lambda i: (i, 0, 0))
        out_shape = jax.ShapeDtypeStruct((n // LANES, ncol, LANES), F32)
    else:
        out_spec = pl.BlockSpec((tr, ncol), lambda i: (i, 0))
        out_shape = jax.ShapeDtypeStruct((n, ncol), F32)
    return pl.pallas_call(
        functools.partial(_filter_kernel, transposed=transposed),
        grid=(n // tr,),
        in_specs=[
            pl.BlockSpec((tr, LANES), lambda i: (i, 0)),
            pl.BlockSpec((LANES, fw), const), pl.BlockSpec((1, fw), const), pl.BlockSpec((1, fw), const),
            pl.BlockSpec((fw, fw), const), pl.BlockSpec((1, fw), const),
            pl.BlockSpec((fw, ncol), const), pl.BlockSpec((1, ncol), const),
        ],
        out_specs=out_spec,
        out_shape=out_shape,
        compiler_params=_cparams("arbitrary"),
        name="hyena_filters",
    )(feat, w1p, b1, freq, w2, b2, w3, decay)


def dft_constants(seq):
    n = 2 * seq
    n1 = n // LANES
    a_rows = n1 // 2
    k1 = np.arange(n1, dtype=np.float64)[:, None]
    a = np.arange(a_rows, dtype=np.float64)[None, :]
    th = 2.0 * np.pi * k1 * a / n1
    c1, s1 = np.cos(th), np.sin(th)
    m1_data = np.block([[c1, s1], [-s1, c1]])
    m1_real = np.concatenate([c1, -s1], axis=0)
    m3 = np.block([[c1.T, -s1.T], [s1.T, c1.T]]) / n
    b = np.arange(LANES, dtype=np.float64)[None, :]
    psi = 2.0 * np.pi * k1 * b / n
    twc, tws = np.cos(psi), np.sin(psi)
    bb = np.arange(LANES, dtype=np.float64)
    phi = 2.0 * np.pi * np.outer(bb, bb) / LANES
    c2, s2 = np.cos(phi), np.sin(phi)
    w2f = np.block([[c2, -s2], [s2, c2]])
    w2i = np.block([[c2, s2], [-s2, c2]])
    f = lambda x: jnp.asarray(x, dtype=F32)
    return dict(m1_data=f(m1_data), m1_real=f(m1_real), m3=f(m3), twc=f(twc[:, None, :]), tws=f(tws[:, None, :]),
                w2f=f(w2f), w2i=f(w2i), n1=n1, a_rows=a_rows)


def _leftmm_kernel(m_ref, x_ref, o_ref):
    o_ref[...] = jnp.dot(m_ref[...], x_ref[...], precision=HI, preferred_element_type=F32)


def left_matmul(m, x, ct):
    r, k = m.shape
    cols = x.shape[1]
    return pl.pallas_call(
        _leftmm_kernel,
        grid=(cols // ct,),
        in_specs=[pl.BlockSpec((r, k), lambda j: (0, 0)), pl.BlockSpec((k, ct), lambda j: (0, j))],
        out_specs=pl.BlockSpec((r, ct), lambda j: (0, j)),
        out_shape=jax.ShapeDtypeStruct((r, cols), F32),
        compiler_params=_cparams("arbitrary"),
        name="dft_rows",
    )(m, x)


def _leftmm_gate_kernel(m_ref, z_ref, x_ref, zold_ref, bias_ref, o_ref):
    y = jnp.dot(m_ref[...], z_ref[...], precision=HI, preferred_element_type=F32)
    zold = zold_ref[...]
    o_ref[...] = x_ref[...] * (y + bias_ref[...] * zold)


def left_matmul_gate(m, z, xg, zold, bias_cols, ct):
    r, k = m.shape
    cols = z.shape[1]
    col = lambda j: (0, j)
    return pl.pallas_call(
        _leftmm_gate_kernel,
        grid=(cols // ct,),
        in_specs=[pl.BlockSpec((r, k), lambda j: (0, 0)), pl.BlockSpec((k, ct), col), pl.BlockSpec((r, ct), col),
                  pl.BlockSpec((r, ct), col), pl.BlockSpec((1, ct), col)],
        out_specs=pl.BlockSpec((r, ct), col),
        out_shape=jax.ShapeDtypeStruct((r, cols), F32),
        compiler_params=_cparams("arbitrary"),
        name="idft_rows_gate",
    )(m, z, xg, zold, bias_cols)


def _twiddle_fwd(yr, yi, c, s):
    return yr * c + yi * s, yi * c - yr * s


def _spectrum_kernel(yr_ref, yi_ref, twc_ref, tws_ref, w2f_ref, hr_ref, hi_ref, *, kb):
    for t in range(kb):
        ypr, ypi = _twiddle_fwd(yr_ref[t], yi_ref[t], twc_ref[t], tws_ref[t])
        x = jnp.dot(jnp.concatenate([ypr, ypi], axis=1), w2f_ref[...], precision=HI, preferred_element_type=F32)
        xr, xi = x[:, :LANES], x[:, LANES:]
        for o in range(HY_ORDER):
            f0 = slice((2 * o) * HY_WIDTH, (2 * o + 1) * HY_WIDTH)
            f1 = slice((2 * o + 1) * HY_WIDTH, (2 * o + 2) * HY_WIDTH)
            hr_ref[o, t] = xr[f0] + xr[f1]
            hi_ref[o, t] = xi[f0] - xi[f1]


def filter_spectrum(y, consts, kb):
    n1 = consts["n1"]
    nc = 2 * HY_ORDER * HY_WIDTH
    y4 = y.reshape(2, n1, nc, LANES)
    tw = pl.BlockSpec((kb, 1, LANES), lambda i: (i, 0, 0))
    out = pl.BlockSpec((HY_ORDER, kb, HY_WIDTH, LANES), lambda i: (0, i, 0, 0))
    shp = jax.ShapeDtypeStruct((HY_ORDER, n1, HY_WIDTH, LANES), F32)
    return pl.pallas_call(
        functools.partial(_spectrum_kernel, kb=kb),
        grid=(n1 // kb,),
        in_specs=[
            pl.BlockSpec((None, kb, nc, LANES), lambda i: (0, i, 0, 0)),
            pl.BlockSpec((None, kb, nc, LANES), lambda i: (1, i, 0, 0)),
            tw, tw,
            pl.BlockSpec((2 * LANES, 2 * LANES), lambda i: (0, 0)),
        ],
        out_specs=[out, out],
        out_shape=[shp, shp],
        compiler_params=_cparams("arbitrary"),
        name="filter_spectrum",
    )(y4, y4, consts["twc"], consts["tws"], consts["w2f"])


def _freq_kernel(yr_ref, yi_ref, twc_ref, tws_ref, w2f_ref, w2i_ref, hr_ref, hi_ref, zr_ref, zi_ref, *, kb):
    for t in range(kb):
        c, s = twc_ref[t], tws_ref[t]
        ypr, ypi = _twiddle_fwd(yr_ref[t], yi_ref[t], c, s)
        x = jnp.dot(jnp.concatenate([ypr, ypi], axis=1), w2f_ref[...], precision=HI, preferred_element_type=F32)
        xr, xi = x[:, :LANES], x[:, LANES:]
        hr, hi = hr_ref[t], hi_ref[t]
        gr = xr * hr - xi * hi
        gi = xr * hi + xi * hr
        z = jnp.dot(jnp.concatenate([gr, gi], axis=1), w2i_ref[...], precision=HI, preferred_element_type=F32)
        zr, zi = z[:, :LANES], z[:, LANES:]
        zr_ref[t] = zr * c - zi * s
        zi_ref[t] = zi * c + zr * s


def freq_multiply(y, hr, hi, order, consts, kb):
    n1 = consts["n1"]
    y4 = y.reshape(2, n1, HY_WIDTH, LANES)
    tw = pl.BlockSpec((kb, 1, LANES), lambda i: (i, 0, 0))
    blk = lambda p: pl.BlockSpec((None, kb, HY_WIDTH, LANES), lambda i: (p, i, 0, 0))
    wspec = pl.BlockSpec((2 * LANES, 2 * LANES), lambda i: (0, 0))
    out = pl.BlockSpec((kb, HY_WIDTH, LANES), lambda i: (i, 0, 0))
    shp = jax.ShapeDtypeStruct((n1, HY_WIDTH, LANES), F32)
    return pl.pallas_call(
        functools.partial(_freq_kernel, kb=kb),
        grid=(n1 // kb,),
        in_specs=[blk(0), blk(1), tw, tw, wspec, wspec, blk(order), blk(order)],
        out_specs=[out, out],
        out_shape=[shp, shp],
        compiler_params=_cparams("arbitrary"),
        name="freq_multiply",
    )(y4, y4, consts["twc"], consts["tws"], consts["w2f"], consts["w2i"], hr, hi)


def _short_conv_rows(u, prev_row, next_row, w_ref, b_ref):
    rows = lax.broadcasted_iota(jnp.int32, u.shape, 0)
    um1 = jnp.where(rows == 0, prev_row, pltpu.roll(u, 1, 0))
    up1 = jnp.where(rows == u.shape[0] - 1, next_row, pltpu.roll(u, u.shape[0] - 1, 0))
    return b_ref[...] + um1 * w_ref[0:1, :] + u * w_ref[1:2, :] + up1 * w_ref[2:3, :]


def _shortconv_t_kernel(u_ref, up_ref, un_ref, w_ref, b_ref, o_ref, *, n_tiles):
    i = pl.program_id(1)
    prev_row = jnp.where(i > 0, up_ref[SUBLANES - 1:SUBLANES, :], 0.0)
    next_row = jnp.where(i < n_tiles - 1, un_ref[0:1, :], 0.0)
    v = _short_conv_rows(u_ref[...], prev_row, next_row, w_ref, b_ref)
    for part in range(HY_ORDER + 1):
        for r in range(v.shape[0] // LANES):
            o_ref[part, r] = v[r * LANES:(r + 1) * LANES, part * HY_WIDTH:(part + 1) * HY_WIDTH].T


def short_conv_transposed(hy, conv_w, conv_b, *, n_batch, seq, tm):
    nt = seq // tm
    per8 = tm // SUBLANES
    a_rows = seq // LANES
    prev = lambda b, i: (jnp.maximum((b * nt + i) * per8 - 1, 0), 0)
    nxt = lambda b, i: (jnp.minimum((b * nt + i + 1) * per8, n_batch * nt * per8 - 1), 0)
    return pl.pallas_call(
        functools.partial(_shortconv_t_kernel, n_tiles=nt),
        grid=(n_batch, nt),
        in_specs=[
            pl.BlockSpec((tm, HY_COLS), lambda b, i: (b * nt + i, 0)),
            pl.BlockSpec((SUBLANES, HY_COLS), prev),
            pl.BlockSpec((SUBLANES, HY_COLS), nxt),
            pl.BlockSpec((3, HY_COLS), lambda b, i: (0, 0)),
            pl.BlockSpec((1, HY_COLS), lambda b, i: (0, 0)),
        ],
        out_specs=pl.BlockSpec((HY_ORDER + 1, None, tm // LANES, HY_WIDTH, LANES), lambda b, i: (0, b, i, 0, 0)),
        out_shape=jax.ShapeDtypeStruct((HY_ORDER + 1, n_batch, a_rows, HY_WIDTH, LANES), F32),
        compiler_params=_cparams("arbitrary", "arbitrary"),
        name="short_conv_transposed",
    )(hy, hy, hy, conv_w, conv_b)


def _untranspose_kernel(z_ref, o_ref):
    for r in range(z_ref.shape[0]):
        o_ref[r * LANES:(r + 1) * LANES, :] = z_ref[r].T.astype(BF16)


def untranspose(zt, *, n_batch, seq):
    a_rows = seq // LANES
    ta = min(SUBLANES, a_rows)
    nt = a_rows // ta
    return pl.pallas_call(
        _untranspose_kernel,
        grid=(n_batch, nt),
        in_specs=[pl.BlockSpec((None, ta, HY_WIDTH, LANES), lambda b, i: (b, i, 0, 0))],
        out_specs=pl.BlockSpec((ta * LANES, HY_WIDTH), lambda b, i: (b * nt + i, 0)),
        out_shape=jax.ShapeDtypeStruct((n_batch * seq, HY_WIDTH), BF16),
        compiler_params=_cparams("arbitrary", "arbitrary"),
        name="untranspose",
    )(zt)


def hyena_latent(hy, conv_w, conv_b, hy_bias, filt_t, consts, *, n_batch, seq):
    assert n_batch == 2, "the two batches are packed as real / imaginary parts of one DFT"
    n1, a_rows = consts["n1"], consts["a_rows"]
    cols = HY_WIDTH * LANES
    ct = min(cols, 2048)
    kb = min(n1, 4)
    parts = short_conv_transposed(hy, conv_w, conv_b, n_batch=n_batch, seq=seq, tm=min(seq, 1024))
    parts = parts.reshape(HY_ORDER + 1, n_batch * a_rows, cols)
    yf = left_matmul(consts["m1_real"], filt_t.reshape(a_rows, 2 * HY_ORDER * cols), ct)
    hr, hi = filter_spectrum(yf, consts, 1)
    z = parts[HY_ORDER]
    for o in range(HY_ORDER):
        y = left_matmul(consts["m1_data"], z, ct)
        zr, zi = freq_multiply(y, hr, hi, o, consts, kb)
        zcat = jnp.concatenate([zr.reshape(n1, cols), zi.reshape(n1, cols)], axis=0)
        bias_cols = jnp.repeat(hy_bias[o], LANES)[None, :]
        z = left_matmul_gate(consts["m3"], zcat, parts[o], z, bias_cols, ct)
    return untranspose(z.reshape(n_batch, a_rows, HY_WIDTH, LANES), n_batch=n_batch, seq=seq)


def _ctx_hyena_kernel(u_ref, w_ref, b_ref, h_ref, bias_ref, fc_ref, fs_ref, fct_ref, fst_ref, o_ref):
    u = u_ref[...]
    zero = jnp.zeros((1, u.shape[1]), F32)
    v = _short_conv_rows(u, zero, zero, w_ref, b_ref)
    z = v[:, HY_ORDER * HY_WIDTH:]
    mm = lambda a, b: jnp.dot(a, b, precision=HI, preferred_element_type=F32)
    fc, fs = fc_ref[...], fs_ref[...]
    inv_n = 1.0 / fc.shape[0]
    for o in range(HY_ORDER):
        h0 = h_ref[:, (2 * o) * HY_WIDTH:(2 * o + 1) * HY_WIDTH]
        h1 = h_ref[:, (2 * o + 1) * HY_WIDTH:(2 * o + 2) * HY_WIDTH]
        sr, si = mm(fc, h0 + h1), mm(fs, h1 - h0)
        zr, zi = mm(fc, z), -mm(fs, z)
        gr = zr * sr - zi * si
        gi = zr * si + zi * sr
        y = (mm(fct_ref[...], gr) - mm(fst_ref[...], gi)) * inv_n
        z = v[:, o * HY_WIDTH:(o + 1) * HY_WIDTH] * (y + bias_ref[o:o + 1, :] * z)
    o_ref[...] = z.astype(BF16)


def ctx_hyena(hy, conv_w, conv_b, filt, hy_bias, *, n_batch, seq, n_ctx):
    n = 2 * n_ctx
    k = np.arange(n, dtype=np.float64)[:, None]
    m = np.arange(n_ctx, dtype=np.float64)[None, :]
    ang = 2.0 * np.pi * k * m / n
    fc, fs = jnp.asarray(np.cos(ang), F32), jnp.asarray(np.sin(ang), F32)
    ctx0 = (n_batch * seq) // n_ctx
    const = lambda b: (0, 0)
    return pl.pallas_call(
        _ctx_hyena_kernel,
        grid=(n_batch,),
        in_specs=[
            pl.BlockSpec((n_ctx, HY_COLS), lambda b: (ctx0 + b, 0)),
            pl.BlockSpec((3, HY_COLS), const), pl.BlockSpec((1, HY_COLS), const),
            pl.BlockSpec((n_ctx, 2 * HY_ORDER * HY_WIDTH), const),
            pl.BlockSpec((HY_ORDER, HY_WIDTH), const),
            pl.BlockSpec((n, n_ctx), const), pl.BlockSpec((n, n_ctx), const),
            pl.BlockSpec((n_ctx, n), const), pl.BlockSpec((n_ctx, n), const),
        ],
        out_specs=pl.BlockSpec((n_ctx, HY_WIDTH), lambda b: (b, 0)),
        out_shape=jax.ShapeDtypeStruct((n_batch * n_ctx, HY_WIDTH), BF16),
        compiler_params=_cparams("arbitrary"),
        name="ctx_hyena",
    )(hy, conv_w, conv_b, filt, hy_bias, fc, fs, fc.T, fs.T)


def _outproj_kernel(x_ref, yh_ref, yw_ref, yd_ref, mod_ref, wo_ref, g_ref, wq_ref, xo_ref, h2_ref, qp_ref):
    y = jnp.dot(yh_ref[...], wo_ref[0:HY_WIDTH, :], preferred_element_type=F32)
    y = y + jnp.dot(yw_ref[...], wo_ref[HY_WIDTH:HY_WIDTH + WA_Q_COLS, :], preferred_element_type=F32)
    y = y + jnp.dot(yd_ref[...], wo_ref[HY_WIDTH + WA_Q_COLS:, :], preferred_element_type=F32)
    xn = x_ref[...] + mod_ref[2:3, :] * y
    xo_ref[...] = xn
    h2 = _rmsnorm_mod(xn, g_ref[...], mod_ref[3:4, :], mod_ref[4:5, :]).astype(BF16)
    h2_ref[...] = h2
    qp_ref[...] = jnp.dot(h2, wq_ref[...], preferred_element_type=F32)


def out_proj(x_all, y_hy, y_wa, y_da, mod, wo_bf, g2, wq_bf, *, n_rows, n_batch, seq, tm):
    d = x_all.shape[1]
    tpb = seq // tm
    grp = functools.partial(_group_of_tile, tiles_per_batch=tpb, n_batch=n_batch)
    row = lambda i: (i, 0)
    const = lambda i: (0, 0)
    mix = HY_WIDTH + WA_Q_COLS + DA_V_COLS
    return pl.pallas_call(
        _outproj_kernel,
        grid=(n_rows // tm,),
        in_specs=[
            pl.BlockSpec((tm, d), row), pl.BlockSpec((tm, HY_WIDTH), row), pl.BlockSpec((tm, WA_Q_COLS), row),
            pl.BlockSpec((tm, DA_V_COLS), row),
            pl.BlockSpec((None, N_MOD, d), lambda i: (grp(i), 0, 0)),
            pl.BlockSpec((mix, d), const), pl.BlockSpec((1, d), const), pl.BlockSpec((d, PEER_QCOLS), const),
        ],
        out_specs=[pl.BlockSpec((tm, d), row), pl.BlockSpec((tm, d), row), pl.BlockSpec((tm, PEER_QCOLS), row)],
        out_shape=[jax.ShapeDtypeStruct((n_rows, d), F32), jax.ShapeDtypeStruct((n_rows, d), BF16),
                   jax.ShapeDtypeStruct((n_rows, PEER_QCOLS), F32)],
        compiler_params=_cparams("arbitrary"),
        name="out_proj",
    )(x_all, y_hy, y_wa, y_da, mod, wo_bf, g2, wq_bf)


def _sort_pairs(n):
    pairs = []

    def merge(lo, cnt, r):
        step = r * 2
        if step < cnt:
            merge(lo, cnt, step)
            merge(lo + r, cnt, step)
            for i in range(lo + r, lo + cnt - r, step):
                pairs.append((i, i + r))
        else:
            pairs.append((lo, lo + r))

    def sort(lo, cnt):
        if cnt > 1:
            half = cnt // 2
            sort(lo, half)
            sort(lo + half, half)
            merge(lo, cnt, 1)

    sort(0, n)
    return pairs


def _sort_desc(vals):
    n = 1
    while n < len(vals):
        n *= 2
    v = list(vals) + [None] * (n - len(vals))
    for i, j in _sort_pairs(n):
        a, b = v[i], v[j]
        if b is None:
            continue
        if a is None:
            v[i], v[j] = b, None
        else:
            v[i], v[j] = jnp.maximum(a, b), jnp.minimum(a, b)
    return v[:len(vals)]


def _top16_rows(s):
    k = PEER_TOPK
    v = _sort_desc([s[SUBLANES * r:SUBLANES * (r + 1), :] for r in range(s.shape[0] // SUBLANES)])
    for shift in (4, 2, 1):
        other = [pltpu.roll(x, shift, 0) for x in v]
        v = [jnp.maximum(v[i], other[k - 1 - i]) for i in range(k)]
        d = k // 2
        while d >= 1:
            for i in range(k):
                if (i & d) == 0:
                    a, b = v[i], v[i + d]
                    v[i], v[i + d] = jnp.maximum(a, b), jnp.minimum(a, b)
            d //= 2
    return v


def _peer_topk_kernel(q_ref, keys_ref, thr_ref, e1_ref, s2_ref, e2_ref):
    k = PEER_TOPK
    tmk = q_ref.shape[0]
    sub = lax.broadcasted_iota(jnp.int32, (SUBLANES, tmk), 0)
    scores, tops = [], []
    for hp in range(2 * PEER_HEADS):
        s = lax.dot_general(keys_ref[hp], q_ref[:, hp * N_KEYS:(hp + 1) * N_KEYS], (((1,), (1,)), ((), ())),
                            precision=HI, preferred_element_type=F32)
        scores.append(s)
        top = _top16_rows(s)
        top.append(jnp.max(jnp.where(s < top[k - 1][0:1, :], s, NEG_BIG), axis=0, keepdims=True))
        tops.append(top)
    packed = []
    for p in range(2):
        lst = []
        for r in range(k + 1):
            acc = jnp.broadcast_to(tops[p][r], (SUBLANES, tmk))
            for h in range(1, PEER_HEADS):
                acc = jnp.where(sub == h, tops[2 * h + p][r], acc)
            lst.append(acc)
        packed.append(lst)
    cand = [packed[0][i] + packed[1][j] for i in range(k) for j in range(k) if (i + 1) * (j + 1) <= k]
    cs = _sort_desc(cand)
    c17 = jnp.maximum(cs[k], jnp.maximum(packed[0][k] + packed[1][0], packed[0][0] + packed[1][k]))
    tau = 0.5 * (cs[k - 1] + c17)
    zsum = jnp.zeros_like(tau)
    for r in range(k):
        zsum = zsum + jnp.exp(cs[r] - cs[0])
    inv_z = 1.0 / zsum
    for h in range(PEER_HEADS):
        s1, s2 = scores[2 * h], scores[2 * h + 1]
        row = lambda x: x[h:h + 1, :]
        thr_ref[h] = row(tau) - s1
        e1_ref[h] = jnp.exp(s1 - row(packed[0][0]))
        s2_ref[h] = s2
        e2_ref[h] = jnp.exp(s2 - row(packed[1][0])) * row(inv_z)


def peer_topk(qp, keys, *, tmk):
    t = qp.shape[0]
    out = pl.BlockSpec((PEER_HEADS, N_KEYS, tmk), lambda i: (0, 0, i))
    shp = jax.ShapeDtypeStruct((PEER_HEADS, N_KEYS, t), F32)
    return pl.pallas_call(
        _peer_topk_kernel,
        grid=(t // tmk,),
        in_specs=[pl.BlockSpec((tmk, PEER_QCOLS), lambda i: (i, 0)),
                  pl.BlockSpec((2 * PEER_HEADS, N_KEYS, PEER_QDIM // 2), lambda i: (0, 0, 0))],
        out_specs=[out, out, out, out],
        out_shape=[shp, shp, shp, shp],
        compiler_params=_cparams("arbitrary"),
        name="peer_topk",
    )(qp, keys)


def _peer_dense_kernel(x_ref, h_ref, mod_ref, u_ref, vt_ref, thr_ref, e1_ref, s2_ref, e2_ref, g_ref, o_ref,
                       a_scr, w_scr, acc_scr, *, final_norm):
    e = pl.program_id(1)

    @pl.when(e == 0)
    def _():
        acc_scr[...] = jnp.zeros(acc_scr.shape, F32)

    a_scr[...] = lax.dot_general(u_ref[...], h_ref[...], (((1,), (1,)), ((), ())), preferred_element_type=F32)
    n_i = u_ref.shape[0] // N_KEYS
    sqrt_half = math.sqrt(0.5)
    for ii in range(n_i):
        rows = slice(ii * N_KEYS, (ii + 1) * N_KEYS)
        a = a_scr[rows, :]
        act = 0.5 * a * (1.0 + lax.erf(a * sqrt_half))
        gate = None
        for h in range(PEER_HEADS):
            sel = s2_ref[h] >= thr_ref[h, ii:ii + 1, :]
            term = jnp.where(sel, e2_ref[h] * e1_ref[h, ii:ii + 1, :], 0.0)
            gate = term if gate is None else gate + term
        w_scr[rows, :] = (gate * act).astype(BF16)
    acc_scr[...] += jnp.dot(vt_ref[...], w_scr[...], preferred_element_type=F32)

    @pl.when(e == pl.num_programs(1) - 1)
    def _():
        xn = x_ref[...] + mod_ref[5:6, :] * acc_scr[...].T
        if final_norm:
            ms = jnp.mean(xn * xn, axis=-1, keepdims=True)
            xn = xn * lax.rsqrt(ms + NORM_EPS) * g_ref[...]
        o_ref[...] = xn


def peer_dense(x, h2, mod, u_bf, vt_bf, thr, e1, s2, e2, g_final, *, n_batch, seq, tm, te, final_norm):
    n_rows, d = x.shape
    tpb = seq // tm
    grp = functools.partial(_group_of_tile, tiles_per_batch=tpb, n_batch=n_batch)
    n_i = te // N_KEYS
    tok = lambda i, e: (i, 0)
    rows_i = pl.BlockSpec((PEER_HEADS, n_i, tm), lambda i, e: (0, e, i))
    full = pl.BlockSpec((PEER_HEADS, N_KEYS, tm), lambda i, e: (0, 0, i))
    return pl.pallas_call(
        functools.partial(_peer_dense_kernel, final_norm=final_norm),
        grid=(n_rows // tm, N_EXPERTS // te),
        in_specs=[
            pl.BlockSpec((tm, d), tok), pl.BlockSpec((tm, d), tok),
            pl.BlockSpec((None, N_MOD, d), lambda i, e: (grp(i), 0, 0)),
            pl.BlockSpec((te, d), lambda i, e: (e, 0)),
            pl.BlockSpec((d, te), lambda i, e: (0, e)),
            rows_i, rows_i, full, full,
            pl.BlockSpec((1, d), lambda i, e: (0, 0)),
        ],
        out_specs=pl.BlockSpec((tm, d), tok),
        out_shape=jax.ShapeDtypeStruct((n_rows, d), F32),
        scratch_shapes=[pltpu.VMEM((te, tm), F32), pltpu.VMEM((te, tm), BF16), pltpu.VMEM((d, tm), F32)],
        compiler_params=_cparams("arbitrary", "arbitrary"),
        name="peer_dense",
    )(x, h2, mod, u_bf, vt_bf, thr, e1, s2, e2, g_final)


def kernel(x, c, ctx, c_ctx, w_mod, b_mod, norm1_g, w_in, hy_conv_w, hy_conv_b, hy_w1, hy_b1, hy_freq, hy_w2,
           hy_b2, hy_w3, hy_decay, hy_bias, wa_sink, da_lq1, da_lk1, da_lq2, da_lk2, da_norm_g, w_out, norm2_g,
           peer_wq, peer_keys, peer_u, peer_v, final_g):
    n_batch, seq, d = x.shape
    n_ctx = ctx.shape[1]
    depth = w_in.shape[0]
    n_lat = n_batch * seq
    tm = 512 if seq % 512 == 0 else 256
    assert seq % tm == 0 and (n_batch * n_ctx) % tm == 0 and n_lat % n_ctx == 0 and n_batch + 1 <= SUBLANES

    cpad = jnp.zeros((SUBLANES, d), F32).at[:n_batch].set(c).at[n_batch].set(c_ctx)
    mod_all = mod_vectors(cpad, w_mod, b_mod).reshape(depth, SUBLANES, N_MOD, d)
    rope_wa = rope_tables(seq, HEAD_DIM, tm)
    rope_da = rope_tables(seq, DA_QK_DIM, tm)
    consts = dft_constants(seq)
    feat_lat = hyena_filter_features(seq)
    feat_ctx = hyena_filter_features(n_ctx)
    x_all = jnp.concatenate([x.reshape(n_lat, d), ctx.reshape(n_batch * n_ctx, d)], axis=0)

    out = None
    for li in range(depth):
        last = li == depth - 1
        lam_init = DA_LAMBDA_BASE - DA_LAMBDA_AMP * math.exp(-DA_LAMBDA_RATE * li)
        mod = mod_all[li]
        hy, qw, kw, vw, qd, kd, vd, vdt = proj_in(x_all, mod, norm1_g[li][None, :], w_in[li].astype(BF16), rope_wa,
                                             rope_da, n_batch=n_batch, seq=seq, tm=tm)
        w1p = jnp.pad(hy_w1[li], ((0, LANES - HY_EMB_DIM), (0, 0)))
        fargs = (w1p, hy_b1[li][None, :], hy_freq[li][None, :], hy_w2[li], hy_b2[li][None, :], hy_w3[li],
                 hy_decay[li].reshape(1, -1))
        conv_b = hy_conv_b[li][None, :]
        filt_t = hyena_filters(feat_lat, *fargs, transposed=True)
        y_hy = hyena_latent(hy, hy_conv_w[li], conv_b, hy_bias[li], filt_t, consts, n_batch=n_batch, seq=seq)
        y_wa = window_attention(wa_sink[li], qw, kw, vw, n_batch=n_batch, seq=seq, n_ctx=n_ctx)
        lq = jnp.stack([da_lq1[li], da_lk1[li], da_lq2[li], da_lk2[li]], axis=0)
        g128 = jnp.tile(da_norm_g[li], LANES // DA_V_DIM)[None, :]
        tq = min(seq, 512)
        g_rows = jnp.broadcast_to(da_norm_g[li][:, None], (DA_V_DIM, tq))
        y_da = diff_attention(lq, g_rows, qd, kd, vdt, n_batch=n_batch, seq=seq, n_ctx=n_ctx, lam_init=lam_init,
                              tq=tq, tk=min(seq, 256), ahead=3)
        if not last:
            filt_c = hyena_filters(feat_ctx, *fargs, transposed=False)
            yc_hy = ctx_hyena(hy, hy_conv_w[li], conv_b, filt_c, hy_bias[li], n_batch=n_batch, seq=seq, n_ctx=n_ctx)
            yc_wa, yc_da = ctx_attention(wa_sink[li], lq, g128, qw, kw, vw, qd, kd, vd, n_batch=n_batch, seq=seq,
                                         n_ctx=n_ctx, lam_init=lam_init)
            y_hy = jnp.concatenate([y_hy, yc_hy], axis=0)
            y_wa = jnp.concatenate([y_wa, yc_wa], axis=0)
            y_da = jnp.concatenate([y_da, yc_da], axis=0)
        n_rows = n_lat if last else x_all.shape[0]
        x_mid, h2, qp = out_proj(x_all, y_hy, y_wa, y_da, mod, w_out[li].astype(BF16), norm2_g[li][None, :],
                                 peer_wq[li].astype(BF16), n_rows=n_rows, n_batch=n_batch, seq=seq, tm=tm)
        keys = peer_keys[li].reshape(2 * PEER_HEADS, N_KEYS, PEER_QDIM // 2)
        thr, e1, s2, e2 = peer_topk(qp, keys, tmk=256)
        x_all = peer_dense(x_mid, h2, mod, peer_u[li].astype(BF16), peer_v[li].T.astype(BF16), thr, e1, s2, e2,
                           final_g[None, :], n_batch=n_batch, seq=seq, tm=tm, te=1024, final_norm=last)
        out = x_all
    return out.reshape(n_batch, seq, d)
```

```python
import functools
import math

import numpy as np
import jax
import jax.numpy as jnp
from jax import lax
from jax.experimental import pallas as pl
from jax.experimental.pallas import tpu as pltpu

F32 = jnp.float32
BF16 = jnp.bfloat16
HI = lax.Precision.HIGHEST

LANES = 128
SUBLANES = 8
VMEM_LIMIT_BYTES = 56 * 1024 * 1024

D_MODEL = 1024
N_MOD = 6
NORM_EPS = 1e-6
ROPE_BASE = 10000.0
GRID_W = 64
BLOCK = 128
HEAD_DIM = 64
HY_WIDTH = 256
HY_ORDER = 2
HY_BANDS = 16
HY_EMB_DIM = 1 + 2 * HY_BANDS
HY_FILTER_WIDTH = 64
HY_COLS = 3 * HY_WIDTH
WA_Q_HEADS = 8
WA_KV_HEADS = 2
WA_GROUP = 4
WINDOW = 128
WA_Q_COLS = WA_Q_HEADS * HEAD_DIM
WA_KV_COLS = WA_KV_HEADS * HEAD_DIM
WA_COLS = WA_Q_COLS + 2 * WA_KV_COLS
DA_HEADS = 4
DA_QK_DIM = 32
DA_V_DIM = 64
DA_QK_COLS = DA_HEADS * 2 * DA_QK_DIM
DA_V_COLS = DA_HEADS * DA_V_DIM
DA_COLS = 2 * DA_QK_COLS + DA_V_COLS
DA_LAMBDA_BASE = 0.8
DA_LAMBDA_AMP = 0.6
DA_LAMBDA_RATE = 0.3
IN_COLS = HY_COLS + WA_COLS + DA_COLS
PEER_HEADS = 8
N_KEYS = 128
N_EXPERTS = N_KEYS * N_KEYS
PEER_TOPK = 16
PEER_QDIM = 256
PEER_QCOLS = PEER_HEADS * PEER_QDIM
NEG_BIG = -1e30
LOG2E = math.log2(math.e)


def _cparams(*sem):
    return pltpu.CompilerParams(dimension_semantics=sem, vmem_limit_bytes=VMEM_LIMIT_BYTES)


def _rmsnorm_mod(x, g, shift, scale):
    ms = jnp.mean(x * x, axis=-1, keepdims=True)
    return (x * lax.rsqrt(ms + NORM_EPS) * g) * (1.0 + scale) + shift


def _mod_kernel(c_ref, w_ref, b_ref, o_ref):
    c = c_ref[...]
    act = c * jax.nn.sigmoid(c)
    o_ref[...] = jnp.dot(act, w_ref[...], precision=HI, preferred_element_type=F32) + b_ref[...]


def mod_vectors(cpad, w_mod, b_mod):
    n_layers, d, n = w_mod.shape
    tn = 1536
    return pl.pallas_call(
        _mod_kernel,
        grid=(n_layers, n // tn),
        in_specs=[
            pl.BlockSpec((SUBLANES, d), lambda l, j: (0, 0)),
            pl.BlockSpec((None, d, tn), lambda l, j: (l, 0, j)),
            pl.BlockSpec((None, 1, tn), lambda l, j: (l, 0, j)),
        ],
        out_specs=pl.BlockSpec((None, SUBLANES, tn), lambda l, j: (l, 0, j)),
        out_shape=jax.ShapeDtypeStruct((n_layers, SUBLANES, n), F32),
        compiler_params=_cparams("arbitrary", "arbitrary"),
        name="mod_vectors",
    )(cpad, w_mod, b_mod.reshape(n_layers, 1, n))


def _rope128(x, cos, sin_a, sin_b, m):
    return x * cos + pltpu.roll(x, LANES - m, 1) * sin_a + pltpu.roll(x, m, 1) * sin_b


def _proj_in_kernel(x_ref, mod_ref, g_ref, w_ref, cw_ref, saw_ref, sbw_ref, cd_ref, sad_ref, sbd_ref,
                    hy_ref, qw_ref, kw_ref, vw_ref, qd_ref, kd_ref, vd_ref, vdt_ref):
    h = _rmsnorm_mod(x_ref[...], g_ref[...], mod_ref[0:1, :], mod_ref[1:2, :]).astype(BF16)
    hy_ref[...] = jnp.dot(h, w_ref[:, 0:HY_COLS], preferred_element_type=F32)
    wa = jnp.dot(h, w_ref[:, HY_COLS:HY_COLS + WA_COLS], preferred_element_type=F32)
    da = jnp.dot(h, w_ref[:, HY_COLS + WA_COLS:IN_COLS], preferred_element_type=F32)
    cw, saw, sbw = cw_ref[...], saw_ref[...], sbw_ref[...]
    cd, sad, sbd = cd_ref[...], sad_ref[...], sbd_ref[...]
    wa_scale = HEAD_DIM ** -0.5
    da_scale = DA_QK_DIM ** -0.5 * LOG2E
    for kb in range(WA_Q_COLS // LANES):
        blk = wa[:, kb * LANES:(kb + 1) * LANES]
        qw_ref[:, kb * LANES:(kb + 1) * LANES] = (_rope128(blk, cw, saw, sbw, 16) * wa_scale).astype(BF16)
    kw_ref[...] = _rope128(wa[:, WA_Q_COLS:WA_Q_COLS + WA_KV_COLS], cw, saw, sbw, 16).astype(BF16)
    vw_ref[...] = wa[:, WA_Q_COLS + WA_KV_COLS:WA_COLS].astype(BF16)
    for kb in range(DA_QK_COLS // LANES):
        blk = da[:, kb * LANES:(kb + 1) * LANES]
        qd_ref[:, kb * LANES:(kb + 1) * LANES] = (_rope128(blk, cd, sad, sbd, 8) * da_scale).astype(BF16)
        blk = da[:, DA_QK_COLS + kb * LANES:DA_QK_COLS + (kb + 1) * LANES]
        kd_ref[:, kb * LANES:(kb + 1) * LANES] = _rope128(blk, cd, sad, sbd, 8).astype(BF16)
    vd = da[:, 2 * DA_QK_COLS:DA_COLS]
    vd_ref[...] = vd.astype(BF16)
    vdt_ref[...] = vd.T.astype(BF16)


def _group_of_tile(i, tiles_per_batch, n_batch):
    return jnp.minimum(i // tiles_per_batch, n_batch)


def proj_in(x_all, mod, g, w_bf, rope_wa, rope_da, *, n_batch, seq, tm):
    t_all, d = x_all.shape
    tpb = seq // tm
    grp = functools.partial(_group_of_tile, tiles_per_batch=tpb, n_batch=n_batch)
    pos = lambda i: (jnp.where(i < n_batch * tpb, i % tpb, tpb), 0)
    row = lambda i: (i, 0)
    const = lambda i: (0, 0)
    tab = pl.BlockSpec((tm, LANES), pos)
    outs = [(HY_COLS, F32), (WA_Q_COLS, BF16), (WA_KV_COLS, BF16), (WA_KV_COLS, BF16),
            (DA_QK_COLS, BF16), (DA_QK_COLS, BF16), (DA_V_COLS, BF16)]
    return pl.pallas_call(
        _proj_in_kernel,
        grid=(t_all // tm,),
        in_specs=[
            pl.BlockSpec((tm, d), row),
            pl.BlockSpec((None, N_MOD, d), lambda i: (grp(i), 0, 0)),
            pl.BlockSpec((1, d), const),
            pl.BlockSpec((d, IN_COLS), const),
            tab, tab, tab, tab, tab, tab,
        ],
        out_specs=[pl.BlockSpec((tm, n), row) for n, _ in outs] + [pl.BlockSpec((DA_V_COLS, tm), lambda i: (0, i))],
        out_shape=[jax.ShapeDtypeStruct((t_all, n), dt) for n, dt in outs]
        + [jax.ShapeDtypeStruct((DA_V_COLS, t_all), BF16)],
        compiler_params=_cparams("arbitrary"),
        name="proj_in",
    )(x_all, mod, g, w_bf, *rope_wa, *rope_da)


def rope_tables(seq, head_dim, pad_rows):
    half = head_dim // 2
    axis_dim = half
    m = axis_dim // 2
    t = np.arange(seq)
    rowcol = np.stack([t // GRID_W, t % GRID_W], axis=0).astype(np.float32)
    inv_freq = (ROPE_BASE ** (-jnp.arange(0, axis_dim, 2, dtype=F32) / axis_dim))
    ang = jnp.asarray(rowcol)[:, :, None] * inv_freq[None, None, :]
    cos, sin = jnp.cos(ang), jnp.sin(ang)
    lane = np.arange(LANES)
    dd = lane % head_dim
    axis = dd // half
    sub = dd % half
    idx = sub % m
    first = sub < m
    cos_t = cos[axis, :, idx].T
    sin_t = sin[axis, :, idx].T
    sin_a = jnp.where(first[None, :], -sin_t, 0.0)
    sin_b = jnp.where(first[None, :], 0.0, sin_t)
    pad1 = jnp.ones((pad_rows, LANES), F32)
    pad0 = jnp.zeros((pad_rows, LANES), F32)
    return (jnp.concatenate([cos_t, pad1], 0), jnp.concatenate([sin_a, pad0], 0),
            jnp.concatenate([sin_b, pad0], 0))


def _softmax_sink_attend(q, k, v, sink, valid):
    s = lax.dot_general(q, k, (((1,), (1,)), ((), ())), preferred_element_type=F32)
    if valid is not None:
        nm = valid.shape[1]
        s = jnp.concatenate([jnp.where(valid, s[:, :nm], NEG_BIG), s[:, nm:]], axis=1)
    m = jnp.maximum(jnp.max(s, axis=1, keepdims=True), sink)
    p = jnp.exp(s - m)
    den = jnp.sum(p, axis=1, keepdims=True) + jnp.exp(sink - m)
    return jnp.dot(p.astype(BF16), v, preferred_element_type=F32) / den


def _wattn_kernel(sink_ref, q_ref, kp_ref, kc_ref, kn_ref, vp_ref, vc_ref, vn_ref, kx_ref, vx_ref, o_ref,
                  *, n_blocks):
    n = pl.program_id(1)
    r = lax.broadcasted_iota(jnp.int32, (BLOCK, 3 * BLOCK), 0)
    j = lax.broadcasted_iota(jnp.int32, (BLOCK, 3 * BLOCK), 1)
    lo = jnp.where(n > 0, 0, BLOCK)
    hi = jnp.where(n < n_blocks - 1, 3 * BLOCK, 2 * BLOCK)
    valid = (j >= jnp.maximum(r, lo)) & (j <= r + 2 * WINDOW) & (j < hi)
    outs = []
    for hk in range(WA_KV_HEADS):
        sl = slice(hk * HEAD_DIM, (hk + 1) * HEAD_DIM)
        k = jnp.concatenate([kp_ref[:, sl], kc_ref[:, sl], kn_ref[:, sl], kx_ref[:, sl]], axis=0)
        v = jnp.concatenate([vp_ref[:, sl], vc_ref[:, sl], vn_ref[:, sl], vx_ref[:, sl]], axis=0)
        for g in range(WA_GROUP):
            h = hk * WA_GROUP + g
            q = q_ref[:, h * HEAD_DIM:(h + 1) * HEAD_DIM]
            outs.append(_softmax_sink_attend(q, k, v, sink_ref[h], valid))
    o_ref[...] = jnp.concatenate(outs, axis=1).astype(BF16)


def window_attention(sink, qw, kw, vw, *, n_batch, seq, n_ctx):
    nb = seq // BLOCK
    ctx0 = (n_batch * seq) // n_ctx
    cur = lambda b, n: (b * nb + n, 0)
    prev = lambda b, n: (b * nb + jnp.maximum(n - 1, 0), 0)
    nxt = lambda b, n: (b * nb + jnp.minimum(n + 1, nb - 1), 0)
    ctx = lambda b, n: (ctx0 + b, 0)
    kv = lambda f: pl.BlockSpec((BLOCK, WA_KV_COLS), f)
    return pl.pallas_call(
        functools.partial(_wattn_kernel, n_blocks=nb),
        grid=(n_batch, nb),
        in_specs=[
            pl.BlockSpec(memory_space=pltpu.SMEM),
            pl.BlockSpec((BLOCK, WA_Q_COLS), cur),
            kv(prev), kv(cur), kv(nxt), kv(prev), kv(cur), kv(nxt),
            pl.BlockSpec((n_ctx, WA_KV_COLS), ctx),
            pl.BlockSpec((n_ctx, WA_KV_COLS), ctx),
        ],
        out_specs=pl.BlockSpec((BLOCK, WA_Q_COLS), cur),
        out_shape=jax.ShapeDtypeStruct((n_batch * seq, WA_Q_COLS), BF16),
        compiler_params=_cparams("arbitrary", "arbitrary"),
        name="window_attention",
    )(sink, qw, kw, kw, kw, vw, vw, vw, kw, vw)


def _diff_lambda(lq_ref, lam_init):
    a = jnp.sum(lq_ref[0:1, :] * lq_ref[1:2, :], axis=1, keepdims=True)
    b = jnp.sum(lq_ref[2:3, :] * lq_ref[3:4, :], axis=1, keepdims=True)
    return jnp.exp(a) - jnp.exp(b) + lam_init


def _diff_finish(acc1, l1, acc2, l2, lam, g128, lam_init):
    lane = lax.broadcasted_iota(jnp.int32, acc1[0].shape, 1)
    lo = lane < DA_V_DIM
    blocks = []
    for vb in range(DA_HEADS // 2):
        o = []
        for h in (2 * vb, 2 * vb + 1):
            o.append(acc1[h] / l1[h] - lam * (acc2[h] / l2[h]))
        blk = jnp.where(lo, o[0], o[1])
        sq = blk * blk
        s_lo = jnp.sum(jnp.where(lo, sq, 0.0), axis=1, keepdims=True)
        s_hi = jnp.sum(jnp.where(lo, 0.0, sq), axis=1, keepdims=True)
        ms = jnp.where(lo, s_lo, s_hi) * (1.0 / DA_V_DIM)
        blocks.append(blk * lax.rsqrt(ms + NORM_EPS) * g128 * (1.0 - lam_init))
    return jnp.concatenate(blocks, axis=1)


def _masked_q(q_ref, hm):
    kb, sub = divmod(hm, LANES // DA_QK_DIM)
    blk = q_ref[:, kb * LANES:(kb + 1) * LANES]
    lane = lax.broadcasted_iota(jnp.int32, blk.shape, 1)
    keep = (lane >= sub * DA_QK_DIM) & (lane < (sub + 1) * DA_QK_DIM)
    return jnp.where(keep, blk, jnp.zeros_like(blk))


def _dattn_kernel(lq_ref, g_ref, q_ref, k_ref, vt_ref, kx_ref, vxt_ref, o_ref, qm_scr, m_scr, l_scr, acc_scr,
                  *, lam_init, tk, ahead):
    n_hm = 2 * DA_HEADS
    per_blk = LANES // DA_QK_DIM
    for hm in range(n_hm):
        qm_scr[hm] = _masked_q(q_ref, hm)
    m_scr[...] = jnp.full(m_scr.shape, NEG_BIG, F32)
    l_scr[...] = jnp.zeros(l_scr.shape, F32)
    acc_scr[...] = jnp.zeros(acc_scr.shape, F32)

    def update(kc, vtc):
        def scores(hm):
            kb = hm // per_blk
            return lax.dot_general(kc[:, kb * LANES:(kb + 1) * LANES], qm_scr[hm], (((1,), (1,)), ((), ())),
                                   preferred_element_type=F32)

        sts = [scores(hm) for hm in range(ahead)]
        for hm in range(n_hm):
            h = hm // 2
            if hm + ahead < n_hm:
                sts.append(scores(hm + ahead))
            st = sts[hm]
            m_old = m_scr[hm]
            m_new = jnp.maximum(m_old, jnp.max(st, axis=0, keepdims=True))
            alpha = jnp.exp2(m_old - m_new)
            pt = jnp.exp2(st - m_new)
            l_scr[hm] = alpha * l_scr[hm] + jnp.sum(pt, axis=0, keepdims=True)
            acc_scr[hm] = alpha * acc_scr[hm] + jnp.dot(vtc[h * DA_V_DIM:(h + 1) * DA_V_DIM, :], pt.astype(BF16),
                                                        preferred_element_type=F32)
            m_scr[hm] = m_new

    def body(j, carry):
        start = pl.multiple_of(j * tk, tk)
        update(k_ref[pl.ds(start, tk), :], vt_ref[:, pl.ds(start, tk)])
        return carry

    lax.fori_loop(0, k_ref.shape[0] // tk, body, 0)
    update(kx_ref[...], vxt_ref[...])
    lam = _diff_lambda(lq_ref, lam_init)
    heads = []
    for h in range(DA_HEADS):
        o = acc_scr[2 * h] / l_scr[2 * h] - lam * (acc_scr[2 * h + 1] / l_scr[2 * h + 1])
        ms = jnp.mean(o * o, axis=0, keepdims=True)
        heads.append(o * lax.rsqrt(ms + NORM_EPS) * g_ref[...] * (1.0 - lam_init))
    o_ref[...] = jnp.concatenate(heads, axis=0).T.astype(BF16)


def diff_attention(lq, g_rows, qd, kd, vdt, *, n_batch, seq, n_ctx, lam_init, tq, tk, ahead=2):
    nq = seq // tq
    ctx0 = (n_batch * seq) // n_ctx
    n_hm = 2 * DA_HEADS
    return pl.pallas_call(
        functools.partial(_dattn_kernel, lam_init=lam_init, tk=tk, ahead=ahead),
        grid=(n_batch, nq),
        in_specs=[
            pl.BlockSpec((4, DA_QK_DIM), lambda b, i: (0, 0)),
            pl.BlockSpec((DA_V_DIM, tq), lambda b, i: (0, 0)),
            pl.BlockSpec((tq, DA_QK_COLS), lambda b, i: (b * nq + i, 0)),
            pl.BlockSpec((seq, DA_QK_COLS), lambda b, i: (b, 0)),
            pl.BlockSpec((DA_V_COLS, seq), lambda b, i: (0, b)),
            pl.BlockSpec((n_ctx, DA_QK_COLS), lambda b, i: (ctx0 + b, 0)),
            pl.BlockSpec((DA_V_COLS, n_ctx), lambda b, i: (0, ctx0 + b)),
        ],
        out_specs=pl.BlockSpec((tq, DA_V_COLS), lambda b, i: (b * nq + i, 0)),
        out_shape=jax.ShapeDtypeStruct((n_batch * seq, DA_V_COLS), BF16),
        scratch_shapes=[
            pltpu.VMEM((n_hm, tq, LANES), BF16),
            pltpu.VMEM((n_hm, 1, tq), F32),
            pltpu.VMEM((n_hm, 1, tq), F32),
            pltpu.VMEM((n_hm, DA_V_DIM, tq), F32),
        ],
        compiler_params=_cparams("arbitrary", "arbitrary"),
        name="diff_attention",
    )(lq, g_rows, qd, kd, vdt, kd, vdt)


def _ctx_attn_kernel(sink_ref, lq_ref, g_ref, qw_ref, kw_ref, vw_ref, qd_ref, kd_ref, vd_ref, ow_ref, od_ref,
                     *, lam_init):
    outs = []
    for hk in range(WA_KV_HEADS):
        sl = slice(hk * HEAD_DIM, (hk + 1) * HEAD_DIM)
        k, v = kw_ref[:, sl], vw_ref[:, sl]
        for g in range(WA_GROUP):
            h = hk * WA_GROUP + g
            outs.append(_softmax_sink_attend(qw_ref[:, h * HEAD_DIM:(h + 1) * HEAD_DIM], k, v, sink_ref[h], None))
    ow_ref[...] = jnp.concatenate(outs, axis=1).astype(BF16)
    per_blk = LANES // DA_QK_DIM
    acc, den = [], []
    for hm in range(2 * DA_HEADS):
        kb, vb = hm // per_blk, hm // 4
        s = lax.dot_general(_masked_q(qd_ref, hm), kd_ref[:, kb * LANES:(kb + 1) * LANES],
                            (((1,), (1,)), ((), ())), preferred_element_type=F32)
        p = jnp.exp2(s - jnp.max(s, axis=1, keepdims=True))
        den.append(jnp.sum(p, axis=1, keepdims=True))
        acc.append(jnp.dot(p.astype(BF16), vd_ref[:, vb * LANES:(vb + 1) * LANES], preferred_element_type=F32))
    lam = _diff_lambda(lq_ref, lam_init)
    od_ref[...] = _diff_finish(acc[0::2], den[0::2], acc[1::2], den[1::2], lam, g_ref[...], lam_init).astype(BF16)


def ctx_attention(sink, lq, g128, qw, kw, vw, qd, kd, vd, *, n_batch, seq, n_ctx, lam_init):
    ctx0 = (n_batch * seq) // n_ctx
    ctx = lambda b: (ctx0 + b, 0)
    out = lambda b: (b, 0)
    spec = lambda n, f: pl.BlockSpec((n_ctx, n), f)
    return pl.pallas_call(
        functools.partial(_ctx_attn_kernel, lam_init=lam_init),
        grid=(n_batch,),
        in_specs=[
            pl.BlockSpec(memory_space=pltpu.SMEM),
            pl.BlockSpec((4, DA_QK_DIM), lambda b: (0, 0)),
            pl.BlockSpec((1, LANES), lambda b: (0, 0)),
            spec(WA_Q_COLS, ctx), spec(WA_KV_COLS, ctx), spec(WA_KV_COLS, ctx),
            spec(DA_QK_COLS, ctx), spec(DA_QK_COLS, ctx), spec(DA_V_COLS, ctx),
        ],
        out_specs=[spec(WA_Q_COLS, out), spec(DA_V_COLS, out)],
        out_shape=[jax.ShapeDtypeStruct((n_batch * n_ctx, WA_Q_COLS), BF16),
                   jax.ShapeDtypeStruct((n_batch * n_ctx, DA_V_COLS), BF16)],
        compiler_params=_cparams("arbitrary"),
        name="ctx_attention",
    )(sink, lq, g128, qw, kw, vw, qd, kd, vd)


def _filter_kernel(feat_ref, w1_ref, b1_ref, fr_ref, w2_ref, b2_ref, w3_ref, dec_ref, o_ref, *, transposed):
    feat = feat_ref[...]
    fr = fr_ref[...]
    z = jnp.sin(fr * (jnp.dot(feat, w1_ref[...], precision=HI, preferred_element_type=F32) + b1_ref[...]))
    z = jnp.sin(fr * (jnp.dot(z, w2_ref[...], precision=HI, preferred_element_type=F32) + b2_ref[...]))
    h = jnp.dot(z, w3_ref[...], precision=HI, preferred_element_type=F32)
    h = h * jnp.exp(-feat[:, 0:1] * jnp.abs(dec_ref[...]))
    rows = lax.broadcasted_iota(jnp.int32, h.shape, 0) + pl.program_id(0) * h.shape[0]
    cols = lax.broadcasted_iota(jnp.int32, h.shape, 1)
    h = jnp.where((rows == 0) & (cols % (2 * HY_WIDTH) >= HY_WIDTH), 0.0, h)
    if transposed:
        for r in range(h.shape[0] // LANES):
            o_ref[r] = h[r * LANES:(r + 1) * LANES, :].T
    else:
        o_ref[...] = h


def hyena_filter_features(n):
    t = jnp.linspace(0.0, 1.0, n, dtype=F32)[:, None]
    bands = jnp.linspace(1e-4, HY_BANDS - 1, HY_BANDS, dtype=F32)
    phase = (2.0 * math.pi / n) * jnp.arange(n, dtype=F32)[:, None] * bands
    feat = jnp.concatenate([t, jnp.cos(phase), -jnp.sin(phase)], axis=-1)
    return jnp.pad(feat, ((0, 0), (0, LANES - HY_EMB_DIM)))


def hyena_filters(feat, w1p, b1, freq, w2, b2, w3, decay, *, transposed):
    n = feat.shape[0]
    tr = min(n, 512)
    ncol = w3.shape[1]
    fw = w2.shape[0]
    const = lambda i: (0, 0)
    if transposed:
        out_spec = pl.BlockSpec((tr // LANES, ncol, LANES), lambda i: (i, 0, 0))
        out_shape = jax.ShapeDtypeStruct((n // LANES, ncol, LANES), F32)
    else:
        out_spec = pl.BlockSpec((tr, ncol), lambda i: (i, 0))
        out_shape = jax.ShapeDtypeStruct((n, ncol), F32)
    return pl.pallas_call(
        functools.partial(_filter_kernel, transposed=transposed),
        grid=(n // tr,),
        in_specs=[
            pl.BlockSpec((tr, LANES), lambda i: (i, 0)),
            pl.BlockSpec((LANES, fw), const), pl.BlockSpec((1, fw), const), pl.BlockSpec((1, fw), const),
            pl.BlockSpec((fw, fw), const), pl.BlockSpec((1, fw), const),
            pl.BlockSpec((fw, ncol), const), pl.BlockSpec((1, ncol), const),
        ],
        out_specs=out_spec,
        out_shape=out_shape,
        compiler_params=_cparams("arbitrary"),
        name="hyena_filters",
    )(feat, w1p, b1, freq, w2, b2, w3, decay)


def dft_constants(seq):
    n = 2 * seq
    n1 = n // LANES
    a_rows = n1 // 2
    k1 = np.arange(n1, dtype=np.float64)[:, None]
    a = np.arange(a_rows, dtype=np.float64)[None, :]
    th = 2.0 * np.pi * k1 * a / n1
    c1, s1 = np.cos(th), np.sin(th)
    m1_data = np.block([[c1, s1], [-s1, c1]])
    m1_real = np.concatenate([c1, -s1], axis=0)
    m3 = np.block([[c1.T, -s1.T], [s1.T, c1.T]]) / n
    b = np.arange(LANES, dtype=np.float64)[None, :]
    psi = 2.0 * np.pi * k1 * b / n
    twc, tws = np.cos(psi), np.sin(psi)
    bb = np.arange(LANES, dtype=np.float64)
    phi = 2.0 * np.pi * np.outer(bb, bb) / LANES
    c2, s2 = np.cos(phi), np.sin(phi)
    w2f = np.block([[c2, -s2], [s2, c2]])
    w2i = np.block([[c2, s2], [-s2, c2]])
    f = lambda x: jnp.asarray(x, dtype=F32)
    return dict(m1_data=f(m1_data), m1_real=f(m1_real), m3=f(m3), twc=f(twc[:, None, :]), tws=f(tws[:, None, :]),
                w2f=f(w2f), w2i=f(w2i), n1=n1, a_rows=a_rows)


def _leftmm_kernel(m_ref, x_ref, o_ref):
    o_ref[...] = jnp.dot(m_ref[...], x_ref[...], precision=HI, preferred_element_type=F32)


def left_matmul(m, x, ct):
    r, k = m.shape
    cols = x.shape[1]
    return pl.pallas_call(
        _leftmm_kernel,
        grid=(cols // ct,),
        in_specs=[pl.BlockSpec((r, k), lambda j: (0, 0)), pl.BlockSpec((k, ct), lambda j: (0, j))],
        out_specs=pl.BlockSpec((r, ct), lambda j: (0, j)),
        out_shape=jax.ShapeDtypeStruct((r, cols), F32),
        compiler_params=_cparams("arbitrary"),
        name="dft_rows",
    )(m, x)


def _leftmm_gate_kernel(m_ref, z_ref, x_ref, zold_ref, bias_ref, o_ref):
    y = jnp.dot(m_ref[...], z_ref[...], precision=HI, preferred_element_type=F32)
    zold = zold_ref[...]
    o_ref[...] = x_ref[...] * (y + bias_ref[...] * zold)


def left_matmul_gate(m, z, xg, zold, bias_cols, ct):
    r, k = m.shape
    cols = z.shape[1]
    col = lambda j: (0, j)
    return pl.pallas_call(
        _leftmm_gate_kernel,
        grid=(cols // ct,),
        in_specs=[pl.BlockSpec((r, k), lambda j: (0, 0)), pl.BlockSpec((k, ct), col), pl.BlockSpec((r, ct), col),
                  pl.BlockSpec((r, ct), col), pl.BlockSpec((1, ct), col)],
        out_specs=pl.BlockSpec((r, ct), col),
        out_shape=jax.ShapeDtypeStruct((r, cols), F32),
        compiler_params=_cparams("arbitrary"),
        name="idft_rows_gate",
    )(m, z, xg, zold, bias_cols)


def _twiddle_fwd(yr, yi, c, s):
    return yr * c + yi * s, yi * c - yr * s


def _spectrum_kernel(yr_ref, yi_ref, twc_ref, tws_ref, w2f_ref, hr_ref, hi_ref, *, kb):
    for t in range(kb):
        ypr, ypi = _twiddle_fwd(yr_ref[t], yi_ref[t], twc_ref[t], tws_ref[t])
        x = jnp.dot(jnp.concatenate([ypr, ypi], axis=1), w2f_ref[...], precision=HI, preferred_element_type=F32)
        xr, xi = x[:, :LANES], x[:, LANES:]
        for o in range(HY_ORDER):
            f0 = slice((2 * o) * HY_WIDTH, (2 * o + 1) * HY_WIDTH)
            f1 = slice((2 * o + 1) * HY_WIDTH, (2 * o + 2) * HY_WIDTH)
            hr_ref[o, t] = xr[f0] + xr[f1]
            hi_ref[o, t] = xi[f0] - xi[f1]


def filter_spectrum(y, consts, kb):
    n1 = consts["n1"]
    nc = 2 * HY_ORDER * HY_WIDTH
    y4 = y.reshape(2, n1, nc, LANES)
    tw = pl.BlockSpec((kb, 1, LANES), lambda i: (i, 0, 0))
    out = pl.BlockSpec((HY_ORDER, kb, HY_WIDTH, LANES), lambda i: (0, i, 0, 0))
    shp = jax.ShapeDtypeStruct((HY_ORDER, n1, HY_WIDTH, LANES), F32)
    return pl.pallas_call(
        functools.partial(_spectrum_kernel, kb=kb),
        grid=(n1 // kb,),
        in_specs=[
            pl.BlockSpec((None, kb, nc, LANES), lambda i: (0, i, 0, 0)),
            pl.BlockSpec((None, kb, nc, LANES), lambda i: (1, i, 0, 0)),
            tw, tw,
            pl.BlockSpec((2 * LANES, 2 * LANES), lambda i: (0, 0)),
        ],
        out_specs=[out, out],
        out_shape=[shp, shp],
        compiler_params=_cparams("arbitrary"),
        name="filter_spectrum",
    )(y4, y4, consts["twc"], consts["tws"], consts["w2f"])


def _freq_kernel(yr_ref, yi_ref, twc_ref, tws_ref, w2f_ref, w2i_ref, hr_ref, hi_ref, zr_ref, zi_ref, *, kb):
    for t in range(kb):
        c, s = twc_ref[t], tws_ref[t]
        ypr, ypi = _twiddle_fwd(yr_ref[t], yi_ref[t], c, s)
        x = jnp.dot(jnp.concatenate([ypr, ypi], axis=1), w2f_ref[...], precision=HI, preferred_element_type=F32)
        xr, xi = x[:, :LANES], x[:, LANES:]
        hr, hi = hr_ref[t], hi_ref[t]
        gr = xr * hr - xi * hi
        gi = xr * hi + xi * hr
        z = jnp.dot(jnp.concatenate([gr, gi], axis=1), w2i_ref[...], precision=HI, preferred_element_type=F32)
        zr, zi = z[:, :LANES], z[:, LANES:]
        zr_ref[t] = zr * c - zi * s
        zi_ref[t] = zi * c + zr * s


def freq_multiply(y, hr, hi, order, consts, kb):
    n1 = consts["n1"]
    y4 = y.reshape(2, n1, HY_WIDTH, LANES)
    tw = pl.BlockSpec((kb, 1, LANES), lambda i: (i, 0, 0))
    blk = lambda p: pl.BlockSpec((None, kb, HY_WIDTH, LANES), lambda i: (p, i, 0, 0))
    wspec = pl.BlockSpec((2 * LANES, 2 * LANES), lambda i: (0, 0))
    out = pl.BlockSpec((kb, HY_WIDTH, LANES), lambda i: (i, 0, 0))
    shp = jax.ShapeDtypeStruct((n1, HY_WIDTH, LANES), F32)
    return pl.pallas_call(
        functools.partial(_freq_kernel, kb=kb),
        grid=(n1 // kb,),
        in_specs=[blk(0), blk(1), tw, tw, wspec, wspec, blk(order), blk(order)],
        out_specs=[out, out],
        out_shape=[shp, shp],
        compiler_params=_cparams("arbitrary"),
        name="freq_multiply",
    )(y4, y4, consts["twc"], consts["tws"], consts["w2f"], consts["w2i"], hr, hi)


def _short_conv_rows(u, prev_row, next_row, w_ref, b_ref):
    rows = lax.broadcasted_iota(jnp.int32, u.shape, 0)
    um1 = jnp.where(rows == 0, prev_row, pltpu.roll(u, 1, 0))
    up1 = jnp.where(rows == u.shape[0] - 1, next_row, pltpu.roll(u, u.shape[0] - 1, 0))
    return b_ref[...] + um1 * w_ref[0:1, :] + u * w_ref[1:2, :] + up1 * w_ref[2:3, :]


def _shortconv_t_kernel(u_ref, up_ref, un_ref, w_ref, b_ref, o_ref, *, n_tiles):
    i = pl.program_id(1)
    prev_row = jnp.where(i > 0, up_ref[SUBLANES - 1:SUBLANES, :], 0.0)
    next_row = jnp.where(i < n_tiles - 1, un_ref[0:1, :], 0.0)
    v = _short_conv_rows(u_ref[...], prev_row, next_row, w_ref, b_ref)
    for part in range(HY_ORDER + 1):
        for r in range(v.shape[0] // LANES):
            o_ref[part, r] = v[r * LANES:(r + 1) * LANES, part * HY_WIDTH:(part + 1) * HY_WIDTH].T


def short_conv_transposed(hy, conv_w, conv_b, *, n_batch, seq, tm):
    nt = seq // tm
    per8 = tm // SUBLANES
    a_rows = seq // LANES
    prev = lambda b, i: (jnp.maximum((b * nt + i) * per8 - 1, 0), 0)
    nxt = lambda b, i: (jnp.minimum((b * nt + i + 1) * per8, n_batch * nt * per8 - 1), 0)
    return pl.pallas_call(
        functools.partial(_shortconv_t_kernel, n_tiles=nt),
        grid=(n_batch, nt),
        in_specs=[
            pl.BlockSpec((tm, HY_COLS), lambda b, i: (b * nt + i, 0)),
            pl.BlockSpec((SUBLANES, HY_COLS), prev),
            pl.BlockSpec((SUBLANES, HY_COLS), nxt),
            pl.BlockSpec((3, HY_COLS), lambda b, i: (0, 0)),
            pl.BlockSpec((1, HY_COLS), lambda b, i: (0, 0)),
        ],
        out_specs=pl.BlockSpec((HY_ORDER + 1, None, tm // LANES, HY_WIDTH, LANES), lambda b, i: (0, b, i, 0, 0)),
        out_shape=jax.ShapeDtypeStruct((HY_ORDER + 1, n_batch, a_rows, HY_WIDTH, LANES), F32),
        compiler_params=_cparams("arbitrary", "arbitrary"),
        name="short_conv_transposed",
    )(hy, hy, hy, conv_w, conv_b)


def _untranspose_kernel(z_ref, o_ref):
    for r in range(z_ref.shape[0]):
        o_ref[r * LANES:(r + 1) * LANES, :] = z_ref[r].T.astype(BF16)


def untranspose(zt, *, n_batch, seq):
    a_rows = seq // LANES
    ta = min(SUBLANES, a_rows)
    nt = a_rows // ta
    return pl.pallas_call(
        _untranspose_kernel,
        grid=(n_batch, nt),
        in_specs=[pl.BlockSpec((None, ta, HY_WIDTH, LANES), lambda b, i: (b, i, 0, 0))],
        out_specs=pl.BlockSpec((ta * LANES, HY_WIDTH), lambda b, i: (b * nt + i, 0)),
        out_shape=jax.ShapeDtypeStruct((n_batch * seq, HY_WIDTH), BF16),
        compiler_params=_cparams("arbitrary", "arbitrary"),
        name="untranspose",
    )(zt)


def hyena_latent(hy, conv_w, conv_b, hy_bias, filt_t, consts, *, n_batch, seq):
    assert n_batch == 2, "the two batches are packed as real / imaginary parts of one DFT"
    n1, a_rows = consts["n1"], consts["a_rows"]
    cols = HY_WIDTH * LANES
    ct = min(cols, 2048)
    kb = min(n1, 4)
    parts = short_conv_transposed(hy, conv_w, conv_b, n_batch=n_batch, seq=seq, tm=min(seq, 1024))
    parts = parts.reshape(HY_ORDER + 1, n_batch * a_rows, cols)
    yf = left_matmul(consts["m1_real"], filt_t.reshape(a_rows, 2 * HY_ORDER * cols), ct)
    hr, hi = filter_spectrum(yf, consts, 1)
    z = parts[HY_ORDER]
    for o in range(HY_ORDER):
        y = left_matmul(consts["m1_data"], z, ct)
        zr, zi = freq_multiply(y, hr, hi, o, consts, kb)
        zcat = jnp.concatenate([zr.reshape(n1, cols), zi.reshape(n1, cols)], axis=0)
        bias_cols = jnp.repeat(hy_bias[o], LANES)[None, :]
        z = left_matmul_gate(consts["m3"], zcat, parts[o], z, bias_cols, ct)
    return untranspose(z.reshape(n_batch, a_rows, HY_WIDTH, LANES), n_batch=n_batch, seq=seq)


def _ctx_hyena_kernel(u_ref, w_ref, b_ref, h_ref, bias_ref, fc_ref, fs_ref, fct_ref, fst_ref, o_ref):
    u = u_ref[...]
    zero = jnp.zeros((1, u.shape[1]), F32)
    v = _short_conv_rows(u, zero, zero, w_ref, b_ref)
    z = v[:, HY_ORDER * HY_WIDTH:]
    mm = lambda a, b: jnp.dot(a, b, precision=HI, preferred_element_type=F32)
    fc, fs = fc_ref[...], fs_ref[...]
    inv_n = 1.0 / fc.shape[0]
    for o in range(HY_ORDER):
        h0 = h_ref[:, (2 * o) * HY_WIDTH:(2 * o + 1) * HY_WIDTH]
        h1 = h_ref[:, (2 * o + 1) * HY_WIDTH:(2 * o + 2) * HY_WIDTH]
        sr, si = mm(fc, h0 + h1), mm(fs, h1 - h0)
        zr, zi = mm(fc, z), -mm(fs, z)
        gr = zr * sr - zi * si
        gi = zr * si + zi * sr
        y = (mm(fct_ref[...], gr) - mm(fst_ref[...], gi)) * inv_n
        z = v[:, o * HY_WIDTH:(o + 1) * HY_WIDTH] * (y + bias_ref[o:o + 1, :] * z)
    o_ref[...] = z.astype(BF16)


def ctx_hyena(hy, conv_w, conv_b, filt, hy_bias, *, n_batch, seq, n_ctx):
    n = 2 * n_ctx
    k = np.arange(n, dtype=np.float64)[:, None]
    m = np.arange(n_ctx, dtype=np.float64)[None, :]
    ang = 2.0 * np.pi * k * m / n
    fc, fs = jnp.asarray(np.cos(ang), F32), jnp.asarray(np.sin(ang), F32)
    ctx0 = (n_batch * seq) // n_ctx
    const = lambda b: (0, 0)
    return pl.pallas_call(
        _ctx_hyena_kernel,
        grid=(n_batch,),
        in_specs=[
            pl.BlockSpec((n_ctx, HY_COLS), lambda b: (ctx0 + b, 0)),
            pl.BlockSpec((3, HY_COLS), const), pl.BlockSpec((1, HY_COLS), const),
            pl.BlockSpec((n_ctx, 2 * HY_ORDER * HY_WIDTH), const),
            pl.BlockSpec((HY_ORDER, HY_WIDTH), const),
            pl.BlockSpec((n, n_ctx), const), pl.BlockSpec((n, n_ctx), const),
            pl.BlockSpec((n_ctx, n), const), pl.BlockSpec((n_ctx, n), const),
        ],
        out_specs=pl.BlockSpec((n_ctx, HY_WIDTH), lambda b: (b, 0)),
        out_shape=jax.ShapeDtypeStruct((n_batch * n_ctx, HY_WIDTH), BF16),
        compiler_params=_cparams("arbitrary"),
        name="ctx_hyena",
    )(hy, conv_w, conv_b, filt, hy_bias, fc, fs, fc.T, fs.T)


def _outproj_kernel(x_ref, yh_ref, yw_ref, yd_ref, mod_ref, wo_ref, g_ref, wq_ref, xo_ref, h2_ref, qp_ref):
    y = jnp.dot(yh_ref[...], wo_ref[0:HY_WIDTH, :], preferred_element_type=F32)
    y = y + jnp.dot(yw_ref[...], wo_ref[HY_WIDTH:HY_WIDTH + WA_Q_COLS, :], preferred_element_type=F32)
    y = y + jnp.dot(yd_ref[...], wo_ref[HY_WIDTH + WA_Q_COLS:, :], preferred_element_type=F32)
    xn = x_ref[...] + mod_ref[2:3, :] * y
    xo_ref[...] = xn
    h2 = _rmsnorm_mod(xn, g_ref[...], mod_ref[3:4, :], mod_ref[4:5, :]).astype(BF16)
    h2_ref[...] = h2
    qp_ref[...] = jnp.dot(h2, wq_ref[...], preferred_element_type=F32)


def out_proj(x_all, y_hy, y_wa, y_da, mod, wo_bf, g2, wq_bf, *, n_rows, n_batch, seq, tm):
    d = x_all.shape[1]
    tpb = seq // tm
    grp = functools.partial(_group_of_tile, tiles_per_batch=tpb, n_batch=n_batch)
    row = lambda i: (i, 0)
    const = lambda i: (0, 0)
    mix = HY_WIDTH + WA_Q_COLS + DA_V_COLS
    return pl.pallas_call(
        _outproj_kernel,
        grid=(n_rows // tm,),
        in_specs=[
            pl.BlockSpec((tm, d), row), pl.BlockSpec((tm, HY_WIDTH), row), pl.BlockSpec((tm, WA_Q_COLS), row),
            pl.BlockSpec((tm, DA_V_COLS), row),
            pl.BlockSpec((None, N_MOD, d), lambda i: (grp(i), 0, 0)),
            pl.BlockSpec((mix, d), const), pl.BlockSpec((1, d), const), pl.BlockSpec((d, PEER_QCOLS), const),
        ],
        out_specs=[pl.BlockSpec((tm, d), row), pl.BlockSpec((tm, d), row), pl.BlockSpec((tm, PEER_QCOLS), row)],
        out_shape=[jax.ShapeDtypeStruct((n_rows, d), F32), jax.ShapeDtypeStruct((n_rows, d), BF16),
                   jax.ShapeDtypeStruct((n_rows, PEER_QCOLS), F32)],
        compiler_params=_cparams("arbitrary"),
        name="out_proj",
    )(x_all, y_hy, y_wa, y_da, mod, wo_bf, g2, wq_bf)


def _sort_pairs(n):
    pairs = []

    def merge(lo, cnt, r):
        step = r * 2
        if step < cnt:
            merge(lo, cnt, step)
            merge(lo + r, cnt, step)
            for i in range(lo + r, lo + cnt - r, step):
                pairs.append((i, i + r))
        else:
            pairs.append((lo, lo + r))

    def sort(lo, cnt):
        if cnt > 1:
            half = cnt // 2
            sort(lo, half)
            sort(lo + half, half)
            merge(lo, cnt, 1)

    sort(0, n)
    return pairs


def _sort_desc(vals):
    n = 1
    while n < len(vals):
        n *= 2
    v = list(vals) + [None] * (n - len(vals))
    for i, j in _sort_pairs(n):
        a, b = v[i], v[j]
        if b is None:
            continue
        if a is None:
            v[i], v[j] = b, None
        else:
            v[i], v[j] = jnp.maximum(a, b), jnp.minimum(a, b)
    return v[:len(vals)]


def _top16_rows(s):
    k = PEER_TOPK
    v = _sort_desc([s[SUBLANES * r:SUBLANES * (r + 1), :] for r in range(s.shape[0] // SUBLANES)])
    for shift in (4, 2, 1):
        other = [pltpu.roll(x, shift, 0) for x in v]
        v = [jnp.maximum(v[i], other[k - 1 - i]) for i in range(k)]
        d = k // 2
        while d >= 1:
            for i in range(k):
                if (i & d) == 0:
                    a, b = v[i], v[i + d]
                    v[i], v[i + d] = jnp.maximum(a, b), jnp.minimum(a, b)
            d //= 2
    return v


def _peer_topk_kernel(q_ref, keys_ref, thr_ref, e1_ref, e2_ref):
    k = PEER_TOPK
    tmk = q_ref.shape[0]
    sub = lax.broadcasted_iota(jnp.int32, (SUBLANES, tmk), 0)
    scores, tops = [], []
    for hp in range(2 * PEER_HEADS):
        s = lax.dot_general(keys_ref[hp], q_ref[:, hp * N_KEYS:(hp + 1) * N_KEYS], (((1,), (1,)), ((), ())),
                            precision=HI, preferred_element_type=F32)
        scores.append(s)
        top = _top16_rows(s)
        top.append(jnp.max(jnp.where(s < top[k - 1][0:1, :], s, NEG_BIG), axis=0, keepdims=True))
        tops.append(top)
    packed = []
    for p in range(2):
        lst = []
        for r in range(k + 1):
            acc = jnp.broadcast_to(tops[p][r], (SUBLANES, tmk))
            for h in range(1, PEER_HEADS):
                acc = jnp.where(sub == h, tops[2 * h + p][r], acc)
            lst.append(acc)
        packed.append(lst)
    cand = [packed[0][i] + packed[1][j] for i in range(k) for j in range(k) if (i + 1) * (j + 1) <= k]
    cs = _sort_desc(cand)
    c17 = jnp.maximum(cs[k], jnp.maximum(packed[0][k] + packed[1][0], packed[0][0] + packed[1][k]))
    tau = 0.5 * (cs[k - 1] + c17)
    zsum = jnp.zeros_like(tau)
    for r in range(k):
        zsum = zsum + jnp.exp(cs[r] - cs[0])
    inv_z = 1.0 / zsum
    for h in range(PEER_HEADS):
        s1, s2 = scores[2 * h], scores[2 * h + 1]
        row = lambda x: x[h:h + 1, :]
        b0, iz = row(packed[1][0]), row(inv_z)
        thr_ref[h] = jnp.exp(row(tau) - s1 - b0) * iz
        e1_ref[h] = 0.5 * jnp.exp(s1 - row(packed[0][0]))
        e2_ref[h] = jnp.exp(s2 - b0) * iz


def peer_topk(qp, keys, *, tmk):
    t = qp.shape[0]
    out = pl.BlockSpec((PEER_HEADS, N_KEYS, tmk), lambda i: (0, 0, i))
    shp = jax.ShapeDtypeStruct((PEER_HEADS, N_KEYS, t), F32)
    return pl.pallas_call(
        _peer_topk_kernel,
        grid=(t // tmk,),
        in_specs=[pl.BlockSpec((tmk, PEER_QCOLS), lambda i: (i, 0)),
                  pl.BlockSpec((2 * PEER_HEADS, N_KEYS, PEER_QDIM // 2), lambda i: (0, 0, 0))],
        out_specs=[out, out, out],
        out_shape=[shp, shp, shp],
        compiler_params=_cparams("arbitrary"),
        name="peer_topk",
    )(qp, keys)


def _peer_dense_kernel(x_ref, h_ref, mod_ref, u_ref, vt_ref, thr_ref, e1_ref, e2_ref, g_ref, o_ref,
                       a_scr, w_scr, acc_scr, *, final_norm, chunk):
    e = pl.program_id(1)

    @pl.when(e == 0)
    def _():
        acc_scr[...] = jnp.zeros(acc_scr.shape, F32)

    n_chunks = u_ref.shape[0] // chunk
    per_chunk = chunk // N_KEYS
    sqrt_half = math.sqrt(0.5)

    def pre_activations(p):
        rows = slice(p * chunk, (p + 1) * chunk)
        a_scr[rows, :] = lax.dot_general(u_ref[rows, :], h_ref[...], (((1,), (1,)), ((), ())),
                                         preferred_element_type=F32)

    pre_activations(0)
    for p in range(n_chunks):
        if p + 1 < n_chunks:
            pre_activations(p + 1)
        for ii in range(p * per_chunk, (p + 1) * per_chunk):
            rows = slice(ii * N_KEYS, (ii + 1) * N_KEYS)
            a = a_scr[rows, :]
            act = a * (1.0 + lax.erf(a * sqrt_half))
            gate = None
            for h in range(PEER_HEADS):
                e2 = e2_ref[h]
                term = jnp.where(e2 >= thr_ref[h, ii:ii + 1, :], e2 * e1_ref[h, ii:ii + 1, :], 0.0)
                gate = term if gate is None else gate + term
            w_scr[rows, :] = (gate * act).astype(BF16)
        rows = slice(p * chunk, (p + 1) * chunk)
        acc_scr[...] += jnp.dot(vt_ref[:, rows], w_scr[rows, :], preferred_element_type=F32)

    @pl.when(e == pl.num_programs(1) - 1)
    def _():
        xn = x_ref[...] + mod_ref[5:6, :] * acc_scr[...].T
        if final_norm:
            ms = jnp.mean(xn * xn, axis=-1, keepdims=True)
            xn = xn * lax.rsqrt(ms + NORM_EPS) * g_ref[...]
        o_ref[...] = xn


def peer_dense(x, h2, mod, u_bf, vt_bf, thr, e1, e2, g_final, *, n_batch, seq, tm, te, final_norm):
    n_rows, d = x.shape
    tpb = seq // tm
    grp = functools.partial(_group_of_tile, tiles_per_batch=tpb, n_batch=n_batch)
    n_i = te // N_KEYS
    tok = lambda i, e: (i, 0)
    rows_i = pl.BlockSpec((PEER_HEADS, n_i, tm), lambda i, e: (0, e, i))
    full = pl.BlockSpec((PEER_HEADS, N_KEYS, tm), lambda i, e: (0, 0, i))
    return pl.pallas_call(
        functools.partial(_peer_dense_kernel, final_norm=final_norm, chunk=min(te, 512)),
        grid=(n_rows // tm, N_EXPERTS // te),
        in_specs=[
            pl.BlockSpec((tm, d), tok), pl.BlockSpec((tm, d), tok),
            pl.BlockSpec((None, N_MOD, d), lambda i, e: (grp(i), 0, 0)),
            pl.BlockSpec((te, d), lambda i, e: (e, 0)),
            pl.BlockSpec((d, te), lambda i, e: (0, e)),
            rows_i, rows_i, full,
            pl.BlockSpec((1, d), lambda i, e: (0, 0)),
        ],
        out_specs=pl.BlockSpec((tm, d), tok),
        out_shape=jax.ShapeDtypeStruct((n_rows, d), F32),
        scratch_shapes=[pltpu.VMEM((te, tm), F32), pltpu.VMEM((te, tm), BF16), pltpu.VMEM((d, tm), F32)],
        compiler_params=_cparams("arbitrary", "arbitrary"),
        name="peer_dense",
    )(x, h2, mod, u_bf, vt_bf, thr, e1, e2, g_final)


def kernel(x, c, ctx, c_ctx, w_mod, b_mod, norm1_g, w_in, hy_conv_w, hy_conv_b, hy_w1, hy_b1, hy_freq, hy_w2,
           hy_b2, hy_w3, hy_decay, hy_bias, wa_sink, da_lq1, da_lk1, da_lq2, da_lk2, da_norm_g, w_out, norm2_g,
           peer_wq, peer_keys, peer_u, peer_v, final_g):
    n_batch, seq, d = x.shape
    n_ctx = ctx.shape[1]
    depth = w_in.shape[0]
    n_lat = n_batch * seq
    tm = 512 if seq % 512 == 0 else 256
    assert seq % tm == 0 and (n_batch * n_ctx) % tm == 0 and n_lat % n_ctx == 0 and n_batch + 1 <= SUBLANES

    cpad = jnp.zeros((SUBLANES, d), F32).at[:n_batch].set(c).at[n_batch].set(c_ctx)
    mod_all = mod_vectors(cpad, w_mod, b_mod).reshape(depth, SUBLANES, N_MOD, d)
    rope_wa = rope_tables(seq, HEAD_DIM, tm)
    rope_da = rope_tables(seq, DA_QK_DIM, tm)
    consts = dft_constants(seq)
    feat_lat = hyena_filter_features(seq)
    feat_ctx = hyena_filter_features(n_ctx)
    x_all = jnp.concatenate([x.reshape(n_lat, d), ctx.reshape(n_batch * n_ctx, d)], axis=0)

    out = None
    for li in range(depth):
        last = li == depth - 1
        lam_init = DA_LAMBDA_BASE - DA_LAMBDA_AMP * math.exp(-DA_LAMBDA_RATE * li)
        mod = mod_all[li]
        hy, qw, kw, vw, qd, kd, vd, vdt = proj_in(x_all, mod, norm1_g[li][None, :], w_in[li].astype(BF16), rope_wa,
                                             rope_da, n_batch=n_batch, seq=seq, tm=tm)
        w1p = jnp.pad(hy_w1[li], ((0, LANES - HY_EMB_DIM), (0, 0)))
        fargs = (w1p, hy_b1[li][None, :], hy_freq[li][None, :], hy_w2[li], hy_b2[li][None, :], hy_w3[li],
                 hy_decay[li].reshape(1, -1))
        conv_b = hy_conv_b[li][None, :]
        filt_t = hyena_filters(feat_lat, *fargs, transposed=True)
        y_hy = hyena_latent(hy, hy_conv_w[li], conv_b, hy_bias[li], filt_t, consts, n_batch=n_batch, seq=seq)
        y_wa = window_attention(wa_sink[li], qw, kw, vw, n_batch=n_batch, seq=seq, n_ctx=n_ctx)
        lq = jnp.stack([da_lq1[li], da_lk1[li], da_lq2[li], da_lk2[li]], axis=0)
        g128 = jnp.tile(da_norm_g[li], LANES // DA_V_DIM)[None, :]
        tq = min(seq, 512)
        g_rows = jnp.broadcast_to(da_norm_g[li][:, None], (DA_V_DIM, tq))
        y_da = diff_attention(lq, g_rows, qd, kd, vdt, n_batch=n_batch, seq=seq, n_ctx=n_ctx, lam_init=lam_init,
                              tq=tq, tk=min(seq, 256), ahead=3)
        if not last:
            filt_c = hyena_filters(feat_ctx, *fargs, transposed=False)
            yc_hy = ctx_hyena(hy, hy_conv_w[li], conv_b, filt_c, hy_bias[li], n_batch=n_batch, seq=seq, n_ctx=n_ctx)
            yc_wa, yc_da = ctx_attention(wa_sink[li], lq, g128, qw, kw, vw, qd, kd, vd, n_batch=n_batch, seq=seq,
                                         n_ctx=n_ctx, lam_init=lam_init)
            y_hy = jnp.concatenate([y_hy, yc_hy], axis=0)
            y_wa = jnp.concatenate([y_wa, yc_wa], axis=0)
            y_da = jnp.concatenate([y_da, yc_da], axis=0)
        n_rows = n_lat if last else x_all.shape[0]
        x_mid, h2, qp = out_proj(x_all, y_hy, y_wa, y_da, mod, w_out[li].astype(BF16), norm2_g[li][None, :],
                                 peer_wq[li].astype(BF16), n_rows=n_rows, n_batch=n_batch, seq=seq, tm=tm)
        keys = peer_keys[li].reshape(2 * PEER_HEADS, N_KEYS, PEER_QDIM // 2)
        thr, e1, e2 = peer_topk(qp, keys, tmk=256)
        x_all = peer_dense(x_mid, h2, mod, peer_u[li].astype(BF16), peer_v[li].T.astype(BF16), thr, e1, e2,
                           final_g[None, :], n_batch=n_batch, seq=seq, tm=tm, te=2048, final_norm=last)
        out = x_all
    return out.reshape(n_batch, seq, d)
```

```python
import functools
import math

import numpy as np
import jax
import jax.numpy as jnp
from jax import lax
from jax.experimental import pallas as pl
from jax.experimental.pallas import tpu as pltpu

F32 = jnp.float32
BF16 = jnp.bfloat16
HI = lax.Precision.HIGHEST

LANES = 128
SUBLANES = 8
BF16_ROWS = 16
VMEM_LIMIT_BYTES = 56 * 1024 * 1024

D_MODEL = 1024
N_MOD = 6
NORM_EPS = 1e-6
ROPE_BASE = 10000.0
GRID_W = 64
BLOCK = 128
HEAD_DIM = 64
HY_WIDTH = 256
HY_ORDER = 2
HY_BANDS = 16
HY_EMB_DIM = 1 + 2 * HY_BANDS
HY_FILTER_WIDTH = 64
HY_COLS = 3 * HY_WIDTH
WA_Q_HEADS = 8
WA_KV_HEADS = 2
WA_GROUP = 4
WINDOW = 128
WA_Q_COLS = WA_Q_HEADS * HEAD_DIM
WA_KV_COLS = WA_KV_HEADS * HEAD_DIM
WA_COLS = WA_Q_COLS + 2 * WA_KV_COLS
DA_HEADS = 4
DA_QK_DIM = 32
DA_V_DIM = 64
DA_QK_COLS = DA_HEADS * 2 * DA_QK_DIM
DA_V_COLS = DA_HEADS * DA_V_DIM
DA_COLS = 2 * DA_QK_COLS + DA_V_COLS
DA_LAMBDA_BASE = 0.8
DA_LAMBDA_AMP = 0.6
DA_LAMBDA_RATE = 0.3
IN_COLS = HY_COLS + WA_COLS + DA_COLS
PEER_HEADS = 8
N_KEYS = 128
N_EXPERTS = N_KEYS * N_KEYS
PEER_TOPK = 16
PEER_QDIM = 256
PEER_QCOLS = PEER_HEADS * PEER_QDIM
NEG_BIG = -1e30
LOG2E = math.log2(math.e)


def _cparams(*sem):
    return pltpu.CompilerParams(dimension_semantics=sem, vmem_limit_bytes=VMEM_LIMIT_BYTES)


def _rmsnorm_mod(x, g, shift, scale):
    ms = jnp.mean(x * x, axis=-1, keepdims=True)
    return (x * lax.rsqrt(ms + NORM_EPS) * g) * (1.0 + scale) + shift


def _mod_kernel(c_ref, w_ref, b_ref, o_ref):
    c = c_ref[...]
    act = c * jax.nn.sigmoid(c)
    o_ref[...] = jnp.dot(act, w_ref[...], precision=HI, preferred_element_type=F32) + b_ref[...]


def mod_vectors(cpad, w_mod, b_mod):
    n_layers, d, n = w_mod.shape
    tn = 1536
    return pl.pallas_call(
        _mod_kernel,
        grid=(n_layers, n // tn),
        in_specs=[
            pl.BlockSpec((SUBLANES, d), lambda l, j: (0, 0)),
            pl.BlockSpec((None, d, tn), lambda l, j: (l, 0, j)),
            pl.BlockSpec((None, 1, tn), lambda l, j: (l, 0, j)),
        ],
        out_specs=pl.BlockSpec((None, SUBLANES, tn), lambda l, j: (l, 0, j)),
        out_shape=jax.ShapeDtypeStruct((n_layers, SUBLANES, n), F32),
        compiler_params=_cparams("arbitrary", "arbitrary"),
        name="mod_vectors",
    )(cpad, w_mod, b_mod.reshape(n_layers, 1, n))


def _rope128(x, cos, sin_a, sin_b, m):
    return x * cos + pltpu.roll(x, LANES - m, 1) * sin_a + pltpu.roll(x, m, 1) * sin_b


def _proj_in_kernel(x_ref, mod_ref, g_ref, w_ref, cw_ref, saw_ref, sbw_ref, cd_ref, sad_ref, sbd_ref,
                    hy_ref, qw_ref, kw_ref, vw_ref, qd_ref, kd_ref, vd_ref, vdt_ref):
    h = _rmsnorm_mod(x_ref[...], g_ref[...], mod_ref[0:1, :], mod_ref[1:2, :]).astype(BF16)
    hy_ref[...] = jnp.dot(h, w_ref[:, 0:HY_COLS], preferred_element_type=F32)
    wa = jnp.dot(h, w_ref[:, HY_COLS:HY_COLS + WA_COLS], preferred_element_type=F32)
    da = jnp.dot(h, w_ref[:, HY_COLS + WA_COLS:IN_COLS], preferred_element_type=F32)
    cw, saw, sbw = cw_ref[...], saw_ref[...], sbw_ref[...]
    cd, sad, sbd = cd_ref[...], sad_ref[...], sbd_ref[...]
    wa_scale = HEAD_DIM ** -0.5
    da_scale = DA_QK_DIM ** -0.5 * LOG2E
    for kb in range(WA_Q_COLS // LANES):
        blk = wa[:, kb * LANES:(kb + 1) * LANES]
        qw_ref[:, kb * LANES:(kb + 1) * LANES] = (_rope128(blk, cw, saw, sbw, 16) * wa_scale).astype(BF16)
    kw_ref[...] = _rope128(wa[:, WA_Q_COLS:WA_Q_COLS + WA_KV_COLS], cw, saw, sbw, 16).astype(BF16)
    vw_ref[...] = wa[:, WA_Q_COLS + WA_KV_COLS:WA_COLS].astype(BF16)
    for kb in range(DA_QK_COLS // LANES):
        blk = da[:, kb * LANES:(kb + 1) * LANES]
        qd_ref[:, kb * LANES:(kb + 1) * LANES] = (_rope128(blk, cd, sad, sbd, 8) * da_scale).astype(BF16)
        blk = da[:, DA_QK_COLS + kb * LANES:DA_QK_COLS + (kb + 1) * LANES]
        kd_ref[:, kb * LANES:(kb + 1) * LANES] = _rope128(blk, cd, sad, sbd, 8).astype(BF16)
    vd = da[:, 2 * DA_QK_COLS:DA_COLS]
    vd_ref[...] = vd.astype(BF16)
    vdt_ref[...] = vd.T.astype(BF16)


def _group_of_tile(i, tiles_per_batch, n_batch):
    return jnp.minimum(i // tiles_per_batch, n_batch)


def proj_in(x_all, mod, g, w_bf, rope_wa, rope_da, *, n_batch, seq, tm):
    t_all, d = x_all.shape
    tpb = seq // tm
    grp = functools.partial(_group_of_tile, tiles_per_batch=tpb, n_batch=n_batch)
    pos = lambda i: (jnp.where(i < n_batch * tpb, i % tpb, tpb), 0)
    row = lambda i: (i, 0)
    const = lambda i: (0, 0)
    tab = pl.BlockSpec((tm, LANES), pos)
    outs = [(HY_COLS, F32), (WA_Q_COLS, BF16), (WA_KV_COLS, BF16), (WA_KV_COLS, BF16),
            (DA_QK_COLS, BF16), (DA_QK_COLS, BF16), (DA_V_COLS, BF16)]
    return pl.pallas_call(
        _proj_in_kernel,
        grid=(t_all // tm,),
        in_specs=[
            pl.BlockSpec((tm, d), row),
            pl.BlockSpec((None, N_MOD, d), lambda i: (grp(i), 0, 0)),
            pl.BlockSpec((1, d), const),
            pl.BlockSpec((d, IN_COLS), const),
            tab, tab, tab, tab, tab, tab,
        ],
        out_specs=[pl.BlockSpec((tm, n), row) for n, _ in outs] + [pl.BlockSpec((DA_V_COLS, tm), lambda i: (0, i))],
        out_shape=[jax.ShapeDtypeStruct((t_all, n), dt) for n, dt in outs]
        + [jax.ShapeDtypeStruct((DA_V_COLS, t_all), BF16)],
        compiler_params=_cparams("arbitrary"),
        name="proj_in",
    )(x_all, mod, g, w_bf, *rope_wa, *rope_da)


def rope_tables(seq, head_dim, pad_rows):
    half = head_dim // 2
    axis_dim = half
    m = axis_dim // 2
    t = np.arange(seq)
    rowcol = np.stack([t // GRID_W, t % GRID_W], axis=0).astype(np.float32)
    inv_freq = (ROPE_BASE ** (-jnp.arange(0, axis_dim, 2, dtype=F32) / axis_dim))
    ang = jnp.asarray(rowcol)[:, :, None] * inv_freq[None, None, :]
    cos, sin = jnp.cos(ang), jnp.sin(ang)
    lane = np.arange(LANES)
    dd = lane % head_dim
    axis = dd // half
    sub = dd % half
    idx = sub % m
    first = sub < m
    cos_t = cos[axis, :, idx].T
    sin_t = sin[axis, :, idx].T
    sin_a = jnp.where(first[None, :], -sin_t, 0.0)
    sin_b = jnp.where(first[None, :], 0.0, sin_t)
    pad1 = jnp.ones((pad_rows, LANES), F32)
    pad0 = jnp.zeros((pad_rows, LANES), F32)
    return (jnp.concatenate([cos_t, pad1], 0), jnp.concatenate([sin_a, pad0], 0),
            jnp.concatenate([sin_b, pad0], 0))


def _softmax_sink_attend(q, k, v, sink, valid):
    s = lax.dot_general(q, k, (((1,), (1,)), ((), ())), preferred_element_type=F32)
    if valid is not None:
        nm = valid.shape[1]
        s = jnp.concatenate([jnp.where(valid, s[:, :nm], NEG_BIG), s[:, nm:]], axis=1)
    m = jnp.maximum(jnp.max(s, axis=1, keepdims=True), sink)
    p = jnp.exp(s - m)
    den = jnp.sum(p, axis=1, keepdims=True) + jnp.exp(sink - m)
    return jnp.dot(p.astype(BF16), v, preferred_element_type=F32) / den


def _wattn_kernel(sink_ref, q_ref, kp_ref, kc_ref, kn_ref, vp_ref, vc_ref, vn_ref, kx_ref, vx_ref, o_ref,
                  *, n_blocks):
    n = pl.program_id(1)
    r = lax.broadcasted_iota(jnp.int32, (BLOCK, 3 * BLOCK), 0)
    j = lax.broadcasted_iota(jnp.int32, (BLOCK, 3 * BLOCK), 1)
    lo = jnp.where(n > 0, 0, BLOCK)
    hi = jnp.where(n < n_blocks - 1, 3 * BLOCK, 2 * BLOCK)
    valid = (j >= jnp.maximum(r, lo)) & (j <= r + 2 * WINDOW) & (j < hi)
    outs = []
    for hk in range(WA_KV_HEADS):
        sl = slice(hk * HEAD_DIM, (hk + 1) * HEAD_DIM)
        k = jnp.concatenate([kp_ref[:, sl], kc_ref[:, sl], kn_ref[:, sl], kx_ref[:, sl]], axis=0)
        v = jnp.concatenate([vp_ref[:, sl], vc_ref[:, sl], vn_ref[:, sl], vx_ref[:, sl]], axis=0)
        for g in range(WA_GROUP):
            h = hk * WA_GROUP + g
            q = q_ref[:, h * HEAD_DIM:(h + 1) * HEAD_DIM]
            outs.append(_softmax_sink_attend(q, k, v, sink_ref[h], valid))
    o_ref[...] = jnp.concatenate(outs, axis=1).astype(BF16)


def window_attention(sink, qw, kw, vw, *, n_batch, seq, n_ctx):
    nb = seq // BLOCK
    ctx0 = (n_batch * seq) // n_ctx
    cur = lambda b, n: (b * nb + n, 0)
    prev = lambda b, n: (b * nb + jnp.maximum(n - 1, 0), 0)
    nxt = lambda b, n: (b * nb + jnp.minimum(n + 1, nb - 1), 0)
    ctx = lambda b, n: (ctx0 + b, 0)
    kv = lambda f: pl.BlockSpec((BLOCK, WA_KV_COLS), f)
    return pl.pallas_call(
        functools.partial(_wattn_kernel, n_blocks=nb),
        grid=(n_batch, nb),
        in_specs=[
            pl.BlockSpec(memory_space=pltpu.SMEM),
            pl.BlockSpec((BLOCK, WA_Q_COLS), cur),
            kv(prev), kv(cur), kv(nxt), kv(prev), kv(cur), kv(nxt),
            pl.BlockSpec((n_ctx, WA_KV_COLS), ctx),
            pl.BlockSpec((n_ctx, WA_KV_COLS), ctx),
        ],
        out_specs=pl.BlockSpec((BLOCK, WA_Q_COLS), cur),
        out_shape=jax.ShapeDtypeStruct((n_batch * seq, WA_Q_COLS), BF16),
        compiler_params=_cparams("arbitrary", "arbitrary"),
        name="window_attention",
    )(sink, qw, kw, kw, kw, vw, vw, vw, kw, vw)


def _diff_lambda(lq_ref, lam_init):
    a = jnp.sum(lq_ref[0:1, :] * lq_ref[1:2, :], axis=1, keepdims=True)
    b = jnp.sum(lq_ref[2:3, :] * lq_ref[3:4, :], axis=1, keepdims=True)
    return jnp.exp(a) - jnp.exp(b) + lam_init


def _diff_finish(acc1, l1, acc2, l2, lam, g128, lam_init):
    lane = lax.broadcasted_iota(jnp.int32, acc1[0].shape, 1)
    lo = lane < DA_V_DIM
    blocks = []
    for vb in range(DA_HEADS // 2):
        o = []
        for h in (2 * vb, 2 * vb + 1):
            o.append(acc1[h] / l1[h] - lam * (acc2[h] / l2[h]))
        blk = jnp.where(lo, o[0], o[1])
        sq = blk * blk
        s_lo = jnp.sum(jnp.where(lo, sq, 0.0), axis=1, keepdims=True)
        s_hi = jnp.sum(jnp.where(lo, 0.0, sq), axis=1, keepdims=True)
        ms = jnp.where(lo, s_lo, s_hi) * (1.0 / DA_V_DIM)
        blocks.append(blk * lax.rsqrt(ms + NORM_EPS) * g128 * (1.0 - lam_init))
    return jnp.concatenate(blocks, axis=1)


def _masked_q(q_ref, hm):
    kb, sub = divmod(hm, LANES // DA_QK_DIM)
    blk = q_ref[:, kb * LANES:(kb + 1) * LANES]
    lane = lax.broadcasted_iota(jnp.int32, blk.shape, 1)
    keep = (lane >= sub * DA_QK_DIM) & (lane < (sub + 1) * DA_QK_DIM)
    return jnp.where(keep, blk, jnp.zeros_like(blk))


def _dattn_kernel(lq_ref, g_ref, q_ref, k_ref, vt_ref, kx_ref, vxt_ref, o_ref, qm_scr, m_scr, l_scr, acc_scr,
                  *, lam_init, tk, ahead, unroll):
    n_hm = 2 * DA_HEADS
    per_blk = LANES // DA_QK_DIM
    for hm in range(n_hm):
        qm_scr[hm] = _masked_q(q_ref, hm)
    m_scr[...] = jnp.full(m_scr.shape, NEG_BIG, F32)
    l_scr[...] = jnp.zeros(l_scr.shape, F32)
    acc_scr[...] = jnp.zeros(acc_scr.shape, F32)

    def update(chunks):
        chains = [(c, hm) for c in range(len(chunks)) for hm in range(n_hm)]

        def scores(chain):
            c, hm = chain
            kb = hm // per_blk
            return lax.dot_general(chunks[c][0][:, kb * LANES:(kb + 1) * LANES], qm_scr[hm],
                                   (((1,), (1,)), ((), ())), preferred_element_type=F32)

        sts = [scores(ch) for ch in chains[:ahead]]
        for n, (c, hm) in enumerate(chains):
            h = hm // 2
            vtc = chunks[c][1]
            if n + ahead < len(chains):
                sts.append(scores(chains[n + ahead]))
            st = sts[n]
            sts[n] = None
            m_old = m_scr[hm]
            m_new = jnp.maximum(m_old, jnp.max(st, axis=0, keepdims=True))
            alpha = jnp.exp2(m_old - m_new)
            pt = jnp.exp2(st - m_new)
            l_scr[hm] = alpha * l_scr[hm] + jnp.sum(pt, axis=0, keepdims=True)
            acc_scr[hm] = alpha * acc_scr[hm] + jnp.dot(vtc[h * DA_V_DIM:(h + 1) * DA_V_DIM, :], pt.astype(BF16),
                                                        preferred_element_type=F32)
            m_scr[hm] = m_new

    def body(j, carry):
        chunks = []
        for c in range(unroll):
            start = pl.multiple_of((j * unroll + c) * tk, tk)
            chunks.append((k_ref[pl.ds(start, tk), :], vt_ref[:, pl.ds(start, tk)]))
        update(chunks)
        return carry

    lax.fori_loop(0, k_ref.shape[0] // (tk * unroll), body, 0)
    n_x = kx_ref.shape[0] // tk
    update([(kx_ref[c * tk:(c + 1) * tk, :], vxt_ref[:, c * tk:(c + 1) * tk]) for c in range(n_x)])
    lam = _diff_lambda(lq_ref, lam_init)
    heads = []
    for h in range(DA_HEADS):
        o = acc_scr[2 * h] / l_scr[2 * h] - lam * (acc_scr[2 * h + 1] / l_scr[2 * h + 1])
        ms = jnp.mean(o * o, axis=0, keepdims=True)
        heads.append(o * lax.rsqrt(ms + NORM_EPS) * g_ref[...] * (1.0 - lam_init))
    o_ref[...] = jnp.concatenate(heads, axis=0).T.astype(BF16)


def diff_attention(lq, g_rows, qd, kd, vdt, *, n_batch, seq, n_ctx, lam_init, tq, tk, ahead, unroll):
    nq = seq // tq
    ctx0 = (n_batch * seq) // n_ctx
    n_hm = 2 * DA_HEADS
    return pl.pallas_call(
        functools.partial(_dattn_kernel, lam_init=lam_init, tk=tk, ahead=ahead, unroll=unroll),
        grid=(n_batch, nq),
        in_specs=[
            pl.BlockSpec((4, DA_QK_DIM), lambda b, i: (0, 0)),
            pl.BlockSpec((DA_V_DIM, tq), lambda b, i: (0, 0)),
            pl.BlockSpec((tq, DA_QK_COLS), lambda b, i: (b * nq + i, 0)),
            pl.BlockSpec((seq, DA_QK_COLS), lambda b, i: (b, 0)),
            pl.BlockSpec((DA_V_COLS, seq), lambda b, i: (0, b)),
            pl.BlockSpec((n_ctx, DA_QK_COLS), lambda b, i: (ctx0 + b, 0)),
            pl.BlockSpec((DA_V_COLS, n_ctx), lambda b, i: (0, ctx0 + b)),
        ],
        out_specs=pl.BlockSpec((tq, DA_V_COLS), lambda b, i: (b * nq + i, 0)),
        out_shape=jax.ShapeDtypeStruct((n_batch * seq, DA_V_COLS), BF16),
        scratch_shapes=[
            pltpu.VMEM((n_hm, tq, LANES), BF16),
            pltpu.VMEM((n_hm, 1, tq), F32),
            pltpu.VMEM((n_hm, 1, tq), F32),
            pltpu.VMEM((n_hm, DA_V_DIM, tq), F32),
        ],
        compiler_params=_cparams("arbitrary", "arbitrary"),
        name="diff_attention",
    )(lq, g_rows, qd, kd, vdt, kd, vdt)


def _ctx_attn_kernel(sink_ref, lq_ref, g_ref, qw_ref, kw_ref, vw_ref, qd_ref, kd_ref, vd_ref, ow_ref, od_ref,
                     *, lam_init):
    outs = []
    for hk in range(WA_KV_HEADS):
        sl = slice(hk * HEAD_DIM, (hk + 1) * HEAD_DIM)
        k, v = kw_ref[:, sl], vw_ref[:, sl]
        for g in range(WA_GROUP):
            h = hk * WA_GROUP + g
            outs.append(_softmax_sink_attend(qw_ref[:, h * HEAD_DIM:(h + 1) * HEAD_DIM], k, v, sink_ref[h], None))
    ow_ref[...] = jnp.concatenate(outs, axis=1).astype(BF16)
    per_blk = LANES // DA_QK_DIM
    acc, den = [], []
    for hm in range(2 * DA_HEADS):
        kb, vb = hm // per_blk, hm // 4
        s = lax.dot_general(_masked_q(qd_ref, hm), kd_ref[:, kb * LANES:(kb + 1) * LANES],
                            (((1,), (1,)), ((), ())), preferred_element_type=F32)
        p = jnp.exp2(s - jnp.max(s, axis=1, keepdims=True))
        den.append(jnp.sum(p, axis=1, keepdims=True))
        acc.append(jnp.dot(p.astype(BF16), vd_ref[:, vb * LANES:(vb + 1) * LANES], preferred_element_type=F32))
    lam = _diff_lambda(lq_ref, lam_init)
    od_ref[...] = _diff_finish(acc[0::2], den[0::2], acc[1::2], den[1::2], lam, g_ref[...], lam_init).astype(BF16)


def ctx_attention(sink, lq, g128, qw, kw, vw, qd, kd, vd, *, n_batch, seq, n_ctx, lam_init):
    ctx0 = (n_batch * seq) // n_ctx
    ctx = lambda b: (ctx0 + b, 0)
    out = lambda b: (b, 0)
    spec = lambda n, f: pl.BlockSpec((n_ctx, n), f)
    return pl.pallas_call(
        functools.partial(_ctx_attn_kernel, lam_init=lam_init),
        grid=(n_batch,),
        in_specs=[
            pl.BlockSpec(memory_space=pltpu.SMEM),
            pl.BlockSpec((4, DA_QK_DIM), lambda b: (0, 0)),
            pl.BlockSpec((1, LANES), lambda b: (0, 0)),
            spec(WA_Q_COLS, ctx), spec(WA_KV_COLS, ctx), spec(WA_KV_COLS, ctx),
            spec(DA_QK_COLS, ctx), spec(DA_QK_COLS, ctx), spec(DA_V_COLS, ctx),
        ],
        out_specs=[spec(WA_Q_COLS, out), spec(DA_V_COLS, out)],
        out_shape=[jax.ShapeDtypeStruct((n_batch * n_ctx, WA_Q_COLS), BF16),
                   jax.ShapeDtypeStruct((n_batch * n_ctx, DA_V_COLS), BF16)],
        compiler_params=_cparams("arbitrary"),
        name="ctx_attention",
    )(sink, lq, g128, qw, kw, vw, qd, kd, vd)


def _filter_kernel(feat_ref, w1_ref, b1_ref, fr_ref, w2_ref, b2_ref, w3_ref, dec_ref, o_ref, *, transposed):
    feat = feat_ref[...]
    fr = fr_ref[...]
    z = jnp.sin(fr * (jnp.dot(feat, w1_ref[...], precision=HI, preferred_element_type=F32) + b1_ref[...]))
    z = jnp.sin(fr * (jnp.dot(z, w2_ref[...], precision=HI, preferred_element_type=F32) + b2_ref[...]))
    h = jnp.dot(z, w3_ref[...], precision=HI, preferred_element_type=F32)
    h = h * jnp.exp(-feat[:, 0:1] * jnp.abs(dec_ref[...]))
    rows = lax.broadcasted_iota(jnp.int32, h.shape, 0) + pl.program_id(0) * h.shape[0]
    cols = lax.broadcasted_iota(jnp.int32, h.shape, 1)
    h = jnp.where((rows == 0) & (cols % (2 * HY_WIDTH) >= HY_WIDTH), 0.0, h)
    if transposed:
        for r in range(h.shape[0] // LANES):
            o_ref[r] = h[r * LANES:(r + 1) * LANES, :].T
    else:
        o_ref[...] = h


def hyena_filter_features(n):
    t = jnp.linspace(0.0, 1.0, n, dtype=F32)[:, None]
    bands = jnp.linspace(1e-4, HY_BANDS - 1, HY_BANDS, dtype=F32)
    phase = (2.0 * math.pi / n) * jnp.arange(n, dtype=F32)[:, None] * bands
    feat = jnp.concatenate([t, jnp.cos(phase), -jnp.sin(phase)], axis=-1)
    return jnp.pad(feat, ((0, 0), (0, LANES - HY_EMB_DIM)))


def hyena_filters(feat, w1p, b1, freq, w2, b2, w3, decay, *, transposed):
    n = feat.shape[0]
    tr = min(n, 512)
    ncol = w3.shape[1]
    fw = w2.shape[0]
    const = lambda i: (0, 0)
    if transposed:
        out_spec = pl.BlockSpec((tr // LANES, ncol, LANES), lambda i: (i, 0, 0))
        out_shape = jax.ShapeDtypeStruct((n // LANES, ncol, LANES), F32)
    else:
        out_spec = pl.BlockSpec((tr, ncol), lambda i: (i, 0))
        out_shape = jax.ShapeDtypeStruct((n, ncol), F32)
    return pl.pallas_call(
        functools.partial(_filter_kernel, transposed=transposed),
        grid=(n // tr,),
        in_specs=[
            pl.BlockSpec((tr, LANES), lambda i: (i, 0)),
            pl.BlockSpec((LANES, fw), const), pl.BlockSpec((1, fw), const), pl.BlockSpec((1, fw), const),
            pl.BlockSpec((fw, fw), const), pl.BlockSpec((1, fw), const),
            pl.BlockSpec((fw, ncol), const), pl.BlockSpec((1, ncol), const),
        ],
        out_specs=out_spec,
        out_shape=out_shape,
        compiler_params=_cparams("arbitrary"),
        name="hyena_filters",
    )(feat, w1p, b1, freq, w2, b2, w3, decay)


def dft_constants(seq):
    n = 2 * seq
    n1 = n // LANES
    a_rows = n1 // 2
    k1 = np.arange(n1, dtype=np.float64)[:, None]
    a = np.arange(a_rows, dtype=np.float64)[None, :]
    th = 2.0 * np.pi * k1 * a / n1
    c1, s1 = np.cos(th), np.sin(th)
    m1_data = np.block([[c1, s1], [-s1, c1]])
    m1_real = np.concatenate([c1, -s1], axis=0)
    m3 = np.block([[c1.T, -s1.T], [s1.T, c1.T]]) / n
    b = np.arange(LANES, dtype=np.float64)[None, :]
    psi = 2.0 * np.pi * k1 * b / n
    twc, tws = np.cos(psi), np.sin(psi)
    bb = np.arange(LANES, dtype=np.float64)
    phi = 2.0 * np.pi * np.outer(bb, bb) / LANES
    c2, s2 = np.cos(phi), np.sin(phi)
    w2f = np.block([[c2, -s2], [s2, c2]])
    w2i = np.block([[c2, s2], [-s2, c2]])
    f = lambda x: jnp.asarray(x, dtype=F32)
    return dict(m1_data=f(m1_data), m1_real=f(m1_real), m3=f(m3), twc=f(twc[:, None, :]), tws=f(tws[:, None, :]),
                w2f=f(w2f), w2i=f(w2i), n1=n1, a_rows=a_rows)


def _leftmm_kernel(m_ref, x_ref, o_ref):
    o_ref[...] = jnp.dot(m_ref[...], x_ref[...], precision=HI, preferred_element_type=F32)


def left_matmul(m, x, ct):
    r, k = m.shape
    cols = x.shape[1]
    return pl.pallas_call(
        _leftmm_kernel,
        grid=(cols // ct,),
        in_specs=[pl.BlockSpec((r, k), lambda j: (0, 0)), pl.BlockSpec((k, ct), lambda j: (0, j))],
        out_specs=pl.BlockSpec((r, ct), lambda j: (0, j)),
        out_shape=jax.ShapeDtypeStruct((r, cols), F32),
        compiler_params=_cparams("arbitrary"),
        name="dft_rows",
    )(m, x)


def _leftmm_gate_kernel(m_ref, z_ref, x_ref, zold_ref, bias_ref, o_ref):
    y = jnp.dot(m_ref[...], z_ref[...], precision=HI, preferred_element_type=F32)
    zold = zold_ref[...]
    o_ref[...] = x_ref[...] * (y + bias_ref[...] * zold)


def left_matmul_gate(m, z, xg, zold, bias_cols, ct):
    r, k = m.shape
    cols = z.shape[1]
    col = lambda j: (0, j)
    return pl.pallas_call(
        _leftmm_gate_kernel,
        grid=(cols // ct,),
        in_specs=[pl.BlockSpec((r, k), lambda j: (0, 0)), pl.BlockSpec((k, ct), col), pl.BlockSpec((r, ct), col),
                  pl.BlockSpec((r, ct), col), pl.BlockSpec((1, ct), col)],
        out_specs=pl.BlockSpec((r, ct), col),
        out_shape=jax.ShapeDtypeStruct((r, cols), F32),
        compiler_params=_cparams("arbitrary"),
        name="idft_rows_gate",
    )(m, z, xg, zold, bias_cols)


def _twiddle_fwd(yr, yi, c, s):
    return yr * c + yi * s, yi * c - yr * s


def _spectrum_kernel(yr_ref, yi_ref, twc_ref, tws_ref, w2f_ref, hr_ref, hi_ref, *, kb):
    for t in range(kb):
        ypr, ypi = _twiddle_fwd(yr_ref[t], yi_ref[t], twc_ref[t], tws_ref[t])
        x = jnp.dot(jnp.concatenate([ypr, ypi], axis=1), w2f_ref[...], precision=HI, preferred_element_type=F32)
        xr, xi = x[:, :LANES], x[:, LANES:]
        for o in range(HY_ORDER):
            f0 = slice((2 * o) * HY_WIDTH, (2 * o + 1) * HY_WIDTH)
            f1 = slice((2 * o + 1) * HY_WIDTH, (2 * o + 2) * HY_WIDTH)
            hr_ref[o, t] = xr[f0] + xr[f1]
            hi_ref[o, t] = xi[f0] - xi[f1]


def filter_spectrum(y, consts, kb):
    n1 = consts["n1"]
    nc = 2 * HY_ORDER * HY_WIDTH
    y4 = y.reshape(2, n1, nc, LANES)
    tw = pl.BlockSpec((kb, 1, LANES), lambda i: (i, 0, 0))
    out = pl.BlockSpec((HY_ORDER, kb, HY_WIDTH, LANES), lambda i: (0, i, 0, 0))
    shp = jax.ShapeDtypeStruct((HY_ORDER, n1, HY_WIDTH, LANES), F32)
    return pl.pallas_call(
        functools.partial(_spectrum_kernel, kb=kb),
        grid=(n1 // kb,),
        in_specs=[
            pl.BlockSpec((None, kb, nc, LANES), lambda i: (0, i, 0, 0)),
            pl.BlockSpec((None, kb, nc, LANES), lambda i: (1, i, 0, 0)),
            tw, tw,
            pl.BlockSpec((2 * LANES, 2 * LANES), lambda i: (0, 0)),
        ],
        out_specs=[out, out],
        out_shape=[shp, shp],
        compiler_params=_cparams("arbitrary"),
        name="filter_spectrum",
    )(y4, y4, consts["twc"], consts["tws"], consts["w2f"])


def _freq_kernel(yr_ref, yi_ref, twc_ref, tws_ref, w2f_ref, w2i_ref, hr_ref, hi_ref, zr_ref, zi_ref, *, kb):
    for t in range(kb):
        c, s = twc_ref[t], tws_ref[t]
        ypr, ypi = _twiddle_fwd(yr_ref[t], yi_ref[t], c, s)
        x = jnp.dot(jnp.concatenate([ypr, ypi], axis=1), w2f_ref[...], precision=HI, preferred_element_type=F32)
        xr, xi = x[:, :LANES], x[:, LANES:]
        hr, hi = hr_ref[t], hi_ref[t]
        gr = xr * hr - xi * hi
        gi = xr * hi + xi * hr
        z = jnp.dot(jnp.concatenate([gr, gi], axis=1), w2i_ref[...], precision=HI, preferred_element_type=F32)
        zr, zi = z[:, :LANES], z[:, LANES:]
        zr_ref[t] = zr * c - zi * s
        zi_ref[t] = zi * c + zr * s


def freq_multiply(y, hr, hi, order, consts, kb):
    n1 = consts["n1"]
    y4 = y.reshape(2, n1, HY_WIDTH, LANES)
    tw = pl.BlockSpec((kb, 1, LANES), lambda i: (i, 0, 0))
    blk = lambda p: pl.BlockSpec((None, kb, HY_WIDTH, LANES), lambda i: (p, i, 0, 0))
    wspec = pl.BlockSpec((2 * LANES, 2 * LANES), lambda i: (0, 0))
    out = pl.BlockSpec((kb, HY_WIDTH, LANES), lambda i: (i, 0, 0))
    shp = jax.ShapeDtypeStruct((n1, HY_WIDTH, LANES), F32)
    return pl.pallas_call(
        functools.partial(_freq_kernel, kb=kb),
        grid=(n1 // kb,),
        in_specs=[blk(0), blk(1), tw, tw, wspec, wspec, blk(order), blk(order)],
        out_specs=[out, out],
        out_shape=[shp, shp],
        compiler_params=_cparams("arbitrary"),
        name="freq_multiply",
    )(y4, y4, consts["twc"], consts["tws"], consts["w2f"], consts["w2i"], hr, hi)


def _short_conv_rows(u, prev_row, next_row, w_ref, b_ref):
    rows = lax.broadcasted_iota(jnp.int32, u.shape, 0)
    um1 = jnp.where(rows == 0, prev_row, pltpu.roll(u, 1, 0))
    up1 = jnp.where(rows == u.shape[0] - 1, next_row, pltpu.roll(u, u.shape[0] - 1, 0))
    return b_ref[...] + um1 * w_ref[0:1, :] + u * w_ref[1:2, :] + up1 * w_ref[2:3, :]


def _shortconv_t_kernel(u_ref, up_ref, un_ref, w_ref, b_ref, o_ref, *, n_tiles):
    i = pl.program_id(1)
    prev_row = jnp.where(i > 0, up_ref[SUBLANES - 1:SUBLANES, :], 0.0)
    next_row = jnp.where(i < n_tiles - 1, un_ref[0:1, :], 0.0)
    v = _short_conv_rows(u_ref[...], prev_row, next_row, w_ref, b_ref)
    for part in range(HY_ORDER + 1):
        for r in range(v.shape[0] // LANES):
            o_ref[part, r] = v[r * LANES:(r + 1) * LANES, part * HY_WIDTH:(part + 1) * HY_WIDTH].T


def short_conv_transposed(hy, conv_w, conv_b, *, n_batch, seq, tm):
    nt = seq // tm
    per8 = tm // SUBLANES
    a_rows = seq // LANES
    prev = lambda b, i: (jnp.maximum((b * nt + i) * per8 - 1, 0), 0)
    nxt = lambda b, i: (jnp.minimum((b * nt + i + 1) * per8, n_batch * nt * per8 - 1), 0)
    return pl.pallas_call(
        functools.partial(_shortconv_t_kernel, n_tiles=nt),
        grid=(n_batch, nt),
        in_specs=[
            pl.BlockSpec((tm, HY_COLS), lambda b, i: (b * nt + i, 0)),
            pl.BlockSpec((SUBLANES, HY_COLS), prev),
            pl.BlockSpec((SUBLANES, HY_COLS), nxt),
            pl.BlockSpec((3, HY_COLS), lambda b, i: (0, 0)),
            pl.BlockSpec((1, HY_COLS), lambda b, i: (0, 0)),
        ],
        out_specs=pl.BlockSpec((HY_ORDER + 1, None, tm // LANES, HY_WIDTH, LANES), lambda b, i: (0, b, i, 0, 0)),
        out_shape=jax.ShapeDtypeStruct((HY_ORDER + 1, n_batch, a_rows, HY_WIDTH, LANES), F32),
        compiler_params=_cparams("arbitrary", "arbitrary"),
        name="short_conv_transposed",
    )(hy, hy, hy, conv_w, conv_b)


def _untranspose_kernel(z_ref, o_ref):
    for r in range(z_ref.shape[0]):
        o_ref[r * LANES:(r + 1) * LANES, :] = z_ref[r].T.astype(BF16)


def untranspose(zt, *, n_batch, seq):
    a_rows = seq // LANES
    ta = min(SUBLANES, a_rows)
    nt = a_rows // ta
    return pl.pallas_call(
        _untranspose_kernel,
        grid=(n_batch, nt),
        in_specs=[pl.BlockSpec((None, ta, HY_WIDTH, LANES), lambda b, i: (b, i, 0, 0))],
        out_specs=pl.BlockSpec((ta * LANES, HY_WIDTH), lambda b, i: (b * nt + i, 0)),
        out_shape=jax.ShapeDtypeStruct((n_batch * seq, HY_WIDTH), BF16),
        compiler_params=_cparams("arbitrary", "arbitrary"),
        name="untranspose",
    )(zt)


def hyena_latent(hy, conv_w, conv_b, hy_bias, filt_t, consts, *, n_batch, seq):
    assert n_batch == 2, "the two batches are packed as real / imaginary parts of one DFT"
    n1, a_rows = consts["n1"], consts["a_rows"]
    cols = HY_WIDTH * LANES
    ct = min(cols, 2048)
    kb = min(n1, 4)
    parts = short_conv_transposed(hy, conv_w, conv_b, n_batch=n_batch, seq=seq, tm=min(seq, 1024))
    parts = parts.reshape(HY_ORDER + 1, n_batch * a_rows, cols)
    yf = left_matmul(consts["m1_real"], filt_t.reshape(a_rows, 2 * HY_ORDER * cols), ct)
    hr, hi = filter_spectrum(yf, consts, 1)
    z = parts[HY_ORDER]
    for o in range(HY_ORDER):
        y = left_matmul(consts["m1_data"], z, ct)
        zr, zi = freq_multiply(y, hr, hi, o, consts, kb)
        zcat = jnp.concatenate([zr.reshape(n1, cols), zi.reshape(n1, cols)], axis=0)
        bias_cols = jnp.repeat(hy_bias[o], LANES)[None, :]
        z = left_matmul_gate(consts["m3"], zcat, parts[o], z, bias_cols, ct)
    return untranspose(z.reshape(n_batch, a_rows, HY_WIDTH, LANES), n_batch=n_batch, seq=seq)


def _ctx_hyena_kernel(u_ref, w_ref, b_ref, h_ref, bias_ref, fc_ref, fs_ref, fct_ref, fst_ref, o_ref):
    u = u_ref[...]
    zero = jnp.zeros((1, u.shape[1]), F32)
    v = _short_conv_rows(u, zero, zero, w_ref, b_ref)
    z = v[:, HY_ORDER * HY_WIDTH:]
    mm = lambda a, b: jnp.dot(a, b, precision=HI, preferred_element_type=F32)
    fc, fs = fc_ref[...], fs_ref[...]
    inv_n = 1.0 / fc.shape[0]
    for o in range(HY_ORDER):
        h0 = h_ref[:, (2 * o) * HY_WIDTH:(2 * o + 1) * HY_WIDTH]
        h1 = h_ref[:, (2 * o + 1) * HY_WIDTH:(2 * o + 2) * HY_WIDTH]
        sr, si = mm(fc, h0 + h1), mm(fs, h1 - h0)
        zr, zi = mm(fc, z), -mm(fs, z)
        gr = zr * sr - zi * si
        gi = zr * si + zi * sr
        y = (mm(fct_ref[...], gr) - mm(fst_ref[...], gi)) * inv_n
        z = v[:, o * HY_WIDTH:(o + 1) * HY_WIDTH] * (y + bias_ref[o:o + 1, :] * z)
    o_ref[...] = z.astype(BF16)


def ctx_hyena(hy, conv_w, conv_b, filt, hy_bias, *, n_batch, seq, n_ctx):
    n = 2 * n_ctx
    k = np.arange(n, dtype=np.float64)[:, None]
    m = np.arange(n_ctx, dtype=np.float64)[None, :]
    ang = 2.0 * np.pi * k * m / n
    fc, fs = jnp.asarray(np.cos(ang), F32), jnp.asarray(np.sin(ang), F32)
    ctx0 = (n_batch * seq) // n_ctx
    const = lambda b: (0, 0)
    return pl.pallas_call(
        _ctx_hyena_kernel,
        grid=(n_batch,),
        in_specs=[
            pl.BlockSpec((n_ctx, HY_COLS), lambda b: (ctx0 + b, 0)),
            pl.BlockSpec((3, HY_COLS), const), pl.BlockSpec((1, HY_COLS), const),
            pl.BlockSpec((n_ctx, 2 * HY_ORDER * HY_WIDTH), const),
            pl.BlockSpec((HY_ORDER, HY_WIDTH), const),
            pl.BlockSpec((n, n_ctx), const), pl.BlockSpec((n, n_ctx), const),
            pl.BlockSpec((n_ctx, n), const), pl.BlockSpec((n_ctx, n), const),
        ],
        out_specs=pl.BlockSpec((n_ctx, HY_WIDTH), lambda b: (b, 0)),
        out_shape=jax.ShapeDtypeStruct((n_batch * n_ctx, HY_WIDTH), BF16),
        compiler_params=_cparams("arbitrary"),
        name="ctx_hyena",
    )(hy, conv_w, conv_b, filt, hy_bias, fc, fs, fc.T, fs.T)


def _outproj_kernel(x_ref, yh_ref, yw_ref, yd_ref, mod_ref, wo_ref, g_ref, wq_ref, xo_ref, h2_ref, qp_ref):
    y = jnp.dot(yh_ref[...], wo_ref[0:HY_WIDTH, :], preferred_element_type=F32)
    y = y + jnp.dot(yw_ref[...], wo_ref[HY_WIDTH:HY_WIDTH + WA_Q_COLS, :], preferred_element_type=F32)
    y = y + jnp.dot(yd_ref[...], wo_ref[HY_WIDTH + WA_Q_COLS:, :], preferred_element_type=F32)
    xn = x_ref[...] + mod_ref[2:3, :] * y
    xo_ref[...] = xn
    h2 = _rmsnorm_mod(xn, g_ref[...], mod_ref[3:4, :], mod_ref[4:5, :]).astype(BF16)
    h2_ref[...] = h2
    qp_ref[...] = jnp.dot(h2, wq_ref[...], preferred_element_type=F32)


def out_proj(x_all, y_hy, y_wa, y_da, mod, wo_bf, g2, wq_bf, *, n_rows, n_batch, seq, tm):
    d = x_all.shape[1]
    tpb = seq // tm
    grp = functools.partial(_group_of_tile, tiles_per_batch=tpb, n_batch=n_batch)
    row = lambda i: (i, 0)
    const = lambda i: (0, 0)
    mix = HY_WIDTH + WA_Q_COLS + DA_V_COLS
    return pl.pallas_call(
        _outproj_kernel,
        grid=(n_rows // tm,),
        in_specs=[
            pl.BlockSpec((tm, d), row), pl.BlockSpec((tm, HY_WIDTH), row), pl.BlockSpec((tm, WA_Q_COLS), row),
            pl.BlockSpec((tm, DA_V_COLS), row),
            pl.BlockSpec((None, N_MOD, d), lambda i: (grp(i), 0, 0)),
            pl.BlockSpec((mix, d), const), pl.BlockSpec((1, d), const), pl.BlockSpec((d, PEER_QCOLS), const),
        ],
        out_specs=[pl.BlockSpec((tm, d), row), pl.BlockSpec((tm, d), row), pl.BlockSpec((tm, PEER_QCOLS), row)],
        out_shape=[jax.ShapeDtypeStruct((n_rows, d), F32), jax.ShapeDtypeStruct((n_rows, d), BF16),
                   jax.ShapeDtypeStruct((n_rows, PEER_QCOLS), F32)],
        compiler_params=_cparams("arbitrary"),
        name="out_proj",
    )(x_all, y_hy, y_wa, y_da, mod, wo_bf, g2, wq_bf)


def _sort_pairs(n):
    pairs = []

    def merge(lo, cnt, r):
        step = r * 2
        if step < cnt:
            merge(lo, cnt, step)
            merge(lo + r, cnt, step)
            for i in range(lo + r, lo + cnt - r, step):
                pairs.append((i, i + r))
        else:
            pairs.append((lo, lo + r))

    def sort(lo, cnt):
        if cnt > 1:
            half = cnt // 2
            sort(lo, half)
            sort(lo + half, half)
            merge(lo, cnt, 1)

    sort(0, n)
    return pairs


def _sort_desc(vals):
    n = 1
    while n < len(vals):
        n *= 2
    v = list(vals) + [None] * (n - len(vals))
    for i, j in _sort_pairs(n):
        a, b = v[i], v[j]
        if b is None:
            continue
        if a is None:
            v[i], v[j] = b, None
        else:
            v[i], v[j] = jnp.maximum(a, b), jnp.minimum(a, b)
    return v[:len(vals)]


def _top16_rows(s):
    k = PEER_TOPK
    v = _sort_desc([s[SUBLANES * r:SUBLANES * (r + 1), :] for r in range(s.shape[0] // SUBLANES)])
    for shift in (4, 2, 1):
        other = [pltpu.roll(x, shift, 0) for x in v]
        v = [jnp.maximum(v[i], other[k - 1 - i]) for i in range(k)]
        d = k // 2
        while d >= 1:
            for i in range(k):
                if (i & d) == 0:
                    a, b = v[i], v[i + d]
                    v[i], v[i + d] = jnp.maximum(a, b), jnp.minimum(a, b)
            d //= 2
    return v


def _peer_topk_kernel(q_ref, keys_ref, cnt_ref, e1_ref, rank_ref, e2_ref):
    k = PEER_TOPK
    tmk = q_ref.shape[0]
    sub = lax.broadcasted_iota(jnp.int32, (SUBLANES, tmk), 0)
    scores, tops = [], []
    for hp in range(2 * PEER_HEADS):
        s = lax.dot_general(keys_ref[hp], q_ref[:, hp * N_KEYS:(hp + 1) * N_KEYS], (((1,), (1,)), ((), ())),
                            precision=HI, preferred_element_type=F32)
        scores.append(s)
        top = _top16_rows(s)
        top.append(jnp.max(jnp.where(s < top[k - 1][0:1, :], s, NEG_BIG), axis=0, keepdims=True))
        tops.append(top)
    packed = []
    for p in range(2):
        lst = []
        for r in range(k + 1):
            acc = jnp.broadcast_to(tops[p][r], (SUBLANES, tmk))
            for h in range(1, PEER_HEADS):
                acc = jnp.where(sub == h, tops[2 * h + p][r], acc)
            lst.append(acc)
        packed.append(lst)
    cand = [packed[0][i] + packed[1][j] for i in range(k) for j in range(k) if (i + 1) * (j + 1) <= k]
    cs = _sort_desc(cand)
    c17 = jnp.maximum(cs[k], jnp.maximum(packed[0][k] + packed[1][0], packed[0][0] + packed[1][k]))
    tau = 0.5 * (cs[k - 1] + c17)
    zsum = jnp.zeros_like(tau)
    for r in range(k):
        zsum = zsum + jnp.exp(cs[r] - cs[0])
    inv_z = 1.0 / zsum
    for h in range(PEER_HEADS):
        s1, s2 = scores[2 * h], scores[2 * h + 1]
        row = lambda x: x[h:h + 1, :]
        top2 = [t2[0:1, :] for t2 in tops[2 * h + 1][:k]]
        cnt_ref[h] = _count_sorted(top2, row(tau) - s1, strict=False)
        rank_ref[h] = _count_sorted(top2, s2, strict=True).astype(BF16)
        e1_ref[h] = 0.5 * jnp.exp(s1 - row(packed[0][0]))
        e2_ref[h] = (jnp.exp(s2 - row(packed[1][0])) * row(inv_z)).astype(BF16)


def _count_sorted(tops, x, *, strict):
    ge = (lambda a: a > x) if strict else (lambda a: a >= x)
    b3 = ge(tops[7])
    b2 = ge(jnp.where(b3, tops[11], tops[3]))
    b1 = ge(jnp.where(b3, jnp.where(b2, tops[13], tops[9]), jnp.where(b2, tops[5], tops[1])))
    hi = jnp.where(b2, jnp.where(b1, tops[14], tops[12]), jnp.where(b1, tops[10], tops[8]))
    lo = jnp.where(b2, jnp.where(b1, tops[6], tops[4]), jnp.where(b1, tops[2], tops[0]))
    b0 = ge(jnp.where(b3, hi, lo))
    cnt = (jnp.where(b3, 8.0, 0.0) + jnp.where(b2, 4.0, 0.0)) + (jnp.where(b1, 2.0, 0.0) + jnp.where(b0, 1.0, 0.0))
    return jnp.where(ge(tops[15]), 16.0, cnt)


def peer_topk(qp, keys, *, tmk):
    t = qp.shape[0]
    out = pl.BlockSpec((PEER_HEADS, N_KEYS, tmk), lambda i: (0, 0, i))
    f32 = jax.ShapeDtypeStruct((PEER_HEADS, N_KEYS, t), F32)
    b16 = jax.ShapeDtypeStruct((PEER_HEADS, N_KEYS, t), BF16)
    return pl.pallas_call(
        _peer_topk_kernel,
        grid=(t // tmk,),
        in_specs=[pl.BlockSpec((tmk, PEER_QCOLS), lambda i: (i, 0)),
                  pl.BlockSpec((2 * PEER_HEADS, N_KEYS, PEER_QDIM // 2), lambda i: (0, 0, 0))],
        out_specs=[out, out, out, out],
        out_shape=[f32, f32, b16, b16],
        compiler_params=_cparams("arbitrary"),
        name="peer_topk",
    )(qp, keys)


def _peer_dense_kernel(x_ref, h_ref, mod_ref, u_ref, vt_ref, cnt_ref, e1_ref, rank_ref, e2_ref, g_ref, o_ref,
                       a_scr, w_scr, acc_scr, *, final_norm, chunk):
    e = pl.program_id(1)

    @pl.when(e == 0)
    def _():
        acc_scr[...] = jnp.zeros(acc_scr.shape, F32)

    n_chunks = u_ref.shape[0] // chunk
    per_chunk = chunk // N_KEYS
    sqrt_half = math.sqrt(0.5)
    tm = h_ref.shape[0]
    tiles = N_KEYS // BF16_ROWS
    zero = jnp.zeros((), BF16)

    def pre_activations(p):
        rows = slice(p * chunk, (p + 1) * chunk)
        a_scr[rows, :] = lax.dot_general(u_ref[rows, :], h_ref[...], (((1,), (1,)), ((), ())),
                                         preferred_element_type=F32)

    pre_activations(0)
    for p in range(n_chunks):
        if p + 1 < n_chunks:
            pre_activations(p + 1)
        for ii in range(p * per_chunk, (p + 1) * per_chunk):
            rows = slice(ii * N_KEYS, (ii + 1) * N_KEYS)
            a = a_scr[rows, :]
            act = a * (1.0 + lax.erf(a * sqrt_half))
            act = act.astype(BF16).reshape(tiles, BF16_ROWS, tm)
            gate = None
            for h in range(PEER_HEADS):
                cnt = jnp.broadcast_to(cnt_ref[h, ii:ii + 1, :], (BF16_ROWS, tm)).astype(BF16)
                e1 = jnp.broadcast_to(e1_ref[h, ii:ii + 1, :], (BF16_ROWS, tm)).astype(BF16)
                term = jnp.where(rank_ref[h] < cnt[None], e2_ref[h] * e1[None], zero)
                gate = term if gate is None else gate + term
            w_scr[rows, :] = (gate * act).reshape(N_KEYS, tm)
        rows = slice(p * chunk, (p + 1) * chunk)
        acc_scr[...] += jnp.dot(vt_ref[:, rows], w_scr[rows, :], preferred_element_type=F32)

    @pl.when(e == pl.num_programs(1) - 1)
    def _():
        xn = x_ref[...] + mod_ref[5:6, :] * acc_scr[...].T
        if final_norm:
            ms = jnp.mean(xn * xn, axis=-1, keepdims=True)
            xn = xn * lax.rsqrt(ms + NORM_EPS) * g_ref[...]
        o_ref[...] = xn


def peer_dense(x, h2, mod, u_bf, vt_bf, cnt, e1, rank, e2, g_final, *, n_batch, seq, tm, te, chunk, final_norm):
    n_rows, d = x.shape
    tpb = seq // tm
    grp = functools.partial(_group_of_tile, tiles_per_batch=tpb, n_batch=n_batch)
    n_i = te // N_KEYS
    tok = lambda i, e: (i, 0)
    rows_i = pl.BlockSpec((PEER_HEADS, n_i, tm), lambda i, e: (0, e, i))
    tiles = N_KEYS // BF16_ROWS
    full = pl.BlockSpec((PEER_HEADS, tiles, BF16_ROWS, tm), lambda i, e: (0, 0, 0, i))
    rank = rank.reshape(PEER_HEADS, tiles, BF16_ROWS, -1)
    e2 = e2.reshape(PEER_HEADS, tiles, BF16_ROWS, -1)
    return pl.pallas_call(
        functools.partial(_peer_dense_kernel, final_norm=final_norm, chunk=chunk),
        grid=(n_rows // tm, N_EXPERTS // te),
        in_specs=[
            pl.BlockSpec((tm, d), tok), pl.BlockSpec((tm, d), tok),
            pl.BlockSpec((None, N_MOD, d), lambda i, e: (grp(i), 0, 0)),
            pl.BlockSpec((te, d), lambda i, e: (e, 0)),
            pl.BlockSpec((d, te), lambda i, e: (0, e)),
            rows_i, rows_i, full, full,
            pl.BlockSpec((1, d), lambda i, e: (0, 0)),
        ],
        out_specs=pl.BlockSpec((tm, d), tok),
        out_shape=jax.ShapeDtypeStruct((n_rows, d), F32),
        scratch_shapes=[pltpu.VMEM((te, tm), F32), pltpu.VMEM((te, tm), BF16), pltpu.VMEM((d, tm), F32)],
        compiler_params=_cparams("arbitrary", "arbitrary"),
        name="peer_dense",
    )(x, h2, mod, u_bf, vt_bf, cnt, e1, rank, e2, g_final)


def kernel(x, c, ctx, c_ctx, w_mod, b_mod, norm1_g, w_in, hy_conv_w, hy_conv_b, hy_w1, hy_b1, hy_freq, hy_w2,
           hy_b2, hy_w3, hy_decay, hy_bias, wa_sink, da_lq1, da_lk1, da_lq2, da_lk2, da_norm_g, w_out, norm2_g,
           peer_wq, peer_keys, peer_u, peer_v, final_g):
    n_batch, seq, d = x.shape
    n_ctx = ctx.shape[1]
    depth = w_in.shape[0]
    n_lat = n_batch * seq
    tm = 512 if seq % 512 == 0 else 256
    assert seq % tm == 0 and (n_batch * n_ctx) % tm == 0 and n_lat % n_ctx == 0 and n_batch + 1 <= SUBLANES

    cpad = jnp.zeros((SUBLANES, d), F32).at[:n_batch].set(c).at[n_batch].set(c_ctx)
    mod_all = mod_vectors(cpad, w_mod, b_mod).reshape(depth, SUBLANES, N_MOD, d)
    rope_wa = rope_tables(seq, HEAD_DIM, tm)
    rope_da = rope_tables(seq, DA_QK_DIM, tm)
    consts = dft_constants(seq)
    feat_lat = hyena_filter_features(seq)
    feat_ctx = hyena_filter_features(n_ctx)
    x_all = jnp.concatenate([x.reshape(n_lat, d), ctx.reshape(n_batch * n_ctx, d)], axis=0)

    out = None
    for li in range(depth):
        last = li == depth - 1
        lam_init = DA_LAMBDA_BASE - DA_LAMBDA_AMP * math.exp(-DA_LAMBDA_RATE * li)
        mod = mod_all[li]
        hy, qw, kw, vw, qd, kd, vd, vdt = proj_in(x_all, mod, norm1_g[li][None, :], w_in[li].astype(BF16), rope_wa,
                                             rope_da, n_batch=n_batch, seq=seq, tm=tm)
        w1p = jnp.pad(hy_w1[li], ((0, LANES - HY_EMB_DIM), (0, 0)))
        fargs = (w1p, hy_b1[li][None, :], hy_freq[li][None, :], hy_w2[li], hy_b2[li][None, :], hy_w3[li],
                 hy_decay[li].reshape(1, -1))
        conv_b = hy_conv_b[li][None, :]
        filt_t = hyena_filters(feat_lat, *fargs, transposed=True)
        y_hy = hyena_latent(hy, hy_conv_w[li], conv_b, hy_bias[li], filt_t, consts, n_batch=n_batch, seq=seq)
        y_wa = window_attention(wa_sink[li], qw, kw, vw, n_batch=n_batch, seq=seq, n_ctx=n_ctx)
        lq = jnp.stack([da_lq1[li], da_lk1[li], da_lq2[li], da_lk2[li]], axis=0)
        g128 = jnp.tile(da_norm_g[li], LANES // DA_V_DIM)[None, :]
        tq = min(seq, 512)
        g_rows = jnp.broadcast_to(da_norm_g[li][:, None], (DA_V_DIM, tq))
        y_da = diff_attention(lq, g_rows, qd, kd, vdt, n_batch=n_batch, seq=seq, n_ctx=n_ctx, lam_init=lam_init,
                              tq=tq, tk=min(seq, 256), ahead=3, unroll=4 if seq % 1024 == 0 else 1)
        if not last:
            filt_c = hyena_filters(feat_ctx, *fargs, transposed=False)
            yc_hy = ctx_hyena(hy, hy_conv_w[li], conv_b, filt_c, hy_bias[li], n_batch=n_batch, seq=seq, n_ctx=n_ctx)
            yc_wa, yc_da = ctx_attention(wa_sink[li], lq, g128, qw, kw, vw, qd, kd, vd, n_batch=n_batch, seq=seq,
                                         n_ctx=n_ctx, lam_init=lam_init)
            y_hy = jnp.concatenate([y_hy, yc_hy], axis=0)
            y_wa = jnp.concatenate([y_wa, yc_wa], axis=0)
            y_da = jnp.concatenate([y_da, yc_da], axis=0)
        n_rows = n_lat if last else x_all.shape[0]
        x_mid, h2, qp = out_proj(x_all, y_hy, y_wa, y_da, mod, w_out[li].astype(BF16), norm2_g[li][None, :],
                                 peer_wq[li].astype(BF16), n_rows=n_rows, n_batch=n_batch, seq=seq, tm=tm)
        keys = peer_keys[li].reshape(2 * PEER_HEADS, N_KEYS, PEER_QDIM // 2)
        cnt, e1, rank, e2 = peer_topk(qp, keys, tmk=256)
        x_all = peer_dense(x_mid, h2, mod, peer_u[li].astype(BF16), peer_v[li].T.astype(BF16), cnt, e1, rank, e2,
                           final_g[None, :], n_batch=n_batch, seq=seq, tm=tm, te=2048, chunk=1024, final_norm=last)
        out = x_all
    return out.reshape(n_batch, seq, d)
```

```python
import functools
import math

import numpy as np
import jax
import jax.numpy as jnp
from jax import lax
from jax.experimental import pallas as pl
from jax.experimental.pallas import tpu as pltpu

F32 = jnp.float32
BF16 = jnp.bfloat16
HI = lax.Precision.HIGHEST

LANES = 128
SUBLANES = 8
BF16_ROWS = 16
VMEM_LIMIT_BYTES = 56 * 1024 * 1024

D_MODEL = 1024
N_MOD = 6
NORM_EPS = 1e-6
ROPE_BASE = 10000.0
GRID_W = 64
BLOCK = 128
HEAD_DIM = 64
HY_WIDTH = 256
HY_ORDER = 2
HY_BANDS = 16
HY_EMB_DIM = 1 + 2 * HY_BANDS
HY_FILTER_WIDTH = 64
HY_COLS = 3 * HY_WIDTH
WA_Q_HEADS = 8
WA_KV_HEADS = 2
WA_GROUP = 4
WINDOW = 128
WA_Q_COLS = WA_Q_HEADS * HEAD_DIM
WA_KV_COLS = WA_KV_HEADS * HEAD_DIM
WA_COLS = WA_Q_COLS + 2 * WA_KV_COLS
DA_HEADS = 4
DA_QK_DIM = 32
DA_V_DIM = 64
DA_QK_COLS = DA_HEADS * 2 * DA_QK_DIM
DA_V_COLS = DA_HEADS * DA_V_DIM
DA_COLS = 2 * DA_QK_COLS + DA_V_COLS
DA_LAMBDA_BASE = 0.8
DA_LAMBDA_AMP = 0.6
DA_LAMBDA_RATE = 0.3
IN_COLS = HY_COLS + WA_COLS + DA_COLS
PEER_HEADS = 8
N_KEYS = 128
N_EXPERTS = N_KEYS * N_KEYS
PEER_TOPK = 16
PEER_QDIM = 256
PEER_QCOLS = PEER_HEADS * PEER_QDIM
NEG_BIG = -1e30
LOG2E = math.log2(math.e)


def _cparams(*sem):
    return pltpu.CompilerParams(dimension_semantics=sem, vmem_limit_bytes=VMEM_LIMIT_BYTES)


def _rmsnorm_mod(x, g, shift, scale):
    ms = jnp.mean(x * x, axis=-1, keepdims=True)
    return (x * lax.rsqrt(ms + NORM_EPS) * g) * (1.0 + scale) + shift


def _mod_kernel(c_ref, w_ref, b_ref, o_ref):
    c = c_ref[...]
    act = c * jax.nn.sigmoid(c)
    o_ref[...] = jnp.dot(act, w_ref[...], precision=HI, preferred_element_type=F32) + b_ref[...]


def mod_vectors(cpad, w_mod, b_mod):
    n_layers, d, n = w_mod.shape
    tn = 1536
    return pl.pallas_call(
        _mod_kernel,
        grid=(n_layers, n // tn),
        in_specs=[
            pl.BlockSpec((SUBLANES, d), lambda l, j: (0, 0)),
            pl.BlockSpec((None, d, tn), lambda l, j: (l, 0, j)),
            pl.BlockSpec((None, 1, tn), lambda l, j: (l, 0, j)),
        ],
        out_specs=pl.BlockSpec((None, SUBLANES, tn), lambda l, j: (l, 0, j)),
        out_shape=jax.ShapeDtypeStruct((n_layers, SUBLANES, n), F32),
        compiler_params=_cparams("arbitrary", "arbitrary"),
        name="mod_vectors",
    )(cpad, w_mod, b_mod.reshape(n_layers, 1, n))


def _rope128(x, cos, sin_a, sin_b, m):
    return x * cos + pltpu.roll(x, LANES - m, 1) * sin_a + pltpu.roll(x, m, 1) * sin_b


def _proj_in_kernel(x_ref, mod_ref, g_ref, w_ref, cw_ref, saw_ref, sbw_ref, cd_ref, sad_ref, sbd_ref,
                    hy_ref, qw_ref, kw_ref, vw_ref, qd_ref, kd_ref, vd_ref, vdt_ref, vwt_ref):
    h = _rmsnorm_mod(x_ref[...], g_ref[...], mod_ref[0:1, :], mod_ref[1:2, :]).astype(BF16)
    hy_ref[...] = jnp.dot(h, w_ref[:, 0:HY_COLS], preferred_element_type=F32)
    wa = jnp.dot(h, w_ref[:, HY_COLS:HY_COLS + WA_COLS], preferred_element_type=F32)
    da = jnp.dot(h, w_ref[:, HY_COLS + WA_COLS:IN_COLS], preferred_element_type=F32)
    cw, saw, sbw = cw_ref[...], saw_ref[...], sbw_ref[...]
    cd, sad, sbd = cd_ref[...], sad_ref[...], sbd_ref[...]
    wa_scale = HEAD_DIM ** -0.5
    da_scale = DA_QK_DIM ** -0.5 * LOG2E
    for kb in range(WA_Q_COLS // LANES):
        blk = wa[:, kb * LANES:(kb + 1) * LANES]
        qw_ref[:, kb * LANES:(kb + 1) * LANES] = (_rope128(blk, cw, saw, sbw, 16) * wa_scale).astype(BF16)
    kw_ref[...] = _rope128(wa[:, WA_Q_COLS:WA_Q_COLS + WA_KV_COLS], cw, saw, sbw, 16).astype(BF16)
    vw = wa[:, WA_Q_COLS + WA_KV_COLS:WA_COLS]
    vw_ref[...] = vw.astype(BF16)
    vwt_ref[...] = vw.T.astype(BF16)
    for kb in range(DA_QK_COLS // LANES):
        blk = da[:, kb * LANES:(kb + 1) * LANES]
        qd_ref[:, kb * LANES:(kb + 1) * LANES] = (_rope128(blk, cd, sad, sbd, 8) * da_scale).astype(BF16)
        blk = da[:, DA_QK_COLS + kb * LANES:DA_QK_COLS + (kb + 1) * LANES]
        kd_ref[:, kb * LANES:(kb + 1) * LANES] = _rope128(blk, cd, sad, sbd, 8).astype(BF16)
    vd = da[:, 2 * DA_QK_COLS:DA_COLS]
    vd_ref[...] = vd.astype(BF16)
    vdt_ref[...] = vd.T.astype(BF16)


def _group_of_tile(i, tiles_per_batch, n_batch):
    return jnp.minimum(i // tiles_per_batch, n_batch)


def proj_in(x_all, mod, g, w_bf, rope_wa, rope_da, *, n_batch, seq, tm):
    t_all, d = x_all.shape
    tpb = seq // tm
    grp = functools.partial(_group_of_tile, tiles_per_batch=tpb, n_batch=n_batch)
    pos = lambda i: (jnp.where(i < n_batch * tpb, i % tpb, tpb), 0)
    row = lambda i: (i, 0)
    const = lambda i: (0, 0)
    tab = pl.BlockSpec((tm, LANES), pos)
    outs = [(HY_COLS, F32), (WA_Q_COLS, BF16), (WA_KV_COLS, BF16), (WA_KV_COLS, BF16),
            (DA_QK_COLS, BF16), (DA_QK_COLS, BF16), (DA_V_COLS, BF16)]
    return pl.pallas_call(
        _proj_in_kernel,
        grid=(t_all // tm,),
        in_specs=[
            pl.BlockSpec((tm, d), row),
            pl.BlockSpec((None, N_MOD, d), lambda i: (grp(i), 0, 0)),
            pl.BlockSpec((1, d), const),
            pl.BlockSpec((d, IN_COLS), const),
            tab, tab, tab, tab, tab, tab,
        ],
        out_specs=[pl.BlockSpec((tm, n), row) for n, _ in outs]
        + [pl.BlockSpec((DA_V_COLS, tm), lambda i: (0, i)), pl.BlockSpec((WA_KV_COLS, tm), lambda i: (0, i))],
        out_shape=[jax.ShapeDtypeStruct((t_all, n), dt) for n, dt in outs]
        + [jax.ShapeDtypeStruct((DA_V_COLS, t_all), BF16), jax.ShapeDtypeStruct((WA_KV_COLS, t_all), BF16)],
        compiler_params=_cparams("arbitrary"),
        name="proj_in",
    )(x_all, mod, g, w_bf, *rope_wa, *rope_da)


def rope_tables(seq, head_dim, pad_rows):
    half = head_dim // 2
    axis_dim = half
    m = axis_dim // 2
    t = np.arange(seq)
    rowcol = np.stack([t // GRID_W, t % GRID_W], axis=0).astype(np.float32)
    inv_freq = (ROPE_BASE ** (-jnp.arange(0, axis_dim, 2, dtype=F32) / axis_dim))
    ang = jnp.asarray(rowcol)[:, :, None] * inv_freq[None, None, :]
    cos, sin = jnp.cos(ang), jnp.sin(ang)
    lane = np.arange(LANES)
    dd = lane % head_dim
    axis = dd // half
    sub = dd % half
    idx = sub % m
    first = sub < m
    cos_t = cos[axis, :, idx].T
    sin_t = sin[axis, :, idx].T
    sin_a = jnp.where(first[None, :], -sin_t, 0.0)
    sin_b = jnp.where(first[None, :], 0.0, sin_t)
    pad1 = jnp.ones((pad_rows, LANES), F32)
    pad0 = jnp.zeros((pad_rows, LANES), F32)
    return (jnp.concatenate([cos_t, pad1], 0), jnp.concatenate([sin_a, pad0], 0),
            jnp.concatenate([sin_b, pad0], 0))


def _softmax_sink_attend(q, k, v, sink, valid):
    s = lax.dot_general(q, k, (((1,), (1,)), ((), ())), preferred_element_type=F32)
    if valid is not None:
        nm = valid.shape[1]
        s = jnp.concatenate([jnp.where(valid, s[:, :nm], NEG_BIG), s[:, nm:]], axis=1)
    m = jnp.maximum(jnp.max(s, axis=1, keepdims=True), sink)
    p = jnp.exp(s - m)
    den = jnp.sum(p, axis=1, keepdims=True) + jnp.exp(sink - m)
    return jnp.dot(p.astype(BF16), v, preferred_element_type=F32) / den


def _wattn_kernel(sink_ref, q_ref, kp_ref, kc_ref, kn_ref, kx_ref, vp_ref, vc_ref, vn_ref, vx_ref, o_ref, qm_scr,
                  *, n_pairs, ahead):
    m_idx = pl.program_id(1)
    nq = 2 * BLOCK
    n_loc = 4 * BLOCK
    kk = lax.broadcasted_iota(jnp.int32, (n_loc, nq), 0)
    qq = lax.broadcasted_iota(jnp.int32, (n_loc, nq), 1)
    lo = jnp.where(m_idx > 0, 0, BLOCK)
    hi = jnp.where(m_idx < n_pairs - 1, n_loc, n_loc - BLOCK)
    valid = (kk >= jnp.maximum(qq, lo)) & (kk <= qq + 2 * WINDOW) & (kk < hi)
    k_all = jnp.concatenate([kp_ref[...], kc_ref[...], kn_ref[...], kx_ref[...]], axis=0)
    vt_all = jnp.concatenate([vp_ref[...], vc_ref[...], vn_ref[...], vx_ref[...]], axis=1)
    lane = lax.broadcasted_iota(jnp.int32, (nq, LANES), 1)
    for h in range(WA_Q_HEADS):
        hk = h // WA_GROUP
        blk = q_ref[:, (h // 2) * LANES:(h // 2 + 1) * LANES]
        if h % 2 != hk:
            blk = jnp.concatenate([blk[:, HEAD_DIM:], blk[:, :HEAD_DIM]], axis=1)
        keep = (lane >= hk * HEAD_DIM) & (lane < (hk + 1) * HEAD_DIM)
        qm_scr[h] = jnp.where(keep, blk, jnp.zeros_like(blk))

    def scores(h):
        return lax.dot_general(k_all, qm_scr[h], (((1,), (1,)), ((), ())), preferred_element_type=F32)

    sts = [scores(h) for h in range(ahead)]
    outs = []
    for h in range(WA_Q_HEADS):
        hk = h // WA_GROUP
        if h + ahead < WA_Q_HEADS:
            sts.append(scores(h + ahead))
        st = sts[h]
        sts[h] = None
        st = jnp.concatenate([jnp.where(valid, st[:n_loc], NEG_BIG), st[n_loc:]], axis=0)
        sink = sink_ref[h]
        m = jnp.maximum(jnp.max(st, axis=0, keepdims=True), sink)
        p = jnp.exp(st - m)
        den = jnp.sum(p, axis=0, keepdims=True) + jnp.exp(sink - m)
        o = jnp.dot(vt_all[hk * HEAD_DIM:(hk + 1) * HEAD_DIM, :], p.astype(BF16), preferred_element_type=F32)
        outs.append(o / den)
    o_ref[...] = jnp.concatenate(outs, axis=0).T.astype(BF16)


def window_attention(sink, qw, kw, vwt, *, n_batch, seq, n_ctx):
    nb = seq // BLOCK
    n_pairs = nb // 2
    assert nb % 2 == 0
    ctx0 = (n_batch * seq) // n_ctx
    cur = lambda b, m: (b * n_pairs + m, 0)
    prev = lambda b, m: (b * nb + jnp.maximum(2 * m - 1, 0), 0)
    nxt = lambda b, m: (b * nb + jnp.minimum(2 * m + 2, nb - 1), 0)
    ctx = lambda b, m: (ctx0 + b, 0)
    swap = lambda f: (lambda b, m: f(b, m)[::-1])
    return pl.pallas_call(
        functools.partial(_wattn_kernel, n_pairs=n_pairs, ahead=3),
        grid=(n_batch, n_pairs),
        in_specs=[
            pl.BlockSpec(memory_space=pltpu.SMEM),
            pl.BlockSpec((2 * BLOCK, WA_Q_COLS), cur),
            pl.BlockSpec((BLOCK, WA_KV_COLS), prev), pl.BlockSpec((2 * BLOCK, WA_KV_COLS), cur),
            pl.BlockSpec((BLOCK, WA_KV_COLS), nxt), pl.BlockSpec((n_ctx, WA_KV_COLS), ctx),
            pl.BlockSpec((WA_KV_COLS, BLOCK), swap(prev)), pl.BlockSpec((WA_KV_COLS, 2 * BLOCK), swap(cur)),
            pl.BlockSpec((WA_KV_COLS, BLOCK), swap(nxt)), pl.BlockSpec((WA_KV_COLS, n_ctx), swap(ctx)),
        ],
        out_specs=pl.BlockSpec((2 * BLOCK, WA_Q_COLS), cur),
        out_shape=jax.ShapeDtypeStruct((n_batch * seq, WA_Q_COLS), BF16),
        scratch_shapes=[pltpu.VMEM((WA_Q_HEADS, 2 * BLOCK, LANES), BF16)],
        compiler_params=_cparams("arbitrary", "arbitrary"),
        name="window_attention",
    )(sink, qw, kw, kw, kw, kw, vwt, vwt, vwt, vwt)


def _diff_lambda(lq_ref, lam_init):
    a = jnp.sum(lq_ref[0:1, :] * lq_ref[1:2, :], axis=1, keepdims=True)
    b = jnp.sum(lq_ref[2:3, :] * lq_ref[3:4, :], axis=1, keepdims=True)
    return jnp.exp(a) - jnp.exp(b) + lam_init


def _diff_finish(acc1, l1, acc2, l2, lam, g128, lam_init):
    lane = lax.broadcasted_iota(jnp.int32, acc1[0].shape, 1)
    lo = lane < DA_V_DIM
    blocks = []
    for vb in range(DA_HEADS // 2):
        o = []
        for h in (2 * vb, 2 * vb + 1):
            o.append(acc1[h] / l1[h] - lam * (acc2[h] / l2[h]))
        blk = jnp.where(lo, o[0], o[1])
        sq = blk * blk
        s_lo = jnp.sum(jnp.where(lo, sq, 0.0), axis=1, keepdims=True)
        s_hi = jnp.sum(jnp.where(lo, 0.0, sq), axis=1, keepdims=True)
        ms = jnp.where(lo, s_lo, s_hi) * (1.0 / DA_V_DIM)
        blocks.append(blk * lax.rsqrt(ms + NORM_EPS) * g128 * (1.0 - lam_init))
    return jnp.concatenate(blocks, axis=1)


def _masked_q(q_ref, hm):
    kb, sub = divmod(hm, LANES // DA_QK_DIM)
    blk = q_ref[:, kb * LANES:(kb + 1) * LANES]
    lane = lax.broadcasted_iota(jnp.int32, blk.shape, 1)
    keep = (lane >= sub * DA_QK_DIM) & (lane < (sub + 1) * DA_QK_DIM)
    return jnp.where(keep, blk, jnp.zeros_like(blk))


def _dattn_kernel(lq_ref, g_ref, q_ref, k_ref, vt_ref, kx_ref, vxt_ref, o_ref, qm_scr, m_scr, l_scr, acc_scr,
                  *, lam_init, tk, ahead, unroll):
    n_hm = 2 * DA_HEADS
    per_blk = LANES // DA_QK_DIM
    for hm in range(n_hm):
        qm_scr[hm] = _masked_q(q_ref, hm)
    m_scr[...] = jnp.full(m_scr.shape, NEG_BIG, F32)
    l_scr[...] = jnp.zeros(l_scr.shape, F32)
    acc_scr[...] = jnp.zeros(acc_scr.shape, F32)

    def update(chunks):
        chains = [(c, hm) for c in range(len(chunks)) for hm in range(n_hm)]

        def scores(chain):
            c, hm = chain
            kb = hm // per_blk
            return lax.dot_general(chunks[c][0][:, kb * LANES:(kb + 1) * LANES], qm_scr[hm],
                                   (((1,), (1,)), ((), ())), preferred_element_type=F32)

        sts = [scores(ch) for ch in chains[:ahead]]
        for n, (c, hm) in enumerate(chains):
            h = hm // 2
            vtc = chunks[c][1]
            if n + ahead < len(chains):
                sts.append(scores(chains[n + ahead]))
            st = sts[n]
            sts[n] = None
            m_old = m_scr[hm]
            m_new = jnp.maximum(m_old, jnp.max(st, axis=0, keepdims=True))
            alpha = jnp.exp2(m_old - m_new)
            pt = jnp.exp2(st - m_new)
            l_scr[hm] = alpha * l_scr[hm] + jnp.sum(pt, axis=0, keepdims=True)
            acc_scr[hm] = alpha * acc_scr[hm] + jnp.dot(vtc[h * DA_V_DIM:(h + 1) * DA_V_DIM, :], pt.astype(BF16),
                                                        preferred_element_type=F32)
            m_scr[hm] = m_new

    def body(j, carry):
        chunks = []
        for c in range(unroll):
            start = pl.multiple_of((j * unroll + c) * tk, tk)
            chunks.append((k_ref[pl.ds(start, tk), :], vt_ref[:, pl.ds(start, tk)]))
        update(chunks)
        return carry

    lax.fori_loop(0, k_ref.shape[0] // (tk * unroll), body, 0)
    n_x = kx_ref.shape[0] // tk
    update([(kx_ref[c * tk:(c + 1) * tk, :], vxt_ref[:, c * tk:(c + 1) * tk]) for c in range(n_x)])
    lam = _diff_lambda(lq_ref, lam_init)
    heads = []
    for h in range(DA_HEADS):
        o = acc_scr[2 * h] / l_scr[2 * h] - lam * (acc_scr[2 * h + 1] / l_scr[2 * h + 1])
        ms = jnp.mean(o * o, axis=0, keepdims=True)
        heads.append(o * lax.rsqrt(ms + NORM_EPS) * g_ref[...] * (1.0 - lam_init))
    o_ref[...] = jnp.concatenate(heads, axis=0).T.astype(BF16)


def diff_attention(lq, g_rows, qd, kd, vdt, *, n_batch, seq, n_ctx, lam_init, tq, tk, ahead, unroll):
    nq = seq // tq
    ctx0 = (n_batch * seq) // n_ctx
    n_hm = 2 * DA_HEADS
    return pl.pallas_call(
        functools.partial(_dattn_kernel, lam_init=lam_init, tk=tk, ahead=ahead, unroll=unroll),
        grid=(n_batch, nq),
        in_specs=[
            pl.BlockSpec((4, DA_QK_DIM), lambda b, i: (0, 0)),
            pl.BlockSpec((DA_V_DIM, tq), lambda b, i: (0, 0)),
            pl.BlockSpec((tq, DA_QK_COLS), lambda b, i: (b * nq + i, 0)),
            pl.BlockSpec((seq, DA_QK_COLS), lambda b, i: (b, 0)),
            pl.BlockSpec((DA_V_COLS, seq), lambda b, i: (0, b)),
            pl.BlockSpec((n_ctx, DA_QK_COLS), lambda b, i: (ctx0 + b, 0)),
            pl.BlockSpec((DA_V_COLS, n_ctx), lambda b, i: (0, ctx0 + b)),
        ],
        out_specs=pl.BlockSpec((tq, DA_V_COLS), lambda b, i: (b * nq + i, 0)),
        out_shape=jax.ShapeDtypeStruct((n_batch * seq, DA_V_COLS), BF16),
        scratch_shapes=[
            pltpu.VMEM((n_hm, tq, LANES), BF16),
            pltpu.VMEM((n_hm, 1, tq), F32),
            pltpu.VMEM((n_hm, 1, tq), F32),
            pltpu.VMEM((n_hm, DA_V_DIM, tq), F32),
        ],
        compiler_params=_cparams("arbitrary", "arbitrary"),
        name="diff_attention",
    )(lq, g_rows, qd, kd, vdt, kd, vdt)


def _ctx_attn_kernel(sink_ref, lq_ref, g_ref, qw_ref, kw_ref, vw_ref, qd_ref, kd_ref, vd_ref, ow_ref, od_ref,
                     *, lam_init):
    outs = []
    for hk in range(WA_KV_HEADS):
        sl = slice(hk * HEAD_DIM, (hk + 1) * HEAD_DIM)
        k, v = kw_ref[:, sl], vw_ref[:, sl]
        for g in range(WA_GROUP):
            h = hk * WA_GROUP + g
            outs.append(_softmax_sink_attend(qw_ref[:, h * HEAD_DIM:(h + 1) * HEAD_DIM], k, v, sink_ref[h], None))
    ow_ref[...] = jnp.concatenate(outs, axis=1).astype(BF16)
    per_blk = LANES // DA_QK_DIM
    acc, den = [], []
    for hm in range(2 * DA_HEADS):
        kb, vb = hm // per_blk, hm // 4
        s = lax.dot_general(_masked_q(qd_ref, hm), kd_ref[:, kb * LANES:(kb + 1) * LANES],
                            (((1,), (1,)), ((), ())), preferred_element_type=F32)
        p = jnp.exp2(s - jnp.max(s, axis=1, keepdims=True))
        den.append(jnp.sum(p, axis=1, keepdims=True))
        acc.append(jnp.dot(p.astype(BF16), vd_ref[:, vb * LANES:(vb + 1) * LANES], preferred_element_type=F32))
    lam = _diff_lambda(lq_ref, lam_init)
    od_ref[...] = _diff_finish(acc[0::2], den[0::2], acc[1::2], den[1::2], lam, g_ref[...], lam_init).astype(BF16)


def ctx_attention(sink, lq, g128, qw, kw, vw, qd, kd, vd, *, n_batch, seq, n_ctx, lam_init):
    ctx0 = (n_batch * seq) // n_ctx
    ctx = lambda b: (ctx0 + b, 0)
    out = lambda b: (b, 0)
    spec = lambda n, f: pl.BlockSpec((n_ctx, n), f)
    return pl.pallas_call(
        functools.partial(_ctx_attn_kernel, lam_init=lam_init),
        grid=(n_batch,),
        in_specs=[
            pl.BlockSpec(memory_space=pltpu.SMEM),
            pl.BlockSpec((4, DA_QK_DIM), lambda b: (0, 0)),
            pl.BlockSpec((1, LANES), lambda b: (0, 0)),
            spec(WA_Q_COLS, ctx), spec(WA_KV_COLS, ctx), spec(WA_KV_COLS, ctx),
            spec(DA_QK_COLS, ctx), spec(DA_QK_COLS, ctx), spec(DA_V_COLS, ctx),
        ],
        out_specs=[spec(WA_Q_COLS, out), spec(DA_V_COLS, out)],
        out_shape=[jax.ShapeDtypeStruct((n_batch * n_ctx, WA_Q_COLS), BF16),
                   jax.ShapeDtypeStruct((n_batch * n_ctx, DA_V_COLS), BF16)],
        compiler_params=_cparams("arbitrary"),
        name="ctx_attention",
    )(sink, lq, g128, qw, kw, vw, qd, kd, vd)


def _filter_kernel(feat_ref, w1_ref, b1_ref, fr_ref, w2_ref, b2_ref, w3_ref, dec_ref, o_ref, *, transposed):
    feat = feat_ref[...]
    fr = fr_ref[...]
    z = jnp.sin(fr * (jnp.dot(feat, w1_ref[...], precision=HI, preferred_element_type=F32) + b1_ref[...]))
    z = jnp.sin(fr * (jnp.dot(z, w2_ref[...], precision=HI, preferred_element_type=F32) + b2_ref[...]))
    h = jnp.dot(z, w3_ref[...], precision=HI, preferred_element_type=F32)
    h = h * jnp.exp(-feat[:, 0:1] * jnp.abs(dec_ref[...]))
    rows = lax.broadcasted_iota(jnp.int32, h.shape, 0) + pl.program_id(0) * h.shape[0]
    cols = lax.broadcasted_iota(jnp.int32, h.shape, 1)
    h = jnp.where((rows == 0) & (cols % (2 * HY_WIDTH) >= HY_WIDTH), 0.0, h)
    if transposed:
        for r in range(h.shape[0] // LANES):
            o_ref[r] = h[r * LANES:(r + 1) * LANES, :].T
    else:
        o_ref[...] = h


def hyena_filter_features(n):
    t = jnp.linspace(0.0, 1.0, n, dtype=F32)[:, None]
    bands = jnp.linspace(1e-4, HY_BANDS - 1, HY_BANDS, dtype=F32)
    phase = (2.0 * math.pi / n) * jnp.arange(n, dtype=F32)[:, None] * bands
    feat = jnp.concatenate([t, jnp.cos(phase), -jnp.sin(phase)], axis=-1)
    return jnp.pad(feat, ((0, 0), (0, LANES - HY_EMB_DIM)))


def hyena_filters(feat, w1p, b1, freq, w2, b2, w3, decay, *, transposed):
    n = feat.shape[0]
    tr = min(n, 512)
    ncol = w3.shape[1]
    fw = w2.shape[0]
    const = lambda i: (0, 0)
    if transposed:
        out_spec = pl.BlockSpec((tr // LANES, ncol, LANES), lambda i: (i, 0, 0))
        out_shape = jax.ShapeDtypeStruct((n // LANES, ncol, LANES), F32)
    else:
        out_spec = pl.BlockSpec((tr, ncol), lambda i: (i, 0))
        out_shape = jax.ShapeDtypeStruct((n, ncol), F32)
    return pl.pallas_call(
        functools.partial(_filter_kernel, transposed=transposed),
        grid=(n // tr,),
        in_specs=[
            pl.BlockSpec((tr, LANES), lambda i: (i, 0)),
            pl.BlockSpec((LANES, fw), const), pl.BlockSpec((1, fw), const), pl.BlockSpec((1, fw), const),
            pl.BlockSpec((fw, fw), const), pl.BlockSpec((1, fw), const),
            pl.BlockSpec((fw, ncol), const), pl.BlockSpec((1, ncol), const),
        ],
        out_specs=out_spec,
        out_shape=out_shape,
        compiler_params=_cparams("arbitrary"),
        name="hyena_filters",
    )(feat, w1p, b1, freq, w2, b2, w3, decay)


def dft_constants(seq):
    n = 2 * seq
    n1 = n // LANES
    a_rows = n1 // 2
    k1 = np.arange(n1, dtype=np.float64)[:, None]
    a = np.arange(a_rows, dtype=np.float64)[None, :]
    th = 2.0 * np.pi * k1 * a / n1
    c1, s1 = np.cos(th), np.sin(th)
    m1_data = np.block([[c1, s1], [-s1, c1]])
    m1_real = np.concatenate([c1, -s1], axis=0)
    m3 = np.block([[c1.T, -s1.T], [s1.T, c1.T]]) / n
    b = np.arange(LANES, dtype=np.float64)[None, :]
    psi = 2.0 * np.pi * k1 * b / n
    twc, tws = np.cos(psi), np.sin(psi)
    bb = np.arange(LANES, dtype=np.float64)
    phi = 2.0 * np.pi * np.outer(bb, bb) / LANES
    c2, s2 = np.cos(phi), np.sin(phi)
    w2f = np.block([[c2, -s2], [s2, c2]])
    w2i = np.block([[c2, s2], [-s2, c2]])
    f = lambda x: jnp.asarray(x, dtype=F32)
    return dict(m1_data=f(m1_data), m1_real=f(m1_real), m3=f(m3), twc=f(twc[:, None, :]), tws=f(tws[:, None, :]),
                w2f=f(w2f), w2i=f(w2i), n1=n1, a_rows=a_rows)


def _leftmm_kernel(m_ref, x_ref, o_ref):
    o_ref[...] = jnp.dot(m_ref[...], x_ref[...], precision=HI, preferred_element_type=F32)


def left_matmul(m, x, ct):
    r, k = m.shape
    cols = x.shape[1]
    return pl.pallas_call(
        _leftmm_kernel,
        grid=(cols // ct,),
        in_specs=[pl.BlockSpec((r, k), lambda j: (0, 0)), pl.BlockSpec((k, ct), lambda j: (0, j))],
        out_specs=pl.BlockSpec((r, ct), lambda j: (0, j)),
        out_shape=jax.ShapeDtypeStruct((r, cols), F32),
        compiler_params=_cparams("arbitrary"),
        name="dft_rows",
    )(m, x)


def _leftmm_gate_kernel(m_ref, z_ref, x_ref, zold_ref, bias_ref, o_ref):
    y = jnp.dot(m_ref[...], z_ref[...], precision=HI, preferred_element_type=F32)
    zold = zold_ref[...]
    o_ref[...] = x_ref[...] * (y + bias_ref[...] * zold)


def left_matmul_gate(m, z, xg, zold, bias_cols, ct):
    r, k = m.shape
    cols = z.shape[1]
    col = lambda j: (0, j)
    return pl.pallas_call(
        _leftmm_gate_kernel,
        grid=(cols // ct,),
        in_specs=[pl.BlockSpec((r, k), lambda j: (0, 0)), pl.BlockSpec((k, ct), col), pl.BlockSpec((r, ct), col),
                  pl.BlockSpec((r, ct), col), pl.BlockSpec((1, ct), col)],
        out_specs=pl.BlockSpec((r, ct), col),
        out_shape=jax.ShapeDtypeStruct((r, cols), F32),
        compiler_params=_cparams("arbitrary"),
        name="idft_rows_gate",
    )(m, z, xg, zold, bias_cols)


def _twiddle_fwd(yr, yi, c, s):
    return yr * c + yi * s, yi * c - yr * s


def _spectrum_kernel(yr_ref, yi_ref, twc_ref, tws_ref, w2f_ref, hr_ref, hi_ref, *, kb):
    for t in range(kb):
        ypr, ypi = _twiddle_fwd(yr_ref[t], yi_ref[t], twc_ref[t], tws_ref[t])
        x = jnp.dot(jnp.concatenate([ypr, ypi], axis=1), w2f_ref[...], precision=HI, preferred_element_type=F32)
        xr, xi = x[:, :LANES], x[:, LANES:]
        for o in range(HY_ORDER):
            f0 = slice((2 * o) * HY_WIDTH, (2 * o + 1) * HY_WIDTH)
            f1 = slice((2 * o + 1) * HY_WIDTH, (2 * o + 2) * HY_WIDTH)
            hr_ref[o, t] = xr[f0] + xr[f1]
            hi_ref[o, t] = xi[f0] - xi[f1]


def filter_spectrum(y, consts, kb):
    n1 = consts["n1"]
    nc = 2 * HY_ORDER * HY_WIDTH
    y4 = y.reshape(2, n1, nc, LANES)
    tw = pl.BlockSpec((kb, 1, LANES), lambda i: (i, 0, 0))
    out = pl.BlockSpec((HY_ORDER, kb, HY_WIDTH, LANES), lambda i: (0, i, 0, 0))
    shp = jax.ShapeDtypeStruct((HY_ORDER, n1, HY_WIDTH, LANES), F32)
    return pl.pallas_call(
        functools.partial(_spectrum_kernel, kb=kb),
        grid=(n1 // kb,),
        in_specs=[
            pl.BlockSpec((None, kb, nc, LANES), lambda i: (0, i, 0, 0)),
            pl.BlockSpec((None, kb, nc, LANES), lambda i: (1, i, 0, 0)),
            tw, tw,
            pl.BlockSpec((2 * LANES, 2 * LANES), lambda i: (0, 0)),
        ],
        out_specs=[out, out],
        out_shape=[shp, shp],
        compiler_params=_cparams("arbitrary"),
        name="filter_spectrum",
    )(y4, y4, consts["twc"], consts["tws"], consts["w2f"])


def _freq_kernel(yr_ref, yi_ref, twc_ref, tws_ref, w2f_ref, w2i_ref, hr_ref, hi_ref, zr_ref, zi_ref, *, kb):
    for t in range(kb):
        c, s = twc_ref[t], tws_ref[t]
        ypr, ypi = _twiddle_fwd(yr_ref[t], yi_ref[t], c, s)
        x = jnp.dot(jnp.concatenate([ypr, ypi], axis=1), w2f_ref[...], precision=HI, preferred_element_type=F32)
        xr, xi = x[:, :LANES], x[:, LANES:]
        hr, hi = hr_ref[t], hi_ref[t]
        gr = xr * hr - xi * hi
        gi = xr * hi + xi * hr
        z = jnp.dot(jnp.concatenate([gr, gi], axis=1), w2i_ref[...], precision=HI, preferred_element_type=F32)
        zr, zi = z[:, :LANES], z[:, LANES:]
        zr_ref[t] = zr * c - zi * s
        zi_ref[t] = zi * c + zr * s


def freq_multiply(y, hr, hi, order, consts, kb):
    n1 = consts["n1"]
    y4 = y.reshape(2, n1, HY_WIDTH, LANES)
    tw = pl.BlockSpec((kb, 1, LANES), lambda i: (i, 0, 0))
    blk = lambda p: pl.BlockSpec((None, kb, HY_WIDTH, LANES), lambda i: (p, i, 0, 0))
    wspec = pl.BlockSpec((2 * LANES, 2 * LANES), lambda i: (0, 0))
    out = pl.BlockSpec((kb, HY_WIDTH, LANES), lambda i: (i, 0, 0))
    shp = jax.ShapeDtypeStruct((n1, HY_WIDTH, LANES), F32)
    return pl.pallas_call(
        functools.partial(_freq_kernel, kb=kb),
        grid=(n1 // kb,),
        in_specs=[blk(0), blk(1), tw, tw, wspec, wspec, blk(order), blk(order)],
        out_specs=[out, out],
        out_shape=[shp, shp],
        compiler_params=_cparams("arbitrary"),
        name="freq_multiply",
    )(y4, y4, consts["twc"], consts["tws"], consts["w2f"], consts["w2i"], hr, hi)


def _short_conv_rows(u, prev_row, next_row, w_ref, b_ref):
    rows = lax.broadcasted_iota(jnp.int32, u.shape, 0)
    um1 = jnp.where(rows == 0, prev_row, pltpu.roll(u, 1, 0))
    up1 = jnp.where(rows == u.shape[0] - 1, next_row, pltpu.roll(u, u.shape[0] - 1, 0))
    return b_ref[...] + um1 * w_ref[0:1, :] + u * w_ref[1:2, :] + up1 * w_ref[2:3, :]


def _shortconv_t_kernel(u_ref, up_ref, un_ref, w_ref, b_ref, o_ref, *, n_tiles):
    i = pl.program_id(1)
    prev_row = jnp.where(i > 0, up_ref[SUBLANES - 1:SUBLANES, :], 0.0)
    next_row = jnp.where(i < n_tiles - 1, un_ref[0:1, :], 0.0)
    v = _short_conv_rows(u_ref[...], prev_row, next_row, w_ref, b_ref)
    for part in range(HY_ORDER + 1):
        for r in range(v.shape[0] // LANES):
            o_ref[part, r] = v[r * LANES:(r + 1) * LANES, part * HY_WIDTH:(part + 1) * HY_WIDTH].T


def short_conv_transposed(hy, conv_w, conv_b, *, n_batch, seq, tm):
    nt = seq // tm
    per8 = tm // SUBLANES
    a_rows = seq // LANES
    prev = lambda b, i: (jnp.maximum((b * nt + i) * per8 - 1, 0), 0)
    nxt = lambda b, i: (jnp.minimum((b * nt + i + 1) * per8, n_batch * nt * per8 - 1), 0)
    return pl.pallas_call(
        functools.partial(_shortconv_t_kernel, n_tiles=nt),
        grid=(n_batch, nt),
        in_specs=[
            pl.BlockSpec((tm, HY_COLS), lambda b, i: (b * nt + i, 0)),
            pl.BlockSpec((SUBLANES, HY_COLS), prev),
            pl.BlockSpec((SUBLANES, HY_COLS), nxt),
            pl.BlockSpec((3, HY_COLS), lambda b, i: (0, 0)),
            pl.BlockSpec((1, HY_COLS), lambda b, i: (0, 0)),
        ],
        out_specs=pl.BlockSpec((HY_ORDER + 1, None, tm // LANES, HY_WIDTH, LANES), lambda b, i: (0, b, i, 0, 0)),
        out_shape=jax.ShapeDtypeStruct((HY_ORDER + 1, n_batch, a_rows, HY_WIDTH, LANES), F32),
        compiler_params=_cparams("arbitrary", "arbitrary"),
        name="short_conv_transposed",
    )(hy, hy, hy, conv_w, conv_b)


def _untranspose_kernel(z_ref, o_ref):
    for r in range(z_ref.shape[0]):
        o_ref[r * LANES:(r + 1) * LANES, :] = z_ref[r].T.astype(BF16)


def untranspose(zt, *, n_batch, seq):
    a_rows = seq // LANES
    ta = min(SUBLANES, a_rows)
    nt = a_rows // ta
    return pl.pallas_call(
        _untranspose_kernel,
        grid=(n_batch, nt),
        in_specs=[pl.BlockSpec((None, ta, HY_WIDTH, LANES), lambda b, i: (b, i, 0, 0))],
        out_specs=pl.BlockSpec((ta * LANES, HY_WIDTH), lambda b, i: (b * nt + i, 0)),
        out_shape=jax.ShapeDtypeStruct((n_batch * seq, HY_WIDTH), BF16),
        compiler_params=_cparams("arbitrary", "arbitrary"),
        name="untranspose",
    )(zt)


def hyena_latent(hy, conv_w, conv_b, hy_bias, filt_t, consts, *, n_batch, seq):
    assert n_batch == 2, "the two batches are packed as real / imaginary parts of one DFT"
    n1, a_rows = consts["n1"], consts["a_rows"]
    cols = HY_WIDTH * LANES
    ct = min(cols, 2048)
    kb = min(n1, 4)
    parts = short_conv_transposed(hy, conv_w, conv_b, n_batch=n_batch, seq=seq, tm=min(seq, 1024))
    parts = parts.reshape(HY_ORDER + 1, n_batch * a_rows, cols)
    yf = left_matmul(consts["m1_real"], filt_t.reshape(a_rows, 2 * HY_ORDER * cols), ct)
    hr, hi = filter_spectrum(yf, consts, 1)
    z = parts[HY_ORDER]
    for o in range(HY_ORDER):
        y = left_matmul(consts["m1_data"], z, ct)
        zr, zi = freq_multiply(y, hr, hi, o, consts, kb)
        zcat = jnp.concatenate([zr.reshape(n1, cols), zi.reshape(n1, cols)], axis=0)
        bias_cols = jnp.repeat(hy_bias[o], LANES)[None, :]
        z = left_matmul_gate(consts["m3"], zcat, parts[o], z, bias_cols, ct)
    return untranspose(z.reshape(n_batch, a_rows, HY_WIDTH, LANES), n_batch=n_batch, seq=seq)


def _ctx_hyena_kernel(u_ref, w_ref, b_ref, h_ref, bias_ref, fc_ref, fs_ref, fct_ref, fst_ref, o_ref):
    u = u_ref[...]
    zero = jnp.zeros((1, u.shape[1]), F32)
    v = _short_conv_rows(u, zero, zero, w_ref, b_ref)
    z = v[:, HY_ORDER * HY_WIDTH:]
    mm = lambda a, b: jnp.dot(a, b, precision=HI, preferred_element_type=F32)
    fc, fs = fc_ref[...], fs_ref[...]
    inv_n = 1.0 / fc.shape[0]
    for o in range(HY_ORDER):
        h0 = h_ref[:, (2 * o) * HY_WIDTH:(2 * o + 1) * HY_WIDTH]
        h1 = h_ref[:, (2 * o + 1) * HY_WIDTH:(2 * o + 2) * HY_WIDTH]
        sr, si = mm(fc, h0 + h1), mm(fs, h1 - h0)
        zr, zi = mm(fc, z), -mm(fs, z)
        gr = zr * sr - zi * si
        gi = zr * si + zi * sr
        y = (mm(fct_ref[...], gr) - mm(fst_ref[...], gi)) * inv_n
        z = v[:, o * HY_WIDTH:(o + 1) * HY_WIDTH] * (y + bias_ref[o:o + 1, :] * z)
    o_ref[...] = z.astype(BF16)


def ctx_hyena(hy, conv_w, conv_b, filt, hy_bias, *, n_batch, seq, n_ctx):
    n = 2 * n_ctx
    k = np.arange(n, dtype=np.float64)[:, None]
    m = np.arange(n_ctx, dtype=np.float64)[None, :]
    ang = 2.0 * np.pi * k * m / n
    fc, fs = jnp.asarray(np.cos(ang), F32), jnp.asarray(np.sin(ang), F32)
    ctx0 = (n_batch * seq) // n_ctx
    const = lambda b: (0, 0)
    return pl.pallas_call(
        _ctx_hyena_kernel,
        grid=(n_batch,),
        in_specs=[
            pl.BlockSpec((n_ctx, HY_COLS), lambda b: (ctx0 + b, 0)),
            pl.BlockSpec((3, HY_COLS), const), pl.BlockSpec((1, HY_COLS), const),
            pl.BlockSpec((n_ctx, 2 * HY_ORDER * HY_WIDTH), const),
            pl.BlockSpec((HY_ORDER, HY_WIDTH), const),
            pl.BlockSpec((n, n_ctx), const), pl.BlockSpec((n, n_ctx), const),
            pl.BlockSpec((n_ctx, n), const), pl.BlockSpec((n_ctx, n), const),
        ],
        out_specs=pl.BlockSpec((n_ctx, HY_WIDTH), lambda b: (b, 0)),
        out_shape=jax.ShapeDtypeStruct((n_batch * n_ctx, HY_WIDTH), BF16),
        compiler_params=_cparams("arbitrary"),
        name="ctx_hyena",
    )(hy, conv_w, conv_b, filt, hy_bias, fc, fs, fc.T, fs.T)


def _outproj_kernel(x_ref, yh_ref, yw_ref, yd_ref, mod_ref, wo_ref, g_ref, wq_ref, xo_ref, h2_ref, qp_ref):
    y = jnp.dot(yh_ref[...], wo_ref[0:HY_WIDTH, :], preferred_element_type=F32)
    y = y + jnp.dot(yw_ref[...], wo_ref[HY_WIDTH:HY_WIDTH + WA_Q_COLS, :], preferred_element_type=F32)
    y = y + jnp.dot(yd_ref[...], wo_ref[HY_WIDTH + WA_Q_COLS:, :], preferred_element_type=F32)
    xn = x_ref[...] + mod_ref[2:3, :] * y
    xo_ref[...] = xn
    h2 = _rmsnorm_mod(xn, g_ref[...], mod_ref[3:4, :], mod_ref[4:5, :]).astype(BF16)
    h2_ref[...] = h2
    qp_ref[...] = jnp.dot(h2, wq_ref[...], preferred_element_type=F32)


def out_proj(x_all, y_hy, y_wa, y_da, mod, wo_bf, g2, wq_bf, *, n_rows, n_batch, seq, tm):
    d = x_all.shape[1]
    tpb = seq // tm
    grp = functools.partial(_group_of_tile, tiles_per_batch=tpb, n_batch=n_batch)
    row = lambda i: (i, 0)
    const = lambda i: (0, 0)
    mix = HY_WIDTH + WA_Q_COLS + DA_V_COLS
    return pl.pallas_call(
        _outproj_kernel,
        grid=(n_rows // tm,),
        in_specs=[
            pl.BlockSpec((tm, d), row), pl.BlockSpec((tm, HY_WIDTH), row), pl.BlockSpec((tm, WA_Q_COLS), row),
            pl.BlockSpec((tm, DA_V_COLS), row),
            pl.BlockSpec((None, N_MOD, d), lambda i: (grp(i), 0, 0)),
            pl.BlockSpec((mix, d), const), pl.BlockSpec((1, d), const), pl.BlockSpec((d, PEER_QCOLS), const),
        ],
        out_specs=[pl.BlockSpec((tm, d), row), pl.BlockSpec((tm, d), row), pl.BlockSpec((tm, PEER_QCOLS), row)],
        out_shape=[jax.ShapeDtypeStruct((n_rows, d), F32), jax.ShapeDtypeStruct((n_rows, d), BF16),
                   jax.ShapeDtypeStruct((n_rows, PEER_QCOLS), F32)],
        compiler_params=_cparams("arbitrary"),
        name="out_proj",
    )(x_all, y_hy, y_wa, y_da, mod, wo_bf, g2, wq_bf)


def _sort_pairs(n):
    pairs = []

    def merge(lo, cnt, r):
        step = r * 2
        if step < cnt:
            merge(lo, cnt, step)
            merge(lo + r, cnt, step)
            for i in range(lo + r, lo + cnt - r, step):
                pairs.append((i, i + r))
        else:
            pairs.append((lo, lo + r))

    def sort(lo, cnt):
        if cnt > 1:
            half = cnt // 2
            sort(lo, half)
            sort(lo + half, half)
            merge(lo, cnt, 1)

    sort(0, n)
    return pairs


def _sort_desc(vals):
    n = 1
    while n < len(vals):
        n *= 2
    v = list(vals) + [None] * (n - len(vals))
    for i, j in _sort_pairs(n):
        a, b = v[i], v[j]
        if b is None:
            continue
        if a is None:
            v[i], v[j] = b, None
        else:
            v[i], v[j] = jnp.maximum(a, b), jnp.minimum(a, b)
    return v[:len(vals)]


def _top16_rows(s):
    k = PEER_TOPK
    v = _sort_desc([s[SUBLANES * r:SUBLANES * (r + 1), :] for r in range(s.shape[0] // SUBLANES)])
    for shift in (4, 2, 1):
        other = [pltpu.roll(x, shift, 0) for x in v]
        v = [jnp.maximum(v[i], other[k - 1 - i]) for i in range(k)]
        d = k // 2
        while d >= 1:
            for i in range(k):
                if (i & d) == 0:
                    a, b = v[i], v[i + d]
                    v[i], v[i + d] = jnp.maximum(a, b), jnp.minimum(a, b)
            d //= 2
    return v


def _peer_topk_kernel(q_ref, keys_ref, cnt_ref, e1_ref, rank_ref, e2_ref):
    k = PEER_TOPK
    tmk = q_ref.shape[0]
    sub = lax.broadcasted_iota(jnp.int32, (SUBLANES, tmk), 0)
    scores, tops = [], []
    for hp in range(2 * PEER_HEADS):
        s = lax.dot_general(keys_ref[hp], q_ref[:, hp * N_KEYS:(hp + 1) * N_KEYS], (((1,), (1,)), ((), ())),
                            precision=HI, preferred_element_type=F32)
        scores.append(s)
        top = _top16_rows(s)
        top.append(jnp.max(jnp.where(s < top[k - 1][0:1, :], s, NEG_BIG), axis=0, keepdims=True))
        tops.append(top)
    packed = []
    for p in range(2):
        lst = []
        for r in range(k + 1):
            acc = jnp.broadcast_to(tops[p][r], (SUBLANES, tmk))
            for h in range(1, PEER_HEADS):
                acc = jnp.where(sub == h, tops[2 * h + p][r], acc)
            lst.append(acc)
        packed.append(lst)
    cand = [packed[0][i] + packed[1][j] for i in range(k) for j in range(k) if (i + 1) * (j + 1) <= k]
    cs = _sort_desc(cand)
    c17 = jnp.maximum(cs[k], jnp.maximum(packed[0][k] + packed[1][0], packed[0][0] + packed[1][k]))
    tau = 0.5 * (cs[k - 1] + c17)
    zsum = jnp.zeros_like(tau)
    for r in range(k):
        zsum = zsum + jnp.exp(cs[r] - cs[0])
    inv_z = 1.0 / zsum
    for h in range(PEER_HEADS):
        s1, s2 = scores[2 * h], scores[2 * h + 1]
        row = lambda x: x[h:h + 1, :]
        top2 = [t2[0:1, :] for t2 in tops[2 * h + 1][:k]]
        cnt_ref[h] = _count_sorted(top2, row(tau) - s1, strict=False)
        tiled = lambda x: x.astype(BF16).reshape(N_KEYS // BF16_ROWS, BF16_ROWS, tmk)
        rank_ref[h] = tiled(_count_sorted(top2, s2, strict=True))
        e1_ref[h] = 0.5 * jnp.exp(s1 - row(packed[0][0]))
        e2_ref[h] = tiled(jnp.exp(s2 - row(packed[1][0])) * row(inv_z))


def _count_sorted(tops, x, *, strict):
    ge = (lambda a: a > x) if strict else (lambda a: a >= x)
    b3 = ge(tops[7])
    b2 = ge(jnp.where(b3, tops[11], tops[3]))
    b1 = ge(jnp.where(b3, jnp.where(b2, tops[13], tops[9]), jnp.where(b2, tops[5], tops[1])))
    hi = jnp.where(b2, jnp.where(b1, tops[14], tops[12]), jnp.where(b1, tops[10], tops[8]))
    lo = jnp.where(b2, jnp.where(b1, tops[6], tops[4]), jnp.where(b1, tops[2], tops[0]))
    b0 = ge(jnp.where(b3, hi, lo))
    cnt = (jnp.where(b3, 8.0, 0.0) + jnp.where(b2, 4.0, 0.0)) + (jnp.where(b1, 2.0, 0.0) + jnp.where(b0, 1.0, 0.0))
    return jnp.where(ge(tops[15]), 16.0, cnt)


def peer_topk(qp, keys, *, tmk):
    t = qp.shape[0]
    tiles = N_KEYS // BF16_ROWS
    out = pl.BlockSpec((PEER_HEADS, N_KEYS, tmk), lambda i: (0, 0, i))
    out16 = pl.BlockSpec((PEER_HEADS, tiles, BF16_ROWS, tmk), lambda i: (0, 0, 0, i))
    f32 = jax.ShapeDtypeStruct((PEER_HEADS, N_KEYS, t), F32)
    b16 = jax.ShapeDtypeStruct((PEER_HEADS, tiles, BF16_ROWS, t), BF16)
    return pl.pallas_call(
        _peer_topk_kernel,
        grid=(t // tmk,),
        in_specs=[pl.BlockSpec((tmk, PEER_QCOLS), lambda i: (i, 0)),
                  pl.BlockSpec((2 * PEER_HEADS, N_KEYS, PEER_QDIM // 2), lambda i: (0, 0, 0))],
        out_specs=[out, out, out16, out16],
        out_shape=[f32, f32, b16, b16],
        compiler_params=_cparams("arbitrary"),
        name="peer_topk",
    )(qp, keys)


def _peer_dense_kernel(x_ref, h_ref, mod_ref, u_ref, vt_ref, cnt_ref, e1_ref, rank_ref, e2_ref, g_ref, o_ref,
                       a_scr, w_scr, acc_scr, *, final_norm, chunk):
    e = pl.program_id(1)

    @pl.when(e == 0)
    def _():
        acc_scr[...] = jnp.zeros(acc_scr.shape, F32)

    n_chunks = u_ref.shape[0] // chunk
    per_chunk = chunk // N_KEYS
    sqrt_half = math.sqrt(0.5)
    tm = h_ref.shape[0]
    tiles = N_KEYS // BF16_ROWS
    zero = jnp.zeros((), BF16)

    def pre_activations(p):
        rows = slice(p * chunk, (p + 1) * chunk)
        a_scr[rows, :] = lax.dot_general(u_ref[rows, :], h_ref[...], (((1,), (1,)), ((), ())),
                                         preferred_element_type=F32)

    pre_activations(0)
    for p in range(n_chunks):
        if p + 1 < n_chunks:
            pre_activations(p + 1)
        for ii in range(p * per_chunk, (p + 1) * per_chunk):
            rows = slice(ii * N_KEYS, (ii + 1) * N_KEYS)
            a = a_scr[rows, :]
            act = a * (1.0 + lax.erf(a * sqrt_half))
            act = act.astype(BF16).reshape(tiles, BF16_ROWS, tm)
            gate = None
            for h in range(PEER_HEADS):
                cnt = jnp.broadcast_to(cnt_ref[h, ii:ii + 1, :], (BF16_ROWS, tm)).astype(BF16)
                e1 = jnp.broadcast_to(e1_ref[h, ii:ii + 1, :], (BF16_ROWS, tm)).astype(BF16)
                term = jnp.where(rank_ref[h] < cnt[None], e2_ref[h] * e1[None], zero)
                gate = term if gate is None else gate + term
            w_scr[rows, :] = (gate * act).reshape(N_KEYS, tm)
        rows = slice(p * chunk, (p + 1) * chunk)
        acc_scr[...] += jnp.dot(vt_ref[:, rows], w_scr[rows, :], preferred_element_type=F32)

    @pl.when(e == pl.num_programs(1) - 1)
    def _():
        xn = x_ref[...] + mod_ref[5:6, :] * acc_scr[...].T
        if final_norm:
            ms = jnp.mean(xn * xn, axis=-1, keepdims=True)
            xn = xn * lax.rsqrt(ms + NORM_EPS) * g_ref[...]
        o_ref[...] = xn


def peer_dense(x, h2, mod, u_bf, vt_bf, cnt, e1, rank, e2, g_final, *, n_batch, seq, tm, te, chunk, final_norm):
    n_rows, d = x.shape
    tpb = seq // tm
    grp = functools.partial(_group_of_tile, tiles_per_batch=tpb, n_batch=n_batch)
    n_i = te // N_KEYS
    tok = lambda i, e: (i, 0)
    rows_i = pl.BlockSpec((PEER_HEADS, n_i, tm), lambda i, e: (0, e, i))
    tiles = N_KEYS // BF16_ROWS
    full = pl.BlockSpec((PEER_HEADS, tiles, BF16_ROWS, tm), lambda i, e: (0, 0, 0, i))
    return pl.pallas_call(
        functools.partial(_peer_dense_kernel, final_norm=final_norm, chunk=chunk),
        grid=(n_rows // tm, N_EXPERTS // te),
        in_specs=[
            pl.BlockSpec((tm, d), tok), pl.BlockSpec((tm, d), tok),
            pl.BlockSpec((None, N_MOD, d), lambda i, e: (grp(i), 0, 0)),
            pl.BlockSpec((te, d), lambda i, e: (e, 0)),
            pl.BlockSpec((d, te), lambda i, e: (0, e)),
            rows_i, rows_i, full, full,
            pl.BlockSpec((1, d), lambda i, e: (0, 0)),
        ],
        out_specs=pl.BlockSpec((tm, d), tok),
        out_shape=jax.ShapeDtypeStruct((n_rows, d), F32),
        scratch_shapes=[pltpu.VMEM((te, tm), F32), pltpu.VMEM((te, tm), BF16), pltpu.VMEM((d, tm), F32)],
        compiler_params=_cparams("arbitrary", "arbitrary"),
        name="peer_dense",
    )(x, h2, mod, u_bf, vt_bf, cnt, e1, rank, e2, g_final)


def kernel(x, c, ctx, c_ctx, w_mod, b_mod, norm1_g, w_in, hy_conv_w, hy_conv_b, hy_w1, hy_b1, hy_freq, hy_w2,
           hy_b2, hy_w3, hy_decay, hy_bias, wa_sink, da_lq1, da_lk1, da_lq2, da_lk2, da_norm_g, w_out, norm2_g,
           peer_wq, peer_keys, peer_u, peer_v, final_g):
    n_batch, seq, d = x.shape
    n_ctx = ctx.shape[1]
    depth = w_in.shape[0]
    n_lat = n_batch * seq
    tm = 512 if seq % 512 == 0 else 256
    assert seq % tm == 0 and (n_batch * n_ctx) % tm == 0 and n_lat % n_ctx == 0 and n_batch + 1 <= SUBLANES

    cpad = jnp.zeros((SUBLANES, d), F32).at[:n_batch].set(c).at[n_batch].set(c_ctx)
    mod_all = mod_vectors(cpad, w_mod, b_mod).reshape(depth, SUBLANES, N_MOD, d)
    rope_wa = rope_tables(seq, HEAD_DIM, tm)
    rope_da = rope_tables(seq, DA_QK_DIM, tm)
    consts = dft_constants(seq)
    feat_lat = hyena_filter_features(seq)
    feat_ctx = hyena_filter_features(n_ctx)
    x_all = jnp.concatenate([x.reshape(n_lat, d), ctx.reshape(n_batch * n_ctx, d)], axis=0)

    out = None
    for li in range(depth):
        last = li == depth - 1
        lam_init = DA_LAMBDA_BASE - DA_LAMBDA_AMP * math.exp(-DA_LAMBDA_RATE * li)
        mod = mod_all[li]
        hy, qw, kw, vw, qd, kd, vd, vdt, vwt = proj_in(x_all, mod, norm1_g[li][None, :], w_in[li].astype(BF16), rope_wa,
                                             rope_da, n_batch=n_batch, seq=seq, tm=tm)
        w1p = jnp.pad(hy_w1[li], ((0, LANES - HY_EMB_DIM), (0, 0)))
        fargs = (w1p, hy_b1[li][None, :], hy_freq[li][None, :], hy_w2[li], hy_b2[li][None, :], hy_w3[li],
                 hy_decay[li].reshape(1, -1))
        conv_b = hy_conv_b[li][None, :]
        filt_t = hyena_filters(feat_lat, *fargs, transposed=True)
        y_hy = hyena_latent(hy, hy_conv_w[li], conv_b, hy_bias[li], filt_t, consts, n_batch=n_batch, seq=seq)
        y_wa = window_attention(wa_sink[li], qw, kw, vwt, n_batch=n_batch, seq=seq, n_ctx=n_ctx)
        lq = jnp.stack([da_lq1[li], da_lk1[li], da_lq2[li], da_lk2[li]], axis=0)
        g128 = jnp.tile(da_norm_g[li], LANES // DA_V_DIM)[None, :]
        tq = min(seq, 512)
        g_rows = jnp.broadcast_to(da_norm_g[li][:, None], (DA_V_DIM, tq))
        y_da = diff_attention(lq, g_rows, qd, kd, vdt, n_batch=n_batch, seq=seq, n_ctx=n_ctx, lam_init=lam_init,
                              tq=tq, tk=min(seq, 256), ahead=3, unroll=4 if seq % 1024 == 0 else 1)
        if not last:
            filt_c = hyena_filters(feat_ctx, *fargs, transposed=False)
            yc_hy = ctx_hyena(hy, hy_conv_w[li], conv_b, filt_c, hy_bias[li], n_batch=n_batch, seq=seq, n_ctx=n_ctx)
            yc_wa, yc_da = ctx_attention(wa_sink[li], lq, g128, qw, kw, vw, qd, kd, vd, n_batch=n_batch, seq=seq,
                                         n_ctx=n_ctx, lam_init=lam_init)
            y_hy = jnp.concatenate([y_hy, yc_hy], axis=0)
            y_wa = jnp.concatenate([y_wa, yc_wa], axis=0)
            y_da = jnp.concatenate([y_da, yc_da], axis=0)
        n_rows = n_lat if last else x_all.shape[0]
        x_mid, h2, qp = out_proj(x_all, y_hy, y_wa, y_da, mod, w_out[li].astype(BF16), norm2_g[li][None, :],
                                 peer_wq[li].astype(BF16), n_rows=n_rows, n_batch=n_batch, seq=seq, tm=tm)
        keys = peer_keys[li].reshape(2 * PEER_HEADS, N_KEYS, PEER_QDIM // 2)
        cnt, e1, rank, e2 = peer_topk(qp, keys, tmk=256)
        x_all = peer_dense(x_mid, h2, mod, peer_u[li].astype(BF16), peer_v[li].T.astype(BF16), cnt, e1, rank, e2,
                           final_g[None, :], n_batch=n_batch, seq=seq, tm=tm, te=2048, chunk=1024, final_norm=last)
        out = x_all
    return out.reshape(n_batch, seq, d)
```

```python
import functools
import math

import numpy as np
import jax
import jax.numpy as jnp
from jax import lax
from jax.experimental import pallas as pl
from jax.experimental.pallas import tpu as pltpu

F32 = jnp.float32
BF16 = jnp.bfloat16
HI = lax.Precision.HIGHEST

LANES = 128
SUBLANES = 8
BF16_ROWS = 16
VMEM_LIMIT_BYTES = 56 * 1024 * 1024

D_MODEL = 1024
N_MOD = 6
NORM_EPS = 1e-6
ROPE_BASE = 10000.0
GRID_W = 64
BLOCK = 128
HEAD_DIM = 64
HY_WIDTH = 256
HY_ORDER = 2
HY_BANDS = 16
HY_EMB_DIM = 1 + 2 * HY_BANDS
HY_FILTER_WIDTH = 64
HY_COLS = 3 * HY_WIDTH
WA_Q_HEADS = 8
WA_KV_HEADS = 2
WA_GROUP = 4
WINDOW = 128
WA_Q_COLS = WA_Q_HEADS * HEAD_DIM
WA_KV_COLS = WA_KV_HEADS * HEAD_DIM
WA_COLS = WA_Q_COLS + 2 * WA_KV_COLS
DA_HEADS = 4
DA_QK_DIM = 32
DA_V_DIM = 64
DA_QK_COLS = DA_HEADS * 2 * DA_QK_DIM
DA_V_COLS = DA_HEADS * DA_V_DIM
DA_COLS = 2 * DA_QK_COLS + DA_V_COLS
DA_LAMBDA_BASE = 0.8
DA_LAMBDA_AMP = 0.6
DA_LAMBDA_RATE = 0.3
IN_COLS = HY_COLS + WA_COLS + DA_COLS
PEER_HEADS = 8
N_KEYS = 128
N_EXPERTS = N_KEYS * N_KEYS
PEER_TOPK = 16
PEER_QDIM = 256
PEER_QCOLS = PEER_HEADS * PEER_QDIM
NEG_BIG = -1e30
LOG2E = math.log2(math.e)


def _cparams(*sem):
    return pltpu.CompilerParams(dimension_semantics=sem, vmem_limit_bytes=VMEM_LIMIT_BYTES)


def _rmsnorm_mod(x, g, shift, scale):
    ms = jnp.mean(x * x, axis=-1, keepdims=True)
    return (x * lax.rsqrt(ms + NORM_EPS) * g) * (1.0 + scale) + shift


def _mod_kernel(c_ref, w_ref, b_ref, o_ref):
    c = c_ref[...]
    act = c * jax.nn.sigmoid(c)
    o_ref[...] = jnp.dot(act, w_ref[...], precision=HI, preferred_element_type=F32) + b_ref[...]


def mod_vectors(cpad, w_mod, b_mod):
    n_layers, d, n = w_mod.shape
    tn = 1536
    return pl.pallas_call(
        _mod_kernel,
        grid=(n_layers, n // tn),
        in_specs=[
            pl.BlockSpec((SUBLANES, d), lambda l, j: (0, 0)),
            pl.BlockSpec((None, d, tn), lambda l, j: (l, 0, j)),
            pl.BlockSpec((None, 1, tn), lambda l, j: (l, 0, j)),
        ],
        out_specs=pl.BlockSpec((None, SUBLANES, tn), lambda l, j: (l, 0, j)),
        out_shape=jax.ShapeDtypeStruct((n_layers, SUBLANES, n), F32),
        compiler_params=_cparams("arbitrary", "arbitrary"),
        name="mod_vectors",
    )(cpad, w_mod, b_mod.reshape(n_layers, 1, n))


def _rope128(x, cos, sin_a, sin_b, m):
    return x * cos + pltpu.roll(x, LANES - m, 1) * sin_a + pltpu.roll(x, m, 1) * sin_b


def _proj_in_kernel(x_ref, mod_ref, g_ref, w_ref, cw_ref, saw_ref, sbw_ref, cd_ref, sad_ref, sbd_ref,
                    hy_ref, qw_ref, kw_ref, vw_ref, qd_ref, kd_ref, vd_ref, vdt_ref, vwt_ref):
    h = _rmsnorm_mod(x_ref[...], g_ref[...], mod_ref[0:1, :], mod_ref[1:2, :]).astype(BF16)
    hy_ref[...] = jnp.dot(h, w_ref[:, 0:HY_COLS], preferred_element_type=F32)
    wa = jnp.dot(h, w_ref[:, HY_COLS:HY_COLS + WA_COLS], preferred_element_type=F32)
    da = jnp.dot(h, w_ref[:, HY_COLS + WA_COLS:IN_COLS], preferred_element_type=F32)
    cw, saw, sbw = cw_ref[...], saw_ref[...], sbw_ref[...]
    cd, sad, sbd = cd_ref[...], sad_ref[...], sbd_ref[...]
    wa_scale = HEAD_DIM ** -0.5
    da_scale = DA_QK_DIM ** -0.5 * LOG2E
    for kb in range(WA_Q_COLS // LANES):
        blk = wa[:, kb * LANES:(kb + 1) * LANES]
        qw_ref[:, kb * LANES:(kb + 1) * LANES] = (_rope128(blk, cw, saw, sbw, 16) * wa_scale).astype(BF16)
    kw_ref[...] = _rope128(wa[:, WA_Q_COLS:WA_Q_COLS + WA_KV_COLS], cw, saw, sbw, 16).astype(BF16)
    vw = wa[:, WA_Q_COLS + WA_KV_COLS:WA_COLS]
    vw_ref[...] = vw.astype(BF16)
    vwt_ref[...] = vw.T.astype(BF16)
    for kb in range(DA_QK_COLS // LANES):
        blk = da[:, kb * LANES:(kb + 1) * LANES]
        qd_ref[:, kb * LANES:(kb + 1) * LANES] = (_rope128(blk, cd, sad, sbd, 8) * da_scale).astype(BF16)
        blk = da[:, DA_QK_COLS + kb * LANES:DA_QK_COLS + (kb + 1) * LANES]
        kd_ref[:, kb * LANES:(kb + 1) * LANES] = _rope128(blk, cd, sad, sbd, 8).astype(BF16)
    vd = da[:, 2 * DA_QK_COLS:DA_COLS]
    vd_ref[...] = vd.astype(BF16)
    vdt_ref[...] = vd.T.astype(BF16)


def _group_of_tile(i, tiles_per_batch, n_batch):
    return jnp.minimum(i // tiles_per_batch, n_batch)


def proj_in(x_all, mod, g, w_bf, rope_wa, rope_da, *, n_batch, seq, tm):
    t_all, d = x_all.shape
    tpb = seq // tm
    grp = functools.partial(_group_of_tile, tiles_per_batch=tpb, n_batch=n_batch)
    pos = lambda i: (jnp.where(i < n_batch * tpb, i % tpb, tpb), 0)
    row = lambda i: (i, 0)
    const = lambda i: (0, 0)
    tab = pl.BlockSpec((tm, LANES), pos)
    outs = [(HY_COLS, F32), (WA_Q_COLS, BF16), (WA_KV_COLS, BF16), (WA_KV_COLS, BF16),
            (DA_QK_COLS, BF16), (DA_QK_COLS, BF16), (DA_V_COLS, BF16)]
    return pl.pallas_call(
        _proj_in_kernel,
        grid=(t_all // tm,),
        in_specs=[
            pl.BlockSpec((tm, d), row),
            pl.BlockSpec((None, N_MOD, d), lambda i: (grp(i), 0, 0)),
            pl.BlockSpec((1, d), const),
            pl.BlockSpec((d, IN_COLS), const),
            tab, tab, tab, tab, tab, tab,
        ],
        out_specs=[pl.BlockSpec((tm, n), row) for n, _ in outs]
        + [pl.BlockSpec((DA_V_COLS, tm), lambda i: (0, i)), pl.BlockSpec((WA_KV_COLS, tm), lambda i: (0, i))],
        out_shape=[jax.ShapeDtypeStruct((t_all, n), dt) for n, dt in outs]
        + [jax.ShapeDtypeStruct((DA_V_COLS, t_all), BF16), jax.ShapeDtypeStruct((WA_KV_COLS, t_all), BF16)],
        compiler_params=_cparams("arbitrary"),
        name="proj_in",
    )(x_all, mod, g, w_bf, *rope_wa, *rope_da)


def rope_tables(seq, head_dim, pad_rows):
    half = head_dim // 2
    axis_dim = half
    m = axis_dim // 2
    t = np.arange(seq)
    rowcol = np.stack([t // GRID_W, t % GRID_W], axis=0).astype(np.float32)
    inv_freq = (ROPE_BASE ** (-jnp.arange(0, axis_dim, 2, dtype=F32) / axis_dim))
    ang = jnp.asarray(rowcol)[:, :, None] * inv_freq[None, None, :]
    cos, sin = jnp.cos(ang), jnp.sin(ang)
    lane = np.arange(LANES)
    dd = lane % head_dim
    axis = dd // half
    sub = dd % half
    idx = sub % m
    first = sub < m
    cos_t = cos[axis, :, idx].T
    sin_t = sin[axis, :, idx].T
    sin_a = jnp.where(first[None, :], -sin_t, 0.0)
    sin_b = jnp.where(first[None, :], 0.0, sin_t)
    pad1 = jnp.ones((pad_rows, LANES), F32)
    pad0 = jnp.zeros((pad_rows, LANES), F32)
    return (jnp.concatenate([cos_t, pad1], 0), jnp.concatenate([sin_a, pad0], 0),
            jnp.concatenate([sin_b, pad0], 0))


def _softmax_sink_attend(q, k, v, sink, valid):
    s = lax.dot_general(q, k, (((1,), (1,)), ((), ())), preferred_element_type=F32)
    if valid is not None:
        nm = valid.shape[1]
        s = jnp.concatenate([jnp.where(valid, s[:, :nm], NEG_BIG), s[:, nm:]], axis=1)
    m = jnp.maximum(jnp.max(s, axis=1, keepdims=True), sink)
    p = jnp.exp(s - m)
    den = jnp.sum(p, axis=1, keepdims=True) + jnp.exp(sink - m)
    return jnp.dot(p.astype(BF16), v, preferred_element_type=F32) / den


def _wattn_kernel(sink_ref, q_ref, kp_ref, kc_ref, kn_ref, kx_ref, vp_ref, vc_ref, vn_ref, vx_ref, o_ref, qm_scr,
                  *, n_pairs, ahead):
    m_idx = pl.program_id(1)
    nq = 2 * BLOCK
    n_loc = 4 * BLOCK
    kk = lax.broadcasted_iota(jnp.int32, (n_loc, nq), 0)
    qq = lax.broadcasted_iota(jnp.int32, (n_loc, nq), 1)
    lo = jnp.where(m_idx > 0, 0, BLOCK)
    hi = jnp.where(m_idx < n_pairs - 1, n_loc, n_loc - BLOCK)
    valid = (kk >= jnp.maximum(qq, lo)) & (kk <= qq + 2 * WINDOW) & (kk < hi)
    k_all = jnp.concatenate([kp_ref[...], kc_ref[...], kn_ref[...], kx_ref[...]], axis=0)
    vt_all = jnp.concatenate([vp_ref[...], vc_ref[...], vn_ref[...], vx_ref[...]], axis=1)
    lane = lax.broadcasted_iota(jnp.int32, (nq, LANES), 1)
    for h in range(WA_Q_HEADS):
        hk = h // WA_GROUP
        blk = q_ref[:, (h // 2) * LANES:(h // 2 + 1) * LANES]
        if h % 2 != hk:
            blk = jnp.concatenate([blk[:, HEAD_DIM:], blk[:, :HEAD_DIM]], axis=1)
        keep = (lane >= hk * HEAD_DIM) & (lane < (hk + 1) * HEAD_DIM)
        qm_scr[h] = jnp.where(keep, blk, jnp.zeros_like(blk))

    def scores(h):
        return lax.dot_general(k_all, qm_scr[h], (((1,), (1,)), ((), ())), preferred_element_type=F32)

    sts = [scores(h) for h in range(ahead)]
    outs = []
    for h in range(WA_Q_HEADS):
        hk = h // WA_GROUP
        if h + ahead < WA_Q_HEADS:
            sts.append(scores(h + ahead))
        st = sts[h]
        sts[h] = None
        st = jnp.concatenate([jnp.where(valid, st[:n_loc], NEG_BIG), st[n_loc:]], axis=0)
        sink = sink_ref[h]
        m = jnp.maximum(jnp.max(st, axis=0, keepdims=True), sink)
        p = jnp.exp(st - m)
        den = jnp.sum(p, axis=0, keepdims=True) + jnp.exp(sink - m)
        o = jnp.dot(vt_all[hk * HEAD_DIM:(hk + 1) * HEAD_DIM, :], p.astype(BF16), preferred_element_type=F32)
        outs.append(o / den)
    o_ref[...] = jnp.concatenate(outs, axis=0).T.astype(BF16)


def window_attention(sink, qw, kw, vwt, *, n_batch, seq, n_ctx):
    nb = seq // BLOCK
    n_pairs = nb // 2
    assert nb % 2 == 0
    ctx0 = (n_batch * seq) // n_ctx
    cur = lambda b, m: (b * n_pairs + m, 0)
    prev = lambda b, m: (b * nb + jnp.maximum(2 * m - 1, 0), 0)
    nxt = lambda b, m: (b * nb + jnp.minimum(2 * m + 2, nb - 1), 0)
    ctx = lambda b, m: (ctx0 + b, 0)
    swap = lambda f: (lambda b, m: f(b, m)[::-1])
    return pl.pallas_call(
        functools.partial(_wattn_kernel, n_pairs=n_pairs, ahead=3),
        grid=(n_batch, n_pairs),
        in_specs=[
            pl.BlockSpec(memory_space=pltpu.SMEM),
            pl.BlockSpec((2 * BLOCK, WA_Q_COLS), cur),
            pl.BlockSpec((BLOCK, WA_KV_COLS), prev), pl.BlockSpec((2 * BLOCK, WA_KV_COLS), cur),
            pl.BlockSpec((BLOCK, WA_KV_COLS), nxt), pl.BlockSpec((n_ctx, WA_KV_COLS), ctx),
            pl.BlockSpec((WA_KV_COLS, BLOCK), swap(prev)), pl.BlockSpec((WA_KV_COLS, 2 * BLOCK), swap(cur)),
            pl.BlockSpec((WA_KV_COLS, BLOCK), swap(nxt)), pl.BlockSpec((WA_KV_COLS, n_ctx), swap(ctx)),
        ],
        out_specs=pl.BlockSpec((2 * BLOCK, WA_Q_COLS), cur),
        out_shape=jax.ShapeDtypeStruct((n_batch * seq, WA_Q_COLS), BF16),
        scratch_shapes=[pltpu.VMEM((WA_Q_HEADS, 2 * BLOCK, LANES), BF16)],
        compiler_params=_cparams("arbitrary", "arbitrary"),
        name="window_attention",
    )(sink, qw, kw, kw, kw, kw, vwt, vwt, vwt, vwt)


def _diff_lambda(lq_ref, lam_init):
    a = jnp.sum(lq_ref[0:1, :] * lq_ref[1:2, :], axis=1, keepdims=True)
    b = jnp.sum(lq_ref[2:3, :] * lq_ref[3:4, :], axis=1, keepdims=True)
    return jnp.exp(a) - jnp.exp(b) + lam_init


def _diff_finish(acc1, l1, acc2, l2, lam, g128, lam_init):
    lane = lax.broadcasted_iota(jnp.int32, acc1[0].shape, 1)
    lo = lane < DA_V_DIM
    blocks = []
    for vb in range(DA_HEADS // 2):
        o = []
        for h in (2 * vb, 2 * vb + 1):
            o.append(acc1[h] / l1[h] - lam * (acc2[h] / l2[h]))
        blk = jnp.where(lo, o[0], o[1])
        sq = blk * blk
        s_lo = jnp.sum(jnp.where(lo, sq, 0.0), axis=1, keepdims=True)
        s_hi = jnp.sum(jnp.where(lo, 0.0, sq), axis=1, keepdims=True)
        ms = jnp.where(lo, s_lo, s_hi) * (1.0 / DA_V_DIM)
        blocks.append(blk * lax.rsqrt(ms + NORM_EPS) * g128 * (1.0 - lam_init))
    return jnp.concatenate(blocks, axis=1)


def _masked_q(q_ref, hm):
    kb, sub = divmod(hm, LANES // DA_QK_DIM)
    blk = q_ref[:, kb * LANES:(kb + 1) * LANES]
    lane = lax.broadcasted_iota(jnp.int32, blk.shape, 1)
    keep = (lane >= sub * DA_QK_DIM) & (lane < (sub + 1) * DA_QK_DIM)
    return jnp.where(keep, blk, jnp.zeros_like(blk))


def _dattn_kernel(lq_ref, g_ref, q_ref, k_ref, vt_ref, kx_ref, vxt_ref, o_ref, qm_scr, m_scr, l_scr, acc_scr,
                  *, lam_init, tk, ahead, unroll):
    n_hm = 2 * DA_HEADS
    per_blk = LANES // DA_QK_DIM
    for hm in range(n_hm):
        qm_scr[hm] = _masked_q(q_ref, hm)
    m_scr[...] = jnp.full(m_scr.shape, NEG_BIG, F32)
    l_scr[...] = jnp.zeros(l_scr.shape, F32)
    acc_scr[...] = jnp.zeros(acc_scr.shape, F32)

    def update(chunks):
        chains = [(c, hm) for c in range(len(chunks)) for hm in range(n_hm)]

        def scores(chain):
            c, hm = chain
            kb = hm // per_blk
            return lax.dot_general(chunks[c][0][:, kb * LANES:(kb + 1) * LANES], qm_scr[hm],
                                   (((1,), (1,)), ((), ())), preferred_element_type=F32)

        sts = [scores(ch) for ch in chains[:ahead]]
        for n, (c, hm) in enumerate(chains):
            h = hm // 2
            vtc = chunks[c][1]
            if n + ahead < len(chains):
                sts.append(scores(chains[n + ahead]))
            st = sts[n]
            sts[n] = None
            m_old = m_scr[hm]
            m_new = jnp.maximum(m_old, jnp.max(st, axis=0, keepdims=True))
            alpha = jnp.exp2(m_old - m_new)
            pt = jnp.exp2(st - m_new)
            l_scr[hm] = alpha * l_scr[hm] + jnp.sum(pt, axis=0, keepdims=True)
            acc_scr[hm] = alpha * acc_scr[hm] + jnp.dot(vtc[h * DA_V_DIM:(h + 1) * DA_V_DIM, :], pt.astype(BF16),
                                                        preferred_element_type=F32)
            m_scr[hm] = m_new

    def body(j, carry):
        chunks = []
        for c in range(unroll):
            start = pl.multiple_of((j * unroll + c) * tk, tk)
            chunks.append((k_ref[pl.ds(start, tk), :], vt_ref[:, pl.ds(start, tk)]))
        update(chunks)
        return carry

    lax.fori_loop(0, k_ref.shape[0] // (tk * unroll), body, 0)
    n_x = kx_ref.shape[0] // tk
    update([(kx_ref[c * tk:(c + 1) * tk, :], vxt_ref[:, c * tk:(c + 1) * tk]) for c in range(n_x)])
    lam = _diff_lambda(lq_ref, lam_init)
    heads = []
    for h in range(DA_HEADS):
        o = acc_scr[2 * h] / l_scr[2 * h] - lam * (acc_scr[2 * h + 1] / l_scr[2 * h + 1])
        ms = jnp.mean(o * o, axis=0, keepdims=True)
        heads.append(o * lax.rsqrt(ms + NORM_EPS) * g_ref[...] * (1.0 - lam_init))
    o_ref[...] = jnp.concatenate(heads, axis=0).T.astype(BF16)


def diff_attention(lq, g_rows, qd, kd, vdt, *, n_batch, seq, n_ctx, lam_init, tq, tk, ahead, unroll):
    nq = seq // tq
    ctx0 = (n_batch * seq) // n_ctx
    n_hm = 2 * DA_HEADS
    return pl.pallas_call(
        functools.partial(_dattn_kernel, lam_init=lam_init, tk=tk, ahead=ahead, unroll=unroll),
        grid=(n_batch, nq),
        in_specs=[
            pl.BlockSpec((4, DA_QK_DIM), lambda b, i: (0, 0)),
            pl.BlockSpec((DA_V_DIM, tq), lambda b, i: (0, 0)),
            pl.BlockSpec((tq, DA_QK_COLS), lambda b, i: (b * nq + i, 0)),
            pl.BlockSpec((seq, DA_QK_COLS), lambda b, i: (b, 0)),
            pl.BlockSpec((DA_V_COLS, seq), lambda b, i: (0, b)),
            pl.BlockSpec((n_ctx, DA_QK_COLS), lambda b, i: (ctx0 + b, 0)),
            pl.BlockSpec((DA_V_COLS, n_ctx), lambda b, i: (0, ctx0 + b)),
        ],
        out_specs=pl.BlockSpec((tq, DA_V_COLS), lambda b, i: (b * nq + i, 0)),
        out_shape=jax.ShapeDtypeStruct((n_batch * seq, DA_V_COLS), BF16),
        scratch_shapes=[
            pltpu.VMEM((n_hm, tq, LANES), BF16),
            pltpu.VMEM((n_hm, 1, tq), F32),
            pltpu.VMEM((n_hm, 1, tq), F32),
            pltpu.VMEM((n_hm, DA_V_DIM, tq), F32),
        ],
        compiler_params=_cparams("arbitrary", "arbitrary"),
        name="diff_attention",
    )(lq, g_rows, qd, kd, vdt, kd, vdt)


def _ctx_attn_kernel(sink_ref, lq_ref, g_ref, qw_ref, kw_ref, vw_ref, qd_ref, kd_ref, vd_ref, ow_ref, od_ref,
                     *, lam_init):
    outs = []
    for hk in range(WA_KV_HEADS):
        sl = slice(hk * HEAD_DIM, (hk + 1) * HEAD_DIM)
        k, v = kw_ref[:, sl], vw_ref[:, sl]
        for g in range(WA_GROUP):
            h = hk * WA_GROUP + g
            outs.append(_softmax_sink_attend(qw_ref[:, h * HEAD_DIM:(h + 1) * HEAD_DIM], k, v, sink_ref[h], None))
    ow_ref[...] = jnp.concatenate(outs, axis=1).astype(BF16)
    per_blk = LANES // DA_QK_DIM
    acc, den = [], []
    for hm in range(2 * DA_HEADS):
        kb, vb = hm // per_blk, hm // 4
        s = lax.dot_general(_masked_q(qd_ref, hm), kd_ref[:, kb * LANES:(kb + 1) * LANES],
                            (((1,), (1,)), ((), ())), preferred_element_type=F32)
        p = jnp.exp2(s - jnp.max(s, axis=1, keepdims=True))
        den.append(jnp.sum(p, axis=1, keepdims=True))
        acc.append(jnp.dot(p.astype(BF16), vd_ref[:, vb * LANES:(vb + 1) * LANES], preferred_element_type=F32))
    lam = _diff_lambda(lq_ref, lam_init)
    od_ref[...] = _diff_finish(acc[0::2], den[0::2], acc[1::2], den[1::2], lam, g_ref[...], lam_init).astype(BF16)


def ctx_attention(sink, lq, g128, qw, kw, vw, qd, kd, vd, *, n_batch, seq, n_ctx, lam_init):
    ctx0 = (n_batch * seq) // n_ctx
    ctx = lambda b: (ctx0 + b, 0)
    out = lambda b: (b, 0)
    spec = lambda n, f: pl.BlockSpec((n_ctx, n), f)
    return pl.pallas_call(
        functools.partial(_ctx_attn_kernel, lam_init=lam_init),
        grid=(n_batch,),
        in_specs=[
            pl.BlockSpec(memory_space=pltpu.SMEM),
            pl.BlockSpec((4, DA_QK_DIM), lambda b: (0, 0)),
            pl.BlockSpec((1, LANES), lambda b: (0, 0)),
            spec(WA_Q_COLS, ctx), spec(WA_KV_COLS, ctx), spec(WA_KV_COLS, ctx),
            spec(DA_QK_COLS, ctx), spec(DA_QK_COLS, ctx), spec(DA_V_COLS, ctx),
        ],
        out_specs=[spec(WA_Q_COLS, out), spec(DA_V_COLS, out)],
        out_shape=[jax.ShapeDtypeStruct((n_batch * n_ctx, WA_Q_COLS), BF16),
                   jax.ShapeDtypeStruct((n_batch * n_ctx, DA_V_COLS), BF16)],
        compiler_params=_cparams("arbitrary"),
        name="ctx_attention",
    )(sink, lq, g128, qw, kw, vw, qd, kd, vd)


def _filter_kernel(feat_ref, w1_ref, b1_ref, fr_ref, w2_ref, b2_ref, w3_ref, dec_ref, o_ref, *, transposed):
    feat = feat_ref[...]
    fr = fr_ref[...]
    z = jnp.sin(fr * (jnp.dot(feat, w1_ref[...], precision=HI, preferred_element_type=F32) + b1_ref[...]))
    z = jnp.sin(fr * (jnp.dot(z, w2_ref[...], precision=HI, preferred_element_type=F32) + b2_ref[...]))
    h = jnp.dot(z, w3_ref[...], precision=HI, preferred_element_type=F32)
    h = h * jnp.exp(-feat[:, 0:1] * jnp.abs(dec_ref[...]))
    rows = lax.broadcasted_iota(jnp.int32, h.shape, 0) + pl.program_id(0) * h.shape[0]
    cols = lax.broadcasted_iota(jnp.int32, h.shape, 1)
    h = jnp.where((rows == 0) & (cols % (2 * HY_WIDTH) >= HY_WIDTH), 0.0, h)
    if transposed:
        for r in range(h.shape[0] // LANES):
            o_ref[r] = h[r * LANES:(r + 1) * LANES, :].T
    else:
        o_ref[...] = h


def hyena_filter_features(n):
    t = jnp.linspace(0.0, 1.0, n, dtype=F32)[:, None]
    bands = jnp.linspace(1e-4, HY_BANDS - 1, HY_BANDS, dtype=F32)
    phase = (2.0 * math.pi / n) * jnp.arange(n, dtype=F32)[:, None] * bands
    feat = jnp.concatenate([t, jnp.cos(phase), -jnp.sin(phase)], axis=-1)
    return jnp.pad(feat, ((0, 0), (0, LANES - HY_EMB_DIM)))


def hyena_filters(feat, w1p, b1, freq, w2, b2, w3, decay, *, transposed):
    n = feat.shape[0]
    tr = min(n, 512)
    ncol = w3.shape[1]
    fw = w2.shape[0]
    const = lambda i: (0, 0)
    if transposed:
        out_spec = pl.BlockSpec((tr // LANES, ncol, LANES), lambda i: (i, 0, 0))
        out_shape = jax.ShapeDtypeStruct((n // LANES, ncol, LANES), F32)
    else:
        out_spec = pl.BlockSpec((tr, ncol), lambda i: (i, 0))
        out_shape = jax.ShapeDtypeStruct((n, ncol), F32)
    return pl.pallas_call(
        functools.partial(_filter_kernel, transposed=transposed),
        grid=(n // tr,),
        in_specs=[
            pl.BlockSpec((tr, LANES), lambda i: (i, 0)),
            pl.BlockSpec((LANES, fw), const), pl.BlockSpec((1, fw), const), pl.BlockSpec((1, fw), const),
            pl.BlockSpec((fw, fw), const), pl.BlockSpec((1, fw), const),
            pl.BlockSpec((fw, ncol), const), pl.BlockSpec((1, ncol), const),
        ],
        out_specs=out_spec,
        out_shape=out_shape,
        compiler_params=_cparams("arbitrary"),
        name="hyena_filters",
    )(feat, w1p, b1, freq, w2, b2, w3, decay)


def dft_constants(seq):
    n = 2 * seq
    n1 = n // LANES
    a_rows = n1 // 2
    k1 = np.arange(n1, dtype=np.float64)[:, None]
    a = np.arange(a_rows, dtype=np.float64)[None, :]
    th = 2.0 * np.pi * k1 * a / n1
    c1, s1 = np.cos(th), np.sin(th)
    m1_data = np.block([[c1, s1], [-s1, c1]])
    half_rows = -(-(n1 // 2 + 1) // SUBLANES) * SUBLANES
    pad = np.zeros((half_rows - (n1 // 2 + 1), a_rows))
    m1_real = np.concatenate([c1[:n1 // 2 + 1], pad, -s1[:n1 // 2 + 1], pad], axis=0)
    m3 =np.block([[c1.T, -s1.T], [s1.T, c1.T]]) / n
    b = np.arange(LANES, dtype=np.float64)[None, :]
    psi = 2.0 * np.pi * k1 * b / n
    twc, tws = np.cos(psi), np.sin(psi)
    bb = np.arange(LANES, dtype=np.float64)
    phi = 2.0 * np.pi * np.outer(bb, bb) / LANES
    c2, s2 = np.cos(phi), np.sin(phi)
    w2f = np.block([[c2, -s2], [s2, c2]])
    w2i = np.block([[c2, s2], [-s2, c2]])
    f = lambda x: jnp.asarray(x, dtype=F32)
    return dict(m1_data=f(m1_data), m1_real=f(m1_real), m3=f(m3), twc=f(twc[:, None, :]), tws=f(tws[:, None, :]),
                w2f=f(w2f), w2i=f(w2i), n1=n1, a_rows=a_rows, half_rows=half_rows)


def _split_bf16(x):
    hi = x.astype(BF16)
    return hi, (x - hi.astype(F32)).astype(BF16)


def _mm3(a, b):
    ah, al = _split_bf16(a)
    bh, bl = _split_bf16(b)
    dot = lambda x, y: jnp.dot(x, y, preferred_element_type=F32)
    return dot(ah, bh) + (dot(ah, bl) + dot(al, bh))


def _leftmm_kernel(m_ref, x_ref, o_ref):
    o_ref[...] = _mm3(m_ref[...], x_ref[...])


def left_matmul(m, x, ct):
    r, k = m.shape
    cols = x.shape[1]
    return pl.pallas_call(
        _leftmm_kernel,
        grid=(cols // ct,),
        in_specs=[pl.BlockSpec((r, k), lambda j: (0, 0)), pl.BlockSpec((k, ct), lambda j: (0, j))],
        out_specs=pl.BlockSpec((r, ct), lambda j: (0, j)),
        out_shape=jax.ShapeDtypeStruct((r, cols), F32),
        compiler_params=_cparams("arbitrary"),
        name="dft_rows",
    )(m, x)


def _leftmm_gate_kernel(m_ref, zr_ref, zi_ref, x_ref, zold_ref, bias_ref, o_ref):
    n1 = zr_ref.shape[0]
    y = _mm3(m_ref[:, :n1], zr_ref[...]) + _mm3(m_ref[:, n1:], zi_ref[...])
    zold = zold_ref[...]
    o_ref[...] = x_ref[...] * (y + bias_ref[...] * zold)


def left_matmul_gate(m, zr, zi, xg, zold, bias_cols, ct):
    r, k = m.shape
    n1, cols = zr.shape
    col = lambda j: (0, j)
    return pl.pallas_call(
        _leftmm_gate_kernel,
        grid=(cols // ct,),
        in_specs=[pl.BlockSpec((r, k), lambda j: (0, 0)), pl.BlockSpec((n1, ct), col), pl.BlockSpec((n1, ct), col),
                  pl.BlockSpec((r, ct), col), pl.BlockSpec((r, ct), col), pl.BlockSpec((1, ct), col)],
        out_specs=pl.BlockSpec((r, ct), col),
        out_shape=jax.ShapeDtypeStruct((r, cols), F32),
        compiler_params=_cparams("arbitrary"),
        name="idft_rows_gate",
    )(m, zr, zi, xg, zold, bias_cols)


def _twiddle_fwd(yr, yi, c, s):
    return yr * c + yi * s, yi * c - yr * s


def _spectrum_kernel(yr_ref, yi_ref, twc_ref, tws_ref, w2f_ref, hr_ref, hi_ref, *, kb, n1):
    for t in range(kb):
        mirrored = pl.program_id(0) * kb + t > n1 // 2
        yi = yi_ref[t]
        ypr, ypi = _twiddle_fwd(yr_ref[t], jnp.where(mirrored, -yi, yi), twc_ref[t], tws_ref[t])
        x = _mm3(jnp.concatenate([ypr, ypi], axis=1), w2f_ref[...])
        xr, xi = x[:, :LANES], x[:, LANES:]
        for o in range(HY_ORDER):
            f0 = slice((2 * o) * HY_WIDTH, (2 * o + 1) * HY_WIDTH)
            f1 = slice((2 * o + 1) * HY_WIDTH, (2 * o + 2) * HY_WIDTH)
            hr_ref[o, t] = xr[f0] + xr[f1]
            hi_ref[o, t] = xi[f0] - xi[f1]


def filter_spectrum(y, consts, kb):
    assert kb == 1
    n1, half_rows = consts["n1"], consts["half_rows"]
    nc = 2 * HY_ORDER * HY_WIDTH
    y4 = y.reshape(2, half_rows, nc, LANES)
    tw = pl.BlockSpec((kb, 1, LANES), lambda i: (i, 0, 0))
    out = pl.BlockSpec((HY_ORDER, kb, HY_WIDTH, LANES), lambda i: (0, i, 0, 0))
    shp = jax.ShapeDtypeStruct((HY_ORDER, n1, HY_WIDTH, LANES), F32)
    src = lambda i: jnp.where(i > n1 // 2, n1 - i, i)
    return pl.pallas_call(
        functools.partial(_spectrum_kernel, kb=kb, n1=n1),
        grid=(n1 // kb,),
        in_specs=[
            pl.BlockSpec((None, kb, nc, LANES), lambda i: (0, src(i), 0, 0)),
            pl.BlockSpec((None, kb, nc, LANES), lambda i: (1, src(i), 0, 0)),
            tw, tw,
            pl.BlockSpec((2 * LANES, 2 * LANES), lambda i: (0, 0)),
        ],
        out_specs=[out, out],
        out_shape=[shp, shp],
        compiler_params=_cparams("arbitrary"),
        name="filter_spectrum",
    )(y4, y4, consts["twc"], consts["tws"], consts["w2f"])


def _freq_kernel(yr_ref, yi_ref, twc_ref, tws_ref, w2f_ref, w2i_ref, hr_ref, hi_ref, zr_ref, zi_ref, *, kb):
    for t in range(kb):
        c, s = twc_ref[t], tws_ref[t]
        ypr, ypi = _twiddle_fwd(yr_ref[t], yi_ref[t], c, s)
        x = _mm3(jnp.concatenate([ypr, ypi], axis=1), w2f_ref[...])
        xr, xi = x[:, :LANES], x[:, LANES:]
        hr, hi = hr_ref[t], hi_ref[t]
        gr = xr * hr - xi * hi
        gi = xr * hi + xi * hr
        z = _mm3(jnp.concatenate([gr, gi], axis=1), w2i_ref[...])
        zr, zi = z[:, :LANES], z[:, LANES:]
        zr_ref[t] = zr * c - zi * s
        zi_ref[t] = zi * c + zr * s


def freq_multiply(y, hr, hi, order, consts, kb):
    n1 = consts["n1"]
    y4 = y.reshape(2, n1, HY_WIDTH, LANES)
    tw = pl.BlockSpec((kb, 1, LANES), lambda i: (i, 0, 0))
    blk = lambda p: pl.BlockSpec((None, kb, HY_WIDTH, LANES), lambda i: (p, i, 0, 0))
    wspec = pl.BlockSpec((2 * LANES, 2 * LANES), lambda i: (0, 0))
    out = pl.BlockSpec((kb, HY_WIDTH, LANES), lambda i: (i, 0, 0))
    shp = jax.ShapeDtypeStruct((n1, HY_WIDTH, LANES), F32)
    return pl.pallas_call(
        functools.partial(_freq_kernel, kb=kb),
        grid=(n1 // kb,),
        in_specs=[blk(0), blk(1), tw, tw, wspec, wspec, blk(order), blk(order)],
        out_specs=[out, out],
        out_shape=[shp, shp],
        compiler_params=_cparams("arbitrary"),
        name="freq_multiply",
    )(y4, y4, consts["twc"], consts["tws"], consts["w2f"], consts["w2i"], hr, hi)


def _short_conv_rows(u, prev_row, next_row, w_ref, b_ref):
    rows = lax.broadcasted_iota(jnp.int32, u.shape, 0)
    um1 = jnp.where(rows == 0, prev_row, pltpu.roll(u, 1, 0))
    up1 = jnp.where(rows == u.shape[0] - 1, next_row, pltpu.roll(u, u.shape[0] - 1, 0))
    return b_ref[...] + um1 * w_ref[0:1, :] + u * w_ref[1:2, :] + up1 * w_ref[2:3, :]


def _shortconv_t_kernel(u_ref, up_ref, un_ref, w_ref, b_ref, o_ref, *, n_tiles):
    i = pl.program_id(1)
    prev_row = jnp.where(i > 0, up_ref[SUBLANES - 1:SUBLANES, :], 0.0)
    next_row = jnp.where(i < n_tiles - 1, un_ref[0:1, :], 0.0)
    v = _short_conv_rows(u_ref[...], prev_row, next_row, w_ref, b_ref)
    for part in range(HY_ORDER + 1):
        for r in range(v.shape[0] // LANES):
            o_ref[part, r] = v[r * LANES:(r + 1) * LANES, part * HY_WIDTH:(part + 1) * HY_WIDTH].T


def short_conv_transposed(hy, conv_w, conv_b, *, n_batch, seq, tm):
    nt = seq // tm
    per8 = tm // SUBLANES
    a_rows = seq // LANES
    prev = lambda b, i: (jnp.maximum((b * nt + i) * per8 - 1, 0), 0)
    nxt = lambda b, i: (jnp.minimum((b * nt + i + 1) * per8, n_batch * nt * per8 - 1), 0)
    return pl.pallas_call(
        functools.partial(_shortconv_t_kernel, n_tiles=nt),
        grid=(n_batch, nt),
        in_specs=[
            pl.BlockSpec((tm, HY_COLS), lambda b, i: (b * nt + i, 0)),
            pl.BlockSpec((SUBLANES, HY_COLS), prev),
            pl.BlockSpec((SUBLANES, HY_COLS), nxt),
            pl.BlockSpec((3, HY_COLS), lambda b, i: (0, 0)),
            pl.BlockSpec((1, HY_COLS), lambda b, i: (0, 0)),
        ],
        out_specs=pl.BlockSpec((HY_ORDER + 1, None, tm // LANES, HY_WIDTH, LANES), lambda b, i: (0, b, i, 0, 0)),
        out_shape=jax.ShapeDtypeStruct((HY_ORDER + 1, n_batch, a_rows, HY_WIDTH, LANES), F32),
        compiler_params=_cparams("arbitrary", "arbitrary"),
        name="short_conv_transposed",
    )(hy, hy, hy, conv_w, conv_b)


def _untranspose_kernel(z_ref, o_ref):
    for r in range(z_ref.shape[0]):
        o_ref[r * LANES:(r + 1) * LANES, :] = z_ref[r].T.astype(BF16)


def untranspose(zt, *, n_batch, seq):
    a_rows = seq // LANES
    ta = min(SUBLANES, a_rows)
    nt = a_rows // ta
    return pl.pallas_call(
        _untranspose_kernel,
        grid=(n_batch, nt),
        in_specs=[pl.BlockSpec((None, ta, HY_WIDTH, LANES), lambda b, i: (b, i, 0, 0))],
        out_specs=pl.BlockSpec((ta * LANES, HY_WIDTH), lambda b, i: (b * nt + i, 0)),
        out_shape=jax.ShapeDtypeStruct((n_batch * seq, HY_WIDTH), BF16),
        compiler_params=_cparams("arbitrary", "arbitrary"),
        name="untranspose",
    )(zt)


def hyena_latent(hy, conv_w, conv_b, hy_bias, filt_t, consts, *, n_batch, seq):
    assert n_batch == 2, "the two batches are packed as real / imaginary parts of one DFT"
    n1, a_rows = consts["n1"], consts["a_rows"]
    cols = HY_WIDTH * LANES
    ct = min(cols, 2048)
    kb = min(n1, 4)
    parts = short_conv_transposed(hy, conv_w, conv_b, n_batch=n_batch, seq=seq, tm=min(seq, 1024))
    parts = parts.reshape(HY_ORDER + 1, n_batch * a_rows, cols)
    yf = left_matmul(consts["m1_real"], filt_t.reshape(a_rows, 2 * HY_ORDER * cols), ct)
    hr, hi = filter_spectrum(yf, consts, 1)
    z = parts[HY_ORDER]
    for o in range(HY_ORDER):
        y = left_matmul(consts["m1_data"], z, ct)
        zr, zi = freq_multiply(y, hr, hi, o, consts, kb)
        bias_cols = jnp.repeat(hy_bias[o], LANES)[None, :]
        z = left_matmul_gate(consts["m3"], zr.reshape(n1, cols), zi.reshape(n1, cols), parts[o], z, bias_cols, ct)
    return untranspose(z.reshape(n_batch, a_rows, HY_WIDTH, LANES), n_batch=n_batch, seq=seq)


def _ctx_hyena_kernel(u_ref, w_ref, b_ref, h_ref, bias_ref, fc_ref, fs_ref, fct_ref, fst_ref, o_ref):
    u = u_ref[...]
    zero = jnp.zeros((1, u.shape[1]), F32)
    v = _short_conv_rows(u, zero, zero, w_ref, b_ref)
    z = v[:, HY_ORDER * HY_WIDTH:]
    mm = lambda a, b: jnp.dot(a, b, precision=HI, preferred_element_type=F32)
    fc, fs = fc_ref[...], fs_ref[...]
    inv_n = 1.0 / fc.shape[0]
    for o in range(HY_ORDER):
        h0 = h_ref[:, (2 * o) * HY_WIDTH:(2 * o + 1) * HY_WIDTH]
        h1 = h_ref[:, (2 * o + 1) * HY_WIDTH:(2 * o + 2) * HY_WIDTH]
        sr, si = mm(fc, h0 + h1), mm(fs, h1 - h0)
        zr, zi = mm(fc, z), -mm(fs, z)
        gr = zr * sr - zi * si
        gi = zr * si + zi * sr
        y = (mm(fct_ref[...], gr) - mm(fst_ref[...], gi)) * inv_n
        z = v[:, o * HY_WIDTH:(o + 1) * HY_WIDTH] * (y + bias_ref[o:o + 1, :] * z)
    o_ref[...] = z.astype(BF16)


def ctx_hyena(hy, conv_w, conv_b, filt, hy_bias, *, n_batch, seq, n_ctx):
    n = 2 * n_ctx
    k = np.arange(n, dtype=np.float64)[:, None]
    m = np.arange(n_ctx, dtype=np.float64)[None, :]
    ang = 2.0 * np.pi * k * m / n
    fc, fs = jnp.asarray(np.cos(ang), F32), jnp.asarray(np.sin(ang), F32)
    ctx0 = (n_batch * seq) // n_ctx
    const = lambda b: (0, 0)
    return pl.pallas_call(
        _ctx_hyena_kernel,
        grid=(n_batch,),
        in_specs=[
            pl.BlockSpec((n_ctx, HY_COLS), lambda b: (ctx0 + b, 0)),
            pl.BlockSpec((3, HY_COLS), const), pl.BlockSpec((1, HY_COLS), const),
            pl.BlockSpec((n_ctx, 2 * HY_ORDER * HY_WIDTH), const),
            pl.BlockSpec((HY_ORDER, HY_WIDTH), const),
            pl.BlockSpec((n, n_ctx), const), pl.BlockSpec((n, n_ctx), const),
            pl.BlockSpec((n_ctx, n), const), pl.BlockSpec((n_ctx, n), const),
        ],
        out_specs=pl.BlockSpec((n_ctx, HY_WIDTH), lambda b: (b, 0)),
        out_shape=jax.ShapeDtypeStruct((n_batch * n_ctx, HY_WIDTH), BF16),
        compiler_params=_cparams("arbitrary"),
        name="ctx_hyena",
    )(hy, conv_w, conv_b, filt, hy_bias, fc, fs, fc.T, fs.T)


def _outproj_kernel(x_ref, yh_ref, yw_ref, yd_ref, mod_ref, wo_ref, g_ref, wq_ref, xo_ref, h2_ref, qp_ref):
    y = jnp.dot(yh_ref[...], wo_ref[0:HY_WIDTH, :], preferred_element_type=F32)
    y = y + jnp.dot(yw_ref[...], wo_ref[HY_WIDTH:HY_WIDTH + WA_Q_COLS, :], preferred_element_type=F32)
    y = y + jnp.dot(yd_ref[...], wo_ref[HY_WIDTH + WA_Q_COLS:, :], preferred_element_type=F32)
    xn = x_ref[...] + mod_ref[2:3, :] * y
    xo_ref[...] = xn
    h2 = _rmsnorm_mod(xn, g_ref[...], mod_ref[3:4, :], mod_ref[4:5, :]).astype(BF16)
    h2_ref[...] = h2
    qp_ref[...] = jnp.dot(h2, wq_ref[...], preferred_element_type=F32)


def out_proj(x_all, y_hy, y_wa, y_da, mod, wo_bf, g2, wq_bf, *, n_rows, n_batch, seq, tm):
    d = x_all.shape[1]
    tpb = seq // tm
    grp = functools.partial(_group_of_tile, tiles_per_batch=tpb, n_batch=n_batch)
    row = lambda i: (i, 0)
    const = lambda i: (0, 0)
    mix = HY_WIDTH + WA_Q_COLS + DA_V_COLS
    return pl.pallas_call(
        _outproj_kernel,
        grid=(n_rows // tm,),
        in_specs=[
            pl.BlockSpec((tm, d), row), pl.BlockSpec((tm, HY_WIDTH), row), pl.BlockSpec((tm, WA_Q_COLS), row),
            pl.BlockSpec((tm, DA_V_COLS), row),
            pl.BlockSpec((None, N_MOD, d), lambda i: (grp(i), 0, 0)),
            pl.BlockSpec((mix, d), const), pl.BlockSpec((1, d), const), pl.BlockSpec((d, PEER_QCOLS), const),
        ],
        out_specs=[pl.BlockSpec((tm, d), row), pl.BlockSpec((tm, d), row), pl.BlockSpec((tm, PEER_QCOLS), row)],
        out_shape=[jax.ShapeDtypeStruct((n_rows, d), F32), jax.ShapeDtypeStruct((n_rows, d), BF16),
                   jax.ShapeDtypeStruct((n_rows, PEER_QCOLS), F32)],
        compiler_params=_cparams("arbitrary"),
        name="out_proj",
    )(x_all, y_hy, y_wa, y_da, mod, wo_bf, g2, wq_bf)


def _sort_pairs(n):
    pairs = []

    def merge(lo, cnt, r):
        step = r * 2
        if step < cnt:
            merge(lo, cnt, step)
            merge(lo + r, cnt, step)
            for i in range(lo + r, lo + cnt - r, step):
                pairs.append((i, i + r))
        else:
            pairs.append((lo, lo + r))

    def sort(lo, cnt):
        if cnt > 1:
            half = cnt // 2
            sort(lo, half)
            sort(lo + half, half)
            merge(lo, cnt, 1)

    sort(0, n)
    return pairs


def _sort_desc(vals):
    n = 1
    while n < len(vals):
        n *= 2
    v = list(vals) + [None] * (n - len(vals))
    for i, j in _sort_pairs(n):
        a, b = v[i], v[j]
        if b is None:
            continue
        if a is None:
            v[i], v[j] = b, None
        else:
            v[i], v[j] = jnp.maximum(a, b), jnp.minimum(a, b)
    return v[:len(vals)]


def _top16_rows(s):
    k = PEER_TOPK
    v = _sort_desc([s[SUBLANES * r:SUBLANES * (r + 1), :] for r in range(s.shape[0] // SUBLANES)])
    for shift in (4, 2, 1):
        other = [pltpu.roll(x, shift, 0) for x in v]
        v = [jnp.maximum(v[i], other[k - 1 - i]) for i in range(k)]
        d = k // 2
        while d >= 1:
            for i in range(k):
                if (i & d) == 0:
                    a, b = v[i], v[i + d]
                    v[i], v[i + d] = jnp.maximum(a, b), jnp.minimum(a, b)
            d //= 2
    return v


def _peer_topk_kernel(q_ref, keys_ref, cnt_ref, e1_ref, rank_ref, e2_ref):
    k = PEER_TOPK
    tmk = q_ref.shape[0]
    sub = lax.broadcasted_iota(jnp.int32, (SUBLANES, tmk), 0)
    scores, tops = [], []
    for hp in range(2 * PEER_HEADS):
        s = lax.dot_general(keys_ref[hp], q_ref[:, hp * N_KEYS:(hp + 1) * N_KEYS], (((1,), (1,)), ((), ())),
                            precision=HI, preferred_element_type=F32)
        scores.append(s)
        top = _top16_rows(s)
        top.append(jnp.max(jnp.where(s < top[k - 1][0:1, :], s, NEG_BIG), axis=0, keepdims=True))
        tops.append(top)
    packed = []
    for p in range(2):
        lst = []
        for r in range(k + 1):
            acc = jnp.broadcast_to(tops[p][r], (SUBLANES, tmk))
            for h in range(1, PEER_HEADS):
                acc = jnp.where(sub == h, tops[2 * h + p][r], acc)
            lst.append(acc)
        packed.append(lst)
    cand = [packed[0][i] + packed[1][j] for i in range(k) for j in range(k) if (i + 1) * (j + 1) <= k]
    cs = _sort_desc(cand)
    c17 = jnp.maximum(cs[k], jnp.maximum(packed[0][k] + packed[1][0], packed[0][0] + packed[1][k]))
    tau = 0.5 * (cs[k - 1] + c17)
    zsum = jnp.zeros_like(tau)
    for r in range(k):
        zsum = zsum + jnp.exp(cs[r] - cs[0])
    inv_z = 1.0 / zsum
    for h in range(PEER_HEADS):
        s1, s2 = scores[2 * h], scores[2 * h + 1]
        row = lambda x: x[h:h + 1, :]
        top2 = [t2[0:1, :] for t2 in tops[2 * h + 1][:k]]
        cnt_ref[h] = _count_sorted(top2, row(tau) - s1, strict=False)
        tiled = lambda x: x.astype(BF16).reshape(N_KEYS // BF16_ROWS, BF16_ROWS, tmk)
        rank_ref[h] = tiled(_count_sorted(top2, s2, strict=True))
        e1_ref[h] = 0.5 * jnp.exp(s1 - row(packed[0][0]))
        e2_ref[h] = tiled(jnp.exp(s2 - row(packed[1][0])) * row(inv_z))


def _count_sorted(tops, x, *, strict):
    ge = (lambda a: a > x) if strict else (lambda a: a >= x)
    b3 = ge(tops[7])
    b2 = ge(jnp.where(b3, tops[11], tops[3]))
    b1 = ge(jnp.where(b3, jnp.where(b2, tops[13], tops[9]), jnp.where(b2, tops[5], tops[1])))
    hi = jnp.where(b2, jnp.where(b1, tops[14], tops[12]), jnp.where(b1, tops[10], tops[8]))
    lo = jnp.where(b2, jnp.where(b1, tops[6], tops[4]), jnp.where(b1, tops[2], tops[0]))
    b0 = ge(jnp.where(b3, hi, lo))
    cnt = (jnp.where(b3, 8.0, 0.0) + jnp.where(b2, 4.0, 0.0)) + (jnp.where(b1, 2.0, 0.0) + jnp.where(b0, 1.0, 0.0))
    return jnp.where(ge(tops[15]), 16.0, cnt)


def peer_topk(qp, keys, *, tmk):
    t = qp.shape[0]
    tiles = N_KEYS // BF16_ROWS
    out = pl.BlockSpec((PEER_HEADS, N_KEYS, tmk), lambda i: (0, 0, i))
    out16 = pl.BlockSpec((PEER_HEADS, tiles, BF16_ROWS, tmk), lambda i: (0, 0, 0, i))
    f32 = jax.ShapeDtypeStruct((PEER_HEADS, N_KEYS, t), F32)
    b16 = jax.ShapeDtypeStruct((PEER_HEADS, tiles, BF16_ROWS, t), BF16)
    return pl.pallas_call(
        _peer_topk_kernel,
        grid=(t // tmk,),
        in_specs=[pl.BlockSpec((tmk, PEER_QCOLS), lambda i: (i, 0)),
                  pl.BlockSpec((2 * PEER_HEADS, N_KEYS, PEER_QDIM // 2), lambda i: (0, 0, 0))],
        out_specs=[out, out, out16, out16],
        out_shape=[f32, f32, b16, b16],
        compiler_params=_cparams("arbitrary"),
        name="peer_topk",
    )(qp, keys)


def _peer_dense_kernel(x_ref, h_ref, mod_ref, u_ref, vt_ref, cnt_ref, e1_ref, rank_ref, e2_ref, g_ref, o_ref,
                       a_scr, w_scr, acc_scr, *, final_norm, chunk):
    e = pl.program_id(1)

    @pl.when(e == 0)
    def _():
        acc_scr[...] = jnp.zeros(acc_scr.shape, F32)

    n_chunks = u_ref.shape[0] // chunk
    per_chunk = chunk // N_KEYS
    sqrt_half = math.sqrt(0.5)
    tm = h_ref.shape[0]
    tiles = N_KEYS // BF16_ROWS
    zero = jnp.zeros((), BF16)

    def pre_activations(p):
        rows = slice(p * chunk, (p + 1) * chunk)
        a_scr[rows, :] = lax.dot_general(u_ref[rows, :], h_ref[...], (((1,), (1,)), ((), ())),
                                         preferred_element_type=F32)

    pre_activations(0)
    for p in range(n_chunks):
        if p + 1 < n_chunks:
            pre_activations(p + 1)
        for ii in range(p * per_chunk, (p + 1) * per_chunk):
            rows = slice(ii * N_KEYS, (ii + 1) * N_KEYS)
            a = a_scr[rows, :]
            act = a * (1.0 + lax.erf(a * sqrt_half))
            act = act.astype(BF16).reshape(tiles, BF16_ROWS, tm)
            gate = None
            for h in range(PEER_HEADS):
                cnt = jnp.broadcast_to(cnt_ref[h, ii:ii + 1, :], (BF16_ROWS, tm)).astype(BF16)
                e1 = jnp.broadcast_to(e1_ref[h, ii:ii + 1, :], (BF16_ROWS, tm)).astype(BF16)
                term = jnp.where(rank_ref[h] < cnt[None], e2_ref[h] * e1[None], zero)
                gate = term if gate is None else gate + term
            w_scr[rows, :] = (gate * act).reshape(N_KEYS, tm)
        rows = slice(p * chunk, (p + 1) * chunk)
        acc_scr[...] += jnp.dot(vt_ref[:, rows], w_scr[rows, :], preferred_element_type=F32)

    @pl.when(e == pl.num_programs(1) - 1)
    def _():
        xn = x_ref[...] + mod_ref[5:6, :] * acc_scr[...].T
        if final_norm:
            ms = jnp.mean(xn * xn, axis=-1, keepdims=True)
            xn = xn * lax.rsqrt(ms + NORM_EPS) * g_ref[...]
        o_ref[...] = xn


def peer_dense(x, h2, mod, u_bf, vt_bf, cnt, e1, rank, e2, g_final, *, n_batch, seq, tm, te, chunk, final_norm):
    n_rows, d = x.shape
    tpb = seq // tm
    grp = functools.partial(_group_of_tile, tiles_per_batch=tpb, n_batch=n_batch)
    n_i = te // N_KEYS
    tok = lambda i, e: (i, 0)
    rows_i = pl.BlockSpec((PEER_HEADS, n_i, tm), lambda i, e: (0, e, i))
    tiles = N_KEYS // BF16_ROWS
    full = pl.BlockSpec((PEER_HEADS, tiles, BF16_ROWS, tm), lambda i, e: (0, 0, 0, i))
    return pl.pallas_call(
        functools.partial(_peer_dense_kernel, final_norm=final_norm, chunk=chunk),
        grid=(n_rows // tm, N_EXPERTS // te),
        in_specs=[
            pl.BlockSpec((tm, d), tok), pl.BlockSpec((tm, d), tok),
            pl.BlockSpec((None, N_MOD, d), lambda i, e: (grp(i), 0, 0)),
            pl.BlockSpec((te, d), lambda i, e: (e, 0)),
            pl.BlockSpec((d, te), lambda i, e: (0, e)),
            rows_i, rows_i, full, full,
            pl.BlockSpec((1, d), lambda i, e: (0, 0)),
        ],
        out_specs=pl.BlockSpec((tm, d), tok),
        out_shape=jax.ShapeDtypeStruct((n_rows, d), F32),
        scratch_shapes=[pltpu.VMEM((te, tm), F32), pltpu.VMEM((te, tm), BF16), pltpu.VMEM((d, tm), F32)],
        compiler_params=_cparams("arbitrary", "arbitrary"),
        name="peer_dense",
    )(x, h2, mod, u_bf, vt_bf, cnt, e1, rank, e2, g_final)


def kernel(x, c, ctx, c_ctx, w_mod, b_mod, norm1_g, w_in, hy_conv_w, hy_conv_b, hy_w1, hy_b1, hy_freq, hy_w2,
           hy_b2, hy_w3, hy_decay, hy_bias, wa_sink, da_lq1, da_lk1, da_lq2, da_lk2, da_norm_g, w_out, norm2_g,
           peer_wq, peer_keys, peer_u, peer_v, final_g):
    n_batch, seq, d = x.shape
    n_ctx = ctx.shape[1]
    depth = w_in.shape[0]
    n_lat = n_batch * seq
    tm = 512 if seq % 512 == 0 else 256
    assert seq % tm == 0 and (n_batch * n_ctx) % tm == 0 and n_lat % n_ctx == 0 and n_batch + 1 <= SUBLANES

    cpad = jnp.zeros((SUBLANES, d), F32).at[:n_batch].set(c).at[n_batch].set(c_ctx)
    mod_all = mod_vectors(cpad, w_mod, b_mod).reshape(depth, SUBLANES, N_MOD, d)
    rope_wa = rope_tables(seq, HEAD_DIM, tm)
    rope_da = rope_tables(seq, DA_QK_DIM, tm)
    consts = dft_constants(seq)
    feat_lat = hyena_filter_features(seq)
    feat_ctx = hyena_filter_features(n_ctx)
    x_all = jnp.concatenate([x.reshape(n_lat, d), ctx.reshape(n_batch * n_ctx, d)], axis=0)

    out = None
    for li in range(depth):
        last = li == depth - 1
        lam_init = DA_LAMBDA_BASE - DA_LAMBDA_AMP * math.exp(-DA_LAMBDA_RATE * li)
        mod = mod_all[li]
        hy, qw, kw, vw, qd, kd, vd, vdt, vwt = proj_in(x_all, mod, norm1_g[li][None, :], w_in[li].astype(BF16), rope_wa,
                                             rope_da, n_batch=n_batch, seq=seq, tm=tm)
        w1p = jnp.pad(hy_w1[li], ((0, LANES - HY_EMB_DIM), (0, 0)))
        fargs = (w1p, hy_b1[li][None, :], hy_freq[li][None, :], hy_w2[li], hy_b2[li][None, :], hy_w3[li],
                 hy_decay[li].reshape(1, -1))
        conv_b = hy_conv_b[li][None, :]
        filt_t = hyena_filters(feat_lat, *fargs, transposed=True)
        y_hy = hyena_latent(hy, hy_conv_w[li], conv_b, hy_bias[li], filt_t, consts, n_batch=n_batch, seq=seq)
        y_wa = window_attention(wa_sink[li], qw, kw, vwt, n_batch=n_batch, seq=seq, n_ctx=n_ctx)
        lq = jnp.stack([da_lq1[li], da_lk1[li], da_lq2[li], da_lk2[li]], axis=0)
        g128 = jnp.tile(da_norm_g[li], LANES // DA_V_DIM)[None, :]
        tq = min(seq, 512)
        g_rows = jnp.broadcast_to(da_norm_g[li][:, None], (DA_V_DIM, tq))
        y_da = diff_attention(lq, g_rows, qd, kd, vdt, n_batch=n_batch, seq=seq, n_ctx=n_ctx, lam_init=lam_init,
                              tq=tq, tk=min(seq, 256), ahead=3, unroll=4 if seq % 1024 == 0 else 1)
        if not last:
            filt_c = hyena_filters(feat_ctx, *fargs, transposed=False)
            yc_hy = ctx_hyena(hy, hy_conv_w[li], conv_b, filt_c, hy_bias[li], n_batch=n_batch, seq=seq, n_ctx=n_ctx)
            yc_wa, yc_da = ctx_attention(wa_sink[li], lq, g128, qw, kw, vw, qd, kd, vd, n_batch=n_batch, seq=seq,
                                         n_ctx=n_ctx, lam_init=lam_init)
            y_hy = jnp.concatenate([y_hy, yc_hy], axis=0)
            y_wa = jnp.concatenate([y_wa, yc_wa], axis=0)
            y_da = jnp.concatenate([y_da, yc_da], axis=0)
        n_rows = n_lat if last else x_all.shape[0]
        x_mid, h2, qp = out_proj(x_all, y_hy, y_wa, y_da, mod, w_out[li].astype(BF16), norm2_g[li][None, :],
                                 peer_wq[li].astype(BF16), n_rows=n_rows, n_batch=n_batch, seq=seq, tm=tm)
        keys = peer_keys[li].reshape(2 * PEER_HEADS, N_KEYS, PEER_QDIM // 2)
        cnt, e1, rank, e2 = peer_topk(qp, keys, tmk=256)
        x_all = peer_dense(x_mid, h2, mod, peer_u[li].astype(BF16), peer_v[li].T.astype(BF16), cnt, e1, rank, e2,
                           final_g[None, :], n_batch=n_batch, seq=seq, tm=tm, te=2048, chunk=1024, final_norm=last)
        out = x_all
    return out.reshape(n_batch, seq, d)
```

```python
import functools
import math

import numpy as np
import jax
import jax.numpy as jnp
from jax import lax
from jax.experimental import pallas as pl
from jax.experimental.pallas import tpu as pltpu

F32 = jnp.float32
BF16 = jnp.bfloat16
HI = lax.Precision.HIGHEST

LANES = 128
SUBLANES = 8
BF16_ROWS = 16
VMEM_LIMIT_BYTES = 56 * 1024 * 1024

D_MODEL = 1024
N_MOD = 6
NORM_EPS = 1e-6
ROPE_BASE = 10000.0
GRID_W = 64
BLOCK = 128
HEAD_DIM = 64
HY_WIDTH = 256
HY_ORDER = 2
HY_BANDS = 16
HY_EMB_DIM = 1 + 2 * HY_BANDS
HY_FILTER_WIDTH = 64
HY_COLS = 3 * HY_WIDTH
WA_Q_HEADS = 8
WA_KV_HEADS = 2
WA_GROUP = 4
WINDOW = 128
WA_Q_COLS = WA_Q_HEADS * HEAD_DIM
WA_KV_COLS = WA_KV_HEADS * HEAD_DIM
WA_COLS = WA_Q_COLS + 2 * WA_KV_COLS
DA_HEADS = 4
DA_QK_DIM = 32
DA_V_DIM = 64
DA_QK_COLS = DA_HEADS * 2 * DA_QK_DIM
DA_V_COLS = DA_HEADS * DA_V_DIM
DA_COLS = 2 * DA_QK_COLS + DA_V_COLS
DA_LAMBDA_BASE = 0.8
DA_LAMBDA_AMP = 0.6
DA_LAMBDA_RATE = 0.3
IN_COLS = HY_COLS + WA_COLS + DA_COLS
PEER_HEADS = 8
N_KEYS = 128
N_EXPERTS = N_KEYS * N_KEYS
PEER_TOPK = 16
PEER_QDIM = 256
PEER_QCOLS = PEER_HEADS * PEER_QDIM
NEG_BIG = -1e30
LOG2E = math.log2(math.e)


def _cparams(*sem):
    return pltpu.CompilerParams(dimension_semantics=sem, vmem_limit_bytes=VMEM_LIMIT_BYTES)


def _rmsnorm_mod(x, g, shift, scale):
    ms = jnp.mean(x * x, axis=-1, keepdims=True)
    return (x * lax.rsqrt(ms + NORM_EPS) * g) * (1.0 + scale) + shift


def _mod_kernel(c_ref, w_ref, b_ref, o_ref):
    c = c_ref[...]
    act = c * jax.nn.sigmoid(c)
    o_ref[...] = jnp.dot(act, w_ref[...], precision=HI, preferred_element_type=F32) + b_ref[...]


def mod_vectors(cpad, w_mod, b_mod):
    n_layers, d, n = w_mod.shape
    tn = 1536
    return pl.pallas_call(
        _mod_kernel,
        grid=(n_layers, n // tn),
        in_specs=[
            pl.BlockSpec((SUBLANES, d), lambda l, j: (0, 0)),
            pl.BlockSpec((None, d, tn), lambda l, j: (l, 0, j)),
            pl.BlockSpec((None, 1, tn), lambda l, j: (l, 0, j)),
        ],
        out_specs=pl.BlockSpec((None, SUBLANES, tn), lambda l, j: (l, 0, j)),
        out_shape=jax.ShapeDtypeStruct((n_layers, SUBLANES, n), F32),
        compiler_params=_cparams("arbitrary", "arbitrary"),
        name="mod_vectors",
    )(cpad, w_mod, b_mod.reshape(n_layers, 1, n))


def _rope128(x, cos, sin_a, sin_b, m):
    return x * cos + pltpu.roll(x, LANES - m, 1) * sin_a + pltpu.roll(x, m, 1) * sin_b


def _proj_in_kernel(x_ref, mod_ref, g_ref, w_ref, cw_ref, saw_ref, sbw_ref, cd_ref, sad_ref, sbd_ref,
                    hy_ref, qw_ref, kw_ref, vw_ref, qd_ref, kd_ref, vd_ref, vdt_ref, vwt_ref):
    h = _rmsnorm_mod(x_ref[...], g_ref[...], mod_ref[0:1, :], mod_ref[1:2, :]).astype(BF16)
    hy_ref[...] = jnp.dot(h, w_ref[:, 0:HY_COLS], preferred_element_type=F32)
    wa = jnp.dot(h, w_ref[:, HY_COLS:HY_COLS + WA_COLS], preferred_element_type=F32)
    da = jnp.dot(h, w_ref[:, HY_COLS + WA_COLS:IN_COLS], preferred_element_type=F32)
    cw, saw, sbw = cw_ref[...], saw_ref[...], sbw_ref[...]
    cd, sad, sbd = cd_ref[...], sad_ref[...], sbd_ref[...]
    wa_scale = HEAD_DIM ** -0.5
    da_scale = DA_QK_DIM ** -0.5 * LOG2E
    for kb in range(WA_Q_COLS // LANES):
        blk = wa[:, kb * LANES:(kb + 1) * LANES]
        qw_ref[:, kb * LANES:(kb + 1) * LANES] = (_rope128(blk, cw, saw, sbw, 16) * wa_scale).astype(BF16)
    kw_ref[...] = _rope128(wa[:, WA_Q_COLS:WA_Q_COLS + WA_KV_COLS], cw, saw, sbw, 16).astype(BF16)
    vw = wa[:, WA_Q_COLS + WA_KV_COLS:WA_COLS]
    vw_ref[...] = vw.astype(BF16)
    vwt_ref[...] = vw.T.astype(BF16)
    for kb in range(DA_QK_COLS // LANES):
        blk = da[:, kb * LANES:(kb + 1) * LANES]
        qd_ref[:, kb * LANES:(kb + 1) * LANES] = (_rope128(blk, cd, sad, sbd, 8) * da_scale).astype(BF16)
        blk = da[:, DA_QK_COLS + kb * LANES:DA_QK_COLS + (kb + 1) * LANES]
        kd_ref[:, kb * LANES:(kb + 1) * LANES] = _rope128(blk, cd, sad, sbd, 8).astype(BF16)
    vd = da[:, 2 * DA_QK_COLS:DA_COLS]
    vd_ref[...] = vd.astype(BF16)
    vdt_ref[...] = vd.T.astype(BF16)


def _group_of_tile(i, tiles_per_batch, n_batch):
    return jnp.minimum(i // tiles_per_batch, n_batch)


def proj_in(x_all, mod, g, w_bf, rope_wa, rope_da, *, n_batch, seq, tm):
    t_all, d = x_all.shape
    tpb = seq // tm
    grp = functools.partial(_group_of_tile, tiles_per_batch=tpb, n_batch=n_batch)
    pos = lambda i: (jnp.where(i < n_batch * tpb, i % tpb, tpb), 0)
    row = lambda i: (i, 0)
    const = lambda i: (0, 0)
    tab = pl.BlockSpec((tm, LANES), pos)
    outs = [(HY_COLS, F32), (WA_Q_COLS, BF16), (WA_KV_COLS, BF16), (WA_KV_COLS, BF16),
            (DA_QK_COLS, BF16), (DA_QK_COLS, BF16), (DA_V_COLS, BF16)]
    return pl.pallas_call(
        _proj_in_kernel,
        grid=(t_all // tm,),
        in_specs=[
            pl.BlockSpec((tm, d), row),
            pl.BlockSpec((None, N_MOD, d), lambda i: (grp(i), 0, 0)),
            pl.BlockSpec((1, d), const),
            pl.BlockSpec((d, IN_COLS), const),
            tab, tab, tab, tab, tab, tab,
        ],
        out_specs=[pl.BlockSpec((tm, n), row) for n, _ in outs]
        + [pl.BlockSpec((DA_V_COLS, tm), lambda i: (0, i)), pl.BlockSpec((WA_KV_COLS, tm), lambda i: (0, i))],
        out_shape=[jax.ShapeDtypeStruct((t_all, n), dt) for n, dt in outs]
        + [jax.ShapeDtypeStruct((DA_V_COLS, t_all), BF16), jax.ShapeDtypeStruct((WA_KV_COLS, t_all), BF16)],
        compiler_params=_cparams("arbitrary"),
        name="proj_in",
    )(x_all, mod, g, w_bf, *rope_wa, *rope_da)


def rope_tables(seq, head_dim, pad_rows):
    half = head_dim // 2
    axis_dim = half
    m = axis_dim // 2
    t = np.arange(seq)
    rowcol = np.stack([t // GRID_W, t % GRID_W], axis=0).astype(np.float32)
    inv_freq = (ROPE_BASE ** (-jnp.arange(0, axis_dim, 2, dtype=F32) / axis_dim))
    ang = jnp.asarray(rowcol)[:, :, None] * inv_freq[None, None, :]
    cos, sin = jnp.cos(ang), jnp.sin(ang)
    lane = np.arange(LANES)
    dd = lane % head_dim
    axis = dd // half
    sub = dd % half
    idx = sub % m
    first = sub < m
    cos_t = cos[axis, :, idx].T
    sin_t = sin[axis, :, idx].T
    sin_a = jnp.where(first[None, :], -sin_t, 0.0)
    sin_b = jnp.where(first[None, :], 0.0, sin_t)
    pad1 = jnp.ones((pad_rows, LANES), F32)
    pad0 = jnp.zeros((pad_rows, LANES), F32)
    return (jnp.concatenate([cos_t, pad1], 0), jnp.concatenate([sin_a, pad0], 0),
            jnp.concatenate([sin_b, pad0], 0))


def _softmax_sink_attend(q, k, v, sink, valid):
    s = lax.dot_general(q, k, (((1,), (1,)), ((), ())), preferred_element_type=F32)
    if valid is not None:
        nm = valid.shape[1]
        s = jnp.concatenate([jnp.where(valid, s[:, :nm], NEG_BIG), s[:, nm:]], axis=1)
    m = jnp.maximum(jnp.max(s, axis=1, keepdims=True), sink)
    p = jnp.exp(s - m)
    den = jnp.sum(p, axis=1, keepdims=True) + jnp.exp(sink - m)
    return jnp.dot(p.astype(BF16), v, preferred_element_type=F32) / den


def _wattn_kernel(sink_ref, q_ref, kp_ref, kc_ref, kn_ref, kx_ref, vp_ref, vc_ref, vn_ref, vx_ref, o_ref, qm_scr,
                  *, n_pairs, ahead):
    m_idx = pl.program_id(1)
    nq = 2 * BLOCK
    n_loc = 4 * BLOCK
    kk = lax.broadcasted_iota(jnp.int32, (n_loc, nq), 0)
    qq = lax.broadcasted_iota(jnp.int32, (n_loc, nq), 1)
    lo = jnp.where(m_idx > 0, 0, BLOCK)
    hi = jnp.where(m_idx < n_pairs - 1, n_loc, n_loc - BLOCK)
    valid = (kk >= jnp.maximum(qq, lo)) & (kk <= qq + 2 * WINDOW) & (kk < hi)
    k_all = jnp.concatenate([kp_ref[...], kc_ref[...], kn_ref[...], kx_ref[...]], axis=0)
    vt_all = jnp.concatenate([vp_ref[...], vc_ref[...], vn_ref[...], vx_ref[...]], axis=1)
    lane = lax.broadcasted_iota(jnp.int32, (nq, LANES), 1)
    for h in range(WA_Q_HEADS):
        hk = h // WA_GROUP
        blk = q_ref[:, (h // 2) * LANES:(h // 2 + 1) * LANES]
        if h % 2 != hk:
            blk = jnp.concatenate([blk[:, HEAD_DIM:], blk[:, :HEAD_DIM]], axis=1)
        keep = (lane >= hk * HEAD_DIM) & (lane < (hk + 1) * HEAD_DIM)
        qm_scr[h] = jnp.where(keep, blk, jnp.zeros_like(blk))

    def scores(h):
        return lax.dot_general(k_all, qm_scr[h], (((1,), (1,)), ((), ())), preferred_element_type=F32)

    sts = [scores(h) for h in range(ahead)]
    outs = []
    for h in range(WA_Q_HEADS):
        hk = h // WA_GROUP
        if h + ahead < WA_Q_HEADS:
            sts.append(scores(h + ahead))
        st = sts[h]
        sts[h] = None
        st = jnp.concatenate([jnp.where(valid, st[:n_loc], NEG_BIG), st[n_loc:]], axis=0)
        sink = sink_ref[h]
        m = jnp.maximum(jnp.max(st, axis=0, keepdims=True), sink)
        p = jnp.exp(st - m)
        den = jnp.sum(p, axis=0, keepdims=True) + jnp.exp(sink - m)
        o = jnp.dot(vt_all[hk * HEAD_DIM:(hk + 1) * HEAD_DIM, :], p.astype(BF16), preferred_element_type=F32)
        outs.append(o / den)
    o_ref[...] = jnp.concatenate(outs, axis=0).T.astype(BF16)


def window_attention(sink, qw, kw, vwt, *, n_batch, seq, n_ctx):
    nb = seq // BLOCK
    n_pairs = nb // 2
    assert nb % 2 == 0
    ctx0 = (n_batch * seq) // n_ctx
    cur = lambda b, m: (b * n_pairs + m, 0)
    prev = lambda b, m: (b * nb + jnp.maximum(2 * m - 1, 0), 0)
    nxt = lambda b, m: (b * nb + jnp.minimum(2 * m + 2, nb - 1), 0)
    ctx = lambda b, m: (ctx0 + b, 0)
    swap = lambda f: (lambda b, m: f(b, m)[::-1])
    return pl.pallas_call(
        functools.partial(_wattn_kernel, n_pairs=n_pairs, ahead=3),
        grid=(n_batch, n_pairs),
        in_specs=[
            pl.BlockSpec(memory_space=pltpu.SMEM),
            pl.BlockSpec((2 * BLOCK, WA_Q_COLS), cur),
            pl.BlockSpec((BLOCK, WA_KV_COLS), prev), pl.BlockSpec((2 * BLOCK, WA_KV_COLS), cur),
            pl.BlockSpec((BLOCK, WA_KV_COLS), nxt), pl.BlockSpec((n_ctx, WA_KV_COLS), ctx),
            pl.BlockSpec((WA_KV_COLS, BLOCK), swap(prev)), pl.BlockSpec((WA_KV_COLS, 2 * BLOCK), swap(cur)),
            pl.BlockSpec((WA_KV_COLS, BLOCK), swap(nxt)), pl.BlockSpec((WA_KV_COLS, n_ctx), swap(ctx)),
        ],
        out_specs=pl.BlockSpec((2 * BLOCK, WA_Q_COLS), cur),
        out_shape=jax.ShapeDtypeStruct((n_batch * seq, WA_Q_COLS), BF16),
        scratch_shapes=[pltpu.VMEM((WA_Q_HEADS, 2 * BLOCK, LANES), BF16)],
        compiler_params=_cparams("arbitrary", "arbitrary"),
        name="window_attention",
    )(sink, qw, kw, kw, kw, kw, vwt, vwt, vwt, vwt)


def _diff_lambda(lq_ref, lam_init):
    a = jnp.sum(lq_ref[0:1, :] * lq_ref[1:2, :], axis=1, keepdims=True)
    b = jnp.sum(lq_ref[2:3, :] * lq_ref[3:4, :], axis=1, keepdims=True)
    return jnp.exp(a) - jnp.exp(b) + lam_init


def _diff_finish(acc1, l1, acc2, l2, lam, g128, lam_init):
    lane = lax.broadcasted_iota(jnp.int32, acc1[0].shape, 1)
    lo = lane < DA_V_DIM
    blocks = []
    for vb in range(DA_HEADS // 2):
        o = []
        for h in (2 * vb, 2 * vb + 1):
            o.append(acc1[h] / l1[h] - lam * (acc2[h] / l2[h]))
        blk = jnp.where(lo, o[0], o[1])
        sq = blk * blk
        s_lo = jnp.sum(jnp.where(lo, sq, 0.0), axis=1, keepdims=True)
        s_hi = jnp.sum(jnp.where(lo, 0.0, sq), axis=1, keepdims=True)
        ms = jnp.where(lo, s_lo, s_hi) * (1.0 / DA_V_DIM)
        blocks.append(blk * lax.rsqrt(ms + NORM_EPS) * g128 * (1.0 - lam_init))
    return jnp.concatenate(blocks, axis=1)


def _masked_q(q_ref, hm):
    kb, sub = divmod(hm, LANES // DA_QK_DIM)
    blk = q_ref[:, kb * LANES:(kb + 1) * LANES]
    lane = lax.broadcasted_iota(jnp.int32, blk.shape, 1)
    keep = (lane >= sub * DA_QK_DIM) & (lane < (sub + 1) * DA_QK_DIM)
    return jnp.where(keep, blk, jnp.zeros_like(blk))


def _dattn_kernel(lq_ref, g_ref, q_ref, k_ref, vt_ref, kx_ref, vxt_ref, o_ref, qm_scr, m_scr, l_scr, acc_scr,
                  *, lam_init, tk, ahead, unroll):
    n_hm = 2 * DA_HEADS
    per_blk = LANES // DA_QK_DIM
    for hm in range(n_hm):
        qm_scr[hm] = _masked_q(q_ref, hm)
    m_scr[...] = jnp.full(m_scr.shape, NEG_BIG, F32)
    l_scr[...] = jnp.zeros(l_scr.shape, F32)
    acc_scr[...] = jnp.zeros(acc_scr.shape, F32)

    def update(chunks):
        chains = [(c, hm) for c in range(len(chunks)) for hm in range(n_hm)]

        def scores(chain):
            c, hm = chain
            kb = hm // per_blk
            return lax.dot_general(chunks[c][0][:, kb * LANES:(kb + 1) * LANES], qm_scr[hm],
                                   (((1,), (1,)), ((), ())), preferred_element_type=F32)

        sts = [scores(ch) for ch in chains[:ahead]]
        for n, (c, hm) in enumerate(chains):
            h = hm // 2
            vtc = chunks[c][1]
            if n + ahead < len(chains):
                sts.append(scores(chains[n + ahead]))
            st = sts[n]
            sts[n] = None
            m_old = m_scr[hm]
            m_new = jnp.maximum(m_old, jnp.max(st, axis=0, keepdims=True))
            alpha = jnp.exp2(m_old - m_new)
            pt = jnp.exp2(st - m_new)
            l_scr[hm] = alpha * l_scr[hm] + jnp.sum(pt, axis=0, keepdims=True)
            acc_scr[hm] = alpha * acc_scr[hm] + jnp.dot(vtc[h * DA_V_DIM:(h + 1) * DA_V_DIM, :], pt.astype(BF16),
                                                        preferred_element_type=F32)
            m_scr[hm] = m_new

    def body(j, carry):
        chunks = []
        for c in range(unroll):
            start = pl.multiple_of((j * unroll + c) * tk, tk)
            chunks.append((k_ref[pl.ds(start, tk), :], vt_ref[:, pl.ds(start, tk)]))
        update(chunks)
        return carry

    lax.fori_loop(0, k_ref.shape[0] // (tk * unroll), body, 0)
    n_x = kx_ref.shape[0] // tk
    update([(kx_ref[c * tk:(c + 1) * tk, :], vxt_ref[:, c * tk:(c + 1) * tk]) for c in range(n_x)])
    lam = _diff_lambda(lq_ref, lam_init)
    heads = []
    for h in range(DA_HEADS):
        o = acc_scr[2 * h] / l_scr[2 * h] - lam * (acc_scr[2 * h + 1] / l_scr[2 * h + 1])
        ms = jnp.mean(o * o, axis=0, keepdims=True)
        heads.append(o * lax.rsqrt(ms + NORM_EPS) * g_ref[...] * (1.0 - lam_init))
    o_ref[...] = jnp.concatenate(heads, axis=0).T.astype(BF16)


def diff_attention(lq, g_rows, qd, kd, vdt, *, n_batch, seq, n_ctx, lam_init, tq, tk, ahead, unroll):
    nq = seq // tq
    ctx0 = (n_batch * seq) // n_ctx
    n_hm = 2 * DA_HEADS
    return pl.pallas_call(
        functools.partial(_dattn_kernel, lam_init=lam_init, tk=tk, ahead=ahead, unroll=unroll),
        grid=(n_batch, nq),
        in_specs=[
            pl.BlockSpec((4, DA_QK_DIM), lambda b, i: (0, 0)),
            pl.BlockSpec((DA_V_DIM, tq), lambda b, i: (0, 0)),
            pl.BlockSpec((tq, DA_QK_COLS), lambda b, i: (b * nq + i, 0)),
            pl.BlockSpec((seq, DA_QK_COLS), lambda b, i: (b, 0)),
            pl.BlockSpec((DA_V_COLS, seq), lambda b, i: (0, b)),
            pl.BlockSpec((n_ctx, DA_QK_COLS), lambda b, i: (ctx0 + b, 0)),
            pl.BlockSpec((DA_V_COLS, n_ctx), lambda b, i: (0, ctx0 + b)),
        ],
        out_specs=pl.BlockSpec((tq, DA_V_COLS), lambda b, i: (b * nq + i, 0)),
        out_shape=jax.ShapeDtypeStruct((n_batch * seq, DA_V_COLS), BF16),
        scratch_shapes=[
            pltpu.VMEM((n_hm, tq, LANES), BF16),
            pltpu.VMEM((n_hm, 1, tq), F32),
            pltpu.VMEM((n_hm, 1, tq), F32),
            pltpu.VMEM((n_hm, DA_V_DIM, tq), F32),
        ],
        compiler_params=_cparams("arbitrary", "arbitrary"),
        name="diff_attention",
    )(lq, g_rows, qd, kd, vdt, kd, vdt)


def _ctx_attn_kernel(sink_ref, lq_ref, g_ref, qw_ref, kw_ref, vw_ref, qd_ref, kd_ref, vd_ref, ow_ref, od_ref,
                     *, lam_init):
    outs = []
    for hk in range(WA_KV_HEADS):
        sl = slice(hk * HEAD_DIM, (hk + 1) * HEAD_DIM)
        k, v = kw_ref[:, sl], vw_ref[:, sl]
        for g in range(WA_GROUP):
            h = hk * WA_GROUP + g
            outs.append(_softmax_sink_attend(qw_ref[:, h * HEAD_DIM:(h + 1) * HEAD_DIM], k, v, sink_ref[h], None))
    ow_ref[...] = jnp.concatenate(outs, axis=1).astype(BF16)
    per_blk = LANES // DA_QK_DIM
    acc, den = [], []
    for hm in range(2 * DA_HEADS):
        kb, vb = hm // per_blk, hm // 4
        s = lax.dot_general(_masked_q(qd_ref, hm), kd_ref[:, kb * LANES:(kb + 1) * LANES],
                            (((1,), (1,)), ((), ())), preferred_element_type=F32)
        p = jnp.exp2(s - jnp.max(s, axis=1, keepdims=True))
        den.append(jnp.sum(p, axis=1, keepdims=True))
        acc.append(jnp.dot(p.astype(BF16), vd_ref[:, vb * LANES:(vb + 1) * LANES], preferred_element_type=F32))
    lam = _diff_lambda(lq_ref, lam_init)
    od_ref[...] = _diff_finish(acc[0::2], den[0::2], acc[1::2], den[1::2], lam, g_ref[...], lam_init).astype(BF16)


def ctx_attention(sink, lq, g128, qw, kw, vw, qd, kd, vd, *, n_batch, seq, n_ctx, lam_init):
    ctx0 = (n_batch * seq) // n_ctx
    ctx = lambda b: (ctx0 + b, 0)
    out = lambda b: (b, 0)
    spec = lambda n, f: pl.BlockSpec((n_ctx, n), f)
    return pl.pallas_call(
        functools.partial(_ctx_attn_kernel, lam_init=lam_init),
        grid=(n_batch,),
        in_specs=[
            pl.BlockSpec(memory_space=pltpu.SMEM),
            pl.BlockSpec((4, DA_QK_DIM), lambda b: (0, 0)),
            pl.BlockSpec((1, LANES), lambda b: (0, 0)),
            spec(WA_Q_COLS, ctx), spec(WA_KV_COLS, ctx), spec(WA_KV_COLS, ctx),
            spec(DA_QK_COLS, ctx), spec(DA_QK_COLS, ctx), spec(DA_V_COLS, ctx),
        ],
        out_specs=[spec(WA_Q_COLS, out), spec(DA_V_COLS, out)],
        out_shape=[jax.ShapeDtypeStruct((n_batch * n_ctx, WA_Q_COLS), BF16),
                   jax.ShapeDtypeStruct((n_batch * n_ctx, DA_V_COLS), BF16)],
        compiler_params=_cparams("arbitrary"),
        name="ctx_attention",
    )(sink, lq, g128, qw, kw, vw, qd, kd, vd)


def _filter_kernel(feat_ref, w1_ref, b1_ref, fr_ref, w2_ref, b2_ref, w3_ref, dec_ref, o_ref, *, transposed):
    feat = feat_ref[...]
    fr = fr_ref[...]
    z = jnp.sin(fr * (jnp.dot(feat, w1_ref[...], precision=HI, preferred_element_type=F32) + b1_ref[...]))
    z = jnp.sin(fr * (jnp.dot(z, w2_ref[...], precision=HI, preferred_element_type=F32) + b2_ref[...]))
    h = jnp.dot(z, w3_ref[...], precision=HI, preferred_element_type=F32)
    h = h * jnp.exp(-feat[:, 0:1] * jnp.abs(dec_ref[...]))
    rows = lax.broadcasted_iota(jnp.int32, h.shape, 0) + pl.program_id(0) * h.shape[0]
    cols = lax.broadcasted_iota(jnp.int32, h.shape, 1)
    h = jnp.where((rows == 0) & (cols % (2 * HY_WIDTH) >= HY_WIDTH), 0.0, h)
    if transposed:
        for r in range(h.shape[0] // LANES):
            o_ref[r] = h[r * LANES:(r + 1) * LANES, :].T
    else:
        o_ref[...] = h


def hyena_filter_features(n):
    t = jnp.linspace(0.0, 1.0, n, dtype=F32)[:, None]
    bands = jnp.linspace(1e-4, HY_BANDS - 1, HY_BANDS, dtype=F32)
    phase = (2.0 * math.pi / n) * jnp.arange(n, dtype=F32)[:, None] * bands
    feat = jnp.concatenate([t, jnp.cos(phase), -jnp.sin(phase)], axis=-1)
    return jnp.pad(feat, ((0, 0), (0, LANES - HY_EMB_DIM)))


def hyena_filters(feat, w1p, b1, freq, w2, b2, w3, decay, *, transposed):
    n = feat.shape[0]
    tr = min(n, 512)
    ncol = w3.shape[1]
    fw = w2.shape[0]
    const = lambda i: (0, 0)
    if transposed:
        out_spec = pl.BlockSpec((tr // LANES, ncol, LANES), lambda i: (i, 0, 0))
        out_shape = jax.ShapeDtypeStruct((n // LANES, ncol, LANES), F32)
    else:
        out_spec = pl.BlockSpec((tr, ncol), lambda i: (i, 0))
        out_shape = jax.ShapeDtypeStruct((n, ncol), F32)
    return pl.pallas_call(
        functools.partial(_filter_kernel, transposed=transposed),
        grid=(n // tr,),
        in_specs=[
            pl.BlockSpec((tr, LANES), lambda i: (i, 0)),
            pl.BlockSpec((LANES, fw), const), pl.BlockSpec((1, fw), const), pl.BlockSpec((1, fw), const),
            pl.BlockSpec((fw, fw), const), pl.BlockSpec((1, fw), const),
            pl.BlockSpec((fw, ncol), const), pl.BlockSpec((1, ncol), const),
        ],
        out_specs=out_spec,
        out_shape=out_shape,
        compiler_params=_cparams("arbitrary"),
        name="hyena_filters",
    )(feat, w1p, b1, freq, w2, b2, w3, decay)


def dft_constants(seq):
    n = 2 * seq
    n1 = n // LANES
    a_rows = n1 // 2
    k1 = np.arange(n1, dtype=np.float64)[:, None]
    a = np.arange(a_rows, dtype=np.float64)[None, :]
    th = 2.0 * np.pi * k1 * a / n1
    c1, s1 = np.cos(th), np.sin(th)
    m1_data = np.block([[c1, s1], [-s1, c1]])
    half_rows = -(-(n1 // 2 + 1) // SUBLANES) * SUBLANES
    pad = np.zeros((half_rows - (n1 // 2 + 1), a_rows))
    m1_real = np.concatenate([c1[:n1 // 2 + 1], pad, -s1[:n1 // 2 + 1], pad], axis=0)
    m3 =np.block([[c1.T, -s1.T], [s1.T, c1.T]]) / n
    b = np.arange(LANES, dtype=np.float64)[None, :]
    psi = 2.0 * np.pi * k1 * b / n
    twc, tws = np.cos(psi), np.sin(psi)
    bb = np.arange(LANES, dtype=np.float64)
    phi = 2.0 * np.pi * np.outer(bb, bb) / LANES
    c2, s2 = np.cos(phi), np.sin(phi)
    w2f = np.block([[c2, -s2], [s2, c2]])
    w2i = np.block([[c2, s2], [-s2, c2]])
    f = lambda x: jnp.asarray(x, dtype=F32)
    return dict(m1_data=f(m1_data), m1_real=f(m1_real), m3=f(m3), twc=f(twc[:, None, :]), tws=f(tws[:, None, :]),
                w2f=f(w2f), w2i=f(w2i), n1=n1, a_rows=a_rows, half_rows=half_rows)


def _split_bf16(x):
    hi = x.astype(BF16)
    return hi, (x - hi.astype(F32)).astype(BF16)


def _mm3(a, b):
    ah, al = _split_bf16(a)
    bh, bl = _split_bf16(b)
    dot = lambda x, y: jnp.dot(x, y, preferred_element_type=F32)
    return dot(ah, bh) + (dot(ah, bl) + dot(al, bh))


def _leftmm_kernel(m_ref, x_ref, o_ref):
    o_ref[...] = _mm3(m_ref[...], x_ref[...])


def left_matmul(m, x, ct):
    r, k = m.shape
    cols = x.shape[1]
    return pl.pallas_call(
        _leftmm_kernel,
        grid=(cols // ct,),
        in_specs=[pl.BlockSpec((r, k), lambda j: (0, 0)), pl.BlockSpec((k, ct), lambda j: (0, j))],
        out_specs=pl.BlockSpec((r, ct), lambda j: (0, j)),
        out_shape=jax.ShapeDtypeStruct((r, cols), F32),
        compiler_params=_cparams("arbitrary"),
        name="dft_rows",
    )(m, x)


def _leftmm_gate_kernel(m_ref, zr_ref, zi_ref, x_ref, zold_ref, bias_ref, o_ref):
    n1 = zr_ref.shape[0]
    y = _mm3(m_ref[:, :n1], zr_ref[...]) + _mm3(m_ref[:, n1:], zi_ref[...])
    zold = zold_ref[...]
    o_ref[...] = x_ref[...] * (y + bias_ref[...] * zold)


def left_matmul_gate(m, zr, zi, xg, zold, bias_cols, ct):
    r, k = m.shape
    n1, cols = zr.shape
    col = lambda j: (0, j)
    return pl.pallas_call(
        _leftmm_gate_kernel,
        grid=(cols // ct,),
        in_specs=[pl.BlockSpec((r, k), lambda j: (0, 0)), pl.BlockSpec((n1, ct), col), pl.BlockSpec((n1, ct), col),
                  pl.BlockSpec((r, ct), col), pl.BlockSpec((r, ct), col), pl.BlockSpec((1, ct), col)],
        out_specs=pl.BlockSpec((r, ct), col),
        out_shape=jax.ShapeDtypeStruct((r, cols), F32),
        compiler_params=_cparams("arbitrary"),
        name="idft_rows_gate",
    )(m, zr, zi, xg, zold, bias_cols)


def _twiddle_fwd(yr, yi, c, s):
    return yr * c + yi * s, yi * c - yr * s


def _spectrum_kernel(yr_ref, yi_ref, twc_ref, tws_ref, w2f_ref, hr_ref, hi_ref, *, kb, n1):
    for t in range(kb):
        mirrored = pl.program_id(0) * kb + t > n1 // 2
        yi = yi_ref[t]
        ypr, ypi = _twiddle_fwd(yr_ref[t], jnp.where(mirrored, -yi, yi), twc_ref[t], tws_ref[t])
        x = _mm3(jnp.concatenate([ypr, ypi], axis=1), w2f_ref[...])
        xr, xi = x[:, :LANES], x[:, LANES:]
        for o in range(HY_ORDER):
            f0 = slice((2 * o) * HY_WIDTH, (2 * o + 1) * HY_WIDTH)
            f1 = slice((2 * o + 1) * HY_WIDTH, (2 * o + 2) * HY_WIDTH)
            hr_ref[o, t] = xr[f0] + xr[f1]
            hi_ref[o, t] = xi[f0] - xi[f1]


def filter_spectrum(y, consts, kb):
    assert kb == 1
    n1, half_rows = consts["n1"], consts["half_rows"]
    nc = 2 * HY_ORDER * HY_WIDTH
    y4 = y.reshape(2, half_rows, nc, LANES)
    tw = pl.BlockSpec((kb, 1, LANES), lambda i: (i, 0, 0))
    out = pl.BlockSpec((HY_ORDER, kb, HY_WIDTH, LANES), lambda i: (0, i, 0, 0))
    shp = jax.ShapeDtypeStruct((HY_ORDER, n1, HY_WIDTH, LANES), F32)
    src = lambda i: jnp.where(i > n1 // 2, n1 - i, i)
    return pl.pallas_call(
        functools.partial(_spectrum_kernel, kb=kb, n1=n1),
        grid=(n1 // kb,),
        in_specs=[
            pl.BlockSpec((None, kb, nc, LANES), lambda i: (0, src(i), 0, 0)),
            pl.BlockSpec((None, kb, nc, LANES), lambda i: (1, src(i), 0, 0)),
            tw, tw,
            pl.BlockSpec((2 * LANES, 2 * LANES), lambda i: (0, 0)),
        ],
        out_specs=[out, out],
        out_shape=[shp, shp],
        compiler_params=_cparams("arbitrary"),
        name="filter_spectrum",
    )(y4, y4, consts["twc"], consts["tws"], consts["w2f"])


def _freq_kernel(yr_ref, yi_ref, twc_ref, tws_ref, w2f_ref, w2i_ref, hr_ref, hi_ref, zr_ref, zi_ref, *, kb):
    for t in range(kb):
        c, s = twc_ref[t], tws_ref[t]
        ypr, ypi = _twiddle_fwd(yr_ref[t], yi_ref[t], c, s)
        x = _mm3(jnp.concatenate([ypr, ypi], axis=1), w2f_ref[...])
        xr, xi = x[:, :LANES], x[:, LANES:]
        hr, hi = hr_ref[t], hi_ref[t]
        gr = xr * hr - xi * hi
        gi = xr * hi + xi * hr
        z = _mm3(jnp.concatenate([gr, gi], axis=1), w2i_ref[...])
        zr, zi = z[:, :LANES], z[:, LANES:]
        zr_ref[t] = zr * c - zi * s
        zi_ref[t] = zi * c + zr * s


def freq_multiply(y, hr, hi, order, consts, kb):
    n1 = consts["n1"]
    y4 = y.reshape(2, n1, HY_WIDTH, LANES)
    tw = pl.BlockSpec((kb, 1, LANES), lambda i: (i, 0, 0))
    blk = lambda p: pl.BlockSpec((None, kb, HY_WIDTH, LANES), lambda i: (p, i, 0, 0))
    wspec = pl.BlockSpec((2 * LANES, 2 * LANES), lambda i: (0, 0))
    out = pl.BlockSpec((kb, HY_WIDTH, LANES), lambda i: (i, 0, 0))
    shp = jax.ShapeDtypeStruct((n1, HY_WIDTH, LANES), F32)
    return pl.pallas_call(
        functools.partial(_freq_kernel, kb=kb),
        grid=(n1 // kb,),
        in_specs=[blk(0), blk(1), tw, tw, wspec, wspec, blk(order), blk(order)],
        out_specs=[out, out],
        out_shape=[shp, shp],
        compiler_params=_cparams("arbitrary"),
        name="freq_multiply",
    )(y4, y4, consts["twc"], consts["tws"], consts["w2f"], consts["w2i"], hr, hi)


def _short_conv_rows(u, prev_row, next_row, w_ref, b_ref):
    rows = lax.broadcasted_iota(jnp.int32, u.shape, 0)
    um1 = jnp.where(rows == 0, prev_row, pltpu.roll(u, 1, 0))
    up1 = jnp.where(rows == u.shape[0] - 1, next_row, pltpu.roll(u, u.shape[0] - 1, 0))
    return b_ref[...] + um1 * w_ref[0:1, :] + u * w_ref[1:2, :] + up1 * w_ref[2:3, :]


def _shortconv_t_kernel(u_ref, up_ref, un_ref, w_ref, b_ref, o_ref, *, n_tiles):
    i = pl.program_id(1)
    prev_row = jnp.where(i > 0, up_ref[SUBLANES - 1:SUBLANES, :], 0.0)
    next_row = jnp.where(i < n_tiles - 1, un_ref[0:1, :], 0.0)
    v = _short_conv_rows(u_ref[...], prev_row, next_row, w_ref, b_ref)
    for part in range(HY_ORDER + 1):
        for r in range(v.shape[0] // LANES):
            o_ref[part, r] = v[r * LANES:(r + 1) * LANES, part * HY_WIDTH:(part + 1) * HY_WIDTH].T


def short_conv_transposed(hy, conv_w, conv_b, *, n_batch, seq, tm):
    nt = seq // tm
    per8 = tm // SUBLANES
    a_rows = seq // LANES
    prev = lambda b, i: (jnp.maximum((b * nt + i) * per8 - 1, 0), 0)
    nxt = lambda b, i: (jnp.minimum((b * nt + i + 1) * per8, n_batch * nt * per8 - 1), 0)
    return pl.pallas_call(
        functools.partial(_shortconv_t_kernel, n_tiles=nt),
        grid=(n_batch, nt),
        in_specs=[
            pl.BlockSpec((tm, HY_COLS), lambda b, i: (b * nt + i, 0)),
            pl.BlockSpec((SUBLANES, HY_COLS), prev),
            pl.BlockSpec((SUBLANES, HY_COLS), nxt),
            pl.BlockSpec((3, HY_COLS), lambda b, i: (0, 0)),
            pl.BlockSpec((1, HY_COLS), lambda b, i: (0, 0)),
        ],
        out_specs=pl.BlockSpec((HY_ORDER + 1, None, tm // LANES, HY_WIDTH, LANES), lambda b, i: (0, b, i, 0, 0)),
        out_shape=jax.ShapeDtypeStruct((HY_ORDER + 1, n_batch, a_rows, HY_WIDTH, LANES), F32),
        compiler_params=_cparams("arbitrary", "arbitrary"),
        name="short_conv_transposed",
    )(hy, hy, hy, conv_w, conv_b)


def _untranspose_kernel(z_ref, o_ref):
    for r in range(z_ref.shape[0]):
        o_ref[r * LANES:(r + 1) * LANES, :] = z_ref[r].T.astype(BF16)


def untranspose(zt, *, n_batch, seq):
    a_rows = seq // LANES
    ta = min(SUBLANES, a_rows)
    nt = a_rows // ta
    return pl.pallas_call(
        _untranspose_kernel,
        grid=(n_batch, nt),
        in_specs=[pl.BlockSpec((None, ta, HY_WIDTH, LANES), lambda b, i: (b, i, 0, 0))],
        out_specs=pl.BlockSpec((ta * LANES, HY_WIDTH), lambda b, i: (b * nt + i, 0)),
        out_shape=jax.ShapeDtypeStruct((n_batch * seq, HY_WIDTH), BF16),
        compiler_params=_cparams("arbitrary", "arbitrary"),
        name="untranspose",
    )(zt)


def hyena_latent(hy, conv_w, conv_b, hy_bias, filt_t, consts, *, n_batch, seq):
    assert n_batch == 2, "the two batches are packed as real / imaginary parts of one DFT"
    n1, a_rows = consts["n1"], consts["a_rows"]
    cols = HY_WIDTH * LANES
    ct = min(cols, 2048)
    kb = min(n1, 4)
    parts = short_conv_transposed(hy, conv_w, conv_b, n_batch=n_batch, seq=seq, tm=min(seq, 1024))
    parts = parts.reshape(HY_ORDER + 1, n_batch * a_rows, cols)
    yf = left_matmul(consts["m1_real"], filt_t.reshape(a_rows, 2 * HY_ORDER * cols), ct)
    hr, hi = filter_spectrum(yf, consts, 1)
    z = parts[HY_ORDER]
    for o in range(HY_ORDER):
        y = left_matmul(consts["m1_data"], z, ct)
        zr, zi = freq_multiply(y, hr, hi, o, consts, kb)
        bias_cols = jnp.repeat(hy_bias[o], LANES)[None, :]
        z = left_matmul_gate(consts["m3"], zr.reshape(n1, cols), zi.reshape(n1, cols), parts[o], z, bias_cols, ct)
    return untranspose(z.reshape(n_batch, a_rows, HY_WIDTH, LANES), n_batch=n_batch, seq=seq)


def _ctx_hyena_kernel(u_ref, w_ref, b_ref, h_ref, bias_ref, fc_ref, fs_ref, fct_ref, fst_ref, o_ref):
    u = u_ref[...]
    zero = jnp.zeros((1, u.shape[1]), F32)
    v = _short_conv_rows(u, zero, zero, w_ref, b_ref)
    z = v[:, HY_ORDER * HY_WIDTH:]
    mm = lambda a, b: jnp.dot(a, b, precision=HI, preferred_element_type=F32)
    fc, fs = fc_ref[...], fs_ref[...]
    inv_n = 1.0 / fc.shape[0]
    for o in range(HY_ORDER):
        h0 = h_ref[:, (2 * o) * HY_WIDTH:(2 * o + 1) * HY_WIDTH]
        h1 = h_ref[:, (2 * o + 1) * HY_WIDTH:(2 * o + 2) * HY_WIDTH]
        sr, si = mm(fc, h0 + h1), mm(fs, h1 - h0)
        zr, zi = mm(fc, z), -mm(fs, z)
        gr = zr * sr - zi * si
        gi = zr * si + zi * sr
        y = (mm(fct_ref[...], gr) - mm(fst_ref[...], gi)) * inv_n
        z = v[:, o * HY_WIDTH:(o + 1) * HY_WIDTH] * (y + bias_ref[o:o + 1, :] * z)
    o_ref[...] = z.astype(BF16)


def ctx_hyena(hy, conv_w, conv_b, filt, hy_bias, *, n_batch, seq, n_ctx):
    n = 2 * n_ctx
    k = np.arange(n, dtype=np.float64)[:, None]
    m = np.arange(n_ctx, dtype=np.float64)[None, :]
    ang = 2.0 * np.pi * k * m / n
    fc, fs = jnp.asarray(np.cos(ang), F32), jnp.asarray(np.sin(ang), F32)
    ctx0 = (n_batch * seq) // n_ctx
    const = lambda b: (0, 0)
    return pl.pallas_call(
        _ctx_hyena_kernel,
        grid=(n_batch,),
        in_specs=[
            pl.BlockSpec((n_ctx, HY_COLS), lambda b: (ctx0 + b, 0)),
            pl.BlockSpec((3, HY_COLS), const), pl.BlockSpec((1, HY_COLS), const),
            pl.BlockSpec((n_ctx, 2 * HY_ORDER * HY_WIDTH), const),
            pl.BlockSpec((HY_ORDER, HY_WIDTH), const),
            pl.BlockSpec((n, n_ctx), const), pl.BlockSpec((n, n_ctx), const),
            pl.BlockSpec((n_ctx, n), const), pl.BlockSpec((n_ctx, n), const),
        ],
        out_specs=pl.BlockSpec((n_ctx, HY_WIDTH), lambda b: (b, 0)),
        out_shape=jax.ShapeDtypeStruct((n_batch * n_ctx, HY_WIDTH), BF16),
        compiler_params=_cparams("arbitrary"),
        name="ctx_hyena",
    )(hy, conv_w, conv_b, filt, hy_bias, fc, fs, fc.T, fs.T)


def _outproj_kernel(x_ref, yh_ref, yw_ref, yd_ref, mod_ref, wo_ref, g_ref, wq_ref, xo_ref, h2_ref, qp_ref):
    y = jnp.dot(yh_ref[...], wo_ref[0:HY_WIDTH, :], preferred_element_type=F32)
    y = y + jnp.dot(yw_ref[...], wo_ref[HY_WIDTH:HY_WIDTH + WA_Q_COLS, :], preferred_element_type=F32)
    y = y + jnp.dot(yd_ref[...], wo_ref[HY_WIDTH + WA_Q_COLS:, :], preferred_element_type=F32)
    xn = x_ref[...] + mod_ref[2:3, :] * y
    xo_ref[...] = xn
    h2 = _rmsnorm_mod(xn, g_ref[...], mod_ref[3:4, :], mod_ref[4:5, :]).astype(BF16)
    h2_ref[...] = h2
    qp_ref[...] = jnp.dot(h2, wq_ref[...], preferred_element_type=F32)


def out_proj(x_all, y_hy, y_wa, y_da, mod, wo_bf, g2, wq_bf, *, n_rows, n_batch, seq, tm):
    d = x_all.shape[1]
    tpb = seq // tm
    grp = functools.partial(_group_of_tile, tiles_per_batch=tpb, n_batch=n_batch)
    row = lambda i: (i, 0)
    const = lambda i: (0, 0)
    mix = HY_WIDTH + WA_Q_COLS + DA_V_COLS
    return pl.pallas_call(
        _outproj_kernel,
        grid=(n_rows // tm,),
        in_specs=[
            pl.BlockSpec((tm, d), row), pl.BlockSpec((tm, HY_WIDTH), row), pl.BlockSpec((tm, WA_Q_COLS), row),
            pl.BlockSpec((tm, DA_V_COLS), row),
            pl.BlockSpec((None, N_MOD, d), lambda i: (grp(i), 0, 0)),
            pl.BlockSpec((mix, d), const), pl.BlockSpec((1, d), const), pl.BlockSpec((d, PEER_QCOLS), const),
        ],
        out_specs=[pl.BlockSpec((tm, d), row), pl.BlockSpec((tm, d), row), pl.BlockSpec((tm, PEER_QCOLS), row)],
        out_shape=[jax.ShapeDtypeStruct((n_rows, d), F32), jax.ShapeDtypeStruct((n_rows, d), BF16),
                   jax.ShapeDtypeStruct((n_rows, PEER_QCOLS), F32)],
        compiler_params=_cparams("arbitrary"),
        name="out_proj",
    )(x_all, y_hy, y_wa, y_da, mod, wo_bf, g2, wq_bf)


def _sort_pairs(n):
    pairs = []

    def merge(lo, cnt, r):
        step = r * 2
        if step < cnt:
            merge(lo, cnt, step)
            merge(lo + r, cnt, step)
            for i in range(lo + r, lo + cnt - r, step):
                pairs.append((i, i + r))
        else:
            pairs.append((lo, lo + r))

    def sort(lo, cnt):
        if cnt > 1:
            half = cnt // 2
            sort(lo, half)
            sort(lo + half, half)
            merge(lo, cnt, 1)

    sort(0, n)
    return pairs


def _sort_desc(vals):
    n = 1
    while n < len(vals):
        n *= 2
    v = list(vals) + [None] * (n - len(vals))
    for i, j in _sort_pairs(n):
        a, b = v[i], v[j]
        if b is None:
            continue
        if a is None:
            v[i], v[j] = b, None
        else:
            v[i], v[j] = jnp.maximum(a, b), jnp.minimum(a, b)
    return v[:len(vals)]


def _top16_rows(s):
    k = PEER_TOPK
    v = _sort_desc([s[SUBLANES * r:SUBLANES * (r + 1), :] for r in range(s.shape[0] // SUBLANES)])
    for shift in (4, 2, 1):
        other = [pltpu.roll(x, shift, 0) for x in v]
        v = [jnp.maximum(v[i], other[k - 1 - i]) for i in range(k)]
        d = k // 2
        while d >= 1:
            for i in range(k):
                if (i & d) == 0:
                    a, b = v[i], v[i + d]
                    v[i], v[i + d] = jnp.maximum(a, b), jnp.minimum(a, b)
            d //= 2
    return v


def _peer_topk_kernel(q_ref, keys_ref, cnt_ref, e1_ref, rank_ref, e2_ref):
    k = PEER_TOPK
    tmk = q_ref.shape[0]
    sub = lax.broadcasted_iota(jnp.int32, (SUBLANES, tmk), 0)
    scores, tops = [], []
    for hp in range(2 * PEER_HEADS):
        s = lax.dot_general(keys_ref[hp], q_ref[:, hp * N_KEYS:(hp + 1) * N_KEYS], (((1,), (1,)), ((), ())),
                            precision=HI, preferred_element_type=F32)
        scores.append(s)
        top = _top16_rows(s)
        top.append(jnp.max(jnp.where(s < top[k - 1][0:1, :], s, NEG_BIG), axis=0, keepdims=True))
        tops.append(top)
    packed = []
    for p in range(2):
        lst = []
        for r in range(k + 1):
            acc = jnp.broadcast_to(tops[p][r], (SUBLANES, tmk))
            for h in range(1, PEER_HEADS):
                acc = jnp.where(sub == h, tops[2 * h + p][r], acc)
            lst.append(acc)
        packed.append(lst)
    cand = [packed[0][i] + packed[1][j] for i in range(k) for j in range(k) if (i + 1) * (j + 1) <= k]
    cs = _sort_desc(cand)
    c17 = jnp.maximum(cs[k], jnp.maximum(packed[0][k] + packed[1][0], packed[0][0] + packed[1][k]))
    tau = 0.5 * (cs[k - 1] + c17)
    zsum = jnp.zeros_like(tau)
    for r in range(k):
        zsum = zsum + jnp.exp(cs[r] - cs[0])
    inv_z = 1.0 / zsum
    for h in range(PEER_HEADS):
        s1, s2 = scores[2 * h], scores[2 * h + 1]
        row = lambda x: x[h:h + 1, :]
        top2 = [t2[0:1, :] for t2 in tops[2 * h + 1][:k]]
        cnt_ref[h] = _count_sorted(top2, row(tau) - s1, strict=False)
        tiled = lambda x: x.astype(BF16).reshape(N_KEYS // BF16_ROWS, BF16_ROWS, tmk)
        rank_ref[h] = tiled(_count_sorted(top2, s2, strict=True))
        e1_ref[h] = 0.5 * jnp.exp(s1 - row(packed[0][0]))
        e2_ref[h] = tiled(jnp.exp(s2 - row(packed[1][0])) * row(inv_z))


def _count_sorted(tops, x, *, strict):
    ge = (lambda a: a > x) if strict else (lambda a: a >= x)
    b3 = ge(tops[7])
    b2 = ge(jnp.where(b3, tops[11], tops[3]))
    b1 = ge(jnp.where(b3, jnp.where(b2, tops[13], tops[9]), jnp.where(b2, tops[5], tops[1])))
    hi = jnp.where(b2, jnp.where(b1, tops[14], tops[12]), jnp.where(b1, tops[10], tops[8]))
    lo = jnp.where(b2, jnp.where(b1, tops[6], tops[4]), jnp.where(b1, tops[2], tops[0]))
    b0 = ge(jnp.where(b3, hi, lo))
    cnt = (jnp.where(b3, 8.0, 0.0) + jnp.where(b2, 4.0, 0.0)) + (jnp.where(b1, 2.0, 0.0) + jnp.where(b0, 1.0, 0.0))
    return jnp.where(ge(tops[15]), 16.0, cnt)


def peer_topk(qp, keys, *, tmk):
    t = qp.shape[0]
    tiles = N_KEYS // BF16_ROWS
    out = pl.BlockSpec((PEER_HEADS, N_KEYS, tmk), lambda i: (0, 0, i))
    out16 = pl.BlockSpec((PEER_HEADS, tiles, BF16_ROWS, tmk), lambda i: (0, 0, 0, i))
    f32 = jax.ShapeDtypeStruct((PEER_HEADS, N_KEYS, t), F32)
    b16 = jax.ShapeDtypeStruct((PEER_HEADS, tiles, BF16_ROWS, t), BF16)
    return pl.pallas_call(
        _peer_topk_kernel,
        grid=(t // tmk,),
        in_specs=[pl.BlockSpec((tmk, PEER_QCOLS), lambda i: (i, 0)),
                  pl.BlockSpec((2 * PEER_HEADS, N_KEYS, PEER_QDIM // 2), lambda i: (0, 0, 0))],
        out_specs=[out, out, out16, out16],
        out_shape=[f32, f32, b16, b16],
        compiler_params=_cparams("arbitrary"),
        name="peer_topk",
    )(qp, keys)


def _peer_dense_kernel(x_ref, h_ref, mod_ref, u_ref, vt_ref, cnt_ref, e1_ref, rank_ref, e2_ref, g_ref, o_ref,
                       a_scr, w_scr, acc_scr, *, final_norm, chunk, col_block):
    e = pl.program_id(1)

    @pl.when(e == 0)
    def _():
        acc_scr[...] = jnp.zeros(acc_scr.shape, F32)

    n_chunks = u_ref.shape[0] // chunk
    per_chunk = chunk // N_KEYS
    sqrt_half = math.sqrt(0.5)
    tm = h_ref.shape[0]
    tiles = N_KEYS // BF16_ROWS
    zero = jnp.zeros((), BF16)

    def pre_activations(p):
        a_scr[p % 2] = lax.dot_general(u_ref[p * chunk:(p + 1) * chunk, :], h_ref[...], (((1,), (1,)), ((), ())),
                                       preferred_element_type=F32)

    pre_activations(0)
    for p in range(n_chunks):
        if p + 1 < n_chunks:
            pre_activations(p + 1)
        for ii, cb in [(ii, cb) for ii in range(p * per_chunk, (p + 1) * per_chunk) for cb in range(tm // col_block)]:
            rows = slice((ii - p * per_chunk) * N_KEYS, (ii - p * per_chunk + 1) * N_KEYS)
            cols = slice(cb * col_block, (cb + 1) * col_block)
            a = a_scr[p % 2, rows, cols]
            act = a * (1.0 + lax.erf(a * sqrt_half))
            act = act.astype(BF16).reshape(tiles, BF16_ROWS, col_block)
            gate = None
            for h in range(PEER_HEADS):
                cnt = jnp.broadcast_to(cnt_ref[h, ii:ii + 1, cols], (BF16_ROWS, col_block)).astype(BF16)
                e1 = jnp.broadcast_to(e1_ref[h, ii:ii + 1, cols], (BF16_ROWS, col_block)).astype(BF16)
                term = jnp.where(rank_ref[h, :, :, cols] < cnt[None], e2_ref[h, :, :, cols] * e1[None], zero)
                gate = term if gate is None else gate + term
            w_scr[p % 2, rows, cols] = (gate * act).reshape(N_KEYS, col_block)
        acc_scr[...] += jnp.dot(vt_ref[:, p * chunk:(p + 1) * chunk], w_scr[p % 2], preferred_element_type=F32)

    @pl.when(e == pl.num_programs(1) - 1)
    def _():
        xn = x_ref[...] + mod_ref[5:6, :] * acc_scr[...].T
        if final_norm:
            ms = jnp.mean(xn * xn, axis=-1, keepdims=True)
            xn = xn * lax.rsqrt(ms + NORM_EPS) * g_ref[...]
        o_ref[...] = xn


def peer_dense(x, h2, mod, u_bf, vt_bf, cnt, e1, rank, e2, g_final, *, n_batch, seq, tm, te, chunk, col_block,
               final_norm):
    n_rows, d = x.shape
    tpb = seq // tm
    grp = functools.partial(_group_of_tile, tiles_per_batch=tpb, n_batch=n_batch)
    n_i = te // N_KEYS
    tok = lambda i, e: (i, 0)
    rows_i = pl.BlockSpec((PEER_HEADS, n_i, tm), lambda i, e: (0, e, i))
    tiles = N_KEYS // BF16_ROWS
    full = pl.BlockSpec((PEER_HEADS, tiles, BF16_ROWS, tm), lambda i, e: (0, 0, 0, i))
    return pl.pallas_call(
        functools.partial(_peer_dense_kernel, final_norm=final_norm, chunk=chunk, col_block=col_block),
        grid=(n_rows // tm, N_EXPERTS // te),
        in_specs=[
            pl.BlockSpec((tm, d), tok), pl.BlockSpec((tm, d), tok),
            pl.BlockSpec((None, N_MOD, d), lambda i, e: (grp(i), 0, 0)),
            pl.BlockSpec((te, d), lambda i, e: (e, 0)),
            pl.BlockSpec((d, te), lambda i, e: (0, e)),
            rows_i, rows_i, full, full,
            pl.BlockSpec((1, d), lambda i, e: (0, 0)),
        ],
        out_specs=pl.BlockSpec((tm, d), tok),
        out_shape=jax.ShapeDtypeStruct((n_rows, d), F32),
        scratch_shapes=[pltpu.VMEM((2, chunk, tm), F32), pltpu.VMEM((2, chunk, tm), BF16), pltpu.VMEM((d, tm), F32)],
        compiler_params=_cparams("arbitrary", "arbitrary"),
        name="peer_dense",
    )(x, h2, mod, u_bf, vt_bf, cnt, e1, rank, e2, g_final)


def kernel(x, c, ctx, c_ctx, w_mod, b_mod, norm1_g, w_in, hy_conv_w, hy_conv_b, hy_w1, hy_b1, hy_freq, hy_w2,
           hy_b2, hy_w3, hy_decay, hy_bias, wa_sink, da_lq1, da_lk1, da_lq2, da_lk2, da_norm_g, w_out, norm2_g,
           peer_wq, peer_keys, peer_u, peer_v, final_g):
    n_batch, seq, d = x.shape
    n_ctx = ctx.shape[1]
    depth = w_in.shape[0]
    n_lat = n_batch * seq
    tm = 512 if seq % 512 == 0 else 256
    assert seq % tm == 0 and (n_batch * n_ctx) % tm == 0 and n_lat % n_ctx == 0 and n_batch + 1 <= SUBLANES

    cpad = jnp.zeros((SUBLANES, d), F32).at[:n_batch].set(c).at[n_batch].set(c_ctx)
    mod_all = mod_vectors(cpad, w_mod, b_mod).reshape(depth, SUBLANES, N_MOD, d)
    rope_wa = rope_tables(seq, HEAD_DIM, tm)
    rope_da = rope_tables(seq, DA_QK_DIM, tm)
    consts = dft_constants(seq)
    feat_lat = hyena_filter_features(seq)
    feat_ctx = hyena_filter_features(n_ctx)
    x_all = jnp.concatenate([x.reshape(n_lat, d), ctx.reshape(n_batch * n_ctx, d)], axis=0)

    out = None
    for li in range(depth):
        last = li == depth - 1
        lam_init = DA_LAMBDA_BASE - DA_LAMBDA_AMP * math.exp(-DA_LAMBDA_RATE * li)
        mod = mod_all[li]
        hy, qw, kw, vw, qd, kd, vd, vdt, vwt = proj_in(x_all, mod, norm1_g[li][None, :], w_in[li].astype(BF16), rope_wa,
                                             rope_da, n_batch=n_batch, seq=seq, tm=tm)
        w1p = jnp.pad(hy_w1[li], ((0, LANES - HY_EMB_DIM), (0, 0)))
        fargs = (w1p, hy_b1[li][None, :], hy_freq[li][None, :], hy_w2[li], hy_b2[li][None, :], hy_w3[li],
                 hy_decay[li].reshape(1, -1))
        conv_b = hy_conv_b[li][None, :]
        filt_t = hyena_filters(feat_lat, *fargs, transposed=True)
        y_hy = hyena_latent(hy, hy_conv_w[li], conv_b, hy_bias[li], filt_t, consts, n_batch=n_batch, seq=seq)
        y_wa = window_attention(wa_sink[li], qw, kw, vwt, n_batch=n_batch, seq=seq, n_ctx=n_ctx)
        lq = jnp.stack([da_lq1[li], da_lk1[li], da_lq2[li], da_lk2[li]], axis=0)
        g128 = jnp.tile(da_norm_g[li], LANES // DA_V_DIM)[None, :]
        tq = min(seq, 512)
        g_rows = jnp.broadcast_to(da_norm_g[li][:, None], (DA_V_DIM, tq))
        y_da = diff_attention(lq, g_rows, qd, kd, vdt, n_batch=n_batch, seq=seq, n_ctx=n_ctx, lam_init=lam_init,
                              tq=tq, tk=min(seq, 256), ahead=2, unroll=8 if seq % 2048 == 0 else 1)
        if not last:
            filt_c = hyena_filters(feat_ctx, *fargs, transposed=False)
            yc_hy = ctx_hyena(hy, hy_conv_w[li], conv_b, filt_c, hy_bias[li], n_batch=n_batch, seq=seq, n_ctx=n_ctx)
            yc_wa, yc_da = ctx_attention(wa_sink[li], lq, g128, qw, kw, vw, qd, kd, vd, n_batch=n_batch, seq=seq,
                                         n_ctx=n_ctx, lam_init=lam_init)
            y_hy = jnp.concatenate([y_hy, yc_hy], axis=0)
            y_wa = jnp.concatenate([y_wa, yc_wa], axis=0)
            y_da = jnp.concatenate([y_da, yc_da], axis=0)
        n_rows = n_lat if last else x_all.shape[0]
        x_mid, h2, qp = out_proj(x_all, y_hy, y_wa, y_da, mod, w_out[li].astype(BF16), norm2_g[li][None, :],
                                 peer_wq[li].astype(BF16), n_rows=n_rows, n_batch=n_batch, seq=seq, tm=tm)
        keys = peer_keys[li].reshape(2 * PEER_HEADS, N_KEYS, PEER_QDIM // 2)
        cnt, e1, rank, e2 = peer_topk(qp, keys, tmk=256)
        x_all = peer_dense(x_mid, h2, mod, peer_u[li].astype(BF16), peer_v[li].T.astype(BF16), cnt, e1, rank, e2,
                           final_g[None, :], n_batch=n_batch, seq=seq, tm=tm, te=2048, chunk=1024, col_block=tm,
                           final_norm=last)
        out = x_all
    return out.reshape(n_batch, seq, d)
```

```python
import functools
import math

import numpy as np
import jax
import jax.numpy as jnp
from jax import lax
from jax.experimental import pallas as pl
from jax.experimental.pallas import tpu as pltpu

F32 = jnp.float32
BF16 = jnp.bfloat16
HI = lax.Precision.HIGHEST

LANES = 128
SUBLANES = 8
BF16_ROWS = 16
VMEM_LIMIT_BYTES = 56 * 1024 * 1024

D_MODEL = 1024
N_MOD = 6
NORM_EPS = 1e-6
ROPE_BASE = 10000.0
GRID_W = 64
BLOCK = 128
HEAD_DIM = 64
HY_WIDTH = 256
HY_ORDER = 2
HY_BANDS = 16
HY_EMB_DIM = 1 + 2 * HY_BANDS
HY_FILTER_WIDTH = 64
HY_COLS = 3 * HY_WIDTH
WA_Q_HEADS = 8
WA_KV_HEADS = 2
WA_GROUP = 4
WINDOW = 128
WA_Q_COLS = WA_Q_HEADS * HEAD_DIM
WA_KV_COLS = WA_KV_HEADS * HEAD_DIM
WA_COLS = WA_Q_COLS + 2 * WA_KV_COLS
DA_HEADS = 4
DA_QK_DIM = 32
DA_V_DIM = 64
DA_QK_COLS = DA_HEADS * 2 * DA_QK_DIM
DA_V_COLS = DA_HEADS * DA_V_DIM
DA_COLS = 2 * DA_QK_COLS + DA_V_COLS
DA_LAMBDA_BASE = 0.8
DA_LAMBDA_AMP = 0.6
DA_LAMBDA_RATE = 0.3
IN_COLS = HY_COLS + WA_COLS + DA_COLS
PEER_HEADS = 8
N_KEYS = 128
N_EXPERTS = N_KEYS * N_KEYS
PEER_TOPK = 16
PEER_QDIM = 256
PEER_QCOLS = PEER_HEADS * PEER_QDIM
NEG_BIG = -1e30
LOG2E = math.log2(math.e)


def _cparams(*sem):
    return pltpu.CompilerParams(dimension_semantics=sem, vmem_limit_bytes=VMEM_LIMIT_BYTES)


def _rmsnorm_mod(x, g, shift, scale):
    ms = jnp.mean(x * x, axis=-1, keepdims=True)
    return (x * lax.rsqrt(ms + NORM_EPS) * g) * (1.0 + scale) + shift


def _mod_kernel(c_ref, w_ref, b_ref, o_ref):
    c = c_ref[...]
    act = c * jax.nn.sigmoid(c)
    o_ref[...] = jnp.dot(act, w_ref[...], precision=HI, preferred_element_type=F32) + b_ref[...]


def mod_vectors(cpad, w_mod, b_mod):
    n_layers, d, n = w_mod.shape
    tn = 1536
    return pl.pallas_call(
        _mod_kernel,
        grid=(n_layers, n // tn),
        in_specs=[
            pl.BlockSpec((SUBLANES, d), lambda l, j: (0, 0)),
            pl.BlockSpec((None, d, tn), lambda l, j: (l, 0, j)),
            pl.BlockSpec((None, 1, tn), lambda l, j: (l, 0, j)),
        ],
        out_specs=pl.BlockSpec((None, SUBLANES, tn), lambda l, j: (l, 0, j)),
        out_shape=jax.ShapeDtypeStruct((n_layers, SUBLANES, n), F32),
        compiler_params=_cparams("arbitrary", "arbitrary"),
        name="mod_vectors",
    )(cpad, w_mod, b_mod.reshape(n_layers, 1, n))


def _rope128(x, cos, sin_a, sin_b, m):
    return x * cos + pltpu.roll(x, LANES - m, 1) * sin_a + pltpu.roll(x, m, 1) * sin_b


def _proj_in_kernel(x_ref, mod_ref, g_ref, w_ref, cw_ref, saw_ref, sbw_ref, cd_ref, sad_ref, sbd_ref,
                    hy_ref, qw_ref, kw_ref, vw_ref, qd_ref, kd_ref, vd_ref, vdt_ref, vwt_ref):
    h = _rmsnorm_mod(x_ref[...], g_ref[...], mod_ref[0:1, :], mod_ref[1:2, :]).astype(BF16)
    hy_ref[...] = jnp.dot(h, w_ref[:, 0:HY_COLS], preferred_element_type=F32)
    wa = jnp.dot(h, w_ref[:, HY_COLS:HY_COLS + WA_COLS], preferred_element_type=F32)
    da = jnp.dot(h, w_ref[:, HY_COLS + WA_COLS:IN_COLS], preferred_element_type=F32)
    cw, saw, sbw = cw_ref[...], saw_ref[...], sbw_ref[...]
    cd, sad, sbd = cd_ref[...], sad_ref[...], sbd_ref[...]
    wa_scale = HEAD_DIM ** -0.5
    da_scale = DA_QK_DIM ** -0.5 * LOG2E
    for kb in range(WA_Q_COLS // LANES):
        blk = wa[:, kb * LANES:(kb + 1) * LANES]
        qw_ref[:, kb * LANES:(kb + 1) * LANES] = (_rope128(blk, cw, saw, sbw, 16) * wa_scale).astype(BF16)
    kw_ref[...] = _rope128(wa[:, WA_Q_COLS:WA_Q_COLS + WA_KV_COLS], cw, saw, sbw, 16).astype(BF16)
    vw = wa[:, WA_Q_COLS + WA_KV_COLS:WA_COLS]
    vw_ref[...] = vw.astype(BF16)
    vwt_ref[...] = vw.T.astype(BF16)
    for kb in range(DA_QK_COLS // LANES):
        blk = da[:, kb * LANES:(kb + 1) * LANES]
        qd_ref[:, kb * LANES:(kb + 1) * LANES] = (_rope128(blk, cd, sad, sbd, 8) * da_scale).astype(BF16)
        blk = da[:, DA_QK_COLS + kb * LANES:DA_QK_COLS + (kb + 1) * LANES]
        kd_ref[:, kb * LANES:(kb + 1) * LANES] = _rope128(blk, cd, sad, sbd, 8).astype(BF16)
    vd = da[:, 2 * DA_QK_COLS:DA_COLS]
    vd_ref[...] = vd.astype(BF16)
    vdt_ref[...] = vd.T.astype(BF16)


def _group_of_tile(i, tiles_per_batch, n_batch):
    return jnp.minimum(i // tiles_per_batch, n_batch)


def proj_in(x_all, mod, g, w_bf, rope_wa, rope_da, *, n_batch, seq, tm):
    t_all, d = x_all.shape
    tpb = seq // tm
    grp = functools.partial(_group_of_tile, tiles_per_batch=tpb, n_batch=n_batch)
    pos = lambda i: (jnp.where(i < n_batch * tpb, i % tpb, tpb), 0)
    row = lambda i: (i, 0)
    const = lambda i: (0, 0)
    tab = pl.BlockSpec((tm, LANES), pos)
    outs = [(HY_COLS, F32), (WA_Q_COLS, BF16), (WA_KV_COLS, BF16), (WA_KV_COLS, BF16),
            (DA_QK_COLS, BF16), (DA_QK_COLS, BF16), (DA_V_COLS, BF16)]
    return pl.pallas_call(
        _proj_in_kernel,
        grid=(t_all // tm,),
        in_specs=[
            pl.BlockSpec((tm, d), row),
            pl.BlockSpec((None, N_MOD, d), lambda i: (grp(i), 0, 0)),
            pl.BlockSpec((1, d), const),
            pl.BlockSpec((d, IN_COLS), const),
            tab, tab, tab, tab, tab, tab,
        ],
        out_specs=[pl.BlockSpec((tm, n), row) for n, _ in outs]
        + [pl.BlockSpec((DA_V_COLS, tm), lambda i: (0, i)), pl.BlockSpec((WA_KV_COLS, tm), lambda i: (0, i))],
        out_shape=[jax.ShapeDtypeStruct((t_all, n), dt) for n, dt in outs]
        + [jax.ShapeDtypeStruct((DA_V_COLS, t_all), BF16), jax.ShapeDtypeStruct((WA_KV_COLS, t_all), BF16)],
        compiler_params=_cparams("arbitrary"),
        name="proj_in",
    )(x_all, mod, g, w_bf, *rope_wa, *rope_da)


def rope_tables(seq, head_dim, pad_rows):
    half = head_dim // 2
    axis_dim = half
    m = axis_dim // 2
    f32 = np.float32
    t = np.arange(seq)
    rowcol = np.stack([t // GRID_W, t % GRID_W], axis=0).astype(f32)
    inv_freq = f32(ROPE_BASE) ** (-np.arange(0, axis_dim, 2, dtype=f32) / f32(axis_dim))
    ang = rowcol[:, :, None] * inv_freq[None, None, :]
    cos, sin = np.cos(ang), np.sin(ang)
    lane = np.arange(LANES)
    dd = lane % head_dim
    axis = dd // half
    sub = dd % half
    idx = sub % m
    first = sub < m
    cos_t = cos[axis, :, idx].T
    sin_t = sin[axis, :, idx].T
    sin_a = np.where(first[None, :], -sin_t, f32(0))
    sin_b = np.where(first[None, :], f32(0), sin_t)
    pad1 = np.ones((pad_rows, LANES), f32)
    pad0 = np.zeros((pad_rows, LANES), f32)
    tables = (np.concatenate([cos_t, pad1], 0), np.concatenate([sin_a, pad0], 0), np.concatenate([sin_b, pad0], 0))
    return tuple(jnp.asarray(x, F32) for x in tables)


def _softmax_sink_attend(q, k, v, sink, valid):
    s = lax.dot_general(q, k, (((1,), (1,)), ((), ())), preferred_element_type=F32)
    if valid is not None:
        nm = valid.shape[1]
        s = jnp.concatenate([jnp.where(valid, s[:, :nm], NEG_BIG), s[:, nm:]], axis=1)
    m = jnp.maximum(jnp.max(s, axis=1, keepdims=True), sink)
    p = jnp.exp(s - m)
    den = jnp.sum(p, axis=1, keepdims=True) + jnp.exp(sink - m)
    return jnp.dot(p.astype(BF16), v, preferred_element_type=F32) / den


def _wattn_kernel(sink_ref, q_ref, kp_ref, kc_ref, kn_ref, kx_ref, vp_ref, vc_ref, vn_ref, vx_ref, o_ref, qm_scr,
                  *, n_pairs, ahead):
    m_idx = pl.program_id(1)
    nq = 2 * BLOCK
    n_loc = 4 * BLOCK
    kk = lax.broadcasted_iota(jnp.int32, (n_loc, nq), 0)
    qq = lax.broadcasted_iota(jnp.int32, (n_loc, nq), 1)
    lo = jnp.where(m_idx > 0, 0, BLOCK)
    hi = jnp.where(m_idx < n_pairs - 1, n_loc, n_loc - BLOCK)
    valid = (kk >= jnp.maximum(qq, lo)) & (kk <= qq + 2 * WINDOW) & (kk < hi)
    k_all = jnp.concatenate([kp_ref[...], kc_ref[...], kn_ref[...], kx_ref[...]], axis=0)
    vt_all = jnp.concatenate([vp_ref[...], vc_ref[...], vn_ref[...], vx_ref[...]], axis=1)
    lane = lax.broadcasted_iota(jnp.int32, (nq, LANES), 1)
    for h in range(WA_Q_HEADS):
        hk = h // WA_GROUP
        blk = q_ref[:, (h // 2) * LANES:(h // 2 + 1) * LANES]
        if h % 2 != hk:
            blk = jnp.concatenate([blk[:, HEAD_DIM:], blk[:, :HEAD_DIM]], axis=1)
        keep = (lane >= hk * HEAD_DIM) & (lane < (hk + 1) * HEAD_DIM)
        qm_scr[h] = jnp.where(keep, blk, jnp.zeros_like(blk))

    def scores(h):
        return lax.dot_general(k_all, qm_scr[h], (((1,), (1,)), ((), ())), preferred_element_type=F32)

    sts = [scores(h) for h in range(ahead)]
    outs = []
    for h in range(WA_Q_HEADS):
        hk = h // WA_GROUP
        if h + ahead < WA_Q_HEADS:
            sts.append(scores(h + ahead))
        st = sts[h]
        sts[h] = None
        st = jnp.concatenate([jnp.where(valid, st[:n_loc], NEG_BIG), st[n_loc:]], axis=0)
        sink = sink_ref[h]
        m = jnp.maximum(jnp.max(st, axis=0, keepdims=True), sink)
        p = jnp.exp(st - m)
        den = jnp.sum(p, axis=0, keepdims=True) + jnp.exp(sink - m)
        o = jnp.dot(vt_all[hk * HEAD_DIM:(hk + 1) * HEAD_DIM, :], p.astype(BF16), preferred_element_type=F32)
        outs.append(o / den)
    o_ref[...] = jnp.concatenate(outs, axis=0).T.astype(BF16)


def window_attention(sink, qw, kw, vwt, *, n_batch, seq, n_ctx):
    nb = seq // BLOCK
    n_pairs = nb // 2
    assert nb % 2 == 0
    ctx0 = (n_batch * seq) // n_ctx
    cur = lambda b, m: (b * n_pairs + m, 0)
    prev = lambda b, m: (b * nb + jnp.maximum(2 * m - 1, 0), 0)
    nxt = lambda b, m: (b * nb + jnp.minimum(2 * m + 2, nb - 1), 0)
    ctx = lambda b, m: (ctx0 + b, 0)
    swap = lambda f: (lambda b, m: f(b, m)[::-1])
    return pl.pallas_call(
        functools.partial(_wattn_kernel, n_pairs=n_pairs, ahead=3),
        grid=(n_batch, n_pairs),
        in_specs=[
            pl.BlockSpec(memory_space=pltpu.SMEM),
            pl.BlockSpec((2 * BLOCK, WA_Q_COLS), cur),
            pl.BlockSpec((BLOCK, WA_KV_COLS), prev), pl.BlockSpec((2 * BLOCK, WA_KV_COLS), cur),
            pl.BlockSpec((BLOCK, WA_KV_COLS), nxt), pl.BlockSpec((n_ctx, WA_KV_COLS), ctx),
            pl.BlockSpec((WA_KV_COLS, BLOCK), swap(prev)), pl.BlockSpec((WA_KV_COLS, 2 * BLOCK), swap(cur)),
            pl.BlockSpec((WA_KV_COLS, BLOCK), swap(nxt)), pl.BlockSpec((WA_KV_COLS, n_ctx), swap(ctx)),
        ],
        out_specs=pl.BlockSpec((2 * BLOCK, WA_Q_COLS), cur),
        out_shape=jax.ShapeDtypeStruct((n_batch * seq, WA_Q_COLS), BF16),
        scratch_shapes=[pltpu.VMEM((WA_Q_HEADS, 2 * BLOCK, LANES), BF16)],
        compiler_params=_cparams("arbitrary", "arbitrary"),
        name="window_attention",
    )(sink, qw, kw, kw, kw, kw, vwt, vwt, vwt, vwt)


def _diff_lambda(lq_ref, lam_init):
    a = jnp.sum(lq_ref[0:1, :] * lq_ref[1:2, :], axis=1, keepdims=True)
    b = jnp.sum(lq_ref[2:3, :] * lq_ref[3:4, :], axis=1, keepdims=True)
    return jnp.exp(a) - jnp.exp(b) + lam_init


def _diff_finish(acc1, l1, acc2, l2, lam, g128, lam_init):
    lane = lax.broadcasted_iota(jnp.int32, acc1[0].shape, 1)
    lo = lane < DA_V_DIM
    blocks = []
    for vb in range(DA_HEADS // 2):
        o = []
        for h in (2 * vb, 2 * vb + 1):
            o.append(acc1[h] / l1[h] - lam * (acc2[h] / l2[h]))
        blk = jnp.where(lo, o[0], o[1])
        sq = blk * blk
        s_lo = jnp.sum(jnp.where(lo, sq, 0.0), axis=1, keepdims=True)
        s_hi = jnp.sum(jnp.where(lo, 0.0, sq), axis=1, keepdims=True)
        ms = jnp.where(lo, s_lo, s_hi) * (1.0 / DA_V_DIM)
        blocks.append(blk * lax.rsqrt(ms + NORM_EPS) * g128 * (1.0 - lam_init))
    return jnp.concatenate(blocks, axis=1)


def _masked_q(q_ref, hm):
    kb, sub = divmod(hm, LANES // DA_QK_DIM)
    blk = q_ref[:, kb * LANES:(kb + 1) * LANES]
    lane = lax.broadcasted_iota(jnp.int32, blk.shape, 1)
    keep = (lane >= sub * DA_QK_DIM) & (lane < (sub + 1) * DA_QK_DIM)
    return jnp.where(keep, blk, jnp.zeros_like(blk))


def _dattn_kernel(lq_ref, g_ref, q_ref, k_ref, vt_ref, kx_ref, vxt_ref, o_ref, qm_scr, m_scr, l_scr, acc_scr,
                  *, lam_init, tk, ahead, unroll):
    n_hm = 2 * DA_HEADS
    per_blk = LANES // DA_QK_DIM
    for hm in range(n_hm):
        qm_scr[hm] = _masked_q(q_ref, hm)
    m_scr[...] = jnp.full(m_scr.shape, NEG_BIG, F32)
    l_scr[...] = jnp.zeros(l_scr.shape, F32)
    acc_scr[...] = jnp.zeros(acc_scr.shape, F32)

    def update(chunks):
        chains = [(c, hm) for c in range(len(chunks)) for hm in range(n_hm)]

        def scores(chain):
            c, hm = chain
            kb = hm // per_blk
            return lax.dot_general(chunks[c][0][:, kb * LANES:(kb + 1) * LANES], qm_scr[hm],
                                   (((1,), (1,)), ((), ())), preferred_element_type=F32)

        sts = [scores(ch) for ch in chains[:ahead]]
        for n, (c, hm) in enumerate(chains):
            h = hm // 2
            vtc = chunks[c][1]
            if n + ahead < len(chains):
                sts.append(scores(chains[n + ahead]))
            st = sts[n]
            sts[n] = None
            m_old = m_scr[hm]
            m_new = jnp.maximum(m_old, jnp.max(st, axis=0, keepdims=True))
            alpha = jnp.exp2(m_old - m_new)
            pt = jnp.exp2(st - m_new)
            l_scr[hm] = alpha * l_scr[hm] + jnp.sum(pt, axis=0, keepdims=True)
            acc_scr[hm] = alpha * acc_scr[hm] + jnp.dot(vtc[h * DA_V_DIM:(h + 1) * DA_V_DIM, :], pt.astype(BF16),
                                                        preferred_element_type=F32)
            m_scr[hm] = m_new

    def body(j, carry):
        chunks = []
        for c in range(unroll):
            start = pl.multiple_of((j * unroll + c) * tk, tk)
            chunks.append((k_ref[pl.ds(start, tk), :], vt_ref[:, pl.ds(start, tk)]))
        update(chunks)
        return carry

    lax.fori_loop(0, k_ref.shape[0] // (tk * unroll), body, 0)
    n_x = kx_ref.shape[0] // tk
    update([(kx_ref[c * tk:(c + 1) * tk, :], vxt_ref[:, c * tk:(c + 1) * tk]) for c in range(n_x)])
    lam = _diff_lambda(lq_ref, lam_init)
    heads = []
    for h in range(DA_HEADS):
        o = acc_scr[2 * h] / l_scr[2 * h] - lam * (acc_scr[2 * h + 1] / l_scr[2 * h + 1])
        ms = jnp.mean(o * o, axis=0, keepdims=True)
        heads.append(o * lax.rsqrt(ms + NORM_EPS) * g_ref[...] * (1.0 - lam_init))
    o_ref[...] = jnp.concatenate(heads, axis=0).T.astype(BF16)


def diff_attention(lq, g_rows, qd, kd, vdt, *, n_batch, seq, n_ctx, lam_init, tq, tk, ahead, unroll):
    nq = seq // tq
    ctx0 = (n_batch * seq) // n_ctx
    n_hm = 2 * DA_HEADS
    return pl.pallas_call(
        functools.partial(_dattn_kernel, lam_init=lam_init, tk=tk, ahead=ahead, unroll=unroll),
        grid=(n_batch, nq),
        in_specs=[
            pl.BlockSpec((4, DA_QK_DIM), lambda b, i: (0, 0)),
            pl.BlockSpec((DA_V_DIM, tq), lambda b, i: (0, 0)),
            pl.BlockSpec((tq, DA_QK_COLS), lambda b, i: (b * nq + i, 0)),
            pl.BlockSpec((seq, DA_QK_COLS), lambda b, i: (b, 0)),
            pl.BlockSpec((DA_V_COLS, seq), lambda b, i: (0, b)),
            pl.BlockSpec((n_ctx, DA_QK_COLS), lambda b, i: (ctx0 + b, 0)),
            pl.BlockSpec((DA_V_COLS, n_ctx), lambda b, i: (0, ctx0 + b)),
        ],
        out_specs=pl.BlockSpec((tq, DA_V_COLS), lambda b, i: (b * nq + i, 0)),
        out_shape=jax.ShapeDtypeStruct((n_batch * seq, DA_V_COLS), BF16),
        scratch_shapes=[
            pltpu.VMEM((n_hm, tq, LANES), BF16),
            pltpu.VMEM((n_hm, 1, tq), F32),
            pltpu.VMEM((n_hm, 1, tq), F32),
            pltpu.VMEM((n_hm, DA_V_DIM, tq), F32),
        ],
        compiler_params=_cparams("arbitrary", "arbitrary"),
        name="diff_attention",
    )(lq, g_rows, qd, kd, vdt, kd, vdt)


def _ctx_attn_kernel(sink_ref, lq_ref, g_ref, qw_ref, kw_ref, vw_ref, qd_ref, kd_ref, vd_ref, ow_ref, od_ref,
                     *, lam_init):
    outs = []
    for hk in range(WA_KV_HEADS):
        sl = slice(hk * HEAD_DIM, (hk + 1) * HEAD_DIM)
        k, v = kw_ref[:, sl], vw_ref[:, sl]
        for g in range(WA_GROUP):
            h = hk * WA_GROUP + g
            outs.append(_softmax_sink_attend(qw_ref[:, h * HEAD_DIM:(h + 1) * HEAD_DIM], k, v, sink_ref[h], None))
    ow_ref[...] = jnp.concatenate(outs, axis=1).astype(BF16)
    per_blk = LANES // DA_QK_DIM
    acc, den = [], []
    for hm in range(2 * DA_HEADS):
        kb, vb = hm // per_blk, hm // 4
        s = lax.dot_general(_masked_q(qd_ref, hm), kd_ref[:, kb * LANES:(kb + 1) * LANES],
                            (((1,), (1,)), ((), ())), preferred_element_type=F32)
        p = jnp.exp2(s - jnp.max(s, axis=1, keepdims=True))
        den.append(jnp.sum(p, axis=1, keepdims=True))
        acc.append(jnp.dot(p.astype(BF16), vd_ref[:, vb * LANES:(vb + 1) * LANES], preferred_element_type=F32))
    lam = _diff_lambda(lq_ref, lam_init)
    od_ref[...] = _diff_finish(acc[0::2], den[0::2], acc[1::2], den[1::2], lam, g_ref[...], lam_init).astype(BF16)


def ctx_attention(sink, lq, g128, qw, kw, vw, qd, kd, vd, *, n_batch, seq, n_ctx, lam_init):
    ctx0 = (n_batch * seq) // n_ctx
    ctx = lambda b: (ctx0 + b, 0)
    out = lambda b: (b, 0)
    spec = lambda n, f: pl.BlockSpec((n_ctx, n), f)
    return pl.pallas_call(
        functools.partial(_ctx_attn_kernel, lam_init=lam_init),
        grid=(n_batch,),
        in_specs=[
            pl.BlockSpec(memory_space=pltpu.SMEM),
            pl.BlockSpec((4, DA_QK_DIM), lambda b: (0, 0)),
            pl.BlockSpec((1, LANES), lambda b: (0, 0)),
            spec(WA_Q_COLS, ctx), spec(WA_KV_COLS, ctx), spec(WA_KV_COLS, ctx),
            spec(DA_QK_COLS, ctx), spec(DA_QK_COLS, ctx), spec(DA_V_COLS, ctx),
        ],
        out_specs=[spec(WA_Q_COLS, out), spec(DA_V_COLS, out)],
        out_shape=[jax.ShapeDtypeStruct((n_batch * n_ctx, WA_Q_COLS), BF16),
                   jax.ShapeDtypeStruct((n_batch * n_ctx, DA_V_COLS), BF16)],
        compiler_params=_cparams("arbitrary"),
        name="ctx_attention",
    )(sink, lq, g128, qw, kw, vw, qd, kd, vd)


def _filter_kernel(feat_ref, w1_ref, b1_ref, fr_ref, w2_ref, b2_ref, w3_ref, dec_ref, o_ref, *, transposed):
    feat = feat_ref[...]
    fr = fr_ref[...]
    z = jnp.sin(fr * (jnp.dot(feat, w1_ref[...], precision=HI, preferred_element_type=F32) + b1_ref[...]))
    z = jnp.sin(fr * (jnp.dot(z, w2_ref[...], precision=HI, preferred_element_type=F32) + b2_ref[...]))
    h = jnp.dot(z, w3_ref[...], precision=HI, preferred_element_type=F32)
    h = h * jnp.exp(-feat[:, 0:1] * jnp.abs(dec_ref[...]))
    rows = lax.broadcasted_iota(jnp.int32, h.shape, 0) + pl.program_id(0) * h.shape[0]
    cols = lax.broadcasted_iota(jnp.int32, h.shape, 1)
    h = jnp.where((rows == 0) & (cols % (2 * HY_WIDTH) >= HY_WIDTH), 0.0, h)
    if transposed:
        for r in range(h.shape[0] // LANES):
            o_ref[r] = h[r * LANES:(r + 1) * LANES, :].T
    else:
        o_ref[...] = h


def hyena_filter_features(n):
    f32 = np.float32
    t = np.linspace(0.0, 1.0, n, dtype=f32)[:, None]
    bands = np.linspace(1e-4, HY_BANDS - 1, HY_BANDS, dtype=f32)
    phase = (f32(2.0 * math.pi / n) * np.arange(n, dtype=f32)[:, None]) * bands
    feat = np.concatenate([t, np.cos(phase), -np.sin(phase)], axis=-1).astype(f32)
    return jnp.asarray(np.pad(feat, ((0, 0), (0, LANES - HY_EMB_DIM))), F32)


def hyena_filters(feat, w1p, b1, freq, w2, b2, w3, decay, *, transposed):
    n = feat.shape[0]
    tr = min(n, 512)
    ncol = w3.shape[1]
    fw = w2.shape[0]
    const = lambda i: (0, 0)
    if transposed:
        out_spec = pl.BlockSpec((tr // LANES, ncol, LANES), lambda i: (i, 0, 0))
        out_shape = jax.ShapeDtypeStruct((n // LANES, ncol, LANES), F32)
    else:
        out_spec = pl.BlockSpec((tr, ncol), lambda i: (i, 0))
        out_shape = jax.ShapeDtypeStruct((n, ncol), F32)
    return pl.pallas_call(
        functools.partial(_filter_kernel, transposed=transposed),
        grid=(n // tr,),
        in_specs=[
            pl.BlockSpec((tr, LANES), lambda i: (i, 0)),
            pl.BlockSpec((LANES, fw), const), pl.BlockSpec((1, fw), const), pl.BlockSpec((1, fw), const),
            pl.BlockSpec((fw, fw), const), pl.BlockSpec((1, fw), const),
            pl.BlockSpec((fw, ncol), const), pl.BlockSpec((1, ncol), const),
        ],
        out_specs=out_spec,
        out_shape=out_shape,
        compiler_params=_cparams("arbitrary"),
        name="hyena_filters",
    )(feat, w1p, b1, freq, w2, b2, w3, decay)


def dft_constants(seq):
    n = 2 * seq
    n1 = n // LANES
    a_rows = n1 // 2
    k1 = np.arange(n1, dtype=np.float64)[:, None]
    a = np.arange(a_rows, dtype=np.float64)[None, :]
    th = 2.0 * np.pi * k1 * a / n1
    c1, s1 = np.cos(th), np.sin(th)
    m1_data = np.block([[c1, s1], [-s1, c1]])
    half_rows = -(-(n1 // 2 + 1) // SUBLANES) * SUBLANES
    pad = np.zeros((half_rows - (n1 // 2 + 1), a_rows))
    m1_real = np.concatenate([c1[:n1 // 2 + 1], pad, -s1[:n1 // 2 + 1], pad], axis=0)
    m3 =np.block([[c1.T, -s1.T], [s1.T, c1.T]]) / n
    b = np.arange(LANES, dtype=np.float64)[None, :]
    psi = 2.0 * np.pi * k1 * b / n
    twc, tws = np.cos(psi), np.sin(psi)
    bb = np.arange(LANES, dtype=np.float64)
    phi = 2.0 * np.pi * np.outer(bb, bb) / LANES
    c2, s2 = np.cos(phi), np.sin(phi)
    w2f = np.block([[c2, -s2], [s2, c2]])
    w2i = np.block([[c2, s2], [-s2, c2]])
    f = lambda x: jnp.asarray(x, dtype=F32)
    return dict(m1_data=f(m1_data), m1_real=f(m1_real), m3=f(m3), twc=f(twc[:, None, :]), tws=f(tws[:, None, :]),
                w2f=f(w2f), w2i=f(w2i), n1=n1, a_rows=a_rows, half_rows=half_rows)


def _split_bf16(x):
    hi = x.astype(BF16)
    return hi, (x - hi.astype(F32)).astype(BF16)


def _mm3(a, b, nt=False):
    ah, al = _split_bf16(a)
    bh, bl = _split_bf16(b)
    dims = (((1,), (1 if nt else 0,)), ((), ()))
    dot = lambda x, y: lax.dot_general(x, y, dims, preferred_element_type=F32)
    return dot(ah, bh) + (dot(ah, bl) + dot(al, bh))


def _leftmm_kernel(m_ref, x_ref, o_ref):
    o_ref[...] = _mm3(m_ref[...], x_ref[...])


def left_matmul(m, x, ct):
    r, k = m.shape
    cols = x.shape[1]
    return pl.pallas_call(
        _leftmm_kernel,
        grid=(cols // ct,),
        in_specs=[pl.BlockSpec((r, k), lambda j: (0, 0)), pl.BlockSpec((k, ct), lambda j: (0, j))],
        out_specs=pl.BlockSpec((r, ct), lambda j: (0, j)),
        out_shape=jax.ShapeDtypeStruct((r, cols), F32),
        compiler_params=_cparams("arbitrary"),
        name="dft_rows",
    )(m, x)


def _leftmm_gate_kernel(m_ref, zr_ref, zi_ref, x_ref, zold_ref, bias_ref, o_ref):
    n1 = zr_ref.shape[0]
    y = _mm3(m_ref[:, :n1], zr_ref[...]) + _mm3(m_ref[:, n1:], zi_ref[...])
    zold = zold_ref[...]
    o_ref[...] = x_ref[...] * (y + bias_ref[...] * zold)


def left_matmul_gate(m, zr, zi, xg, zold, bias_cols, ct):
    r, k = m.shape
    n1, cols = zr.shape
    col = lambda j: (0, j)
    return pl.pallas_call(
        _leftmm_gate_kernel,
        grid=(cols // ct,),
        in_specs=[pl.BlockSpec((r, k), lambda j: (0, 0)), pl.BlockSpec((n1, ct), col), pl.BlockSpec((n1, ct), col),
                  pl.BlockSpec((r, ct), col), pl.BlockSpec((r, ct), col), pl.BlockSpec((1, ct), col)],
        out_specs=pl.BlockSpec((r, ct), col),
        out_shape=jax.ShapeDtypeStruct((r, cols), F32),
        compiler_params=_cparams("arbitrary"),
        name="idft_rows_gate",
    )(m, zr, zi, xg, zold, bias_cols)


def _twiddle_fwd(yr, yi, c, s):
    return yr * c + yi * s, yi * c - yr * s


def _spectrum_kernel(yr_ref, yi_ref, twc_ref, tws_ref, w2f_ref, hr_ref, hi_ref, *, kb, n1):
    for t in range(kb):
        mirrored = pl.program_id(0) * kb + t > n1 // 2
        yi = yi_ref[t]
        ypr, ypi = _twiddle_fwd(yr_ref[t], jnp.where(mirrored, -yi, yi), twc_ref[t], tws_ref[t])
        x = _mm3(jnp.concatenate([ypr, ypi], axis=1), w2f_ref[...])
        xr, xi = x[:, :LANES], x[:, LANES:]
        for o in range(HY_ORDER):
            f0 = slice((2 * o) * HY_WIDTH, (2 * o + 1) * HY_WIDTH)
            f1 = slice((2 * o + 1) * HY_WIDTH, (2 * o + 2) * HY_WIDTH)
            hr_ref[o, t] = xr[f0] + xr[f1]
            hi_ref[o, t] = xi[f0] - xi[f1]


def filter_spectrum(y, consts, kb):
    assert kb == 1
    n1, half_rows = consts["n1"], consts["half_rows"]
    nc = 2 * HY_ORDER * HY_WIDTH
    y4 = y.reshape(2, half_rows, nc, LANES)
    tw = pl.BlockSpec((kb, 1, LANES), lambda i: (i, 0, 0))
    out = pl.BlockSpec((HY_ORDER, kb, HY_WIDTH, LANES), lambda i: (0, i, 0, 0))
    shp = jax.ShapeDtypeStruct((HY_ORDER, n1, HY_WIDTH, LANES), F32)
    src = lambda i: jnp.where(i > n1 // 2, n1 - i, i)
    return pl.pallas_call(
        functools.partial(_spectrum_kernel, kb=kb, n1=n1),
        grid=(n1 // kb,),
        in_specs=[
            pl.BlockSpec((None, kb, nc, LANES), lambda i: (0, src(i), 0, 0)),
            pl.BlockSpec((None, kb, nc, LANES), lambda i: (1, src(i), 0, 0)),
            tw, tw,
            pl.BlockSpec((2 * LANES, 2 * LANES), lambda i: (0, 0)),
        ],
        out_specs=[out, out],
        out_shape=[shp, shp],
        compiler_params=_cparams("arbitrary"),
        name="filter_spectrum",
    )(y4, y4, consts["twc"], consts["tws"], consts["w2f"])


def _freq_kernel(yr_ref, yi_ref, twc_ref, tws_ref, w2f_ref, w2i_ref, hr_ref, hi_ref, zr_ref, zi_ref, *, kb):
    for t in range(kb):
        c, s = twc_ref[t], tws_ref[t]
        ypr, ypi = _twiddle_fwd(yr_ref[t], yi_ref[t], c, s)
        x = _mm3(jnp.concatenate([ypr, ypi], axis=1), w2f_ref[...])
        xr, xi = x[:, :LANES], x[:, LANES:]
        hr, hi = hr_ref[t], hi_ref[t]
        gr = xr * hr - xi * hi
        gi = xr * hi + xi * hr
        z = _mm3(jnp.concatenate([gr, gi], axis=1), w2i_ref[...])
        zr, zi = z[:, :LANES], z[:, LANES:]
        zr_ref[t] = zr * c - zi * s
        zi_ref[t] = zi * c + zr * s


def freq_multiply(y, hr, hi, order, consts, kb):
    n1 = consts["n1"]
    y4 = y.reshape(2, n1, HY_WIDTH, LANES)
    tw = pl.BlockSpec((kb, 1, LANES), lambda i: (i, 0, 0))
    blk = lambda p: pl.BlockSpec((None, kb, HY_WIDTH, LANES), lambda i: (p, i, 0, 0))
    wspec = pl.BlockSpec((2 * LANES, 2 * LANES), lambda i: (0, 0))
    out = pl.BlockSpec((kb, HY_WIDTH, LANES), lambda i: (i, 0, 0))
    shp = jax.ShapeDtypeStruct((n1, HY_WIDTH, LANES), F32)
    return pl.pallas_call(
        functools.partial(_freq_kernel, kb=kb),
        grid=(n1 // kb,),
        in_specs=[blk(0), blk(1), tw, tw, wspec, wspec, blk(order), blk(order)],
        out_specs=[out, out],
        out_shape=[shp, shp],
        compiler_params=_cparams("arbitrary"),
        name="freq_multiply",
    )(y4, y4, consts["twc"], consts["tws"], consts["w2f"], consts["w2i"], hr, hi)


def _short_conv_rows(u, prev_row, next_row, w_ref, b_ref):
    rows = lax.broadcasted_iota(jnp.int32, u.shape, 0)
    um1 = jnp.where(rows == 0, prev_row, pltpu.roll(u, 1, 0))
    up1 = jnp.where(rows == u.shape[0] - 1, next_row, pltpu.roll(u, u.shape[0] - 1, 0))
    return b_ref[...] + um1 * w_ref[0:1, :] + u * w_ref[1:2, :] + up1 * w_ref[2:3, :]


def _shortconv_t_kernel(u_ref, up_ref, un_ref, w_ref, b_ref, o_ref, *, n_tiles):
    i = pl.program_id(1)
    prev_row = jnp.where(i > 0, up_ref[SUBLANES - 1:SUBLANES, :], 0.0)
    next_row = jnp.where(i < n_tiles - 1, un_ref[0:1, :], 0.0)
    v = _short_conv_rows(u_ref[...], prev_row, next_row, w_ref, b_ref)
    for part in range(HY_ORDER + 1):
        for r in range(v.shape[0] // LANES):
            o_ref[part, r] = v[r * LANES:(r + 1) * LANES, part * HY_WIDTH:(part + 1) * HY_WIDTH].T


def short_conv_transposed(hy, conv_w, conv_b, *, n_batch, seq, tm):
    nt = seq // tm
    per8 = tm // SUBLANES
    a_rows = seq // LANES
    prev = lambda b, i: (jnp.maximum((b * nt + i) * per8 - 1, 0), 0)
    nxt = lambda b, i: (jnp.minimum((b * nt + i + 1) * per8, n_batch * nt * per8 - 1), 0)
    return pl.pallas_call(
        functools.partial(_shortconv_t_kernel, n_tiles=nt),
        grid=(n_batch, nt),
        in_specs=[
            pl.BlockSpec((tm, HY_COLS), lambda b, i: (b * nt + i, 0)),
            pl.BlockSpec((SUBLANES, HY_COLS), prev),
            pl.BlockSpec((SUBLANES, HY_COLS), nxt),
            pl.BlockSpec((3, HY_COLS), lambda b, i: (0, 0)),
            pl.BlockSpec((1, HY_COLS), lambda b, i: (0, 0)),
        ],
        out_specs=pl.BlockSpec((HY_ORDER + 1, None, tm // LANES, HY_WIDTH, LANES), lambda b, i: (0, b, i, 0, 0)),
        out_shape=jax.ShapeDtypeStruct((HY_ORDER + 1, n_batch, a_rows, HY_WIDTH, LANES), F32),
        compiler_params=_cparams("arbitrary", "arbitrary"),
        name="short_conv_transposed",
    )(hy, hy, hy, conv_w, conv_b)


def _untranspose_kernel(z_ref, o_ref):
    for r in range(z_ref.shape[0]):
        o_ref[r * LANES:(r + 1) * LANES, :] = z_ref[r].T.astype(BF16)


def untranspose(zt, *, n_batch, seq):
    a_rows = seq // LANES
    ta = min(SUBLANES, a_rows)
    nt = a_rows // ta
    return pl.pallas_call(
        _untranspose_kernel,
        grid=(n_batch, nt),
        in_specs=[pl.BlockSpec((None, ta, HY_WIDTH, LANES), lambda b, i: (b, i, 0, 0))],
        out_specs=pl.BlockSpec((ta * LANES, HY_WIDTH), lambda b, i: (b * nt + i, 0)),
        out_shape=jax.ShapeDtypeStruct((n_batch * seq, HY_WIDTH), BF16),
        compiler_params=_cparams("arbitrary", "arbitrary"),
        name="untranspose",
    )(zt)


def hyena_latent(hy, conv_w, conv_b, hy_bias, filt_t, consts, *, n_batch, seq):
    assert n_batch == 2, "the two batches are packed as real / imaginary parts of one DFT"
    n1, a_rows = consts["n1"], consts["a_rows"]
    cols = HY_WIDTH * LANES
    ct = min(cols, 2048)
    kb = min(n1, 4)
    parts = short_conv_transposed(hy, conv_w, conv_b, n_batch=n_batch, seq=seq, tm=min(seq, 1024))
    parts = parts.reshape(HY_ORDER + 1, n_batch * a_rows, cols)
    yf = left_matmul(consts["m1_real"], filt_t.reshape(a_rows, 2 * HY_ORDER * cols), ct)
    hr, hi = filter_spectrum(yf, consts, 1)
    z = parts[HY_ORDER]
    for o in range(HY_ORDER):
        y = left_matmul(consts["m1_data"], z, ct)
        zr, zi = freq_multiply(y, hr, hi, o, consts, kb)
        bias_cols = jnp.repeat(hy_bias[o], LANES)[None, :]
        z = left_matmul_gate(consts["m3"], zr.reshape(n1, cols), zi.reshape(n1, cols), parts[o], z, bias_cols, ct)
    return untranspose(z.reshape(n_batch, a_rows, HY_WIDTH, LANES), n_batch=n_batch, seq=seq)


def _ctx_hyena_kernel(u_ref, w_ref, b_ref, h_ref, bias_ref, fc_ref, fs_ref, fct_ref, fst_ref, o_ref):
    u = u_ref[...]
    zero = jnp.zeros((1, u.shape[1]), F32)
    v = _short_conv_rows(u, zero, zero, w_ref, b_ref)
    z = v[:, HY_ORDER * HY_WIDTH:]
    mm = lambda a, b: jnp.dot(a, b, precision=HI, preferred_element_type=F32)
    fc, fs = fc_ref[...], fs_ref[...]
    inv_n = 1.0 / fc.shape[0]
    for o in range(HY_ORDER):
        h0 = h_ref[:, (2 * o) * HY_WIDTH:(2 * o + 1) * HY_WIDTH]
        h1 = h_ref[:, (2 * o + 1) * HY_WIDTH:(2 * o + 2) * HY_WIDTH]
        sr, si = mm(fc, h0 + h1), mm(fs, h1 - h0)
        zr, zi = mm(fc, z), -mm(fs, z)
        gr = zr * sr - zi * si
        gi = zr * si + zi * sr
        y = (mm(fct_ref[...], gr) - mm(fst_ref[...], gi)) * inv_n
        z = v[:, o * HY_WIDTH:(o + 1) * HY_WIDTH] * (y + bias_ref[o:o + 1, :] * z)
    o_ref[...] = z.astype(BF16)


def ctx_hyena(hy, conv_w, conv_b, filt, hy_bias, *, n_batch, seq, n_ctx):
    n = 2 * n_ctx
    k = np.arange(n, dtype=np.float64)[:, None]
    m = np.arange(n_ctx, dtype=np.float64)[None, :]
    ang = 2.0 * np.pi * k * m / n
    fc, fs = jnp.asarray(np.cos(ang), F32), jnp.asarray(np.sin(ang), F32)
    ctx0 = (n_batch * seq) // n_ctx
    const = lambda b: (0, 0)
    return pl.pallas_call(
        _ctx_hyena_kernel,
        grid=(n_batch,),
        in_specs=[
            pl.BlockSpec((n_ctx, HY_COLS), lambda b: (ctx0 + b, 0)),
            pl.BlockSpec((3, HY_COLS), const), pl.BlockSpec((1, HY_COLS), const),
            pl.BlockSpec((n_ctx, 2 * HY_ORDER * HY_WIDTH), const),
            pl.BlockSpec((HY_ORDER, HY_WIDTH), const),
            pl.BlockSpec((n, n_ctx), const), pl.BlockSpec((n, n_ctx), const),
            pl.BlockSpec((n_ctx, n), const), pl.BlockSpec((n_ctx, n), const),
        ],
        out_specs=pl.BlockSpec((n_ctx, HY_WIDTH), lambda b: (b, 0)),
        out_shape=jax.ShapeDtypeStruct((n_batch * n_ctx, HY_WIDTH), BF16),
        compiler_params=_cparams("arbitrary"),
        name="ctx_hyena",
    )(hy, conv_w, conv_b, filt, hy_bias, fc, fs, fc.T, fs.T)


def _outproj_kernel(x_ref, yh_ref, yw_ref, yd_ref, mod_ref, wo_ref, g_ref, wq_ref, xo_ref, h2_ref, qp_ref):
    y = jnp.dot(yh_ref[...], wo_ref[0:HY_WIDTH, :], preferred_element_type=F32)
    y = y + jnp.dot(yw_ref[...], wo_ref[HY_WIDTH:HY_WIDTH + WA_Q_COLS, :], preferred_element_type=F32)
    y = y + jnp.dot(yd_ref[...], wo_ref[HY_WIDTH + WA_Q_COLS:, :], preferred_element_type=F32)
    xn = x_ref[...] + mod_ref[2:3, :] * y
    xo_ref[...] = xn
    h2 = _rmsnorm_mod(xn, g_ref[...], mod_ref[3:4, :], mod_ref[4:5, :]).astype(BF16)
    h2_ref[...] = h2
    qp_ref[...] = jnp.dot(h2, wq_ref[...], preferred_element_type=F32)


def out_proj(x_all, y_hy, y_wa, y_da, mod, wo_bf, g2, wq_bf, *, n_rows, n_batch, seq, tm):
    d = x_all.shape[1]
    tpb = seq // tm
    grp = functools.partial(_group_of_tile, tiles_per_batch=tpb, n_batch=n_batch)
    row = lambda i: (i, 0)
    const = lambda i: (0, 0)
    mix = HY_WIDTH + WA_Q_COLS + DA_V_COLS
    return pl.pallas_call(
        _outproj_kernel,
        grid=(n_rows // tm,),
        in_specs=[
            pl.BlockSpec((tm, d), row), pl.BlockSpec((tm, HY_WIDTH), row), pl.BlockSpec((tm, WA_Q_COLS), row),
            pl.BlockSpec((tm, DA_V_COLS), row),
            pl.BlockSpec((None, N_MOD, d), lambda i: (grp(i), 0, 0)),
            pl.BlockSpec((mix, d), const), pl.BlockSpec((1, d), const), pl.BlockSpec((d, PEER_QCOLS), const),
        ],
        out_specs=[pl.BlockSpec((tm, d), row), pl.BlockSpec((tm, d), row), pl.BlockSpec((tm, PEER_QCOLS), row)],
        out_shape=[jax.ShapeDtypeStruct((n_rows, d), F32), jax.ShapeDtypeStruct((n_rows, d), BF16),
                   jax.ShapeDtypeStruct((n_rows, PEER_QCOLS), F32)],
        compiler_params=_cparams("arbitrary"),
        name="out_proj",
    )(x_all, y_hy, y_wa, y_da, mod, wo_bf, g2, wq_bf)


def _sort_pairs(n):
    pairs = []

    def merge(lo, cnt, r):
        step = r * 2
        if step < cnt:
            merge(lo, cnt, step)
            merge(lo + r, cnt, step)
            for i in range(lo + r, lo + cnt - r, step):
                pairs.append((i, i + r))
        else:
            pairs.append((lo, lo + r))

    def sort(lo, cnt):
        if cnt > 1:
            half = cnt // 2
            sort(lo, half)
            sort(lo + half, half)
            merge(lo, cnt, 1)

    sort(0, n)
    return pairs


def _sort_desc(vals):
    n = 1
    while n < len(vals):
        n *= 2
    v = list(vals) + [None] * (n - len(vals))
    for i, j in _sort_pairs(n):
        a, b = v[i], v[j]
        if b is None:
            continue
        if a is None:
            v[i], v[j] = b, None
        else:
            v[i], v[j] = jnp.maximum(a, b), jnp.minimum(a, b)
    return v[:len(vals)]


def _top16_rows(s):
    k = PEER_TOPK
    v = _sort_desc([s[SUBLANES * r:SUBLANES * (r + 1), :] for r in range(s.shape[0] // SUBLANES)])
    for shift in (4, 2, 1):
        other = [pltpu.roll(x, shift, 0) for x in v]
        v = [jnp.maximum(v[i], other[k - 1 - i]) for i in range(k)]
        d = k // 2
        while d >= 1:
            for i in range(k):
                if (i & d) == 0:
                    a, b = v[i], v[i + d]
                    v[i], v[i + d] = jnp.maximum(a, b), jnp.minimum(a, b)
            d //= 2
    return v


def _peer_topk_kernel(q_ref, keys_ref, cnt_ref, e1_ref, rank_ref, e2_ref):
    k = PEER_TOPK
    tmk = q_ref.shape[0]
    sub = lax.broadcasted_iota(jnp.int32, (SUBLANES, tmk), 0)
    scores, tops = [], []
    for hp in range(2 * PEER_HEADS):
        s = _mm3(keys_ref[hp], q_ref[:, hp * N_KEYS:(hp + 1) * N_KEYS], nt=True)
        scores.append(s)
        top = _top16_rows(s)
        top.append(jnp.max(jnp.where(s < top[k - 1][0:1, :], s, NEG_BIG), axis=0, keepdims=True))
        tops.append(top)
    packed = []
    for p in range(2):
        lst = []
        for r in range(k + 1):
            acc = jnp.broadcast_to(tops[p][r], (SUBLANES, tmk))
            for h in range(1, PEER_HEADS):
                acc = jnp.where(sub == h, tops[2 * h + p][r], acc)
            lst.append(acc)
        packed.append(lst)
    cand = [packed[0][i] + packed[1][j] for i in range(k) for j in range(k) if (i + 1) * (j + 1) <= k]
    cs = _sort_desc(cand)
    c17 = jnp.maximum(cs[k], jnp.maximum(packed[0][k] + packed[1][0], packed[0][0] + packed[1][k]))
    tau = 0.5 * (cs[k - 1] + c17)
    zsum = jnp.zeros_like(tau)
    for r in range(k):
        zsum = zsum + jnp.exp(cs[r] - cs[0])
    inv_z = 1.0 / zsum
    for h in range(PEER_HEADS):
        s1, s2 = scores[2 * h], scores[2 * h + 1]
        row = lambda x: x[h:h + 1, :]
        top2 = [t2[0:1, :] for t2 in tops[2 * h + 1][:k]]
        cnt_ref[h] = _count_sorted(top2, row(tau) - s1, strict=False)
        tiled = lambda x: x.astype(BF16).reshape(N_KEYS // BF16_ROWS, BF16_ROWS, tmk)
        rank_ref[h] = tiled(_count_sorted(top2, s2, strict=True))
        e1_ref[h] = 0.5 * jnp.exp(s1 - row(packed[0][0]))
        e2_ref[h] = tiled(jnp.exp(s2 - row(packed[1][0])) * row(inv_z))


def _count_sorted(tops, x, *, strict):
    ge = (lambda a: a > x) if strict else (lambda a: a >= x)
    b3 = ge(tops[7])
    b2 = ge(jnp.where(b3, tops[11], tops[3]))
    b1 = ge(jnp.where(b3, jnp.where(b2, tops[13], tops[9]), jnp.where(b2, tops[5], tops[1])))
    hi = jnp.where(b2, jnp.where(b1, tops[14], tops[12]), jnp.where(b1, tops[10], tops[8]))
    lo = jnp.where(b2, jnp.where(b1, tops[6], tops[4]), jnp.where(b1, tops[2], tops[0]))
    b0 = ge(jnp.where(b3, hi, lo))
    cnt = (jnp.where(b3, 8.0, 0.0) + jnp.where(b2, 4.0, 0.0)) + (jnp.where(b1, 2.0, 0.0) + jnp.where(b0, 1.0, 0.0))
    return jnp.where(ge(tops[15]), 16.0, cnt)


def peer_topk(qp, keys, *, tmk):
    t = qp.shape[0]
    tiles = N_KEYS // BF16_ROWS
    out = pl.BlockSpec((PEER_HEADS, N_KEYS, tmk), lambda i: (0, 0, i))
    out16 = pl.BlockSpec((PEER_HEADS, tiles, BF16_ROWS, tmk), lambda i: (0, 0, 0, i))
    f32 = jax.ShapeDtypeStruct((PEER_HEADS, N_KEYS, t), F32)
    b16 = jax.ShapeDtypeStruct((PEER_HEADS, tiles, BF16_ROWS, t), BF16)
    return pl.pallas_call(
        _peer_topk_kernel,
        grid=(t // tmk,),
        in_specs=[pl.BlockSpec((tmk, PEER_QCOLS), lambda i: (i, 0)),
                  pl.BlockSpec((2 * PEER_HEADS, N_KEYS, PEER_QDIM // 2), lambda i: (0, 0, 0))],
        out_specs=[out, out, out16, out16],
        out_shape=[f32, f32, b16, b16],
        compiler_params=_cparams("arbitrary"),
        name="peer_topk",
    )(qp, keys)


def _peer_dense_kernel(x_ref, h_ref, mod_ref, u_ref, vt_ref, cnt_ref, e1_ref, rank_ref, e2_ref, g_ref, o_ref,
                       a_scr, w_scr, acc_scr, *, final_norm, chunk, col_block):
    e = pl.program_id(1)

    @pl.when(e == 0)
    def _():
        acc_scr[...] = jnp.zeros(acc_scr.shape, F32)

    n_chunks = u_ref.shape[0] // chunk
    per_chunk = chunk // N_KEYS
    sqrt_half = math.sqrt(0.5)
    tm = h_ref.shape[0]
    tiles = N_KEYS // BF16_ROWS
    zero = jnp.zeros((), BF16)

    def pre_activations(p):
        a_scr[p % 2] = lax.dot_general(u_ref[p * chunk:(p + 1) * chunk, :], h_ref[...], (((1,), (1,)), ((), ())),
                                       preferred_element_type=F32)

    pre_activations(0)
    for p in range(n_chunks):
        if p + 1 < n_chunks:
            pre_activations(p + 1)
        for ii, cb in [(ii, cb) for ii in range(p * per_chunk, (p + 1) * per_chunk) for cb in range(tm // col_block)]:
            rows = slice((ii - p * per_chunk) * N_KEYS, (ii - p * per_chunk + 1) * N_KEYS)
            cols = slice(cb * col_block, (cb + 1) * col_block)
            a = a_scr[p % 2, rows, cols]
            act = a * (1.0 + lax.erf(a * sqrt_half))
            act = act.astype(BF16).reshape(tiles, BF16_ROWS, col_block)
            gate = None
            for h in range(PEER_HEADS):
                cnt = jnp.broadcast_to(cnt_ref[h, ii:ii + 1, cols], (BF16_ROWS, col_block)).astype(BF16)
                e1 = jnp.broadcast_to(e1_ref[h, ii:ii + 1, cols], (BF16_ROWS, col_block)).astype(BF16)
                term = jnp.where(rank_ref[h, :, :, cols] < cnt[None], e2_ref[h, :, :, cols] * e1[None], zero)
                gate = term if gate is None else gate + term
            w_scr[p % 2, rows, cols] = (gate * act).reshape(N_KEYS, col_block)
        acc_scr[...] += jnp.dot(vt_ref[:, p * chunk:(p + 1) * chunk], w_scr[p % 2], preferred_element_type=F32)

    @pl.when(e == pl.num_programs(1) - 1)
    def _():
        xn = x_ref[...] + mod_ref[5:6, :] * acc_scr[...].T
        if final_norm:
            ms = jnp.mean(xn * xn, axis=-1, keepdims=True)
            xn = xn * lax.rsqrt(ms + NORM_EPS) * g_ref[...]
        o_ref[...] = xn


def peer_dense(x, h2, mod, u_bf, vt_bf, cnt, e1, rank, e2, g_final, *, n_batch, seq, tm, te, chunk, col_block,
               final_norm):
    n_rows, d = x.shape
    tpb = seq // tm
    grp = functools.partial(_group_of_tile, tiles_per_batch=tpb, n_batch=n_batch)
    n_i = te // N_KEYS
    tok = lambda i, e: (i, 0)
    rows_i = pl.BlockSpec((PEER_HEADS, n_i, tm), lambda i, e: (0, e, i))
    tiles = N_KEYS // BF16_ROWS
    full = pl.BlockSpec((PEER_HEADS, tiles, BF16_ROWS, tm), lambda i, e: (0, 0, 0, i))
    return pl.pallas_call(
        functools.partial(_peer_dense_kernel, final_norm=final_norm, chunk=chunk, col_block=col_block),
        grid=(n_rows // tm, N_EXPERTS // te),
        in_specs=[
            pl.BlockSpec((tm, d), tok), pl.BlockSpec((tm, d), tok),
            pl.BlockSpec((None, N_MOD, d), lambda i, e: (grp(i), 0, 0)),
            pl.BlockSpec((te, d), lambda i, e: (e, 0)),
            pl.BlockSpec((d, te), lambda i, e: (0, e)),
            rows_i, rows_i, full, full,
            pl.BlockSpec((1, d), lambda i, e: (0, 0)),
        ],
        out_specs=pl.BlockSpec((tm, d), tok),
        out_shape=jax.ShapeDtypeStruct((n_rows, d), F32),
        scratch_shapes=[pltpu.VMEM((2, chunk, tm), F32), pltpu.VMEM((2, chunk, tm), BF16), pltpu.VMEM((d, tm), F32)],
        compiler_params=_cparams("arbitrary", "arbitrary"),
        name="peer_dense",
    )(x, h2, mod, u_bf, vt_bf, cnt, e1, rank, e2, g_final)


def kernel(x, c, ctx, c_ctx, w_mod, b_mod, norm1_g, w_in, hy_conv_w, hy_conv_b, hy_w1, hy_b1, hy_freq, hy_w2,
           hy_b2, hy_w3, hy_decay, hy_bias, wa_sink, da_lq1, da_lk1, da_lq2, da_lk2, da_norm_g, w_out, norm2_g,
           peer_wq, peer_keys, peer_u, peer_v, final_g):
    n_batch, seq, d = x.shape
    n_ctx = ctx.shape[1]
    depth = w_in.shape[0]
    n_lat = n_batch * seq
    tm = 512 if seq % 512 == 0 else 256
    assert seq % tm == 0 and (n_batch * n_ctx) % tm == 0 and n_lat % n_ctx == 0 and n_batch + 1 <= SUBLANES

    cpad = jnp.zeros((SUBLANES, d), F32).at[:n_batch].set(c).at[n_batch].set(c_ctx)
    mod_all = mod_vectors(cpad, w_mod, b_mod).reshape(depth, SUBLANES, N_MOD, d)
    rope_wa = rope_tables(seq, HEAD_DIM, tm)
    rope_da = rope_tables(seq, DA_QK_DIM, tm)
    consts = dft_constants(seq)
    feat_lat = hyena_filter_features(seq)
    feat_ctx = hyena_filter_features(n_ctx)
    x_all = jnp.concatenate([x.reshape(n_lat, d), ctx.reshape(n_batch * n_ctx, d)], axis=0)

    out = None
    for li in range(depth):
        last = li == depth - 1
        lam_init = DA_LAMBDA_BASE - DA_LAMBDA_AMP * math.exp(-DA_LAMBDA_RATE * li)
        mod = mod_all[li]
        hy, qw, kw, vw, qd, kd, vd, vdt, vwt = proj_in(x_all, mod, norm1_g[li][None, :], w_in[li].astype(BF16), rope_wa,
                                             rope_da, n_batch=n_batch, seq=seq, tm=tm)
        w1p = jnp.pad(hy_w1[li], ((0, LANES - HY_EMB_DIM), (0, 0)))
        fargs = (w1p, hy_b1[li][None, :], hy_freq[li][None, :], hy_w2[li], hy_b2[li][None, :], hy_w3[li],
                 hy_decay[li].reshape(1, -1))
        conv_b = hy_conv_b[li][None, :]
        filt_t = hyena_filters(feat_lat, *fargs, transposed=True)
        y_hy = hyena_latent(hy, hy_conv_w[li], conv_b, hy_bias[li], filt_t, consts, n_batch=n_batch, seq=seq)
        y_wa = window_attention(wa_sink[li], qw, kw, vwt, n_batch=n_batch, seq=seq, n_ctx=n_ctx)
        lq = jnp.stack([da_lq1[li], da_lk1[li], da_lq2[li], da_lk2[li]], axis=0)
        g128 = jnp.tile(da_norm_g[li], LANES // DA_V_DIM)[None, :]
        tq = min(seq, 512)
        g_rows = jnp.broadcast_to(da_norm_g[li][:, None], (DA_V_DIM, tq))
        y_da = diff_attention(lq, g_rows, qd, kd, vdt, n_batch=n_batch, seq=seq, n_ctx=n_ctx, lam_init=lam_init,
                              tq=tq, tk=min(seq, 256), ahead=2, unroll=8 if seq % 2048 == 0 else 1)
        if not last:
            filt_c = hyena_filters(feat_ctx, *fargs, transposed=False)
            yc_hy = ctx_hyena(hy, hy_conv_w[li], conv_b, filt_c, hy_bias[li], n_batch=n_batch, seq=seq, n_ctx=n_ctx)
            yc_wa, yc_da = ctx_attention(wa_sink[li], lq, g128, qw, kw, vw, qd, kd, vd, n_batch=n_batch, seq=seq,
                                         n_ctx=n_ctx, lam_init=lam_init)
            y_hy = jnp.concatenate([y_hy, yc_hy], axis=0)
            y_wa = jnp.concatenate([y_wa, yc_wa], axis=0)
            y_da = jnp.concatenate([y_da, yc_da], axis=0)
        n_rows = n_lat if last else x_all.shape[0]
        x_mid, h2, qp = out_proj(x_all, y_hy, y_wa, y_da, mod, w_out[li].astype(BF16), norm2_g[li][None, :],
                                 peer_wq[li].astype(BF16), n_rows=n_rows, n_batch=n_batch, seq=seq, tm=tm)
        keys = peer_keys[li].reshape(2 * PEER_HEADS, N_KEYS, PEER_QDIM // 2)
        cnt, e1, rank, e2 = peer_topk(qp, keys, tmk=256)
        x_all = peer_dense(x_mid, h2, mod, peer_u[li].astype(BF16), peer_v[li].T.astype(BF16), cnt, e1, rank, e2,
                           final_g[None, :], n_batch=n_batch, seq=seq, tm=tm, te=2048, chunk=1024, col_block=tm,
                           final_norm=last)
        out = x_all
    return out.reshape(n_batch, seq, d)
```

```python
import functools
import math

import numpy as np
import jax
import jax.numpy as jnp
from jax import lax
from jax.experimental import pallas as pl
from jax.experimental.pallas import tpu as pltpu

F32 = jnp.float32
BF16 = jnp.bfloat16
HI = lax.Precision.HIGHEST

LANES = 128
SUBLANES = 8
BF16_ROWS = 16
VMEM_LIMIT_BYTES = 56 * 1024 * 1024

D_MODEL = 1024
N_MOD = 6
NORM_EPS = 1e-6
ROPE_BASE = 10000.0
GRID_W = 64
BLOCK = 128
HEAD_DIM = 64
HY_WIDTH = 256
HY_ORDER = 2
HY_BANDS = 16
HY_EMB_DIM = 1 + 2 * HY_BANDS
HY_FILTER_WIDTH = 64
HY_COLS = 3 * HY_WIDTH
WA_Q_HEADS = 8
WA_KV_HEADS = 2
WA_GROUP = 4
WINDOW = 128
WA_Q_COLS = WA_Q_HEADS * HEAD_DIM
WA_KV_COLS = WA_KV_HEADS * HEAD_DIM
WA_COLS = WA_Q_COLS + 2 * WA_KV_COLS
DA_HEADS = 4
DA_QK_DIM = 32
DA_V_DIM = 64
DA_QK_COLS = DA_HEADS * 2 * DA_QK_DIM
DA_V_COLS = DA_HEADS * DA_V_DIM
DA_COLS = 2 * DA_QK_COLS + DA_V_COLS
DA_LAMBDA_BASE = 0.8
DA_LAMBDA_AMP = 0.6
DA_LAMBDA_RATE = 0.3
IN_COLS = HY_COLS + WA_COLS + DA_COLS
PEER_HEADS = 8
N_KEYS = 128
N_EXPERTS = N_KEYS * N_KEYS
PEER_TOPK = 16
PEER_QDIM = 256
PEER_QCOLS = PEER_HEADS * PEER_QDIM
NEG_BIG = -1e30
LOG2E = math.log2(math.e)


def _cparams(*sem):
    return pltpu.CompilerParams(dimension_semantics=sem, vmem_limit_bytes=VMEM_LIMIT_BYTES)


def _rmsnorm_mod(x, g, shift, scale):
    ms = jnp.mean(x * x, axis=-1, keepdims=True)
    return (x * lax.rsqrt(ms + NORM_EPS) * g) * (1.0 + scale) + shift


def _mod_kernel(c_ref, w_ref, b_ref, o_ref):
    c = c_ref[...]
    act = c * jax.nn.sigmoid(c)
    o_ref[...] = jnp.dot(act, w_ref[...], precision=HI, preferred_element_type=F32) + b_ref[...]


def mod_vectors(cpad, w_mod, b_mod):
    n_layers, d, n = w_mod.shape
    tn = 1536
    return pl.pallas_call(
        _mod_kernel,
        grid=(n_layers, n // tn),
        in_specs=[
            pl.BlockSpec((SUBLANES, d), lambda l, j: (0, 0)),
            pl.BlockSpec((None, d, tn), lambda l, j: (l, 0, j)),
            pl.BlockSpec((None, 1, tn), lambda l, j: (l, 0, j)),
        ],
        out_specs=pl.BlockSpec((None, SUBLANES, tn), lambda l, j: (l, 0, j)),
        out_shape=jax.ShapeDtypeStruct((n_layers, SUBLANES, n), F32),
        compiler_params=_cparams("arbitrary", "arbitrary"),
        name="mod_vectors",
    )(cpad, w_mod, b_mod.reshape(n_layers, 1, n))


def _rope128(x, cos, sin_a, sin_b, m):
    return x * cos + pltpu.roll(x, LANES - m, 1) * sin_a + pltpu.roll(x, m, 1) * sin_b


def _proj_in_kernel(x_ref, mod_ref, g_ref, w_ref, cw_ref, saw_ref, sbw_ref, cd_ref, sad_ref, sbd_ref,
                    hy_ref, qw_ref, kw_ref, vw_ref, qd_ref, kd_ref, vd_ref, vdt_ref, vwt_ref):
    h = _rmsnorm_mod(x_ref[...], g_ref[...], mod_ref[0:1, :], mod_ref[1:2, :]).astype(BF16)
    hy_ref[...] = jnp.dot(h, w_ref[:, 0:HY_COLS], preferred_element_type=F32)
    wa = jnp.dot(h, w_ref[:, HY_COLS:HY_COLS + WA_COLS], preferred_element_type=F32)
    da = jnp.dot(h, w_ref[:, HY_COLS + WA_COLS:IN_COLS], preferred_element_type=F32)
    cw, saw, sbw = cw_ref[...], saw_ref[...], sbw_ref[...]
    cd, sad, sbd = cd_ref[...], sad_ref[...], sbd_ref[...]
    wa_scale = HEAD_DIM ** -0.5
    da_scale = DA_QK_DIM ** -0.5 * LOG2E
    for kb in range(WA_Q_COLS // LANES):
        blk = wa[:, kb * LANES:(kb + 1) * LANES]
        qw_ref[:, kb * LANES:(kb + 1) * LANES] = (_rope128(blk, cw, saw, sbw, 16) * wa_scale).astype(BF16)
    kw_ref[...] = _rope128(wa[:, WA_Q_COLS:WA_Q_COLS + WA_KV_COLS], cw, saw, sbw, 16).astype(BF16)
    vw = wa[:, WA_Q_COLS + WA_KV_COLS:WA_COLS]
    vw_ref[...] = vw.astype(BF16)
    vwt_ref[...] = vw.T.astype(BF16)
    for kb in range(DA_QK_COLS // LANES):
        blk = da[:, kb * LANES:(kb + 1) * LANES]
        qd_ref[:, kb * LANES:(kb + 1) * LANES] = (_rope128(blk, cd, sad, sbd, 8) * da_scale).astype(BF16)
        blk = da[:, DA_QK_COLS + kb * LANES:DA_QK_COLS + (kb + 1) * LANES]
        kd_ref[:, kb * LANES:(kb + 1) * LANES] = _rope128(blk, cd, sad, sbd, 8).astype(BF16)
    vd = da[:, 2 * DA_QK_COLS:DA_COLS]
    vd_ref[...] = vd.astype(BF16)
    vdt_ref[...] = vd.T.astype(BF16)


def _group_of_tile(i, tiles_per_batch, n_batch):
    return jnp.minimum(i // tiles_per_batch, n_batch)


def proj_in(x_all, mod, g, w_bf, rope_wa, rope_da, *, n_batch, seq, tm):
    t_all, d = x_all.shape
    tpb = seq // tm
    grp = functools.partial(_group_of_tile, tiles_per_batch=tpb, n_batch=n_batch)
    pos = lambda i: (jnp.where(i < n_batch * tpb, i % tpb, tpb), 0)
    row = lambda i: (i, 0)
    const = lambda i: (0, 0)
    tab = pl.BlockSpec((tm, LANES), pos)
    outs = [(HY_COLS, F32), (WA_Q_COLS, BF16), (WA_KV_COLS, BF16), (WA_KV_COLS, BF16),
            (DA_QK_COLS, BF16), (DA_QK_COLS, BF16), (DA_V_COLS, BF16)]
    return pl.pallas_call(
        _proj_in_kernel,
        grid=(t_all // tm,),
        in_specs=[
            pl.BlockSpec((tm, d), row),
            pl.BlockSpec((None, N_MOD, d), lambda i: (grp(i), 0, 0)),
            pl.BlockSpec((1, d), const),
            pl.BlockSpec((d, IN_COLS), const),
            tab, tab, tab, tab, tab, tab,
        ],
        out_specs=[pl.BlockSpec((tm, n), row) for n, _ in outs]
        + [pl.BlockSpec((DA_V_COLS, tm), lambda i: (0, i)), pl.BlockSpec((WA_KV_COLS, tm), lambda i: (0, i))],
        out_shape=[jax.ShapeDtypeStruct((t_all, n), dt) for n, dt in outs]
        + [jax.ShapeDtypeStruct((DA_V_COLS, t_all), BF16), jax.ShapeDtypeStruct((WA_KV_COLS, t_all), BF16)],
        compiler_params=_cparams("arbitrary"),
        name="proj_in",
    )(x_all, mod, g, w_bf, *rope_wa, *rope_da)


def rope_tables(seq, head_dim, pad_rows):
    half = head_dim // 2
    axis_dim = half
    m = axis_dim // 2
    f32 = np.float32
    t = np.arange(seq)
    rowcol = np.stack([t // GRID_W, t % GRID_W], axis=0).astype(f32)
    inv_freq = f32(ROPE_BASE) ** (-np.arange(0, axis_dim, 2, dtype=f32) / f32(axis_dim))
    ang = rowcol[:, :, None] * inv_freq[None, None, :]
    cos, sin = np.cos(ang), np.sin(ang)
    lane = np.arange(LANES)
    dd = lane % head_dim
    axis = dd // half
    sub = dd % half
    idx = sub % m
    first = sub < m
    cos_t = cos[axis, :, idx].T
    sin_t = sin[axis, :, idx].T
    sin_a = np.where(first[None, :], -sin_t, f32(0))
    sin_b = np.where(first[None, :], f32(0), sin_t)
    pad1 = np.ones((pad_rows, LANES), f32)
    pad0 = np.zeros((pad_rows, LANES), f32)
    tables = (np.concatenate([cos_t, pad1], 0), np.concatenate([sin_a, pad0], 0), np.concatenate([sin_b, pad0], 0))
    return tuple(jnp.asarray(x, F32) for x in tables)


def _softmax_sink_attend(q, k, v, sink, valid):
    s = lax.dot_general(q, k, (((1,), (1,)), ((), ())), preferred_element_type=F32)
    if valid is not None:
        nm = valid.shape[1]
        s = jnp.concatenate([jnp.where(valid, s[:, :nm], NEG_BIG), s[:, nm:]], axis=1)
    m = jnp.maximum(jnp.max(s, axis=1, keepdims=True), sink)
    p = jnp.exp(s - m)
    den = jnp.sum(p, axis=1, keepdims=True) + jnp.exp(sink - m)
    return jnp.dot(p.astype(BF16), v, preferred_element_type=F32) / den


def _wattn_kernel(sink_ref, q_ref, kp_ref, kc_ref, kn_ref, kx_ref, vp_ref, vc_ref, vn_ref, vx_ref, o_ref, qm_scr,
                  *, n_pairs, ahead):
    m_idx = pl.program_id(1)
    nq = 2 * BLOCK
    n_loc = 4 * BLOCK
    kk = lax.broadcasted_iota(jnp.int32, (n_loc, nq), 0)
    qq = lax.broadcasted_iota(jnp.int32, (n_loc, nq), 1)
    lo = jnp.where(m_idx > 0, 0, BLOCK)
    hi = jnp.where(m_idx < n_pairs - 1, n_loc, n_loc - BLOCK)
    valid = (kk >= jnp.maximum(qq, lo)) & (kk <= qq + 2 * WINDOW) & (kk < hi)
    k_all = jnp.concatenate([kp_ref[...], kc_ref[...], kn_ref[...], kx_ref[...]], axis=0)
    vt_all = jnp.concatenate([vp_ref[...], vc_ref[...], vn_ref[...], vx_ref[...]], axis=1)
    lane = lax.broadcasted_iota(jnp.int32, (nq, LANES), 1)
    for h in range(WA_Q_HEADS):
        hk = h // WA_GROUP
        blk = q_ref[:, (h // 2) * LANES:(h // 2 + 1) * LANES]
        if h % 2 != hk:
            blk = jnp.concatenate([blk[:, HEAD_DIM:], blk[:, :HEAD_DIM]], axis=1)
        keep = (lane >= hk * HEAD_DIM) & (lane < (hk + 1) * HEAD_DIM)
        qm_scr[h] = jnp.where(keep, blk, jnp.zeros_like(blk))

    def scores(h):
        return lax.dot_general(k_all, qm_scr[h], (((1,), (1,)), ((), ())), preferred_element_type=F32)

    sts = [scores(h) for h in range(ahead)]
    outs = []
    for h in range(WA_Q_HEADS):
        hk = h // WA_GROUP
        if h + ahead < WA_Q_HEADS:
            sts.append(scores(h + ahead))
        st = sts[h]
        sts[h] = None
        st = jnp.concatenate([jnp.where(valid, st[:n_loc], NEG_BIG), st[n_loc:]], axis=0)
        sink = sink_ref[h]
        m = jnp.maximum(jnp.max(st, axis=0, keepdims=True), sink)
        p = jnp.exp(st - m)
        den = jnp.sum(p, axis=0, keepdims=True) + jnp.exp(sink - m)
        o = jnp.dot(vt_all[hk * HEAD_DIM:(hk + 1) * HEAD_DIM, :], p.astype(BF16), preferred_element_type=F32)
        outs.append(o / den)
    o_ref[...] = jnp.concatenate(outs, axis=0).T.astype(BF16)


def window_attention(sink, qw, kw, vwt, *, n_batch, seq, n_ctx):
    nb = seq // BLOCK
    n_pairs = nb // 2
    assert nb % 2 == 0
    ctx0 = (n_batch * seq) // n_ctx
    cur = lambda b, m: (b * n_pairs + m, 0)
    prev = lambda b, m: (b * nb + jnp.maximum(2 * m - 1, 0), 0)
    nxt = lambda b, m: (b * nb + jnp.minimum(2 * m + 2, nb - 1), 0)
    ctx = lambda b, m: (ctx0 + b, 0)
    swap = lambda f: (lambda b, m: f(b, m)[::-1])
    return pl.pallas_call(
        functools.partial(_wattn_kernel, n_pairs=n_pairs, ahead=3),
        grid=(n_batch, n_pairs),
        in_specs=[
            pl.BlockSpec(memory_space=pltpu.SMEM),
            pl.BlockSpec((2 * BLOCK, WA_Q_COLS), cur),
            pl.BlockSpec((BLOCK, WA_KV_COLS), prev), pl.BlockSpec((2 * BLOCK, WA_KV_COLS), cur),
            pl.BlockSpec((BLOCK, WA_KV_COLS), nxt), pl.BlockSpec((n_ctx, WA_KV_COLS), ctx),
            pl.BlockSpec((WA_KV_COLS, BLOCK), swap(prev)), pl.BlockSpec((WA_KV_COLS, 2 * BLOCK), swap(cur)),
            pl.BlockSpec((WA_KV_COLS, BLOCK), swap(nxt)), pl.BlockSpec((WA_KV_COLS, n_ctx), swap(ctx)),
        ],
        out_specs=pl.BlockSpec((2 * BLOCK, WA_Q_COLS), cur),
        out_shape=jax.ShapeDtypeStruct((n_batch * seq, WA_Q_COLS), BF16),
        scratch_shapes=[pltpu.VMEM((WA_Q_HEADS, 2 * BLOCK, LANES), BF16)],
        compiler_params=_cparams("arbitrary", "arbitrary"),
        name="window_attention",
    )(sink, qw, kw, kw, kw, kw, vwt, vwt, vwt, vwt)


def _diff_lambda(lq_ref, lam_init):
    a = jnp.sum(lq_ref[0:1, :] * lq_ref[1:2, :], axis=1, keepdims=True)
    b = jnp.sum(lq_ref[2:3, :] * lq_ref[3:4, :], axis=1, keepdims=True)
    return jnp.exp(a) - jnp.exp(b) + lam_init


def _diff_finish(acc1, l1, acc2, l2, lam, g128, lam_init):
    lane = lax.broadcasted_iota(jnp.int32, acc1[0].shape, 1)
    lo = lane < DA_V_DIM
    blocks = []
    for vb in range(DA_HEADS // 2):
        o = []
        for h in (2 * vb, 2 * vb + 1):
            o.append(acc1[h] / l1[h] - lam * (acc2[h] / l2[h]))
        blk = jnp.where(lo, o[0], o[1])
        sq = blk * blk
        s_lo = jnp.sum(jnp.where(lo, sq, 0.0), axis=1, keepdims=True)
        s_hi = jnp.sum(jnp.where(lo, 0.0, sq), axis=1, keepdims=True)
        ms = jnp.where(lo, s_lo, s_hi) * (1.0 / DA_V_DIM)
        blocks.append(blk * lax.rsqrt(ms + NORM_EPS) * g128 * (1.0 - lam_init))
    return jnp.concatenate(blocks, axis=1)


def _masked_q(q_ref, hm):
    kb, sub = divmod(hm, LANES // DA_QK_DIM)
    blk = q_ref[:, kb * LANES:(kb + 1) * LANES]
    lane = lax.broadcasted_iota(jnp.int32, blk.shape, 1)
    keep = (lane >= sub * DA_QK_DIM) & (lane < (sub + 1) * DA_QK_DIM)
    return jnp.where(keep, blk, jnp.zeros_like(blk))


def _dattn_kernel(lq_ref, g_ref, q_ref, k_ref, vt_ref, kx_ref, vxt_ref, o_ref, qm_scr, m_scr, l_scr, acc_scr,
                  *, lam_init, tk, ahead, unroll):
    n_hm = 2 * DA_HEADS
    per_blk = LANES // DA_QK_DIM
    for hm in range(n_hm):
        qm_scr[hm] = _masked_q(q_ref, hm)
    m_scr[...] = jnp.full(m_scr.shape, NEG_BIG, F32)
    l_scr[...] = jnp.zeros(l_scr.shape, F32)
    acc_scr[...] = jnp.zeros(acc_scr.shape, F32)

    def update(chunks):
        chains = [(c, hm) for c in range(len(chunks)) for hm in range(n_hm)]

        def scores(chain):
            c, hm = chain
            kb = hm // per_blk
            return lax.dot_general(chunks[c][0][:, kb * LANES:(kb + 1) * LANES], qm_scr[hm],
                                   (((1,), (1,)), ((), ())), preferred_element_type=F32)

        sts = [scores(ch) for ch in chains[:ahead]]
        for n, (c, hm) in enumerate(chains):
            h = hm // 2
            vtc = chunks[c][1]
            if n + ahead < len(chains):
                sts.append(scores(chains[n + ahead]))
            st = sts[n]
            sts[n] = None
            m_old = m_scr[hm]
            m_new = jnp.maximum(m_old, jnp.max(st, axis=0, keepdims=True))
            alpha = jnp.exp2(m_old - m_new)
            pt = jnp.exp2(st - m_new)
            l_scr[hm] = alpha * l_scr[hm] + jnp.sum(pt, axis=0, keepdims=True)
            acc_scr[hm] = alpha * acc_scr[hm] + jnp.dot(vtc[h * DA_V_DIM:(h + 1) * DA_V_DIM, :], pt.astype(BF16),
                                                        preferred_element_type=F32)
            m_scr[hm] = m_new

    def body(j, carry):
        chunks = []
        for c in range(unroll):
            start = pl.multiple_of((j * unroll + c) * tk, tk)
            chunks.append((k_ref[pl.ds(start, tk), :], vt_ref[:, pl.ds(start, tk)]))
        update(chunks)
        return carry

    lax.fori_loop(0, k_ref.shape[0] // (tk * unroll), body, 0)
    n_x = kx_ref.shape[0] // tk
    update([(kx_ref[c * tk:(c + 1) * tk, :], vxt_ref[:, c * tk:(c + 1) * tk]) for c in range(n_x)])
    lam = _diff_lambda(lq_ref, lam_init)
    heads = []
    for h in range(DA_HEADS):
        o = acc_scr[2 * h] / l_scr[2 * h] - lam * (acc_scr[2 * h + 1] / l_scr[2 * h + 1])
        ms = jnp.mean(o * o, axis=0, keepdims=True)
        heads.append(o * lax.rsqrt(ms + NORM_EPS) * g_ref[...] * (1.0 - lam_init))
    o_ref[...] = jnp.concatenate(heads, axis=0).T.astype(BF16)


def diff_attention(lq, g_rows, qd, kd, vdt, *, n_batch, seq, n_ctx, lam_init, tq, tk, ahead, unroll):
    nq = seq // tq
    ctx0 = (n_batch * seq) // n_ctx
    n_hm = 2 * DA_HEADS
    return pl.pallas_call(
        functools.partial(_dattn_kernel, lam_init=lam_init, tk=tk, ahead=ahead, unroll=unroll),
        grid=(n_batch, nq),
        in_specs=[
            pl.BlockSpec((4, DA_QK_DIM), lambda b, i: (0, 0)),
            pl.BlockSpec((DA_V_DIM, tq), lambda b, i: (0, 0)),
            pl.BlockSpec((tq, DA_QK_COLS), lambda b, i: (b * nq + i, 0)),
            pl.BlockSpec((seq, DA_QK_COLS), lambda b, i: (b, 0)),
            pl.BlockSpec((DA_V_COLS, seq), lambda b, i: (0, b)),
            pl.BlockSpec((n_ctx, DA_QK_COLS), lambda b, i: (ctx0 + b, 0)),
            pl.BlockSpec((DA_V_COLS, n_ctx), lambda b, i: (0, ctx0 + b)),
        ],
        out_specs=pl.BlockSpec((tq, DA_V_COLS), lambda b, i: (b * nq + i, 0)),
        out_shape=jax.ShapeDtypeStruct((n_batch * seq, DA_V_COLS), BF16),
        scratch_shapes=[
            pltpu.VMEM((n_hm, tq, LANES), BF16),
            pltpu.VMEM((n_hm, 1, tq), F32),
            pltpu.VMEM((n_hm, 1, tq), F32),
            pltpu.VMEM((n_hm, DA_V_DIM, tq), F32),
        ],
        compiler_params=_cparams("arbitrary", "arbitrary"),
        name="diff_attention",
    )(lq, g_rows, qd, kd, vdt, kd, vdt)


def _ctx_attn_kernel(sink_ref, lq_ref, g_ref, qw_ref, kw_ref, vw_ref, qd_ref, kd_ref, vd_ref, ow_ref, od_ref,
                     *, lam_init):
    outs = []
    for hk in range(WA_KV_HEADS):
        sl = slice(hk * HEAD_DIM, (hk + 1) * HEAD_DIM)
        k, v = kw_ref[:, sl], vw_ref[:, sl]
        for g in range(WA_GROUP):
            h = hk * WA_GROUP + g
            outs.append(_softmax_sink_attend(qw_ref[:, h * HEAD_DIM:(h + 1) * HEAD_DIM], k, v, sink_ref[h], None))
    ow_ref[...] = jnp.concatenate(outs, axis=1).astype(BF16)
    per_blk = LANES // DA_QK_DIM
    acc, den = [], []
    for hm in range(2 * DA_HEADS):
        kb, vb = hm // per_blk, hm // 4
        s = lax.dot_general(_masked_q(qd_ref, hm), kd_ref[:, kb * LANES:(kb + 1) * LANES],
                            (((1,), (1,)), ((), ())), preferred_element_type=F32)
        p = jnp.exp2(s - jnp.max(s, axis=1, keepdims=True))
        den.append(jnp.sum(p, axis=1, keepdims=True))
        acc.append(jnp.dot(p.astype(BF16), vd_ref[:, vb * LANES:(vb + 1) * LANES], preferred_element_type=F32))
    lam = _diff_lambda(lq_ref, lam_init)
    od_ref[...] = _diff_finish(acc[0::2], den[0::2], acc[1::2], den[1::2], lam, g_ref[...], lam_init).astype(BF16)


def ctx_attention(sink, lq, g128, qw, kw, vw, qd, kd, vd, *, n_batch, seq, n_ctx, lam_init):
    ctx0 = (n_batch * seq) // n_ctx
    ctx = lambda b: (ctx0 + b, 0)
    out = lambda b: (b, 0)
    spec = lambda n, f: pl.BlockSpec((n_ctx, n), f)
    return pl.pallas_call(
        functools.partial(_ctx_attn_kernel, lam_init=lam_init),
        grid=(n_batch,),
        in_specs=[
            pl.BlockSpec(memory_space=pltpu.SMEM),
            pl.BlockSpec((4, DA_QK_DIM), lambda b: (0, 0)),
            pl.BlockSpec((1, LANES), lambda b: (0, 0)),
            spec(WA_Q_COLS, ctx), spec(WA_KV_COLS, ctx), spec(WA_KV_COLS, ctx),
            spec(DA_QK_COLS, ctx), spec(DA_QK_COLS, ctx), spec(DA_V_COLS, ctx),
        ],
        out_specs=[spec(WA_Q_COLS, out), spec(DA_V_COLS, out)],
        out_shape=[jax.ShapeDtypeStruct((n_batch * n_ctx, WA_Q_COLS), BF16),
                   jax.ShapeDtypeStruct((n_batch * n_ctx, DA_V_COLS), BF16)],
        compiler_params=_cparams("arbitrary"),
        name="ctx_attention",
    )(sink, lq, g128, qw, kw, vw, qd, kd, vd)


def _filter_kernel(feat_ref, w1_ref, b1_ref, fr_ref, w2_ref, b2_ref, w3_ref, dec_ref, o_ref, *, transposed):
    feat = feat_ref[...]
    fr = fr_ref[...]
    z = jnp.sin(fr * (jnp.dot(feat, w1_ref[...], precision=HI, preferred_element_type=F32) + b1_ref[...]))
    z = jnp.sin(fr * (jnp.dot(z, w2_ref[...], precision=HI, preferred_element_type=F32) + b2_ref[...]))
    h = jnp.dot(z, w3_ref[...], precision=HI, preferred_element_type=F32)
    h = h * jnp.exp(-feat[:, 0:1] * jnp.abs(dec_ref[...]))
    rows = lax.broadcasted_iota(jnp.int32, h.shape, 0) + pl.program_id(0) * h.shape[0]
    cols = lax.broadcasted_iota(jnp.int32, h.shape, 1)
    h = jnp.where((rows == 0) & (cols % (2 * HY_WIDTH) >= HY_WIDTH), 0.0, h)
    if transposed:
        for r in range(h.shape[0] // LANES):
            o_ref[r] = h[r * LANES:(r + 1) * LANES, :].T
    else:
        o_ref[...] = h


def hyena_filter_features(n):
    f32 = np.float32
    t = np.linspace(0.0, 1.0, n, dtype=f32)[:, None]
    bands = np.linspace(1e-4, HY_BANDS - 1, HY_BANDS, dtype=f32)
    phase = (f32(2.0 * math.pi / n) * np.arange(n, dtype=f32)[:, None]) * bands
    feat = np.concatenate([t, np.cos(phase), -np.sin(phase)], axis=-1).astype(f32)
    return jnp.asarray(np.pad(feat, ((0, 0), (0, LANES - HY_EMB_DIM))), F32)


def hyena_filters(feat, w1p, b1, freq, w2, b2, w3, decay, *, transposed):
    n = feat.shape[0]
    tr = min(n, 512)
    ncol = w3.shape[1]
    fw = w2.shape[0]
    const = lambda i: (0, 0)
    if transposed:
        out_spec = pl.BlockSpec((tr // LANES, ncol, LANES), lambda i: (i, 0, 0))
        out_shape = jax.ShapeDtypeStruct((n // LANES, ncol, LANES), F32)
    else:
        out_spec = pl.BlockSpec((tr, ncol), lambda i: (i, 0))
        out_shape = jax.ShapeDtypeStruct((n, ncol), F32)
    return pl.pallas_call(
        functools.partial(_filter_kernel, transposed=transposed),
        grid=(n // tr,),
        in_specs=[
            pl.BlockSpec((tr, LANES), lambda i: (i, 0)),
            pl.BlockSpec((LANES, fw), const), pl.BlockSpec((1, fw), const), pl.BlockSpec((1, fw), const),
            pl.BlockSpec((fw, fw), const), pl.BlockSpec((1, fw), const),
            pl.BlockSpec((fw, ncol), const), pl.BlockSpec((1, ncol), const),
        ],
        out_specs=out_spec,
        out_shape=out_shape,
        compiler_params=_cparams("arbitrary"),
        name="hyena_filters",
    )(feat, w1p, b1, freq, w2, b2, w3, decay)


def dft_constants(seq):
    n = 2 * seq
    n1 = n // LANES
    a_rows = n1 // 2
    k1 = np.arange(n1, dtype=np.float64)[:, None]
    a = np.arange(a_rows, dtype=np.float64)[None, :]
    th = 2.0 * np.pi * k1 * a / n1
    c1, s1 = np.cos(th), np.sin(th)
    m1_data = np.block([[c1, s1], [-s1, c1]])
    half_rows = -(-(n1 // 2 + 1) // SUBLANES) * SUBLANES
    pad = np.zeros((half_rows - (n1 // 2 + 1), a_rows))
    m1_real = np.concatenate([c1[:n1 // 2 + 1], pad, -s1[:n1 // 2 + 1], pad], axis=0)
    m3 = np.block([[c1.T, -s1.T], [s1.T, c1.T]]) / n
    b = np.arange(LANES, dtype=np.float64)[None, :]
    psi = 2.0 * np.pi * k1 * b / n
    twc, tws = np.cos(psi), np.sin(psi)
    bb = np.arange(LANES, dtype=np.float64)
    phi = 2.0 * np.pi * np.outer(bb, bb) / LANES
    c2, s2 = np.cos(phi), np.sin(phi)
    w2f = np.block([[c2, -s2], [s2, c2]])
    w2i = np.block([[c2, s2], [-s2, c2]])
    f = lambda x: jnp.asarray(x, dtype=F32)
    return dict(m1_data=f(m1_data), m1_real=f(m1_real), m3=f(m3), twc=f(twc[:, None, :]), tws=f(tws[:, None, :]),
                w2f=f(w2f), w2i=f(w2i), n1=n1, a_rows=a_rows, half_rows=half_rows)


def _split_bf16(x):
    hi = x.astype(BF16)
    return hi, (x - hi.astype(F32)).astype(BF16)


def _mm3(a, b, nt=False):
    ah, al = _split_bf16(a)
    bh, bl = _split_bf16(b)
    dims = (((1,), (1 if nt else 0,)), ((), ()))
    dot = lambda x, y: lax.dot_general(x, y, dims, preferred_element_type=F32)
    return dot(ah, bh) + (dot(ah, bl) + dot(al, bh))


def _leftmm_kernel(m_ref, x_ref, o_ref):
    o_ref[...] = _mm3(m_ref[...], x_ref[...])


def left_matmul(m, x, ct):
    r, k = m.shape
    cols = x.shape[1]
    return pl.pallas_call(
        _leftmm_kernel,
        grid=(cols // ct,),
        in_specs=[pl.BlockSpec((r, k), lambda j: (0, 0)), pl.BlockSpec((k, ct), lambda j: (0, j))],
        out_specs=pl.BlockSpec((r, ct), lambda j: (0, j)),
        out_shape=jax.ShapeDtypeStruct((r, cols), F32),
        compiler_params=_cparams("arbitrary"),
        name="dft_rows",
    )(m, x)


def _leftmm_gate_kernel(m_ref, zr_ref, zi_ref, x_ref, zold_ref, bias_ref, o_ref):
    n1 = zr_ref.shape[0]
    y = _mm3(m_ref[:, :n1], zr_ref[...]) + _mm3(m_ref[:, n1:], zi_ref[...])
    zold = zold_ref[...]
    o_ref[...] = x_ref[...] * (y + bias_ref[...] * zold)


def left_matmul_gate(m, zr, zi, xg, zold, bias_cols, ct):
    r, k = m.shape
    n1, cols = zr.shape
    col = lambda j: (0, j)
    return pl.pallas_call(
        _leftmm_gate_kernel,
        grid=(cols // ct,),
        in_specs=[pl.BlockSpec((r, k), lambda j: (0, 0)), pl.BlockSpec((n1, ct), col), pl.BlockSpec((n1, ct), col),
                  pl.BlockSpec((r, ct), col), pl.BlockSpec((r, ct), col), pl.BlockSpec((1, ct), col)],
        out_specs=pl.BlockSpec((r, ct), col),
        out_shape=jax.ShapeDtypeStruct((r, cols), F32),
        compiler_params=_cparams("arbitrary"),
        name="idft_rows_gate",
    )(m, zr, zi, xg, zold, bias_cols)


def _twiddle_fwd(yr, yi, c, s):
    return yr * c + yi * s, yi * c - yr * s


def _spectrum_kernel(yr_ref, yi_ref, twc_ref, tws_ref, w2f_ref, hr_ref, hi_ref, *, kb, n1):
    for t in range(kb):
        mirrored = pl.program_id(0) * kb + t > n1 // 2
        yi = yi_ref[t]
        ypr, ypi = _twiddle_fwd(yr_ref[t], jnp.where(mirrored, -yi, yi), twc_ref[t], tws_ref[t])
        x = _mm3(jnp.concatenate([ypr, ypi], axis=1), w2f_ref[...])
        xr, xi = x[:, :LANES], x[:, LANES:]
        for o in range(HY_ORDER):
            f0 = slice((2 * o) * HY_WIDTH, (2 * o + 1) * HY_WIDTH)
            f1 = slice((2 * o + 1) * HY_WIDTH, (2 * o + 2) * HY_WIDTH)
            hr_ref[o, t] = xr[f0] + xr[f1]
            hi_ref[o, t] = xi[f0] - xi[f1]


def filter_spectrum(y, consts, kb):
    assert kb == 1
    n1, half_rows = consts["n1"], consts["half_rows"]
    nc = 2 * HY_ORDER * HY_WIDTH
    y4 = y.reshape(2, half_rows, nc, LANES)
    tw = pl.BlockSpec((kb, 1, LANES), lambda i: (i, 0, 0))
    out = pl.BlockSpec((HY_ORDER, kb, HY_WIDTH, LANES), lambda i: (0, i, 0, 0))
    shp = jax.ShapeDtypeStruct((HY_ORDER, n1, HY_WIDTH, LANES), F32)
    src = lambda i: jnp.where(i > n1 // 2, n1 - i, i)
    return pl.pallas_call(
        functools.partial(_spectrum_kernel, kb=kb, n1=n1),
        grid=(n1 // kb,),
        in_specs=[
            pl.BlockSpec((None, kb, nc, LANES), lambda i: (0, src(i), 0, 0)),
            pl.BlockSpec((None, kb, nc, LANES), lambda i: (1, src(i), 0, 0)),
            tw, tw,
            pl.BlockSpec((2 * LANES, 2 * LANES), lambda i: (0, 0)),
        ],
        out_specs=[out, out],
        out_shape=[shp, shp],
        compiler_params=_cparams("arbitrary"),
        name="filter_spectrum",
    )(y4, y4, consts["twc"], consts["tws"], consts["w2f"])


def _freq_kernel(yr_ref, yi_ref, twc_ref, tws_ref, w2f_ref, w2i_ref, hr_ref, hi_ref, zr_ref, zi_ref, *, kb):
    for t in range(kb):
        c, s = twc_ref[t], tws_ref[t]
        ypr, ypi = _twiddle_fwd(yr_ref[t], yi_ref[t], c, s)
        x = _mm3(jnp.concatenate([ypr, ypi], axis=1), w2f_ref[...])
        xr, xi = x[:, :LANES], x[:, LANES:]
        hr, hi = hr_ref[t], hi_ref[t]
        gr = xr * hr - xi * hi
        gi = xr * hi + xi * hr
        z = _mm3(jnp.concatenate([gr, gi], axis=1), w2i_ref[...])
        zr, zi = z[:, :LANES], z[:, LANES:]
        zr_ref[t] = zr * c - zi * s
        zi_ref[t] = zi * c + zr * s


def freq_multiply(y, hr, hi, order, consts, kb):
    n1 = consts["n1"]
    y4 = y.reshape(2, n1, HY_WIDTH, LANES)
    tw = pl.BlockSpec((kb, 1, LANES), lambda i: (i, 0, 0))
    blk = lambda p: pl.BlockSpec((None, kb, HY_WIDTH, LANES), lambda i: (p, i, 0, 0))
    wspec = pl.BlockSpec((2 * LANES, 2 * LANES), lambda i: (0, 0))
    out = pl.BlockSpec((kb, HY_WIDTH, LANES), lambda i: (i, 0, 0))
    shp = jax.ShapeDtypeStruct((n1, HY_WIDTH, LANES), F32)
    return pl.pallas_call(
        functools.partial(_freq_kernel, kb=kb),
        grid=(n1 // kb,),
        in_specs=[blk(0), blk(1), tw, tw, wspec, wspec, blk(order), blk(order)],
        out_specs=[out, out],
        out_shape=[shp, shp],
        compiler_params=_cparams("arbitrary"),
        name="freq_multiply",
    )(y4, y4, consts["twc"], consts["tws"], consts["w2f"], consts["w2i"], hr, hi)


def _short_conv_rows(u, prev_row, next_row, w_ref, b_ref):
    rows = lax.broadcasted_iota(jnp.int32, u.shape, 0)
    um1 = jnp.where(rows == 0, prev_row, pltpu.roll(u, 1, 0))
    up1 = jnp.where(rows == u.shape[0] - 1, next_row, pltpu.roll(u, u.shape[0] - 1, 0))
    return b_ref[...] + um1 * w_ref[0:1, :] + u * w_ref[1:2, :] + up1 * w_ref[2:3, :]


def _shortconv_t_kernel(u_ref, up_ref, un_ref, w_ref, b_ref, o_ref, *, n_tiles):
    i = pl.program_id(1)
    prev_row = jnp.where(i > 0, up_ref[SUBLANES - 1:SUBLANES, :], 0.0)
    next_row = jnp.where(i < n_tiles - 1, un_ref[0:1, :], 0.0)
    v = _short_conv_rows(u_ref[...], prev_row, next_row, w_ref, b_ref)
    for part in range(HY_ORDER + 1):
        for r in range(v.shape[0] // LANES):
            o_ref[part, r] = v[r * LANES:(r + 1) * LANES, part * HY_WIDTH:(part + 1) * HY_WIDTH].T


def short_conv_transposed(hy, conv_w, conv_b, *, n_batch, seq, tm):
    nt = seq // tm
    per8 = tm // SUBLANES
    a_rows = seq // LANES
    prev = lambda b, i: (jnp.maximum((b * nt + i) * per8 - 1, 0), 0)
    nxt = lambda b, i: (jnp.minimum((b * nt + i + 1) * per8, n_batch * nt * per8 - 1), 0)
    return pl.pallas_call(
        functools.partial(_shortconv_t_kernel, n_tiles=nt),
        grid=(n_batch, nt),
        in_specs=[
            pl.BlockSpec((tm, HY_COLS), lambda b, i: (b * nt + i, 0)),
            pl.BlockSpec((SUBLANES, HY_COLS), prev),
            pl.BlockSpec((SUBLANES, HY_COLS), nxt),
            pl.BlockSpec((3, HY_COLS), lambda b, i: (0, 0)),
            pl.BlockSpec((1, HY_COLS), lambda b, i: (0, 0)),
        ],
        out_specs=pl.BlockSpec((HY_ORDER + 1, None, tm // LANES, HY_WIDTH, LANES), lambda b, i: (0, b, i, 0, 0)),
        out_shape=jax.ShapeDtypeStruct((HY_ORDER + 1, n_batch, a_rows, HY_WIDTH, LANES), F32),
        compiler_params=_cparams("arbitrary", "arbitrary"),
        name="short_conv_transposed",
    )(hy, hy, hy, conv_w, conv_b)


def _untranspose_kernel(z_ref, o_ref):
    for r in range(z_ref.shape[0]):
        o_ref[r * LANES:(r + 1) * LANES, :] = z_ref[r].T.astype(BF16)


def untranspose(zt, *, n_batch, seq):
    a_rows = seq // LANES
    ta = min(SUBLANES, a_rows)
    nt = a_rows // ta
    return pl.pallas_call(
        _untranspose_kernel,
        grid=(n_batch, nt),
        in_specs=[pl.BlockSpec((None, ta, HY_WIDTH, LANES), lambda b, i: (b, i, 0, 0))],
        out_specs=pl.BlockSpec((ta * LANES, HY_WIDTH), lambda b, i: (b * nt + i, 0)),
        out_shape=jax.ShapeDtypeStruct((n_batch * seq, HY_WIDTH), BF16),
        compiler_params=_cparams("arbitrary", "arbitrary"),
        name="untranspose",
    )(zt)


def hyena_latent(hy, conv_w, conv_b, hy_bias, filt_t, consts, *, n_batch, seq):
    assert n_batch == 2, "the two batches are packed as real / imaginary parts of one DFT"
    n1, a_rows = consts["n1"], consts["a_rows"]
    cols = HY_WIDTH * LANES
    ct = min(cols, 2048)
    kb = min(n1, 4)
    parts = short_conv_transposed(hy, conv_w, conv_b, n_batch=n_batch, seq=seq, tm=min(seq, 1024))
    parts = parts.reshape(HY_ORDER + 1, n_batch * a_rows, cols)
    yf = left_matmul(consts["m1_real"], filt_t.reshape(a_rows, 2 * HY_ORDER * cols), ct)
    hr, hi = filter_spectrum(yf, consts, 1)
    z = parts[HY_ORDER]
    for o in range(HY_ORDER):
        y = left_matmul(consts["m1_data"], z, ct)
        zr, zi = freq_multiply(y, hr, hi, o, consts, kb)
        bias_cols = jnp.repeat(hy_bias[o], LANES)[None, :]
        z = left_matmul_gate(consts["m3"], zr.reshape(n1, cols), zi.reshape(n1, cols), parts[o], z, bias_cols, ct)
    return untranspose(z.reshape(n_batch, a_rows, HY_WIDTH, LANES), n_batch=n_batch, seq=seq)


def _ctx_hyena_kernel(u_ref, w_ref, b_ref, h_ref, bias_ref, fc_ref, fs_ref, fct_ref, fst_ref, o_ref):
    u = u_ref[...]
    zero = jnp.zeros((1, u.shape[1]), F32)
    v = _short_conv_rows(u, zero, zero, w_ref, b_ref)
    z = v[:, HY_ORDER * HY_WIDTH:]
    mm = lambda a, b: jnp.dot(a, b, precision=HI, preferred_element_type=F32)
    fc, fs = fc_ref[...], fs_ref[...]
    inv_n = 1.0 / fc.shape[0]
    for o in range(HY_ORDER):
        h0 = h_ref[:, (2 * o) * HY_WIDTH:(2 * o + 1) * HY_WIDTH]
        h1 = h_ref[:, (2 * o + 1) * HY_WIDTH:(2 * o + 2) * HY_WIDTH]
        sr, si = mm(fc, h0 + h1), mm(fs, h1 - h0)
        zr, zi = mm(fc, z), -mm(fs, z)
        gr = zr * sr - zi * si
        gi = zr * si + zi * sr
        y = (mm(fct_ref[...], gr) - mm(fst_ref[...], gi)) * inv_n
        z = v[:, o * HY_WIDTH:(o + 1) * HY_WIDTH] * (y + bias_ref[o:o + 1, :] * z)
    o_ref[...] = z.astype(BF16)


def ctx_hyena(hy, conv_w, conv_b, filt, hy_bias, *, n_batch, seq, n_ctx):
    n = 2 * n_ctx
    k = np.arange(n, dtype=np.float64)[:, None]
    m = np.arange(n_ctx, dtype=np.float64)[None, :]
    ang = 2.0 * np.pi * k * m / n
    fc, fs = jnp.asarray(np.cos(ang), F32), jnp.asarray(np.sin(ang), F32)
    ctx0 = (n_batch * seq) // n_ctx
    const = lambda b: (0, 0)
    return pl.pallas_call(
        _ctx_hyena_kernel,
        grid=(n_batch,),
        in_specs=[
            pl.BlockSpec((n_ctx, HY_COLS), lambda b: (ctx0 + b, 0)),
            pl.BlockSpec((3, HY_COLS), const), pl.BlockSpec((1, HY_COLS), const),
            pl.BlockSpec((n_ctx, 2 * HY_ORDER * HY_WIDTH), const),
            pl.BlockSpec((HY_ORDER, HY_WIDTH), const),
            pl.BlockSpec((n, n_ctx), const), pl.BlockSpec((n, n_ctx), const),
            pl.BlockSpec((n_ctx, n), const), pl.BlockSpec((n_ctx, n), const),
        ],
        out_specs=pl.BlockSpec((n_ctx, HY_WIDTH), lambda b: (b, 0)),
        out_shape=jax.ShapeDtypeStruct((n_batch * n_ctx, HY_WIDTH), BF16),
        compiler_params=_cparams("arbitrary"),
        name="ctx_hyena",
    )(hy, conv_w, conv_b, filt, hy_bias, fc, fs, fc.T, fs.T)


def _outproj_kernel(x_ref, yh_ref, yw_ref, yd_ref, mod_ref, wo_ref, g_ref, wq_ref, xo_ref, h2_ref, qp_ref):
    y = jnp.dot(yh_ref[...], wo_ref[0:HY_WIDTH, :], preferred_element_type=F32)
    y = y + jnp.dot(yw_ref[...], wo_ref[HY_WIDTH:HY_WIDTH + WA_Q_COLS, :], preferred_element_type=F32)
    y = y + jnp.dot(yd_ref[...], wo_ref[HY_WIDTH + WA_Q_COLS:, :], preferred_element_type=F32)
    xn = x_ref[...] + mod_ref[2:3, :] * y
    xo_ref[...] = xn
    h2 = _rmsnorm_mod(xn, g_ref[...], mod_ref[3:4, :], mod_ref[4:5, :]).astype(BF16)
    h2_ref[...] = h2
    qp_ref[...] = jnp.dot(h2, wq_ref[...], preferred_element_type=F32)


def out_proj(x_all, y_hy, y_wa, y_da, mod, wo_bf, g2, wq_bf, *, n_rows, n_batch, seq, tm):
    d = x_all.shape[1]
    tpb = seq // tm
    grp = functools.partial(_group_of_tile, tiles_per_batch=tpb, n_batch=n_batch)
    row = lambda i: (i, 0)
    const = lambda i: (0, 0)
    mix = HY_WIDTH + WA_Q_COLS + DA_V_COLS
    return pl.pallas_call(
        _outproj_kernel,
        grid=(n_rows // tm,),
        in_specs=[
            pl.BlockSpec((tm, d), row), pl.BlockSpec((tm, HY_WIDTH), row), pl.BlockSpec((tm, WA_Q_COLS), row),
            pl.BlockSpec((tm, DA_V_COLS), row),
            pl.BlockSpec((None, N_MOD, d), lambda i: (grp(i), 0, 0)),
            pl.BlockSpec((mix, d), const), pl.BlockSpec((1, d), const), pl.BlockSpec((d, PEER_QCOLS), const),
        ],
        out_specs=[pl.BlockSpec((tm, d), row), pl.BlockSpec((tm, d), row), pl.BlockSpec((tm, PEER_QCOLS), row)],
        out_shape=[jax.ShapeDtypeStruct((n_rows, d), F32), jax.ShapeDtypeStruct((n_rows, d), BF16),
                   jax.ShapeDtypeStruct((n_rows, PEER_QCOLS), F32)],
        compiler_params=_cparams("arbitrary"),
        name="out_proj",
    )(x_all, y_hy, y_wa, y_da, mod, wo_bf, g2, wq_bf)


def _sort_pairs(n):
    pairs = []

    def merge(lo, cnt, r):
        step = r * 2
        if step < cnt:
            merge(lo, cnt, step)
            merge(lo + r, cnt, step)
            for i in range(lo + r, lo + cnt - r, step):
                pairs.append((i, i + r))
        else:
            pairs.append((lo, lo + r))

    def sort(lo, cnt):
        if cnt > 1:
            half = cnt // 2
            sort(lo, half)
            sort(lo + half, half)
            merge(lo, cnt, 1)

    sort(0, n)
    return pairs


def _sort_desc(vals):
    n = 1
    while n < len(vals):
        n *= 2
    v = list(vals) + [None] * (n - len(vals))
    for i, j in _sort_pairs(n):
        a, b = v[i], v[j]
        if b is None:
            continue
        if a is None:
            v[i], v[j] = b, None
        else:
            v[i], v[j] = jnp.maximum(a, b), jnp.minimum(a, b)
    return v[:len(vals)]


def _top16_rows(s):
    k = PEER_TOPK
    v = _sort_desc([s[SUBLANES * r:SUBLANES * (r + 1), :] for r in range(s.shape[0] // SUBLANES)])
    for shift in (4, 2, 1):
        other = [pltpu.roll(x, shift, 0) for x in v]
        v = [jnp.maximum(v[i], other[k - 1 - i]) for i in range(k)]
        d = k // 2
        while d >= 1:
            for i in range(k):
                if (i & d) == 0:
                    a, b = v[i], v[i + d]
                    v[i], v[i + d] = jnp.maximum(a, b), jnp.minimum(a, b)
            d //= 2
    return v


def _peer_topk_kernel(q_ref, keys_ref, cnt_ref, e1_ref, rank_ref, e2_ref):
    k = PEER_TOPK
    tmk = q_ref.shape[0]
    sub = lax.broadcasted_iota(jnp.int32, (SUBLANES, tmk), 0)
    scores, tops = [], []
    for hp in range(2 * PEER_HEADS):
        s = _mm3(keys_ref[hp], q_ref[:, hp * N_KEYS:(hp + 1) * N_KEYS], nt=True)
        scores.append(s)
        top = _top16_rows(s)
        top.append(jnp.max(jnp.where(s < top[k - 1][0:1, :], s, NEG_BIG), axis=0, keepdims=True))
        tops.append(top)
    packed = []
    for p in range(2):
        lst = []
        for r in range(k + 1):
            acc = jnp.broadcast_to(tops[p][r], (SUBLANES, tmk))
            for h in range(1, PEER_HEADS):
                acc = jnp.where(sub == h, tops[2 * h + p][r], acc)
            lst.append(acc)
        packed.append(lst)
    cand = [packed[0][i] + packed[1][j] for i in range(k) for j in range(k) if (i + 1) * (j + 1) <= k]
    cs = _sort_desc(cand)
    c17 = jnp.maximum(cs[k], jnp.maximum(packed[0][k] + packed[1][0], packed[0][0] + packed[1][k]))
    tau = 0.5 * (cs[k - 1] + c17)
    zsum = jnp.zeros_like(tau)
    for r in range(k):
        zsum = zsum + jnp.exp(cs[r] - cs[0])
    inv_z = 1.0 / zsum
    for h in range(PEER_HEADS):
        s1, s2 = scores[2 * h], scores[2 * h + 1]
        row = lambda x: x[h:h + 1, :]
        top2 = [t2[0:1, :] for t2 in tops[2 * h + 1][:k]]
        cnt_ref[h] = _count_sorted(top2, row(tau) - s1, strict=False)
        tiled = lambda x: x.astype(BF16).reshape(N_KEYS // BF16_ROWS, BF16_ROWS, tmk)
        rank_ref[h] = tiled(_count_sorted(top2, s2, strict=True))
        e1_ref[h] = 0.5 * jnp.exp(s1 - row(packed[0][0]))
        e2_ref[h] = tiled(jnp.exp(s2 - row(packed[1][0])) * row(inv_z))


def _count_sorted(tops, x, *, strict):
    ge = (lambda a: a > x) if strict else (lambda a: a >= x)
    b3 = ge(tops[7])
    b2 = ge(jnp.where(b3, tops[11], tops[3]))
    b1 = ge(jnp.where(b3, jnp.where(b2, tops[13], tops[9]), jnp.where(b2, tops[5], tops[1])))
    hi = jnp.where(b2, jnp.where(b1, tops[14], tops[12]), jnp.where(b1, tops[10], tops[8]))
    lo = jnp.where(b2, jnp.where(b1, tops[6], tops[4]), jnp.where(b1, tops[2], tops[0]))
    b0 = ge(jnp.where(b3, hi, lo))
    cnt = (jnp.where(b3, 8.0, 0.0) + jnp.where(b2, 4.0, 0.0)) + (jnp.where(b1, 2.0, 0.0) + jnp.where(b0, 1.0, 0.0))
    return jnp.where(ge(tops[15]), 16.0, cnt)


def peer_topk(qp, keys, *, tmk):
    t = qp.shape[0]
    tiles = N_KEYS // BF16_ROWS
    out = pl.BlockSpec((PEER_HEADS, N_KEYS, tmk), lambda i: (0, 0, i))
    out16 = pl.BlockSpec((PEER_HEADS, tiles, BF16_ROWS, tmk), lambda i: (0, 0, 0, i))
    f32 = jax.ShapeDtypeStruct((PEER_HEADS, N_KEYS, t), F32)
    b16 = jax.ShapeDtypeStruct((PEER_HEADS, tiles, BF16_ROWS, t), BF16)
    return pl.pallas_call(
        _peer_topk_kernel,
        grid=(t // tmk,),
        in_specs=[pl.BlockSpec((tmk, PEER_QCOLS), lambda i: (i, 0)),
                  pl.BlockSpec((2 * PEER_HEADS, N_KEYS, PEER_QDIM // 2), lambda i: (0, 0, 0))],
        out_specs=[out, out, out16, out16],
        out_shape=[f32, f32, b16, b16],
        compiler_params=_cparams("arbitrary"),
        name="peer_topk",
    )(qp, keys)


def _peer_dense_kernel(x_ref, h_ref, mod_ref, u_ref, vt_ref, cnt_ref, e1_ref, rank_ref, e2_ref, g_ref, o_ref,
                       a_scr, w_scr, acc_scr, *, final_norm, chunk, col_block):
    e = pl.program_id(1)

    @pl.when(e == 0)
    def _():
        acc_scr[...] = jnp.zeros(acc_scr.shape, F32)

    n_chunks = u_ref.shape[0] // chunk
    per_chunk = chunk // N_KEYS
    sqrt_half = math.sqrt(0.5)
    tm = h_ref.shape[0]
    tiles = N_KEYS // BF16_ROWS
    zero = jnp.zeros((), BF16)

    slots = a_scr.shape[0]
    ahead = slots - 1

    def pre_activations(p):
        a_scr[p % slots] = lax.dot_general(u_ref[p * chunk:(p + 1) * chunk, :], h_ref[...], (((1,), (1,)), ((), ())),
                                           preferred_element_type=F32)

    for p in range(min(ahead, n_chunks)):
        pre_activations(p)
    for p in range(n_chunks):
        if p + ahead < n_chunks:
            pre_activations(p + ahead)
        for ii, cb in [(ii, cb) for ii in range(p * per_chunk, (p + 1) * per_chunk) for cb in range(tm // col_block)]:
            rows = slice((ii - p * per_chunk) * N_KEYS, (ii - p * per_chunk + 1) * N_KEYS)
            cols = slice(cb * col_block, (cb + 1) * col_block)
            a = a_scr[p % slots, rows, cols]
            act = a * (1.0 + lax.erf(a * sqrt_half))
            act = act.astype(BF16).reshape(tiles, BF16_ROWS, col_block)
            gate = None
            for h in range(PEER_HEADS):
                cnt = jnp.broadcast_to(cnt_ref[h, ii:ii + 1, cols], (BF16_ROWS, col_block)).astype(BF16)
                e1 = jnp.broadcast_to(e1_ref[h, ii:ii + 1, cols], (BF16_ROWS, col_block)).astype(BF16)
                term = jnp.where(rank_ref[h, :, :, cols] < cnt[None], e2_ref[h, :, :, cols] * e1[None], zero)
                gate = term if gate is None else gate + term
            w_scr[p % 2, rows, cols] = (gate * act).reshape(N_KEYS, col_block)
        acc_scr[...] += jnp.dot(vt_ref[:, p * chunk:(p + 1) * chunk], w_scr[p % 2], preferred_element_type=F32)

    @pl.when(e == pl.num_programs(1) - 1)
    def _():
        xn = x_ref[...] + mod_ref[5:6, :] * acc_scr[...].T
        if final_norm:
            ms = jnp.mean(xn * xn, axis=-1, keepdims=True)
            xn = xn * lax.rsqrt(ms + NORM_EPS) * g_ref[...]
        o_ref[...] = xn


def peer_dense(x, h2, mod, u_bf, vt_bf, cnt, e1, rank, e2, g_final, *, n_batch, seq, tm, te, chunk, ahead, col_block,
               final_norm):
    n_rows, d = x.shape
    tpb = seq // tm
    grp = functools.partial(_group_of_tile, tiles_per_batch=tpb, n_batch=n_batch)
    n_i = te // N_KEYS
    tok = lambda i, e: (i, 0)
    rows_i = pl.BlockSpec((PEER_HEADS, n_i, tm), lambda i, e: (0, e, i))
    tiles = N_KEYS // BF16_ROWS
    full = pl.BlockSpec((PEER_HEADS, tiles, BF16_ROWS, tm), lambda i, e: (0, 0, 0, i))
    return pl.pallas_call(
        functools.partial(_peer_dense_kernel, final_norm=final_norm, chunk=chunk, col_block=col_block),
        grid=(n_rows // tm, N_EXPERTS // te),
        in_specs=[
            pl.BlockSpec((tm, d), tok), pl.BlockSpec((tm, d), tok),
            pl.BlockSpec((None, N_MOD, d), lambda i, e: (grp(i), 0, 0)),
            pl.BlockSpec((te, d), lambda i, e: (e, 0)),
            pl.BlockSpec((d, te), lambda i, e: (0, e)),
            rows_i, rows_i, full, full,
            pl.BlockSpec((1, d), lambda i, e: (0, 0)),
        ],
        out_specs=pl.BlockSpec((tm, d), tok),
        out_shape=jax.ShapeDtypeStruct((n_rows, d), F32),
        scratch_shapes=[pltpu.VMEM((ahead + 1, chunk, tm), F32), pltpu.VMEM((2, chunk, tm), BF16),
                        pltpu.VMEM((d, tm), F32)],
        compiler_params=_cparams("arbitrary", "arbitrary"),
        name="peer_dense",
    )(x, h2, mod, u_bf, vt_bf, cnt, e1, rank, e2, g_final)


def kernel(x, c, ctx, c_ctx, w_mod, b_mod, norm1_g, w_in, hy_conv_w, hy_conv_b, hy_w1, hy_b1, hy_freq, hy_w2,
           hy_b2, hy_w3, hy_decay, hy_bias, wa_sink, da_lq1, da_lk1, da_lq2, da_lk2, da_norm_g, w_out, norm2_g,
           peer_wq, peer_keys, peer_u, peer_v, final_g):
    n_batch, seq, d = x.shape
    n_ctx = ctx.shape[1]
    depth = w_in.shape[0]
    n_lat = n_batch * seq
    tm = 512 if seq % 512 == 0 else 256
    assert seq % tm == 0 and (n_batch * n_ctx) % tm == 0 and n_lat % n_ctx == 0 and n_batch + 1 <= SUBLANES

    cpad = jnp.zeros((SUBLANES, d), F32).at[:n_batch].set(c).at[n_batch].set(c_ctx)
    mod_all = mod_vectors(cpad, w_mod, b_mod).reshape(depth, SUBLANES, N_MOD, d)
    rope_wa = rope_tables(seq, HEAD_DIM, tm)
    rope_da = rope_tables(seq, DA_QK_DIM, tm)
    consts = dft_constants(seq)
    feat_lat = hyena_filter_features(seq)
    feat_ctx = hyena_filter_features(n_ctx)
    x_all = jnp.concatenate([x.reshape(n_lat, d), ctx.reshape(n_batch * n_ctx, d)], axis=0)

    out = None
    for li in range(depth):
        last = li == depth - 1
        lam_init = DA_LAMBDA_BASE - DA_LAMBDA_AMP * math.exp(-DA_LAMBDA_RATE * li)
        mod = mod_all[li]
        hy, qw, kw, vw, qd, kd, vd, vdt, vwt = proj_in(x_all, mod, norm1_g[li][None, :], w_in[li].astype(BF16), rope_wa,
                                             rope_da, n_batch=n_batch, seq=seq, tm=tm)
        w1p = jnp.pad(hy_w1[li], ((0, LANES - HY_EMB_DIM), (0, 0)))
        fargs = (w1p, hy_b1[li][None, :], hy_freq[li][None, :], hy_w2[li], hy_b2[li][None, :], hy_w3[li],
                 hy_decay[li].reshape(1, -1))
        conv_b = hy_conv_b[li][None, :]
        filt_t = hyena_filters(feat_lat, *fargs, transposed=True)
        y_hy = hyena_latent(hy, hy_conv_w[li], conv_b, hy_bias[li], filt_t, consts, n_batch=n_batch, seq=seq)
        y_wa = window_attention(wa_sink[li], qw, kw, vwt, n_batch=n_batch, seq=seq, n_ctx=n_ctx)
        lq = jnp.stack([da_lq1[li], da_lk1[li], da_lq2[li], da_lk2[li]], axis=0)
        g128 = jnp.tile(da_norm_g[li], LANES // DA_V_DIM)[None, :]
        tq = min(seq, 512)
        g_rows = jnp.broadcast_to(da_norm_g[li][:, None], (DA_V_DIM, tq))
        y_da = diff_attention(lq, g_rows, qd, kd, vdt, n_batch=n_batch, seq=seq, n_ctx=n_ctx, lam_init=lam_init,
                              tq=tq, tk=min(seq, 256), ahead=2, unroll=8 if seq % 2048 == 0 else 1)
        if not last:
            filt_c = hyena_filters(feat_ctx, *fargs, transposed=False)
            yc_hy = ctx_hyena(hy, hy_conv_w[li], conv_b, filt_c, hy_bias[li], n_batch=n_batch, seq=seq, n_ctx=n_ctx)
            yc_wa, yc_da = ctx_attention(wa_sink[li], lq, g128, qw, kw, vw, qd, kd, vd, n_batch=n_batch, seq=seq,
                                         n_ctx=n_ctx, lam_init=lam_init)
            y_hy = jnp.concatenate([y_hy, yc_hy], axis=0)
            y_wa = jnp.concatenate([y_wa, yc_wa], axis=0)
            y_da = jnp.concatenate([y_da, yc_da], axis=0)
        n_rows = n_lat if last else x_all.shape[0]
        x_mid, h2, qp = out_proj(x_all, y_hy, y_wa, y_da, mod, w_out[li].astype(BF16), norm2_g[li][None, :],
                                 peer_wq[li].astype(BF16), n_rows=n_rows, n_batch=n_batch, seq=seq, tm=tm)
        keys = peer_keys[li].reshape(2 * PEER_HEADS, N_KEYS, PEER_QDIM // 2)
        cnt, e1, rank, e2 = peer_topk(qp, keys, tmk=256)
        x_all = peer_dense(x_mid, h2, mod, peer_u[li].astype(BF16), peer_v[li].T.astype(BF16), cnt, e1, rank, e2,
                           final_g[None, :], n_batch=n_batch, seq=seq, tm=tm, te=2048, chunk=1024, ahead=1, col_block=tm,
                           final_norm=last)
        out = x_all
    return out.reshape(n_batch, seq, d)
```

```python
import functools
import math

import numpy as np
import jax
import jax.numpy as jnp
from jax import lax
from jax.experimental import pallas as pl
from jax.experimental.pallas import tpu as pltpu

F32 = jnp.float32
BF16 = jnp.bfloat16
HI = lax.Precision.HIGHEST

LANES = 128
SUBLANES = 8
BF16_ROWS = 16
VMEM_LIMIT_BYTES = 56 * 1024 * 1024
PEER_VMEM_LIMIT_BYTES = 62 * 1024 * 1024

D_MODEL = 1024
N_MOD = 6
NORM_EPS = 1e-6
ROPE_BASE = 10000.0
GRID_W = 64
BLOCK = 128
HEAD_DIM = 64
HY_WIDTH = 256
HY_ORDER = 2
HY_BANDS = 16
HY_EMB_DIM = 1 + 2 * HY_BANDS
HY_FILTER_WIDTH = 64
HY_COLS = 3 * HY_WIDTH
WA_Q_HEADS = 8
WA_KV_HEADS = 2
WA_GROUP = 4
WINDOW = 128
WA_Q_COLS = WA_Q_HEADS * HEAD_DIM
WA_KV_COLS = WA_KV_HEADS * HEAD_DIM
WA_COLS = WA_Q_COLS + 2 * WA_KV_COLS
DA_HEADS = 4
DA_QK_DIM = 32
DA_V_DIM = 64
DA_QK_COLS = DA_HEADS * 2 * DA_QK_DIM
DA_V_COLS = DA_HEADS * DA_V_DIM
DA_COLS = 2 * DA_QK_COLS + DA_V_COLS
DA_LAMBDA_BASE = 0.8
DA_LAMBDA_AMP = 0.6
DA_LAMBDA_RATE = 0.3
IN_COLS = HY_COLS + WA_COLS + DA_COLS
PEER_HEADS = 8
N_KEYS = 128
N_EXPERTS = N_KEYS * N_KEYS
PEER_TOPK = 16
PEER_QDIM = 256
PEER_QCOLS = PEER_HEADS * PEER_QDIM
NEG_BIG = -1e30
LOG2E = math.log2(math.e)


def _cparams(*sem, vmem_limit_bytes=VMEM_LIMIT_BYTES):
    return pltpu.CompilerParams(dimension_semantics=sem, vmem_limit_bytes=vmem_limit_bytes)


def _rmsnorm_mod(x, g, shift, scale):
    ms = jnp.mean(x * x, axis=-1, keepdims=True)
    return (x * lax.rsqrt(ms + NORM_EPS) * g) * (1.0 + scale) + shift


def _mod_kernel(c_ref, w_ref, b_ref, o_ref):
    c = c_ref[...]
    act = c * jax.nn.sigmoid(c)
    o_ref[...] = jnp.dot(act, w_ref[...], precision=HI, preferred_element_type=F32) + b_ref[...]


def mod_vectors(cpad, w_mod, b_mod):
    n_layers, d, n = w_mod.shape
    tn = 1536
    return pl.pallas_call(
        _mod_kernel,
        grid=(n_layers, n // tn),
        in_specs=[
            pl.BlockSpec((SUBLANES, d), lambda l, j: (0, 0)),
            pl.BlockSpec((None, d, tn), lambda l, j: (l, 0, j)),
            pl.BlockSpec((None, 1, tn), lambda l, j: (l, 0, j)),
        ],
        out_specs=pl.BlockSpec((None, SUBLANES, tn), lambda l, j: (l, 0, j)),
        out_shape=jax.ShapeDtypeStruct((n_layers, SUBLANES, n), F32),
        compiler_params=_cparams("arbitrary", "arbitrary"),
        name="mod_vectors",
    )(cpad, w_mod, b_mod.reshape(n_layers, 1, n))


def _rope128(x, cos, sin_a, sin_b, m):
    return x * cos + pltpu.roll(x, LANES - m, 1) * sin_a + pltpu.roll(x, m, 1) * sin_b


def _proj_in_kernel(x_ref, mod_ref, g_ref, w_ref, cw_ref, saw_ref, sbw_ref, cd_ref, sad_ref, sbd_ref,
                    hy_ref, qw_ref, kw_ref, vw_ref, qd_ref, kd_ref, vd_ref, vdt_ref, vwt_ref):
    h = _rmsnorm_mod(x_ref[...], g_ref[...], mod_ref[0:1, :], mod_ref[1:2, :]).astype(BF16)
    hy_ref[...] = jnp.dot(h, w_ref[:, 0:HY_COLS], preferred_element_type=F32)
    wa = jnp.dot(h, w_ref[:, HY_COLS:HY_COLS + WA_COLS], preferred_element_type=F32)
    da = jnp.dot(h, w_ref[:, HY_COLS + WA_COLS:IN_COLS], preferred_element_type=F32)
    cw, saw, sbw = cw_ref[...], saw_ref[...], sbw_ref[...]
    cd, sad, sbd = cd_ref[...], sad_ref[...], sbd_ref[...]
    wa_scale = HEAD_DIM ** -0.5
    da_scale = DA_QK_DIM ** -0.5 * LOG2E
    for kb in range(WA_Q_COLS // LANES):
        blk = wa[:, kb * LANES:(kb + 1) * LANES]
        qw_ref[:, kb * LANES:(kb + 1) * LANES] = (_rope128(blk, cw, saw, sbw, 16) * wa_scale).astype(BF16)
    kw_ref[...] = _rope128(wa[:, WA_Q_COLS:WA_Q_COLS + WA_KV_COLS], cw, saw, sbw, 16).astype(BF16)
    vw = wa[:, WA_Q_COLS + WA_KV_COLS:WA_COLS]
    vw_ref[...] = vw.astype(BF16)
    vwt_ref[...] = vw.T.astype(BF16)
    for kb in range(DA_QK_COLS // LANES):
        blk = da[:, kb * LANES:(kb + 1) * LANES]
        qd_ref[:, kb * LANES:(kb + 1) * LANES] = (_rope128(blk, cd, sad, sbd, 8) * da_scale).astype(BF16)
        blk = da[:, DA_QK_COLS + kb * LANES:DA_QK_COLS + (kb + 1) * LANES]
        kd_ref[:, kb * LANES:(kb + 1) * LANES] = _rope128(blk, cd, sad, sbd, 8).astype(BF16)
    vd = da[:, 2 * DA_QK_COLS:DA_COLS]
    vd_ref[...] = vd.astype(BF16)
    vdt_ref[...] = vd.T.astype(BF16)


def _group_of_tile(i, tiles_per_batch, n_batch):
    return jnp.minimum(i // tiles_per_batch, n_batch)


def proj_in(x_all, mod, g, w_bf, rope_wa, rope_da, *, n_batch, seq, tm):
    t_all, d = x_all.shape
    tpb = seq // tm
    grp = functools.partial(_group_of_tile, tiles_per_batch=tpb, n_batch=n_batch)
    pos = lambda i: (jnp.where(i < n_batch * tpb, i % tpb, tpb), 0)
    row = lambda i: (i, 0)
    const = lambda i: (0, 0)
    tab = pl.BlockSpec((tm, LANES), pos)
    outs = [(HY_COLS, F32), (WA_Q_COLS, BF16), (WA_KV_COLS, BF16), (WA_KV_COLS, BF16),
            (DA_QK_COLS, BF16), (DA_QK_COLS, BF16), (DA_V_COLS, BF16)]
    return pl.pallas_call(
        _proj_in_kernel,
        grid=(t_all // tm,),
        in_specs=[
            pl.BlockSpec((tm, d), row),
            pl.BlockSpec((None, N_MOD, d), lambda i: (grp(i), 0, 0)),
            pl.BlockSpec((1, d), const),
            pl.BlockSpec((d, IN_COLS), const),
            tab, tab, tab, tab, tab, tab,
        ],
        out_specs=[pl.BlockSpec((tm, n), row) for n, _ in outs]
        + [pl.BlockSpec((DA_V_COLS, tm), lambda i: (0, i)), pl.BlockSpec((WA_KV_COLS, tm), lambda i: (0, i))],
        out_shape=[jax.ShapeDtypeStruct((t_all, n), dt) for n, dt in outs]
        + [jax.ShapeDtypeStruct((DA_V_COLS, t_all), BF16), jax.ShapeDtypeStruct((WA_KV_COLS, t_all), BF16)],
        compiler_params=_cparams("arbitrary"),
        name="proj_in",
    )(x_all, mod, g, w_bf, *rope_wa, *rope_da)


def rope_tables(seq, head_dim, pad_rows):
    half = head_dim // 2
    axis_dim = half
    m = axis_dim // 2
    f32 = np.float32
    t = np.arange(seq)
    rowcol = np.stack([t // GRID_W, t % GRID_W], axis=0).astype(f32)
    inv_freq = f32(ROPE_BASE) ** (-np.arange(0, axis_dim, 2, dtype=f32) / f32(axis_dim))
    ang = rowcol[:, :, None] * inv_freq[None, None, :]
    cos, sin = np.cos(ang), np.sin(ang)
    lane = np.arange(LANES)
    dd = lane % head_dim
    axis = dd // half
    sub = dd % half
    idx = sub % m
    first = sub < m
    cos_t = cos[axis, :, idx].T
    sin_t = sin[axis, :, idx].T
    sin_a = np.where(first[None, :], -sin_t, f32(0))
    sin_b = np.where(first[None, :], f32(0), sin_t)
    pad1 = np.ones((pad_rows, LANES), f32)
    pad0 = np.zeros((pad_rows, LANES), f32)
    tables = (np.concatenate([cos_t, pad1], 0), np.concatenate([sin_a, pad0], 0), np.concatenate([sin_b, pad0], 0))
    return tuple(jnp.asarray(x, F32) for x in tables)


def _softmax_sink_attend(q, k, v, sink, valid):
    s = lax.dot_general(q, k, (((1,), (1,)), ((), ())), preferred_element_type=F32)
    if valid is not None:
        nm = valid.shape[1]
        s = jnp.concatenate([jnp.where(valid, s[:, :nm], NEG_BIG), s[:, nm:]], axis=1)
    m = jnp.maximum(jnp.max(s, axis=1, keepdims=True), sink)
    p = jnp.exp(s - m)
    den = jnp.sum(p, axis=1, keepdims=True) + jnp.exp(sink - m)
    return jnp.dot(p.astype(BF16), v, preferred_element_type=F32) / den


def _wattn_kernel(sink_ref, q_ref, kp_ref, kc_ref, kn_ref, kx_ref, vp_ref, vc_ref, vn_ref, vx_ref, o_ref, qm_scr,
                  *, n_pairs, ahead):
    m_idx = pl.program_id(1)
    nq = 2 * BLOCK
    n_loc = 4 * BLOCK
    kk = lax.broadcasted_iota(jnp.int32, (n_loc, nq), 0)
    qq = lax.broadcasted_iota(jnp.int32, (n_loc, nq), 1)
    lo = jnp.where(m_idx > 0, 0, BLOCK)
    hi = jnp.where(m_idx < n_pairs - 1, n_loc, n_loc - BLOCK)
    valid = (kk >= jnp.maximum(qq, lo)) & (kk <= qq + 2 * WINDOW) & (kk < hi)
    k_all = jnp.concatenate([kp_ref[...], kc_ref[...], kn_ref[...], kx_ref[...]], axis=0)
    vt_all = jnp.concatenate([vp_ref[...], vc_ref[...], vn_ref[...], vx_ref[...]], axis=1)
    lane = lax.broadcasted_iota(jnp.int32, (nq, LANES), 1)
    for h in range(WA_Q_HEADS):
        hk = h // WA_GROUP
        blk = q_ref[:, (h // 2) * LANES:(h // 2 + 1) * LANES]
        if h % 2 != hk:
            blk = jnp.concatenate([blk[:, HEAD_DIM:], blk[:, :HEAD_DIM]], axis=1)
        keep = (lane >= hk * HEAD_DIM) & (lane < (hk + 1) * HEAD_DIM)
        qm_scr[h] = jnp.where(keep, blk, jnp.zeros_like(blk))

    def scores(h):
        return lax.dot_general(k_all, qm_scr[h], (((1,), (1,)), ((), ())), preferred_element_type=F32)

    sts = [scores(h) for h in range(ahead)]
    outs = []
    for h in range(WA_Q_HEADS):
        hk = h // WA_GROUP
        if h + ahead < WA_Q_HEADS:
            sts.append(scores(h + ahead))
        st = sts[h]
        sts[h] = None
        st = jnp.concatenate([jnp.where(valid, st[:n_loc], NEG_BIG), st[n_loc:]], axis=0)
        sink = sink_ref[h]
        m = jnp.maximum(jnp.max(st, axis=0, keepdims=True), sink)
        p = jnp.exp(st - m)
        den = jnp.sum(p, axis=0, keepdims=True) + jnp.exp(sink - m)
        o = jnp.dot(vt_all[hk * HEAD_DIM:(hk + 1) * HEAD_DIM, :], p.astype(BF16), preferred_element_type=F32)
        outs.append(o / den)
    o_ref[...] = jnp.concatenate(outs, axis=0).T.astype(BF16)


def window_attention(sink, qw, kw, vwt, *, n_batch, seq, n_ctx):
    nb = seq // BLOCK
    n_pairs = nb // 2
    assert nb % 2 == 0
    ctx0 = (n_batch * seq) // n_ctx
    cur = lambda b, m: (b * n_pairs + m, 0)
    prev = lambda b, m: (b * nb + jnp.maximum(2 * m - 1, 0), 0)
    nxt = lambda b, m: (b * nb + jnp.minimum(2 * m + 2, nb - 1), 0)
    ctx = lambda b, m: (ctx0 + b, 0)
    swap = lambda f: (lambda b, m: f(b, m)[::-1])
    return pl.pallas_call(
        functools.partial(_wattn_kernel, n_pairs=n_pairs, ahead=3),
        grid=(n_batch, n_pairs),
        in_specs=[
            pl.BlockSpec(memory_space=pltpu.SMEM),
            pl.BlockSpec((2 * BLOCK, WA_Q_COLS), cur),
            pl.BlockSpec((BLOCK, WA_KV_COLS), prev), pl.BlockSpec((2 * BLOCK, WA_KV_COLS), cur),
            pl.BlockSpec((BLOCK, WA_KV_COLS), nxt), pl.BlockSpec((n_ctx, WA_KV_COLS), ctx),
            pl.BlockSpec((WA_KV_COLS, BLOCK), swap(prev)), pl.BlockSpec((WA_KV_COLS, 2 * BLOCK), swap(cur)),
            pl.BlockSpec((WA_KV_COLS, BLOCK), swap(nxt)), pl.BlockSpec((WA_KV_COLS, n_ctx), swap(ctx)),
        ],
        out_specs=pl.BlockSpec((2 * BLOCK, WA_Q_COLS), cur),
        out_shape=jax.ShapeDtypeStruct((n_batch * seq, WA_Q_COLS), BF16),
        scratch_shapes=[pltpu.VMEM((WA_Q_HEADS, 2 * BLOCK, LANES), BF16)],
        compiler_params=_cparams("arbitrary", "arbitrary"),
        name="window_attention",
    )(sink, qw, kw, kw, kw, kw, vwt, vwt, vwt, vwt)


def _diff_lambda(lq_ref, lam_init):
    a = jnp.sum(lq_ref[0:1, :] * lq_ref[1:2, :], axis=1, keepdims=True)
    b = jnp.sum(lq_ref[2:3, :] * lq_ref[3:4, :], axis=1, keepdims=True)
    return jnp.exp(a) - jnp.exp(b) + lam_init


def _diff_finish(acc1, l1, acc2, l2, lam, g128, lam_init):
    lane = lax.broadcasted_iota(jnp.int32, acc1[0].shape, 1)
    lo = lane < DA_V_DIM
    blocks = []
    for vb in range(DA_HEADS // 2):
        o = []
        for h in (2 * vb, 2 * vb + 1):
            o.append(acc1[h] / l1[h] - lam * (acc2[h] / l2[h]))
        blk = jnp.where(lo, o[0], o[1])
        sq = blk * blk
        s_lo = jnp.sum(jnp.where(lo, sq, 0.0), axis=1, keepdims=True)
        s_hi = jnp.sum(jnp.where(lo, 0.0, sq), axis=1, keepdims=True)
        ms = jnp.where(lo, s_lo, s_hi) * (1.0 / DA_V_DIM)
        blocks.append(blk * lax.rsqrt(ms + NORM_EPS) * g128 * (1.0 - lam_init))
    return jnp.concatenate(blocks, axis=1)


def _masked_q(q_ref, hm):
    kb, sub = divmod(hm, LANES // DA_QK_DIM)
    blk = q_ref[:, kb * LANES:(kb + 1) * LANES]
    lane = lax.broadcasted_iota(jnp.int32, blk.shape, 1)
    keep = (lane >= sub * DA_QK_DIM) & (lane < (sub + 1) * DA_QK_DIM)
    return jnp.where(keep, blk, jnp.zeros_like(blk))


def _dattn_kernel(lq_ref, g_ref, q_ref, k_ref, vt_ref, kx_ref, vxt_ref, o_ref, qm_scr, m_scr, l_scr, acc_scr,
                  *, lam_init, tk, ahead, unroll):
    n_hm = 2 * DA_HEADS
    per_blk = LANES // DA_QK_DIM
    for hm in range(n_hm):
        qm_scr[hm] = _masked_q(q_ref, hm)
    m_scr[...] = jnp.full(m_scr.shape, NEG_BIG, F32)
    l_scr[...] = jnp.zeros(l_scr.shape, F32)
    acc_scr[...] = jnp.zeros(acc_scr.shape, F32)

    def update(chunks):
        chains = [(c, hm) for c in range(len(chunks)) for hm in range(n_hm)]

        def scores(chain):
            c, hm = chain
            kb = hm // per_blk
            return lax.dot_general(chunks[c][0][:, kb * LANES:(kb + 1) * LANES], qm_scr[hm],
                                   (((1,), (1,)), ((), ())), preferred_element_type=F32)

        sts = [scores(ch) for ch in chains[:ahead]]
        for n, (c, hm) in enumerate(chains):
            h = hm // 2
            vtc = chunks[c][1]
            if n + ahead < len(chains):
                sts.append(scores(chains[n + ahead]))
            st = sts[n]
            sts[n] = None
            m_old = m_scr[hm]
            m_new = jnp.maximum(m_old, jnp.max(st, axis=0, keepdims=True))
            alpha = jnp.exp2(m_old - m_new)
            pt = jnp.exp2(st - m_new)
            l_scr[hm] = alpha * l_scr[hm] + jnp.sum(pt, axis=0, keepdims=True)
            acc_scr[hm] = alpha * acc_scr[hm] + jnp.dot(vtc[h * DA_V_DIM:(h + 1) * DA_V_DIM, :], pt.astype(BF16),
                                                        preferred_element_type=F32)
            m_scr[hm] = m_new

    def body(j, carry):
        chunks = []
        for c in range(unroll):
            start = pl.multiple_of((j * unroll + c) * tk, tk)
            chunks.append((k_ref[pl.ds(start, tk), :], vt_ref[:, pl.ds(start, tk)]))
        update(chunks)
        return carry

    lax.fori_loop(0, k_ref.shape[0] // (tk * unroll), body, 0)
    n_x = kx_ref.shape[0] // tk
    update([(kx_ref[c * tk:(c + 1) * tk, :], vxt_ref[:, c * tk:(c + 1) * tk]) for c in range(n_x)])
    lam = _diff_lambda(lq_ref, lam_init)
    heads = []
    for h in range(DA_HEADS):
        o = acc_scr[2 * h] / l_scr[2 * h] - lam * (acc_scr[2 * h + 1] / l_scr[2 * h + 1])
        ms = jnp.mean(o * o, axis=0, keepdims=True)
        heads.append(o * lax.rsqrt(ms + NORM_EPS) * g_ref[...] * (1.0 - lam_init))
    o_ref[...] = jnp.concatenate(heads, axis=0).T.astype(BF16)


def diff_attention(lq, g_rows, qd, kd, vdt, *, n_batch, seq, n_ctx, lam_init, tq, tk, ahead, unroll):
    nq = seq // tq
    ctx0 = (n_batch * seq) // n_ctx
    n_hm = 2 * DA_HEADS
    return pl.pallas_call(
        functools.partial(_dattn_kernel, lam_init=lam_init, tk=tk, ahead=ahead, unroll=unroll),
        grid=(n_batch, nq),
        in_specs=[
            pl.BlockSpec((4, DA_QK_DIM), lambda b, i: (0, 0)),
            pl.BlockSpec((DA_V_DIM, tq), lambda b, i: (0, 0)),
            pl.BlockSpec((tq, DA_QK_COLS), lambda b, i: (b * nq + i, 0)),
            pl.BlockSpec((seq, DA_QK_COLS), lambda b, i: (b, 0)),
            pl.BlockSpec((DA_V_COLS, seq), lambda b, i: (0, b)),
            pl.BlockSpec((n_ctx, DA_QK_COLS), lambda b, i: (ctx0 + b, 0)),
            pl.BlockSpec((DA_V_COLS, n_ctx), lambda b, i: (0, ctx0 + b)),
        ],
        out_specs=pl.BlockSpec((tq, DA_V_COLS), lambda b, i: (b * nq + i, 0)),
        out_shape=jax.ShapeDtypeStruct((n_batch * seq, DA_V_COLS), BF16),
        scratch_shapes=[
            pltpu.VMEM((n_hm, tq, LANES), BF16),
            pltpu.VMEM((n_hm, 1, tq), F32),
            pltpu.VMEM((n_hm, 1, tq), F32),
            pltpu.VMEM((n_hm, DA_V_DIM, tq), F32),
        ],
        compiler_params=_cparams("arbitrary", "arbitrary"),
        name="diff_attention",
    )(lq, g_rows, qd, kd, vdt, kd, vdt)


def _ctx_attn_kernel(sink_ref, lq_ref, g_ref, qw_ref, kw_ref, vw_ref, qd_ref, kd_ref, vd_ref, ow_ref, od_ref,
                     *, lam_init):
    outs = []
    for hk in range(WA_KV_HEADS):
        sl = slice(hk * HEAD_DIM, (hk + 1) * HEAD_DIM)
        k, v = kw_ref[:, sl], vw_ref[:, sl]
        for g in range(WA_GROUP):
            h = hk * WA_GROUP + g
            outs.append(_softmax_sink_attend(qw_ref[:, h * HEAD_DIM:(h + 1) * HEAD_DIM], k, v, sink_ref[h], None))
    ow_ref[...] = jnp.concatenate(outs, axis=1).astype(BF16)
    per_blk = LANES // DA_QK_DIM
    acc, den = [], []
    for hm in range(2 * DA_HEADS):
        kb, vb = hm // per_blk, hm // 4
        s = lax.dot_general(_masked_q(qd_ref, hm), kd_ref[:, kb * LANES:(kb + 1) * LANES],
                            (((1,), (1,)), ((), ())), preferred_element_type=F32)
        p = jnp.exp2(s - jnp.max(s, axis=1, keepdims=True))
        den.append(jnp.sum(p, axis=1, keepdims=True))
        acc.append(jnp.dot(p.astype(BF16), vd_ref[:, vb * LANES:(vb + 1) * LANES], preferred_element_type=F32))
    lam = _diff_lambda(lq_ref, lam_init)
    od_ref[...] = _diff_finish(acc[0::2], den[0::2], acc[1::2], den[1::2], lam, g_ref[...], lam_init).astype(BF16)


def ctx_attention(sink, lq, g128, qw, kw, vw, qd, kd, vd, *, n_batch, seq, n_ctx, lam_init):
    ctx0 = (n_batch * seq) // n_ctx
    ctx = lambda b: (ctx0 + b, 0)
    out = lambda b: (b, 0)
    spec = lambda n, f: pl.BlockSpec((n_ctx, n), f)
    return pl.pallas_call(
        functools.partial(_ctx_attn_kernel, lam_init=lam_init),
        grid=(n_batch,),
        in_specs=[
            pl.BlockSpec(memory_space=pltpu.SMEM),
            pl.BlockSpec((4, DA_QK_DIM), lambda b: (0, 0)),
            pl.BlockSpec((1, LANES), lambda b: (0, 0)),
            spec(WA_Q_COLS, ctx), spec(WA_KV_COLS, ctx), spec(WA_KV_COLS, ctx),
            spec(DA_QK_COLS, ctx), spec(DA_QK_COLS, ctx), spec(DA_V_COLS, ctx),
        ],
        out_specs=[spec(WA_Q_COLS, out), spec(DA_V_COLS, out)],
        out_shape=[jax.ShapeDtypeStruct((n_batch * n_ctx, WA_Q_COLS), BF16),
                   jax.ShapeDtypeStruct((n_batch * n_ctx, DA_V_COLS), BF16)],
        compiler_params=_cparams("arbitrary"),
        name="ctx_attention",
    )(sink, lq, g128, qw, kw, vw, qd, kd, vd)


def _filter_kernel(feat_ref, w1_ref, b1_ref, fr_ref, w2_ref, b2_ref, w3_ref, dec_ref, o_ref, *, transposed):
    feat = feat_ref[...]
    fr = fr_ref[...]
    z = jnp.sin(fr * (jnp.dot(feat, w1_ref[...], precision=HI, preferred_element_type=F32) + b1_ref[...]))
    z = jnp.sin(fr * (jnp.dot(z, w2_ref[...], precision=HI, preferred_element_type=F32) + b2_ref[...]))
    h = jnp.dot(z, w3_ref[...], precision=HI, preferred_element_type=F32)
    h = h * jnp.exp(-feat[:, 0:1] * jnp.abs(dec_ref[...]))
    rows = lax.broadcasted_iota(jnp.int32, h.shape, 0) + pl.program_id(0) * h.shape[0]
    cols = lax.broadcasted_iota(jnp.int32, h.shape, 1)
    h = jnp.where((rows == 0) & (cols % (2 * HY_WIDTH) >= HY_WIDTH), 0.0, h)
    if transposed:
        for r in range(h.shape[0] // LANES):
            o_ref[r] = h[r * LANES:(r + 1) * LANES, :].T
    else:
        o_ref[...] = h


def hyena_filter_features(n):
    f32 = np.float32
    t = np.linspace(0.0, 1.0, n, dtype=f32)[:, None]
    bands = np.linspace(1e-4, HY_BANDS - 1, HY_BANDS, dtype=f32)
    phase = (f32(2.0 * math.pi / n) * np.arange(n, dtype=f32)[:, None]) * bands
    feat = np.concatenate([t, np.cos(phase), -np.sin(phase)], axis=-1).astype(f32)
    return jnp.asarray(np.pad(feat, ((0, 0), (0, LANES - HY_EMB_DIM))), F32)


def hyena_filters(feat, w1p, b1, freq, w2, b2, w3, decay, *, transposed):
    n = feat.shape[0]
    tr = min(n, 512)
    ncol = w3.shape[1]
    fw = w2.shape[0]
    const = lambda i: (0, 0)
    if transposed:
        out_spec = pl.BlockSpec((tr // LANES, ncol, LANES), lambda i: (i, 0, 0))
        out_shape = jax.ShapeDtypeStruct((n // LANES, ncol, LANES), F32)
    else:
        out_spec = pl.BlockSpec((tr, ncol), lambda i: (i, 0))
        out_shape = jax.ShapeDtypeStruct((n, ncol), F32)
    return pl.pallas_call(
        functools.partial(_filter_kernel, transposed=transposed),
        grid=(n // tr,),
        in_specs=[
            pl.BlockSpec((tr, LANES), lambda i: (i, 0)),
            pl.BlockSpec((LANES, fw), const), pl.BlockSpec((1, fw), const), pl.BlockSpec((1, fw), const),
            pl.BlockSpec((fw, fw), const), pl.BlockSpec((1, fw), const),
            pl.BlockSpec((fw, ncol), const), pl.BlockSpec((1, ncol), const),
        ],
        out_specs=out_spec,
        out_shape=out_shape,
        compiler_params=_cparams("arbitrary"),
        name="hyena_filters",
    )(feat, w1p, b1, freq, w2, b2, w3, decay)


def dft_constants(seq):
    n = 2 * seq
    n1 = n // LANES
    a_rows = n1 // 2
    k1 = np.arange(n1, dtype=np.float64)[:, None]
    a = np.arange(a_rows, dtype=np.float64)[None, :]
    th = 2.0 * np.pi * k1 * a / n1
    c1, s1 = np.cos(th), np.sin(th)
    m1_data = np.block([[c1, s1], [-s1, c1]])
    half_rows = -(-(n1 // 2 + 1) // SUBLANES) * SUBLANES
    pad = np.zeros((half_rows - (n1 // 2 + 1), a_rows))
    m1_real = np.concatenate([c1[:n1 // 2 + 1], pad, -s1[:n1 // 2 + 1], pad], axis=0)
    m3 = np.block([[c1.T, -s1.T], [s1.T, c1.T]]) / n
    b = np.arange(LANES, dtype=np.float64)[None, :]
    psi = 2.0 * np.pi * k1 * b / n
    twc, tws = np.cos(psi), np.sin(psi)
    bb = np.arange(LANES, dtype=np.float64)
    phi = 2.0 * np.pi * np.outer(bb, bb) / LANES
    c2, s2 = np.cos(phi), np.sin(phi)
    w2f = np.block([[c2, -s2], [s2, c2]])
    w2i = np.block([[c2, s2], [-s2, c2]])
    f = lambda x: jnp.asarray(x, dtype=F32)
    return dict(m1_data=f(m1_data), m1_real=f(m1_real), m3=f(m3), twc=f(twc[:, None, :]), tws=f(tws[:, None, :]),
                w2f=f(w2f), w2i=f(w2i), n1=n1, a_rows=a_rows, half_rows=half_rows)


def _split_bf16(x):
    hi = x.astype(BF16)
    return hi, (x - hi.astype(F32)).astype(BF16)


def _mm3(a, b, nt=False):
    ah, al = _split_bf16(a)
    bh, bl = _split_bf16(b)
    dims = (((1,), (1 if nt else 0,)), ((), ()))
    dot = lambda x, y: lax.dot_general(x, y, dims, preferred_element_type=F32)
    return dot(ah, bh) + (dot(ah, bl) + dot(al, bh))


def _leftmm_kernel(m_ref, x_ref, o_ref):
    o_ref[...] = _mm3(m_ref[...], x_ref[...])


def left_matmul(m, x, ct):
    r, k = m.shape
    cols = x.shape[1]
    return pl.pallas_call(
        _leftmm_kernel,
        grid=(cols // ct,),
        in_specs=[pl.BlockSpec((r, k), lambda j: (0, 0)), pl.BlockSpec((k, ct), lambda j: (0, j))],
        out_specs=pl.BlockSpec((r, ct), lambda j: (0, j)),
        out_shape=jax.ShapeDtypeStruct((r, cols), F32),
        compiler_params=_cparams("arbitrary"),
        name="dft_rows",
    )(m, x)


def _leftmm_gate_kernel(m_ref, zr_ref, zi_ref, x_ref, zold_ref, bias_ref, o_ref):
    n1 = zr_ref.shape[0]
    y = _mm3(m_ref[:, :n1], zr_ref[...]) + _mm3(m_ref[:, n1:], zi_ref[...])
    zold = zold_ref[...]
    o_ref[...] = x_ref[...] * (y + bias_ref[...] * zold)


def left_matmul_gate(m, zr, zi, xg, zold, bias_cols, ct):
    r, k = m.shape
    n1, cols = zr.shape
    col = lambda j: (0, j)
    return pl.pallas_call(
        _leftmm_gate_kernel,
        grid=(cols // ct,),
        in_specs=[pl.BlockSpec((r, k), lambda j: (0, 0)), pl.BlockSpec((n1, ct), col), pl.BlockSpec((n1, ct), col),
                  pl.BlockSpec((r, ct), col), pl.BlockSpec((r, ct), col), pl.BlockSpec((1, ct), col)],
        out_specs=pl.BlockSpec((r, ct), col),
        out_shape=jax.ShapeDtypeStruct((r, cols), F32),
        compiler_params=_cparams("arbitrary"),
        name="idft_rows_gate",
    )(m, zr, zi, xg, zold, bias_cols)


def _twiddle_fwd(yr, yi, c, s):
    return yr * c + yi * s, yi * c - yr * s


def _spectrum_kernel(yr_ref, yi_ref, twc_ref, tws_ref, w2f_ref, hr_ref, hi_ref, *, kb, n1):
    for t in range(kb):
        mirrored = pl.program_id(0) * kb + t > n1 // 2
        yi = yi_ref[t]
        ypr, ypi = _twiddle_fwd(yr_ref[t], jnp.where(mirrored, -yi, yi), twc_ref[t], tws_ref[t])
        x = _mm3(jnp.concatenate([ypr, ypi], axis=1), w2f_ref[...])
        xr, xi = x[:, :LANES], x[:, LANES:]
        for o in range(HY_ORDER):
            f0 = slice((2 * o) * HY_WIDTH, (2 * o + 1) * HY_WIDTH)
            f1 = slice((2 * o + 1) * HY_WIDTH, (2 * o + 2) * HY_WIDTH)
            hr_ref[o, t] = xr[f0] + xr[f1]
            hi_ref[o, t] = xi[f0] - xi[f1]


def filter_spectrum(y, consts, kb):
    assert kb == 1
    n1, half_rows = consts["n1"], consts["half_rows"]
    nc = 2 * HY_ORDER * HY_WIDTH
    y4 = y.reshape(2, half_rows, nc, LANES)
    tw = pl.BlockSpec((kb, 1, LANES), lambda i: (i, 0, 0))
    out = pl.BlockSpec((HY_ORDER, kb, HY_WIDTH, LANES), lambda i: (0, i, 0, 0))
    shp = jax.ShapeDtypeStruct((HY_ORDER, n1, HY_WIDTH, LANES), F32)
    src = lambda i: jnp.where(i > n1 // 2, n1 - i, i)
    return pl.pallas_call(
        functools.partial(_spectrum_kernel, kb=kb, n1=n1),
        grid=(n1 // kb,),
        in_specs=[
            pl.BlockSpec((None, kb, nc, LANES), lambda i: (0, src(i), 0, 0)),
            pl.BlockSpec((None, kb, nc, LANES), lambda i: (1, src(i), 0, 0)),
            tw, tw,
            pl.BlockSpec((2 * LANES, 2 * LANES), lambda i: (0, 0)),
        ],
        out_specs=[out, out],
        out_shape=[shp, shp],
        compiler_params=_cparams("arbitrary"),
        name="filter_spectrum",
    )(y4, y4, consts["twc"], consts["tws"], consts["w2f"])


def _freq_kernel(yr_ref, yi_ref, twc_ref, tws_ref, w2f_ref, w2i_ref, hr_ref, hi_ref, zr_ref, zi_ref, *, kb):
    for t in range(kb):
        c, s = twc_ref[t], tws_ref[t]
        ypr, ypi = _twiddle_fwd(yr_ref[t], yi_ref[t], c, s)
        x = _mm3(jnp.concatenate([ypr, ypi], axis=1), w2f_ref[...])
        xr, xi = x[:, :LANES], x[:, LANES:]
        hr, hi = hr_ref[t], hi_ref[t]
        gr = xr * hr - xi * hi
        gi = xr * hi + xi * hr
        z = _mm3(jnp.concatenate([gr, gi], axis=1), w2i_ref[...])
        zr, zi = z[:, :LANES], z[:, LANES:]
        zr_ref[t] = zr * c - zi * s
        zi_ref[t] = zi * c + zr * s


def freq_multiply(y, hr, hi, order, consts, kb):
    n1 = consts["n1"]
    y4 = y.reshape(2, n1, HY_WIDTH, LANES)
    tw = pl.BlockSpec((kb, 1, LANES), lambda i: (i, 0, 0))
    blk = lambda p: pl.BlockSpec((None, kb, HY_WIDTH, LANES), lambda i: (p, i, 0, 0))
    wspec = pl.BlockSpec((2 * LANES, 2 * LANES), lambda i: (0, 0))
    out = pl.BlockSpec((kb, HY_WIDTH, LANES), lambda i: (i, 0, 0))
    shp = jax.ShapeDtypeStruct((n1, HY_WIDTH, LANES), F32)
    return pl.pallas_call(
        functools.partial(_freq_kernel, kb=kb),
        grid=(n1 // kb,),
        in_specs=[blk(0), blk(1), tw, tw, wspec, wspec, blk(order), blk(order)],
        out_specs=[out, out],
        out_shape=[shp, shp],
        compiler_params=_cparams("arbitrary"),
        name="freq_multiply",
    )(y4, y4, consts["twc"], consts["tws"], consts["w2f"], consts["w2i"], hr, hi)


def _short_conv_rows(u, prev_row, next_row, w_ref, b_ref):
    rows = lax.broadcasted_iota(jnp.int32, u.shape, 0)
    um1 = jnp.where(rows == 0, prev_row, pltpu.roll(u, 1, 0))
    up1 = jnp.where(rows == u.shape[0] - 1, next_row, pltpu.roll(u, u.shape[0] - 1, 0))
    return b_ref[...] + um1 * w_ref[0:1, :] + u * w_ref[1:2, :] + up1 * w_ref[2:3, :]


def _shortconv_t_kernel(u_ref, up_ref, un_ref, w_ref, b_ref, o_ref, *, n_tiles):
    i = pl.program_id(1)
    prev_row = jnp.where(i > 0, up_ref[SUBLANES - 1:SUBLANES, :], 0.0)
    next_row = jnp.where(i < n_tiles - 1, un_ref[0:1, :], 0.0)
    v = _short_conv_rows(u_ref[...], prev_row, next_row, w_ref, b_ref)
    for part in range(HY_ORDER + 1):
        for r in range(v.shape[0] // LANES):
            o_ref[part, r] = v[r * LANES:(r + 1) * LANES, part * HY_WIDTH:(part + 1) * HY_WIDTH].T


def short_conv_transposed(hy, conv_w, conv_b, *, n_batch, seq, tm):
    nt = seq // tm
    per8 = tm // SUBLANES
    a_rows = seq // LANES
    prev = lambda b, i: (jnp.maximum((b * nt + i) * per8 - 1, 0), 0)
    nxt = lambda b, i: (jnp.minimum((b * nt + i + 1) * per8, n_batch * nt * per8 - 1), 0)
    return pl.pallas_call(
        functools.partial(_shortconv_t_kernel, n_tiles=nt),
        grid=(n_batch, nt),
        in_specs=[
            pl.BlockSpec((tm, HY_COLS), lambda b, i: (b * nt + i, 0)),
            pl.BlockSpec((SUBLANES, HY_COLS), prev),
            pl.BlockSpec((SUBLANES, HY_COLS), nxt),
            pl.BlockSpec((3, HY_COLS), lambda b, i: (0, 0)),
            pl.BlockSpec((1, HY_COLS), lambda b, i: (0, 0)),
        ],
        out_specs=pl.BlockSpec((HY_ORDER + 1, None, tm // LANES, HY_WIDTH, LANES), lambda b, i: (0, b, i, 0, 0)),
        out_shape=jax.ShapeDtypeStruct((HY_ORDER + 1, n_batch, a_rows, HY_WIDTH, LANES), F32),
        compiler_params=_cparams("arbitrary", "arbitrary"),
        name="short_conv_transposed",
    )(hy, hy, hy, conv_w, conv_b)


def _untranspose_kernel(z_ref, o_ref):
    for r in range(z_ref.shape[0]):
        o_ref[r * LANES:(r + 1) * LANES, :] = z_ref[r].T.astype(BF16)


def untranspose(zt, *, n_batch, seq):
    a_rows = seq // LANES
    ta = min(SUBLANES, a_rows)
    nt = a_rows // ta
    return pl.pallas_call(
        _untranspose_kernel,
        grid=(n_batch, nt),
        in_specs=[pl.BlockSpec((None, ta, HY_WIDTH, LANES), lambda b, i: (b, i, 0, 0))],
        out_specs=pl.BlockSpec((ta * LANES, HY_WIDTH), lambda b, i: (b * nt + i, 0)),
        out_shape=jax.ShapeDtypeStruct((n_batch * seq, HY_WIDTH), BF16),
        compiler_params=_cparams("arbitrary", "arbitrary"),
        name="untranspose",
    )(zt)


def hyena_latent(hy, conv_w, conv_b, hy_bias, filt_t, consts, *, n_batch, seq):
    assert n_batch == 2, "the two batches are packed as real / imaginary parts of one DFT"
    n1, a_rows = consts["n1"], consts["a_rows"]
    cols = HY_WIDTH * LANES
    ct = min(cols, 2048)
    kb = min(n1, 4)
    parts = short_conv_transposed(hy, conv_w, conv_b, n_batch=n_batch, seq=seq, tm=min(seq, 1024))
    parts = parts.reshape(HY_ORDER + 1, n_batch * a_rows, cols)
    yf = left_matmul(consts["m1_real"], filt_t.reshape(a_rows, 2 * HY_ORDER * cols), ct)
    hr, hi = filter_spectrum(yf, consts, 1)
    z = parts[HY_ORDER]
    for o in range(HY_ORDER):
        y = left_matmul(consts["m1_data"], z, ct)
        zr, zi = freq_multiply(y, hr, hi, o, consts, kb)
        bias_cols = jnp.repeat(hy_bias[o], LANES)[None, :]
        z = left_matmul_gate(consts["m3"], zr.reshape(n1, cols), zi.reshape(n1, cols), parts[o], z, bias_cols, ct)
    return untranspose(z.reshape(n_batch, a_rows, HY_WIDTH, LANES), n_batch=n_batch, seq=seq)


def _ctx_hyena_kernel(u_ref, w_ref, b_ref, h_ref, bias_ref, fc_ref, fs_ref, fct_ref, fst_ref, o_ref):
    u = u_ref[...]
    zero = jnp.zeros((1, u.shape[1]), F32)
    v = _short_conv_rows(u, zero, zero, w_ref, b_ref)
    z = v[:, HY_ORDER * HY_WIDTH:]
    mm = lambda a, b: jnp.dot(a, b, precision=HI, preferred_element_type=F32)
    fc, fs = fc_ref[...], fs_ref[...]
    inv_n = 1.0 / fc.shape[0]
    for o in range(HY_ORDER):
        h0 = h_ref[:, (2 * o) * HY_WIDTH:(2 * o + 1) * HY_WIDTH]
        h1 = h_ref[:, (2 * o + 1) * HY_WIDTH:(2 * o + 2) * HY_WIDTH]
        sr, si = mm(fc, h0 + h1), mm(fs, h1 - h0)
        zr, zi = mm(fc, z), -mm(fs, z)
        gr = zr * sr - zi * si
        gi = zr * si + zi * sr
        y = (mm(fct_ref[...], gr) - mm(fst_ref[...], gi)) * inv_n
        z = v[:, o * HY_WIDTH:(o + 1) * HY_WIDTH] * (y + bias_ref[o:o + 1, :] * z)
    o_ref[...] = z.astype(BF16)


def ctx_hyena(hy, conv_w, conv_b, filt, hy_bias, *, n_batch, seq, n_ctx):
    n = 2 * n_ctx
    k = np.arange(n, dtype=np.float64)[:, None]
    m = np.arange(n_ctx, dtype=np.float64)[None, :]
    ang = 2.0 * np.pi * k * m / n
    fc, fs = jnp.asarray(np.cos(ang), F32), jnp.asarray(np.sin(ang), F32)
    ctx0 = (n_batch * seq) // n_ctx
    const = lambda b: (0, 0)
    return pl.pallas_call(
        _ctx_hyena_kernel,
        grid=(n_batch,),
        in_specs=[
            pl.BlockSpec((n_ctx, HY_COLS), lambda b: (ctx0 + b, 0)),
            pl.BlockSpec((3, HY_COLS), const), pl.BlockSpec((1, HY_COLS), const),
            pl.BlockSpec((n_ctx, 2 * HY_ORDER * HY_WIDTH), const),
            pl.BlockSpec((HY_ORDER, HY_WIDTH), const),
            pl.BlockSpec((n, n_ctx), const), pl.BlockSpec((n, n_ctx), const),
            pl.BlockSpec((n_ctx, n), const), pl.BlockSpec((n_ctx, n), const),
        ],
        out_specs=pl.BlockSpec((n_ctx, HY_WIDTH), lambda b: (b, 0)),
        out_shape=jax.ShapeDtypeStruct((n_batch * n_ctx, HY_WIDTH), BF16),
        compiler_params=_cparams("arbitrary"),
        name="ctx_hyena",
    )(hy, conv_w, conv_b, filt, hy_bias, fc, fs, fc.T, fs.T)


def _outproj_kernel(x_ref, yh_ref, yw_ref, yd_ref, mod_ref, wo_ref, g_ref, wq_ref, xo_ref, h2_ref, qp_ref):
    y = jnp.dot(yh_ref[...], wo_ref[0:HY_WIDTH, :], preferred_element_type=F32)
    y = y + jnp.dot(yw_ref[...], wo_ref[HY_WIDTH:HY_WIDTH + WA_Q_COLS, :], preferred_element_type=F32)
    y = y + jnp.dot(yd_ref[...], wo_ref[HY_WIDTH + WA_Q_COLS:, :], preferred_element_type=F32)
    xn = x_ref[...] + mod_ref[2:3, :] * y
    xo_ref[...] = xn
    h2 = _rmsnorm_mod(xn, g_ref[...], mod_ref[3:4, :], mod_ref[4:5, :]).astype(BF16)
    h2_ref[...] = h2
    qp_ref[...] = jnp.dot(h2, wq_ref[...], preferred_element_type=F32)


def out_proj(x_all, y_hy, y_wa, y_da, mod, wo_bf, g2, wq_bf, *, n_rows, n_batch, seq, tm):
    d = x_all.shape[1]
    tpb = seq // tm
    grp = functools.partial(_group_of_tile, tiles_per_batch=tpb, n_batch=n_batch)
    row = lambda i: (i, 0)
    const = lambda i: (0, 0)
    mix = HY_WIDTH + WA_Q_COLS + DA_V_COLS
    return pl.pallas_call(
        _outproj_kernel,
        grid=(n_rows // tm,),
        in_specs=[
            pl.BlockSpec((tm, d), row), pl.BlockSpec((tm, HY_WIDTH), row), pl.BlockSpec((tm, WA_Q_COLS), row),
            pl.BlockSpec((tm, DA_V_COLS), row),
            pl.BlockSpec((None, N_MOD, d), lambda i: (grp(i), 0, 0)),
            pl.BlockSpec((mix, d), const), pl.BlockSpec((1, d), const), pl.BlockSpec((d, PEER_QCOLS), const),
        ],
        out_specs=[pl.BlockSpec((tm, d), row), pl.BlockSpec((tm, d), row), pl.BlockSpec((tm, PEER_QCOLS), row)],
        out_shape=[jax.ShapeDtypeStruct((n_rows, d), F32), jax.ShapeDtypeStruct((n_rows, d), BF16),
                   jax.ShapeDtypeStruct((n_rows, PEER_QCOLS), F32)],
        compiler_params=_cparams("arbitrary"),
        name="out_proj",
    )(x_all, y_hy, y_wa, y_da, mod, wo_bf, g2, wq_bf)


def _sort_pairs(n):
    pairs = []

    def merge(lo, cnt, r):
        step = r * 2
        if step < cnt:
            merge(lo, cnt, step)
            merge(lo + r, cnt, step)
            for i in range(lo + r, lo + cnt - r, step):
                pairs.append((i, i + r))
        else:
            pairs.append((lo, lo + r))

    def sort(lo, cnt):
        if cnt > 1:
            half = cnt // 2
            sort(lo, half)
            sort(lo + half, half)
            merge(lo, cnt, 1)

    sort(0, n)
    return pairs


def _sort_desc(vals):
    n = 1
    while n < len(vals):
        n *= 2
    v = list(vals) + [None] * (n - len(vals))
    for i, j in _sort_pairs(n):
        a, b = v[i], v[j]
        if b is None:
            continue
        if a is None:
            v[i], v[j] = b, None
        else:
            v[i], v[j] = jnp.maximum(a, b), jnp.minimum(a, b)
    return v[:len(vals)]


def _top16_rows(s):
    k = PEER_TOPK
    v = _sort_desc([s[SUBLANES * r:SUBLANES * (r + 1), :] for r in range(s.shape[0] // SUBLANES)])
    for shift in (4, 2, 1):
        other = [pltpu.roll(x, shift, 0) for x in v]
        v = [jnp.maximum(v[i], other[k - 1 - i]) for i in range(k)]
        d = k // 2
        while d >= 1:
            for i in range(k):
                if (i & d) == 0:
                    a, b = v[i], v[i + d]
                    v[i], v[i + d] = jnp.maximum(a, b), jnp.minimum(a, b)
            d //= 2
    return v


def _peer_topk_kernel(q_ref, keys_ref, cnt_ref, e1_ref, rank_ref, e2_ref):
    k = PEER_TOPK
    tmk = q_ref.shape[0]
    sub = lax.broadcasted_iota(jnp.int32, (SUBLANES, tmk), 0)
    scores, tops = [], []
    for hp in range(2 * PEER_HEADS):
        s = _mm3(keys_ref[hp], q_ref[:, hp * N_KEYS:(hp + 1) * N_KEYS], nt=True)
        scores.append(s)
        top = _top16_rows(s)
        top.append(jnp.max(jnp.where(s < top[k - 1][0:1, :], s, NEG_BIG), axis=0, keepdims=True))
        tops.append(top)
    packed = []
    for p in range(2):
        lst = []
        for r in range(k + 1):
            acc = jnp.broadcast_to(tops[p][r], (SUBLANES, tmk))
            for h in range(1, PEER_HEADS):
                acc = jnp.where(sub == h, tops[2 * h + p][r], acc)
            lst.append(acc)
        packed.append(lst)
    cand = [packed[0][i] + packed[1][j] for i in range(k) for j in range(k) if (i + 1) * (j + 1) <= k]
    cs = _sort_desc(cand)
    c17 = jnp.maximum(cs[k], jnp.maximum(packed[0][k] + packed[1][0], packed[0][0] + packed[1][k]))
    tau = 0.5 * (cs[k - 1] + c17)
    zsum = jnp.zeros_like(tau)
    for r in range(k):
        zsum = zsum + jnp.exp(cs[r] - cs[0])
    inv_z = 1.0 / zsum
    for h in range(PEER_HEADS):
        s1, s2 = scores[2 * h], scores[2 * h + 1]
        row = lambda x: x[h:h + 1, :]
        top2 = [t2[0:1, :] for t2 in tops[2 * h + 1][:k]]
        cnt_ref[h] = _count_sorted(top2, row(tau) - s1, strict=False)
        tiled = lambda x: x.astype(BF16).reshape(N_KEYS // BF16_ROWS, BF16_ROWS, tmk)
        rank_ref[h] = tiled(_count_sorted(top2, s2, strict=True))
        e1_ref[h] = 0.5 * jnp.exp(s1 - row(packed[0][0]))
        e2_ref[h] = tiled(jnp.exp(s2 - row(packed[1][0])) * row(inv_z))


def _count_sorted(tops, x, *, strict):
    ge = (lambda a: a > x) if strict else (lambda a: a >= x)
    b3 = ge(tops[7])
    b2 = ge(jnp.where(b3, tops[11], tops[3]))
    b1 = ge(jnp.where(b3, jnp.where(b2, tops[13], tops[9]), jnp.where(b2, tops[5], tops[1])))
    hi = jnp.where(b2, jnp.where(b1, tops[14], tops[12]), jnp.where(b1, tops[10], tops[8]))
    lo = jnp.where(b2, jnp.where(b1, tops[6], tops[4]), jnp.where(b1, tops[2], tops[0]))
    b0 = ge(jnp.where(b3, hi, lo))
    cnt = (jnp.where(b3, 8.0, 0.0) + jnp.where(b2, 4.0, 0.0)) + (jnp.where(b1, 2.0, 0.0) + jnp.where(b0, 1.0, 0.0))
    return jnp.where(ge(tops[15]), 16.0, cnt)


def peer_topk(qp, keys, *, tmk):
    t = qp.shape[0]
    tiles = N_KEYS // BF16_ROWS
    out = pl.BlockSpec((PEER_HEADS, N_KEYS, tmk), lambda i: (0, 0, i))
    out16 = pl.BlockSpec((PEER_HEADS, tiles, BF16_ROWS, tmk), lambda i: (0, 0, 0, i))
    f32 = jax.ShapeDtypeStruct((PEER_HEADS, N_KEYS, t), F32)
    b16 = jax.ShapeDtypeStruct((PEER_HEADS, tiles, BF16_ROWS, t), BF16)
    return pl.pallas_call(
        _peer_topk_kernel,
        grid=(t // tmk,),
        in_specs=[pl.BlockSpec((tmk, PEER_QCOLS), lambda i: (i, 0)),
                  pl.BlockSpec((2 * PEER_HEADS, N_KEYS, PEER_QDIM // 2), lambda i: (0, 0, 0))],
        out_specs=[out, out, out16, out16],
        out_shape=[f32, f32, b16, b16],
        compiler_params=_cparams("arbitrary"),
        name="peer_topk",
    )(qp, keys)


def _peer_dense_kernel(x_ref, h_ref, mod_ref, u_ref, vt_ref, cnt_ref, e1_ref, rank_ref, e2_ref, g_ref, o_ref,
                       a_scr, w_scr, acc_scr, *, final_norm, chunk, col_block):
    e = pl.program_id(1)

    @pl.when(e == 0)
    def _():
        acc_scr[...] = jnp.zeros(acc_scr.shape, F32)

    n_chunks = u_ref.shape[0] // chunk
    per_chunk = chunk // N_KEYS
    sqrt_half = math.sqrt(0.5)
    tm = h_ref.shape[0]
    tiles = N_KEYS // BF16_ROWS
    zero = jnp.zeros((), BF16)

    slots = a_scr.shape[0]
    ahead = slots - 1

    def pre_activations(p):
        a_scr[p % slots] = lax.dot_general(u_ref[p * chunk:(p + 1) * chunk, :], h_ref[...], (((1,), (1,)), ((), ())),
                                           preferred_element_type=F32)

    for p in range(min(ahead, n_chunks)):
        pre_activations(p)
    for p in range(n_chunks):
        if p + ahead < n_chunks:
            pre_activations(p + ahead)
        for ii, cb in [(ii, cb) for ii in range(p * per_chunk, (p + 1) * per_chunk) for cb in range(tm // col_block)]:
            rows = slice((ii - p * per_chunk) * N_KEYS, (ii - p * per_chunk + 1) * N_KEYS)
            cols = slice(cb * col_block, (cb + 1) * col_block)
            a = a_scr[p % slots, rows, cols]
            act = a * (1.0 + lax.erf(a * sqrt_half))
            act = act.astype(BF16).reshape(tiles, BF16_ROWS, col_block)
            gate = None
            for h in range(PEER_HEADS):
                cnt = jnp.broadcast_to(cnt_ref[h, ii:ii + 1, cols], (BF16_ROWS, col_block)).astype(BF16)
                e1 = jnp.broadcast_to(e1_ref[h, ii:ii + 1, cols], (BF16_ROWS, col_block)).astype(BF16)
                term = jnp.where(rank_ref[h, :, :, cols] < cnt[None], e2_ref[h, :, :, cols] * e1[None], zero)
                gate = term if gate is None else gate + term
            w_scr[p % 2, rows, cols] = (gate * act).reshape(N_KEYS, col_block)
        acc_scr[...] += jnp.dot(vt_ref[:, p * chunk:(p + 1) * chunk], w_scr[p % 2], preferred_element_type=F32)

    @pl.when(e == pl.num_programs(1) - 1)
    def _():
        xn = x_ref[...] + mod_ref[5:6, :] * acc_scr[...].T
        if final_norm:
            ms = jnp.mean(xn * xn, axis=-1, keepdims=True)
            xn = xn * lax.rsqrt(ms + NORM_EPS) * g_ref[...]
        o_ref[...] = xn


def peer_dense(x, h2, mod, u_bf, vt_bf, cnt, e1, rank, e2, g_final, *, n_batch, seq, tm, te, chunk, ahead, col_block,
               final_norm):
    n_rows, d = x.shape
    tpb = seq // tm
    grp = functools.partial(_group_of_tile, tiles_per_batch=tpb, n_batch=n_batch)
    n_i = te // N_KEYS
    tok = lambda i, e: (i, 0)
    rows_i = pl.BlockSpec((PEER_HEADS, n_i, tm), lambda i, e: (0, e, i))
    tiles = N_KEYS // BF16_ROWS
    full = pl.BlockSpec((PEER_HEADS, tiles, BF16_ROWS, tm), lambda i, e: (0, 0, 0, i))
    return pl.pallas_call(
        functools.partial(_peer_dense_kernel, final_norm=final_norm, chunk=chunk, col_block=col_block),
        grid=(n_rows // tm, N_EXPERTS // te),
        in_specs=[
            pl.BlockSpec((tm, d), tok), pl.BlockSpec((tm, d), tok),
            pl.BlockSpec((None, N_MOD, d), lambda i, e: (grp(i), 0, 0)),
            pl.BlockSpec((te, d), lambda i, e: (e, 0)),
            pl.BlockSpec((d, te), lambda i, e: (0, e)),
            rows_i, rows_i, full, full,
            pl.BlockSpec((1, d), lambda i, e: (0, 0)),
        ],
        out_specs=pl.BlockSpec((tm, d), tok),
        out_shape=jax.ShapeDtypeStruct((n_rows, d), F32),
        scratch_shapes=[pltpu.VMEM((ahead + 1, chunk, tm), F32), pltpu.VMEM((2, chunk, tm), BF16),
                        pltpu.VMEM((d, tm), F32)],
        compiler_params=_cparams("arbitrary", "arbitrary", vmem_limit_bytes=PEER_VMEM_LIMIT_BYTES),
        name="peer_dense",
    )(x, h2, mod, u_bf, vt_bf, cnt, e1, rank, e2, g_final)


def kernel(x, c, ctx, c_ctx, w_mod, b_mod, norm1_g, w_in, hy_conv_w, hy_conv_b, hy_w1, hy_b1, hy_freq, hy_w2,
           hy_b2, hy_w3, hy_decay, hy_bias, wa_sink, da_lq1, da_lk1, da_lq2, da_lk2, da_norm_g, w_out, norm2_g,
           peer_wq, peer_keys, peer_u, peer_v, final_g):
    n_batch, seq, d = x.shape
    n_ctx = ctx.shape[1]
    depth = w_in.shape[0]
    n_lat = n_batch * seq
    tm = 512 if seq % 512 == 0 else 256
    assert seq % tm == 0 and (n_batch * n_ctx) % tm == 0 and n_lat % n_ctx == 0 and n_batch + 1 <= SUBLANES

    cpad = jnp.zeros((SUBLANES, d), F32).at[:n_batch].set(c).at[n_batch].set(c_ctx)
    mod_all = mod_vectors(cpad, w_mod, b_mod).reshape(depth, SUBLANES, N_MOD, d)
    rope_wa = rope_tables(seq, HEAD_DIM, tm)
    rope_da = rope_tables(seq, DA_QK_DIM, tm)
    consts = dft_constants(seq)
    feat_lat = hyena_filter_features(seq)
    feat_ctx = hyena_filter_features(n_ctx)
    x_all = jnp.concatenate([x.reshape(n_lat, d), ctx.reshape(n_batch * n_ctx, d)], axis=0)

    out = None
    for li in range(depth):
        last = li == depth - 1
        lam_init = DA_LAMBDA_BASE - DA_LAMBDA_AMP * math.exp(-DA_LAMBDA_RATE * li)
        mod = mod_all[li]
        hy, qw, kw, vw, qd, kd, vd, vdt, vwt = proj_in(x_all, mod, norm1_g[li][None, :], w_in[li].astype(BF16), rope_wa,
                                             rope_da, n_batch=n_batch, seq=seq, tm=tm)
        w1p = jnp.pad(hy_w1[li], ((0, LANES - HY_EMB_DIM), (0, 0)))
        fargs = (w1p, hy_b1[li][None, :], hy_freq[li][None, :], hy_w2[li], hy_b2[li][None, :], hy_w3[li],
                 hy_decay[li].reshape(1, -1))
        conv_b = hy_conv_b[li][None, :]
        filt_t = hyena_filters(feat_lat, *fargs, transposed=True)
        y_hy = hyena_latent(hy, hy_conv_w[li], conv_b, hy_bias[li], filt_t, consts, n_batch=n_batch, seq=seq)
        y_wa = window_attention(wa_sink[li], qw, kw, vwt, n_batch=n_batch, seq=seq, n_ctx=n_ctx)
        lq = jnp.stack([da_lq1[li], da_lk1[li], da_lq2[li], da_lk2[li]], axis=0)
        g128 = jnp.tile(da_norm_g[li], LANES // DA_V_DIM)[None, :]
        tq = min(seq, 512)
        g_rows = jnp.broadcast_to(da_norm_g[li][:, None], (DA_V_DIM, tq))
        y_da = diff_attention(lq, g_rows, qd, kd, vdt, n_batch=n_batch, seq=seq, n_ctx=n_ctx, lam_init=lam_init,
                              tq=tq, tk=min(seq, 256), ahead=2, unroll=8 if seq % 2048 == 0 else 1)
        if not last:
            filt_c = hyena_filters(feat_ctx, *fargs, transposed=False)
            yc_hy = ctx_hyena(hy, hy_conv_w[li], conv_b, filt_c, hy_bias[li], n_batch=n_batch, seq=seq, n_ctx=n_ctx)
            yc_wa, yc_da = ctx_attention(wa_sink[li], lq, g128, qw, kw, vw, qd, kd, vd, n_batch=n_batch, seq=seq,
                                         n_ctx=n_ctx, lam_init=lam_init)
            y_hy = jnp.concatenate([y_hy, yc_hy], axis=0)
            y_wa = jnp.concatenate([y_wa, yc_wa], axis=0)
            y_da = jnp.concatenate([y_da, yc_da], axis=0)
        n_rows = n_lat if last else x_all.shape[0]
        x_mid, h2, qp = out_proj(x_all, y_hy, y_wa, y_da, mod, w_out[li].astype(BF16), norm2_g[li][None, :],
                                 peer_wq[li].astype(BF16), n_rows=n_rows, n_batch=n_batch, seq=seq, tm=tm)
        keys = peer_keys[li].reshape(2 * PEER_HEADS, N_KEYS, PEER_QDIM // 2)
        cnt, e1, rank, e2 = peer_topk(qp, keys, tmk=256)
        x_all = peer_dense(x_mid, h2, mod, peer_u[li].astype(BF16), peer_v[li].T.astype(BF16), cnt, e1, rank, e2,
                           final_g[None, :], n_batch=n_batch, seq=seq, tm=tm, te=4096, chunk=1024, ahead=3, col_block=tm,
                           final_norm=last)
        out = x_all
    return out.reshape(n_batch, seq, d)
```

```python
import functools
import math

import numpy as np
import jax
import jax.numpy as jnp
from jax import lax
from jax.experimental import pallas as pl
from jax.experimental.pallas import tpu as pltpu

F32 = jnp.float32
BF16 = jnp.bfloat16
HI = lax.Precision.HIGHEST

LANES = 128
SUBLANES = 8
BF16_ROWS = 16
VMEM_LIMIT_BYTES = 56 * 1024 * 1024
PEER_VMEM_LIMIT_BYTES = 62 * 1024 * 1024

D_MODEL = 1024
N_MOD = 6
NORM_EPS = 1e-6
ROPE_BASE = 10000.0
GRID_W = 64
BLOCK = 128
HEAD_DIM = 64
HY_WIDTH = 256
HY_ORDER = 2
HY_BANDS = 16
HY_EMB_DIM = 1 + 2 * HY_BANDS
HY_FILTER_WIDTH = 64
HY_COLS = 3 * HY_WIDTH
WA_Q_HEADS = 8
WA_KV_HEADS = 2
WA_GROUP = 4
WINDOW = 128
WA_Q_COLS = WA_Q_HEADS * HEAD_DIM
WA_KV_COLS = WA_KV_HEADS * HEAD_DIM
WA_COLS = WA_Q_COLS + 2 * WA_KV_COLS
DA_HEADS = 4
DA_QK_DIM = 32
DA_V_DIM = 64
DA_QK_COLS = DA_HEADS * 2 * DA_QK_DIM
DA_V_COLS = DA_HEADS * DA_V_DIM
DA_VT_ROWS = DA_V_DIM + 16
DA_COLS = 2 * DA_QK_COLS + DA_V_COLS
DA_LAMBDA_BASE = 0.8
DA_LAMBDA_AMP = 0.6
DA_LAMBDA_RATE = 0.3
IN_COLS = HY_COLS + WA_COLS + DA_COLS
PEER_HEADS = 8
N_KEYS = 128
N_EXPERTS = N_KEYS * N_KEYS
PEER_TOPK = 16
PEER_QDIM = 256
PEER_QCOLS = PEER_HEADS * PEER_QDIM
NEG_BIG = -1e30
LOG2E = math.log2(math.e)


def _cparams(*sem, vmem_limit_bytes=VMEM_LIMIT_BYTES):
    return pltpu.CompilerParams(dimension_semantics=sem, vmem_limit_bytes=vmem_limit_bytes)


def _rmsnorm_mod(x, g, shift, scale):
    ms = jnp.mean(x * x, axis=-1, keepdims=True)
    return (x * lax.rsqrt(ms + NORM_EPS) * g) * (1.0 + scale) + shift


def _mod_kernel(c_ref, w_ref, b_ref, o_ref):
    c = c_ref[...]
    act = c * jax.nn.sigmoid(c)
    o_ref[...] = jnp.dot(act, w_ref[...], precision=HI, preferred_element_type=F32) + b_ref[...]


def mod_vectors(cpad, w_mod, b_mod):
    n_layers, d, n = w_mod.shape
    tn = 1536
    return pl.pallas_call(
        _mod_kernel,
        grid=(n_layers, n // tn),
        in_specs=[
            pl.BlockSpec((SUBLANES, d), lambda l, j: (0, 0)),
            pl.BlockSpec((None, d, tn), lambda l, j: (l, 0, j)),
            pl.BlockSpec((None, 1, tn), lambda l, j: (l, 0, j)),
        ],
        out_specs=pl.BlockSpec((None, SUBLANES, tn), lambda l, j: (l, 0, j)),
        out_shape=jax.ShapeDtypeStruct((n_layers, SUBLANES, n), F32),
        compiler_params=_cparams("arbitrary", "arbitrary"),
        name="mod_vectors",
    )(cpad, w_mod, b_mod.reshape(n_layers, 1, n))


def _rope128(x, cos, sin_a, sin_b, m):
    return x * cos + pltpu.roll(x, LANES - m, 1) * sin_a + pltpu.roll(x, m, 1) * sin_b


def _proj_in_kernel(x_ref, mod_ref, g_ref, w_ref, cw_ref, saw_ref, sbw_ref, cd_ref, sad_ref, sbd_ref,
                    hy_ref, qw_ref, kw_ref, vw_ref, qd_ref, kd_ref, vd_ref, vdt_ref, vwt_ref):
    h = _rmsnorm_mod(x_ref[...], g_ref[...], mod_ref[0:1, :], mod_ref[1:2, :]).astype(BF16)
    hy_ref[...] = jnp.dot(h, w_ref[:, 0:HY_COLS], preferred_element_type=F32)
    wa = jnp.dot(h, w_ref[:, HY_COLS:HY_COLS + WA_COLS], preferred_element_type=F32)
    da = jnp.dot(h, w_ref[:, HY_COLS + WA_COLS:IN_COLS], preferred_element_type=F32)
    cw, saw, sbw = cw_ref[...], saw_ref[...], sbw_ref[...]
    cd, sad, sbd = cd_ref[...], sad_ref[...], sbd_ref[...]
    wa_scale = HEAD_DIM ** -0.5
    da_scale = DA_QK_DIM ** -0.5 * LOG2E
    for kb in range(WA_Q_COLS // LANES):
        blk = wa[:, kb * LANES:(kb + 1) * LANES]
        qw_ref[:, kb * LANES:(kb + 1) * LANES] = (_rope128(blk, cw, saw, sbw, 16) * wa_scale).astype(BF16)
    kw_ref[...] = _rope128(wa[:, WA_Q_COLS:WA_Q_COLS + WA_KV_COLS], cw, saw, sbw, 16).astype(BF16)
    vw = wa[:, WA_Q_COLS + WA_KV_COLS:WA_COLS]
    vw_ref[...] = vw.astype(BF16)
    vwt_ref[...] = vw.T.astype(BF16)
    for kb in range(DA_QK_COLS // LANES):
        blk = da[:, kb * LANES:(kb + 1) * LANES]
        qd_ref[:, kb * LANES:(kb + 1) * LANES] = (_rope128(blk, cd, sad, sbd, 8) * da_scale).astype(BF16)
        blk = da[:, DA_QK_COLS + kb * LANES:DA_QK_COLS + (kb + 1) * LANES]
        kd_ref[:, kb * LANES:(kb + 1) * LANES] = _rope128(blk, cd, sad, sbd, 8).astype(BF16)
    vd = da[:, 2 * DA_QK_COLS:DA_COLS]
    vd_ref[...] = vd.astype(BF16)
    ones = jnp.ones((DA_VT_ROWS - DA_V_DIM, vd.shape[0]), BF16)
    for h in range(DA_HEADS):
        vdt_ref[h * DA_VT_ROWS:h * DA_VT_ROWS + DA_V_DIM, :] = vd[:, h * DA_V_DIM:(h + 1) * DA_V_DIM].T.astype(BF16)
        vdt_ref[h * DA_VT_ROWS + DA_V_DIM:(h + 1) * DA_VT_ROWS, :] = ones


def _group_of_tile(i, tiles_per_batch, n_batch):
    return jnp.minimum(i // tiles_per_batch, n_batch)


def proj_in(x_all, mod, g, w_bf, rope_wa, rope_da, *, n_batch, seq, tm):
    t_all, d = x_all.shape
    tpb = seq // tm
    grp = functools.partial(_group_of_tile, tiles_per_batch=tpb, n_batch=n_batch)
    pos = lambda i: (jnp.where(i < n_batch * tpb, i % tpb, tpb), 0)
    row = lambda i: (i, 0)
    const = lambda i: (0, 0)
    tab = pl.BlockSpec((tm, LANES), pos)
    outs = [(HY_COLS, F32), (WA_Q_COLS, BF16), (WA_KV_COLS, BF16), (WA_KV_COLS, BF16),
            (DA_QK_COLS, BF16), (DA_QK_COLS, BF16), (DA_V_COLS, BF16)]
    return pl.pallas_call(
        _proj_in_kernel,
        grid=(t_all // tm,),
        in_specs=[
            pl.BlockSpec((tm, d), row),
            pl.BlockSpec((None, N_MOD, d), lambda i: (grp(i), 0, 0)),
            pl.BlockSpec((1, d), const),
            pl.BlockSpec((d, IN_COLS), const),
            tab, tab, tab, tab, tab, tab,
        ],
        out_specs=[pl.BlockSpec((tm, n), row) for n, _ in outs]
        + [pl.BlockSpec((DA_HEADS * DA_VT_ROWS, tm), lambda i: (0, i)),
           pl.BlockSpec((WA_KV_COLS, tm), lambda i: (0, i))],
        out_shape=[jax.ShapeDtypeStruct((t_all, n), dt) for n, dt in outs]
        + [jax.ShapeDtypeStruct((DA_HEADS * DA_VT_ROWS, t_all), BF16),
           jax.ShapeDtypeStruct((WA_KV_COLS, t_all), BF16)],
        compiler_params=_cparams("arbitrary"),
        name="proj_in",
    )(x_all, mod, g, w_bf, *rope_wa, *rope_da)


def rope_tables(seq, head_dim, pad_rows):
    half = head_dim // 2
    axis_dim = half
    m = axis_dim // 2
    f32 = np.float32
    t = np.arange(seq)
    rowcol = np.stack([t // GRID_W, t % GRID_W], axis=0).astype(f32)
    inv_freq = f32(ROPE_BASE) ** (-np.arange(0, axis_dim, 2, dtype=f32) / f32(axis_dim))
    ang = rowcol[:, :, None] * inv_freq[None, None, :]
    cos, sin = np.cos(ang), np.sin(ang)
    lane = np.arange(LANES)
    dd = lane % head_dim
    axis = dd // half
    sub = dd % half
    idx = sub % m
    first = sub < m
    cos_t = cos[axis, :, idx].T
    sin_t = sin[axis, :, idx].T
    sin_a = np.where(first[None, :], -sin_t, f32(0))
    sin_b = np.where(first[None, :], f32(0), sin_t)
    pad1 = np.ones((pad_rows, LANES), f32)
    pad0 = np.zeros((pad_rows, LANES), f32)
    tables = (np.concatenate([cos_t, pad1], 0), np.concatenate([sin_a, pad0], 0), np.concatenate([sin_b, pad0], 0))
    return tuple(jnp.asarray(x, F32) for x in tables)


def _softmax_sink_attend(q, k, v, sink, valid):
    s = lax.dot_general(q, k, (((1,), (1,)), ((), ())), preferred_element_type=F32)
    if valid is not None:
        nm = valid.shape[1]
        s = jnp.concatenate([jnp.where(valid, s[:, :nm], NEG_BIG), s[:, nm:]], axis=1)
    m = jnp.maximum(jnp.max(s, axis=1, keepdims=True), sink)
    p = jnp.exp(s - m)
    den = jnp.sum(p, axis=1, keepdims=True) + jnp.exp(sink - m)
    return jnp.dot(p.astype(BF16), v, preferred_element_type=F32) / den


def _wattn_kernel(sink_ref, q_ref, kp_ref, kc_ref, kn_ref, kx_ref, vp_ref, vc_ref, vn_ref, vx_ref, o_ref, qm_scr,
                  *, n_pairs, ahead):
    m_idx = pl.program_id(1)
    nq = 2 * BLOCK
    n_loc = 4 * BLOCK
    kk = lax.broadcasted_iota(jnp.int32, (n_loc, nq), 0)
    qq = lax.broadcasted_iota(jnp.int32, (n_loc, nq), 1)
    lo = jnp.where(m_idx > 0, 0, BLOCK)
    hi = jnp.where(m_idx < n_pairs - 1, n_loc, n_loc - BLOCK)
    valid = (kk >= jnp.maximum(qq, lo)) & (kk <= qq + 2 * WINDOW) & (kk < hi)
    k_all = jnp.concatenate([kp_ref[...], kc_ref[...], kn_ref[...], kx_ref[...]], axis=0)
    vt_all = jnp.concatenate([vp_ref[...], vc_ref[...], vn_ref[...], vx_ref[...]], axis=1)
    lane = lax.broadcasted_iota(jnp.int32, (nq, LANES), 1)
    for h in range(WA_Q_HEADS):
        hk = h // WA_GROUP
        blk = q_ref[:, (h // 2) * LANES:(h // 2 + 1) * LANES]
        if h % 2 != hk:
            blk = jnp.concatenate([blk[:, HEAD_DIM:], blk[:, :HEAD_DIM]], axis=1)
        keep = (lane >= hk * HEAD_DIM) & (lane < (hk + 1) * HEAD_DIM)
        qm_scr[h] = jnp.where(keep, blk, jnp.zeros_like(blk))

    def scores(h):
        return lax.dot_general(k_all, qm_scr[h], (((1,), (1,)), ((), ())), preferred_element_type=F32)

    sts = [scores(h) for h in range(ahead)]
    outs = []
    for h in range(WA_Q_HEADS):
        hk = h // WA_GROUP
        if h + ahead < WA_Q_HEADS:
            sts.append(scores(h + ahead))
        st = sts[h]
        sts[h] = None
        st = jnp.concatenate([jnp.where(valid, st[:n_loc], NEG_BIG), st[n_loc:]], axis=0)
        sink = sink_ref[h]
        m = jnp.maximum(jnp.max(st, axis=0, keepdims=True), sink)
        p = jnp.exp(st - m)
        den = jnp.sum(p, axis=0, keepdims=True) + jnp.exp(sink - m)
        o = jnp.dot(vt_all[hk * HEAD_DIM:(hk + 1) * HEAD_DIM, :], p.astype(BF16), preferred_element_type=F32)
        outs.append(o / den)
    o_ref[...] = jnp.concatenate(outs, axis=0).T.astype(BF16)


def window_attention(sink, qw, kw, vwt, *, n_batch, seq, n_ctx):
    nb = seq // BLOCK
    n_pairs = nb // 2
    assert nb % 2 == 0
    ctx0 = (n_batch * seq) // n_ctx
    cur = lambda b, m: (b * n_pairs + m, 0)
    prev = lambda b, m: (b * nb + jnp.maximum(2 * m - 1, 0), 0)
    nxt = lambda b, m: (b * nb + jnp.minimum(2 * m + 2, nb - 1), 0)
    ctx = lambda b, m: (ctx0 + b, 0)
    swap = lambda f: (lambda b, m: f(b, m)[::-1])
    return pl.pallas_call(
        functools.partial(_wattn_kernel, n_pairs=n_pairs, ahead=3),
        grid=(n_batch, n_pairs),
        in_specs=[
            pl.BlockSpec(memory_space=pltpu.SMEM),
            pl.BlockSpec((2 * BLOCK, WA_Q_COLS), cur),
            pl.BlockSpec((BLOCK, WA_KV_COLS), prev), pl.BlockSpec((2 * BLOCK, WA_KV_COLS), cur),
            pl.BlockSpec((BLOCK, WA_KV_COLS), nxt), pl.BlockSpec((n_ctx, WA_KV_COLS), ctx),
            pl.BlockSpec((WA_KV_COLS, BLOCK), swap(prev)), pl.BlockSpec((WA_KV_COLS, 2 * BLOCK), swap(cur)),
            pl.BlockSpec((WA_KV_COLS, BLOCK), swap(nxt)), pl.BlockSpec((WA_KV_COLS, n_ctx), swap(ctx)),
        ],
        out_specs=pl.BlockSpec((2 * BLOCK, WA_Q_COLS), cur),
        out_shape=jax.ShapeDtypeStruct((n_batch * seq, WA_Q_COLS), BF16),
        scratch_shapes=[pltpu.VMEM((WA_Q_HEADS, 2 * BLOCK, LANES), BF16)],
        compiler_params=_cparams("arbitrary", "arbitrary"),
        name="window_attention",
    )(sink, qw, kw, kw, kw, kw, vwt, vwt, vwt, vwt)


def _diff_lambda(lq_ref, lam_init):
    a = jnp.sum(lq_ref[0:1, :] * lq_ref[1:2, :], axis=1, keepdims=True)
    b = jnp.sum(lq_ref[2:3, :] * lq_ref[3:4, :], axis=1, keepdims=True)
    return jnp.exp(a) - jnp.exp(b) + lam_init


def _diff_finish(acc1, l1, acc2, l2, lam, g128, lam_init):
    lane = lax.broadcasted_iota(jnp.int32, acc1[0].shape, 1)
    lo = lane < DA_V_DIM
    blocks = []
    for vb in range(DA_HEADS // 2):
        o = []
        for h in (2 * vb, 2 * vb + 1):
            o.append(acc1[h] / l1[h] - lam * (acc2[h] / l2[h]))
        blk = jnp.where(lo, o[0], o[1])
        sq = blk * blk
        s_lo = jnp.sum(jnp.where(lo, sq, 0.0), axis=1, keepdims=True)
        s_hi = jnp.sum(jnp.where(lo, 0.0, sq), axis=1, keepdims=True)
        ms = jnp.where(lo, s_lo, s_hi) * (1.0 / DA_V_DIM)
        blocks.append(blk * lax.rsqrt(ms + NORM_EPS) * g128 * (1.0 - lam_init))
    return jnp.concatenate(blocks, axis=1)


def _masked_q(q_ref, hm):
    kb, sub = divmod(hm, LANES // DA_QK_DIM)
    blk = q_ref[:, kb * LANES:(kb + 1) * LANES]
    lane = lax.broadcasted_iota(jnp.int32, blk.shape, 1)
    keep = (lane >= sub * DA_QK_DIM) & (lane < (sub + 1) * DA_QK_DIM)
    return jnp.where(keep, blk, jnp.zeros_like(blk))


def _dattn_kernel(lq_ref, g_ref, q_ref, k_ref, vt_ref, kx_ref, vxt_ref, o_ref, qm_scr, m_scr, acc_scr,
                  *, lam_init, tk, ahead, unroll):
    n_hm = 2 * DA_HEADS
    per_blk = LANES // DA_QK_DIM
    for hm in range(n_hm):
        qm_scr[hm] = _masked_q(q_ref, hm)
    m_scr[...] = jnp.full(m_scr.shape, NEG_BIG, F32)
    acc_scr[...] = jnp.zeros(acc_scr.shape, F32)

    def update(chunks):
        chains = [(c, hm) for c in range(len(chunks)) for hm in range(n_hm)]

        def scores(chain):
            c, hm = chain
            kb = hm // per_blk
            return lax.dot_general(chunks[c][0][:, kb * LANES:(kb + 1) * LANES], qm_scr[hm],
                                   (((1,), (1,)), ((), ())), preferred_element_type=F32)

        sts = [scores(ch) for ch in chains[:ahead]]
        for n, (c, hm) in enumerate(chains):
            h = hm // 2
            vtc = chunks[c][1]
            if n + ahead < len(chains):
                sts.append(scores(chains[n + ahead]))
            st = sts[n]
            sts[n] = None
            m_old = m_scr[hm]
            m_new = jnp.maximum(m_old, jnp.max(st, axis=0, keepdims=True))
            alpha = jnp.exp2(m_old - m_new)
            pt = jnp.exp2((st - m_new).astype(BF16))
            acc_scr[hm] = alpha * acc_scr[hm] + jnp.dot(vtc[h * DA_VT_ROWS:(h + 1) * DA_VT_ROWS, :], pt,
                                                        preferred_element_type=F32)
            m_scr[hm] = m_new

    def body(j, carry):
        chunks = []
        for c in range(unroll):
            start = pl.multiple_of((j * unroll + c) * tk, tk)
            chunks.append((k_ref[pl.ds(start, tk), :], vt_ref[:, pl.ds(start, tk)]))
        update(chunks)
        return carry

    lax.fori_loop(0, k_ref.shape[0] // (tk * unroll), body, 0)
    n_x = kx_ref.shape[0] // tk
    update([(kx_ref[c * tk:(c + 1) * tk, :], vxt_ref[:, c * tk:(c + 1) * tk]) for c in range(n_x)])
    lam = _diff_lambda(lq_ref, lam_init)
    heads = []
    for h in range(DA_HEADS):
        a1, a2 = acc_scr[2 * h], acc_scr[2 * h + 1]
        dv = DA_V_DIM
        o = a1[:dv] / a1[dv:dv + 1] - lam * (a2[:dv] / a2[dv:dv + 1])
        ms = jnp.mean(o * o, axis=0, keepdims=True)
        heads.append(o * lax.rsqrt(ms + NORM_EPS) * g_ref[...] * (1.0 - lam_init))
    o_ref[...] = jnp.concatenate(heads, axis=0).T.astype(BF16)


def diff_attention(lq, g_rows, qd, kd, vdt, *, n_batch, seq, n_ctx, lam_init, tq, tk, ahead, unroll):
    nq = seq // tq
    ctx0 = (n_batch * seq) // n_ctx
    n_hm = 2 * DA_HEADS
    return pl.pallas_call(
        functools.partial(_dattn_kernel, lam_init=lam_init, tk=tk, ahead=ahead, unroll=unroll),
        grid=(n_batch, nq),
        in_specs=[
            pl.BlockSpec((4, DA_QK_DIM), lambda b, i: (0, 0)),
            pl.BlockSpec((DA_V_DIM, tq), lambda b, i: (0, 0)),
            pl.BlockSpec((tq, DA_QK_COLS), lambda b, i: (b * nq + i, 0)),
            pl.BlockSpec((seq, DA_QK_COLS), lambda b, i: (b, 0)),
            pl.BlockSpec((DA_HEADS * DA_VT_ROWS, seq), lambda b, i: (0, b)),
            pl.BlockSpec((n_ctx, DA_QK_COLS), lambda b, i: (ctx0 + b, 0)),
            pl.BlockSpec((DA_HEADS * DA_VT_ROWS, n_ctx), lambda b, i: (0, ctx0 + b)),
        ],
        out_specs=pl.BlockSpec((tq, DA_V_COLS), lambda b, i: (b * nq + i, 0)),
        out_shape=jax.ShapeDtypeStruct((n_batch * seq, DA_V_COLS), BF16),
        scratch_shapes=[
            pltpu.VMEM((n_hm, tq, LANES), BF16),
            pltpu.VMEM((n_hm, 1, tq), F32),
            pltpu.VMEM((n_hm, DA_VT_ROWS, tq), F32),
        ],
        compiler_params=_cparams("arbitrary", "arbitrary"),
        name="diff_attention",
    )(lq, g_rows, qd, kd, vdt, kd, vdt)


def _ctx_attn_kernel(sink_ref, lq_ref, g_ref, qw_ref, kw_ref, vw_ref, qd_ref, kd_ref, vd_ref, ow_ref, od_ref,
                     *, lam_init):
    outs = []
    for hk in range(WA_KV_HEADS):
        sl = slice(hk * HEAD_DIM, (hk + 1) * HEAD_DIM)
        k, v = kw_ref[:, sl], vw_ref[:, sl]
        for g in range(WA_GROUP):
            h = hk * WA_GROUP + g
            outs.append(_softmax_sink_attend(qw_ref[:, h * HEAD_DIM:(h + 1) * HEAD_DIM], k, v, sink_ref[h], None))
    ow_ref[...] = jnp.concatenate(outs, axis=1).astype(BF16)
    per_blk = LANES // DA_QK_DIM
    acc, den = [], []
    for hm in range(2 * DA_HEADS):
        kb, vb = hm // per_blk, hm // 4
        s = lax.dot_general(_masked_q(qd_ref, hm), kd_ref[:, kb * LANES:(kb + 1) * LANES],
                            (((1,), (1,)), ((), ())), preferred_element_type=F32)
        p = jnp.exp2(s - jnp.max(s, axis=1, keepdims=True))
        den.append(jnp.sum(p, axis=1, keepdims=True))
        acc.append(jnp.dot(p.astype(BF16), vd_ref[:, vb * LANES:(vb + 1) * LANES], preferred_element_type=F32))
    lam = _diff_lambda(lq_ref, lam_init)
    od_ref[...] = _diff_finish(acc[0::2], den[0::2], acc[1::2], den[1::2], lam, g_ref[...], lam_init).astype(BF16)


def ctx_attention(sink, lq, g128, qw, kw, vw, qd, kd, vd, *, n_batch, seq, n_ctx, lam_init):
    ctx0 = (n_batch * seq) // n_ctx
    ctx = lambda b: (ctx0 + b, 0)
    out = lambda b: (b, 0)
    spec = lambda n, f: pl.BlockSpec((n_ctx, n), f)
    return pl.pallas_call(
        functools.partial(_ctx_attn_kernel, lam_init=lam_init),
        grid=(n_batch,),
        in_specs=[
            pl.BlockSpec(memory_space=pltpu.SMEM),
            pl.BlockSpec((4, DA_QK_DIM), lambda b: (0, 0)),
            pl.BlockSpec((1, LANES), lambda b: (0, 0)),
            spec(WA_Q_COLS, ctx), spec(WA_KV_COLS, ctx), spec(WA_KV_COLS, ctx),
            spec(DA_QK_COLS, ctx), spec(DA_QK_COLS, ctx), spec(DA_V_COLS, ctx),
        ],
        out_specs=[spec(WA_Q_COLS, out), spec(DA_V_COLS, out)],
        out_shape=[jax.ShapeDtypeStruct((n_batch * n_ctx, WA_Q_COLS), BF16),
                   jax.ShapeDtypeStruct((n_batch * n_ctx, DA_V_COLS), BF16)],
        compiler_params=_cparams("arbitrary"),
        name="ctx_attention",
    )(sink, lq, g128, qw, kw, vw, qd, kd, vd)


def _filter_kernel(feat_ref, w1_ref, b1_ref, fr_ref, w2_ref, b2_ref, w3_ref, dec_ref, o_ref, *, transposed):
    feat = feat_ref[...]
    fr = fr_ref[...]
    z = jnp.sin(fr * (jnp.dot(feat, w1_ref[...], precision=HI, preferred_element_type=F32) + b1_ref[...]))
    z = jnp.sin(fr * (jnp.dot(z, w2_ref[...], precision=HI, preferred_element_type=F32) + b2_ref[...]))
    h = jnp.dot(z, w3_ref[...], precision=HI, preferred_element_type=F32)
    h = h * jnp.exp(-feat[:, 0:1] * jnp.abs(dec_ref[...]))
    rows = lax.broadcasted_iota(jnp.int32, h.shape, 0) + pl.program_id(0) * h.shape[0]
    cols = lax.broadcasted_iota(jnp.int32, h.shape, 1)
    h = jnp.where((rows == 0) & (cols % (2 * HY_WIDTH) >= HY_WIDTH), 0.0, h)
    if transposed:
        for r in range(h.shape[0] // LANES):
            o_ref[r] = h[r * LANES:(r + 1) * LANES, :].T
    else:
        o_ref[...] = h


def hyena_filter_features(n):
    f32 = np.float32
    t = np.linspace(0.0, 1.0, n, dtype=f32)[:, None]
    bands = np.linspace(1e-4, HY_BANDS - 1, HY_BANDS, dtype=f32)
    phase = (f32(2.0 * math.pi / n) * np.arange(n, dtype=f32)[:, None]) * bands
    feat = np.concatenate([t, np.cos(phase), -np.sin(phase)], axis=-1).astype(f32)
    return jnp.asarray(np.pad(feat, ((0, 0), (0, LANES - HY_EMB_DIM))), F32)


def hyena_filters(feat, w1p, b1, freq, w2, b2, w3, decay, *, transposed):
    n = feat.shape[0]
    tr = min(n, 512)
    ncol = w3.shape[1]
    fw = w2.shape[0]
    const = lambda i: (0, 0)
    if transposed:
        out_spec = pl.BlockSpec((tr // LANES, ncol, LANES), lambda i: (i, 0, 0))
        out_shape = jax.ShapeDtypeStruct((n // LANES, ncol, LANES), F32)
    else:
        out_spec = pl.BlockSpec((tr, ncol), lambda i: (i, 0))
        out_shape = jax.ShapeDtypeStruct((n, ncol), F32)
    return pl.pallas_call(
        functools.partial(_filter_kernel, transposed=transposed),
        grid=(n // tr,),
        in_specs=[
            pl.BlockSpec((tr, LANES), lambda i: (i, 0)),
            pl.BlockSpec((LANES, fw), const), pl.BlockSpec((1, fw), const), pl.BlockSpec((1, fw), const),
            pl.BlockSpec((fw, fw), const), pl.BlockSpec((1, fw), const),
            pl.BlockSpec((fw, ncol), const), pl.BlockSpec((1, ncol), const),
        ],
        out_specs=out_spec,
        out_shape=out_shape,
        compiler_params=_cparams("arbitrary"),
        name="hyena_filters",
    )(feat, w1p, b1, freq, w2, b2, w3, decay)


def dft_constants(seq):
    n = 2 * seq
    n1 = n // LANES
    a_rows = n1 // 2
    k1 = np.arange(n1, dtype=np.float64)[:, None]
    a = np.arange(a_rows, dtype=np.float64)[None, :]
    th = 2.0 * np.pi * k1 * a / n1
    c1, s1 = np.cos(th), np.sin(th)
    m1_data = np.block([[c1, s1], [-s1, c1]])
    half_rows = -(-(n1 // 2 + 1) // SUBLANES) * SUBLANES
    pad = np.zeros((half_rows - (n1 // 2 + 1), a_rows))
    m1_real = np.concatenate([c1[:n1 // 2 + 1], pad, -s1[:n1 // 2 + 1], pad], axis=0)
    m3 = np.block([[c1.T, -s1.T], [s1.T, c1.T]]) / n
    b = np.arange(LANES, dtype=np.float64)[None, :]
    psi = 2.0 * np.pi * k1 * b / n
    twc, tws = np.cos(psi), np.sin(psi)
    bb = np.arange(LANES, dtype=np.float64)
    phi = 2.0 * np.pi * np.outer(bb, bb) / LANES
    c2, s2 = np.cos(phi), np.sin(phi)
    w2f = np.block([[c2, -s2], [s2, c2]])
    w2i = np.block([[c2, s2], [-s2, c2]])
    f = lambda x: jnp.asarray(x, dtype=F32)
    return dict(m1_data=f(m1_data), m1_real=f(m1_real), m3=f(m3), twc=f(twc[:, None, :]), tws=f(tws[:, None, :]),
                w2f=f(w2f), w2i=f(w2i), n1=n1, a_rows=a_rows, half_rows=half_rows)


def _split_bf16(x):
    hi = x.astype(BF16)
    return hi, (x - hi.astype(F32)).astype(BF16)


def _mm3(a, b, nt=False):
    ah, al = _split_bf16(a)
    bh, bl = _split_bf16(b)
    dims = (((1,), (1 if nt else 0,)), ((), ()))
    dot = lambda x, y: lax.dot_general(x, y, dims, preferred_element_type=F32)
    return dot(ah, bh) + (dot(ah, bl) + dot(al, bh))


def _leftmm_kernel(m_ref, x_ref, o_ref):
    o_ref[...] = _mm3(m_ref[...], x_ref[...])


def left_matmul(m, x, ct):
    r, k = m.shape
    cols = x.shape[1]
    return pl.pallas_call(
        _leftmm_kernel,
        grid=(cols // ct,),
        in_specs=[pl.BlockSpec((r, k), lambda j: (0, 0)), pl.BlockSpec((k, ct), lambda j: (0, j))],
        out_specs=pl.BlockSpec((r, ct), lambda j: (0, j)),
        out_shape=jax.ShapeDtypeStruct((r, cols), F32),
        compiler_params=_cparams("arbitrary"),
        name="dft_rows",
    )(m, x)


def _leftmm_gate_kernel(m_ref, zr_ref, zi_ref, x_ref, zold_ref, bias_ref, o_ref):
    n1 = zr_ref.shape[0]
    y = _mm3(m_ref[:, :n1], zr_ref[...]) + _mm3(m_ref[:, n1:], zi_ref[...])
    zold = zold_ref[...]
    o_ref[...] = x_ref[...] * (y + bias_ref[...] * zold)


def left_matmul_gate(m, zr, zi, xg, zold, bias_cols, ct):
    r, k = m.shape
    n1, cols = zr.shape
    col = lambda j: (0, j)
    return pl.pallas_call(
        _leftmm_gate_kernel,
        grid=(cols // ct,),
        in_specs=[pl.BlockSpec((r, k), lambda j: (0, 0)), pl.BlockSpec((n1, ct), col), pl.BlockSpec((n1, ct), col),
                  pl.BlockSpec((r, ct), col), pl.BlockSpec((r, ct), col), pl.BlockSpec((1, ct), col)],
        out_specs=pl.BlockSpec((r, ct), col),
        out_shape=jax.ShapeDtypeStruct((r, cols), F32),
        compiler_params=_cparams("arbitrary"),
        name="idft_rows_gate",
    )(m, zr, zi, xg, zold, bias_cols)


def _twiddle_fwd(yr, yi, c, s):
    return yr * c + yi * s, yi * c - yr * s


def _spectrum_kernel(yr_ref, yi_ref, twc_ref, tws_ref, w2f_ref, hr_ref, hi_ref, *, kb, n1):
    for t in range(kb):
        mirrored = pl.program_id(0) * kb + t > n1 // 2
        yi = yi_ref[t]
        ypr, ypi = _twiddle_fwd(yr_ref[t], jnp.where(mirrored, -yi, yi), twc_ref[t], tws_ref[t])
        x = _mm3(jnp.concatenate([ypr, ypi], axis=1), w2f_ref[...])
        xr, xi = x[:, :LANES], x[:, LANES:]
        for o in range(HY_ORDER):
            f0 = slice((2 * o) * HY_WIDTH, (2 * o + 1) * HY_WIDTH)
            f1 = slice((2 * o + 1) * HY_WIDTH, (2 * o + 2) * HY_WIDTH)
            hr_ref[o, t] = xr[f0] + xr[f1]
            hi_ref[o, t] = xi[f0] - xi[f1]


def filter_spectrum(y, consts, kb):
    assert kb == 1
    n1, half_rows = consts["n1"], consts["half_rows"]
    nc = 2 * HY_ORDER * HY_WIDTH
    y4 = y.reshape(2, half_rows, nc, LANES)
    tw = pl.BlockSpec((kb, 1, LANES), lambda i: (i, 0, 0))
    out = pl.BlockSpec((HY_ORDER, kb, HY_WIDTH, LANES), lambda i: (0, i, 0, 0))
    shp = jax.ShapeDtypeStruct((HY_ORDER, n1, HY_WIDTH, LANES), F32)
    src = lambda i: jnp.where(i > n1 // 2, n1 - i, i)
    return pl.pallas_call(
        functools.partial(_spectrum_kernel, kb=kb, n1=n1),
        grid=(n1 // kb,),
        in_specs=[
            pl.BlockSpec((None, kb, nc, LANES), lambda i: (0, src(i), 0, 0)),
            pl.BlockSpec((None, kb, nc, LANES), lambda i: (1, src(i), 0, 0)),
            tw, tw,
            pl.BlockSpec((2 * LANES, 2 * LANES), lambda i: (0, 0)),
        ],
        out_specs=[out, out],
        out_shape=[shp, shp],
        compiler_params=_cparams("arbitrary"),
        name="filter_spectrum",
    )(y4, y4, consts["twc"], consts["tws"], consts["w2f"])


def _freq_kernel(yr_ref, yi_ref, twc_ref, tws_ref, w2f_ref, w2i_ref, hr_ref, hi_ref, zr_ref, zi_ref, *, kb):
    for t in range(kb):
        c, s = twc_ref[t], tws_ref[t]
        ypr, ypi = _twiddle_fwd(yr_ref[t], yi_ref[t], c, s)
        x = _mm3(jnp.concatenate([ypr, ypi], axis=1), w2f_ref[...])
        xr, xi = x[:, :LANES], x[:, LANES:]
        hr, hi = hr_ref[t], hi_ref[t]
        gr = xr * hr - xi * hi
        gi = xr * hi + xi * hr
        z = _mm3(jnp.concatenate([gr, gi], axis=1), w2i_ref[...])
        zr, zi = z[:, :LANES], z[:, LANES:]
        zr_ref[t] = zr * c - zi * s
        zi_ref[t] = zi * c + zr * s


def freq_multiply(y, hr, hi, order, consts, kb):
    n1 = consts["n1"]
    y4 = y.reshape(2, n1, HY_WIDTH, LANES)
    tw = pl.BlockSpec((kb, 1, LANES), lambda i: (i, 0, 0))
    blk = lambda p: pl.BlockSpec((None, kb, HY_WIDTH, LANES), lambda i: (p, i, 0, 0))
    wspec = pl.BlockSpec((2 * LANES, 2 * LANES), lambda i: (0, 0))
    out = pl.BlockSpec((kb, HY_WIDTH, LANES), lambda i: (i, 0, 0))
    shp = jax.ShapeDtypeStruct((n1, HY_WIDTH, LANES), F32)
    return pl.pallas_call(
        functools.partial(_freq_kernel, kb=kb),
        grid=(n1 // kb,),
        in_specs=[blk(0), blk(1), tw, tw, wspec, wspec, blk(order), blk(order)],
        out_specs=[out, out],
        out_shape=[shp, shp],
        compiler_params=_cparams("arbitrary"),
        name="freq_multiply",
    )(y4, y4, consts["twc"], consts["tws"], consts["w2f"], consts["w2i"], hr, hi)


def _short_conv_rows(u, prev_row, next_row, w_ref, b_ref):
    rows = lax.broadcasted_iota(jnp.int32, u.shape, 0)
    um1 = jnp.where(rows == 0, prev_row, pltpu.roll(u, 1, 0))
    up1 = jnp.where(rows == u.shape[0] - 1, next_row, pltpu.roll(u, u.shape[0] - 1, 0))
    return b_ref[...] + um1 * w_ref[0:1, :] + u * w_ref[1:2, :] + up1 * w_ref[2:3, :]


def _shortconv_t_kernel(u_ref, up_ref, un_ref, w_ref, b_ref, o_ref, *, n_tiles):
    i = pl.program_id(1)
    prev_row = jnp.where(i > 0, up_ref[SUBLANES - 1:SUBLANES, :], 0.0)
    next_row = jnp.where(i < n_tiles - 1, un_ref[0:1, :], 0.0)
    v = _short_conv_rows(u_ref[...], prev_row, next_row, w_ref, b_ref)
    for part in range(HY_ORDER + 1):
        for r in range(v.shape[0] // LANES):
            o_ref[part, r] = v[r * LANES:(r + 1) * LANES, part * HY_WIDTH:(part + 1) * HY_WIDTH].T


def short_conv_transposed(hy, conv_w, conv_b, *, n_batch, seq, tm):
    nt = seq // tm
    per8 = tm // SUBLANES
    a_rows = seq // LANES
    prev = lambda b, i: (jnp.maximum((b * nt + i) * per8 - 1, 0), 0)
    nxt = lambda b, i: (jnp.minimum((b * nt + i + 1) * per8, n_batch * nt * per8 - 1), 0)
    return pl.pallas_call(
        functools.partial(_shortconv_t_kernel, n_tiles=nt),
        grid=(n_batch, nt),
        in_specs=[
            pl.BlockSpec((tm, HY_COLS), lambda b, i: (b * nt + i, 0)),
            pl.BlockSpec((SUBLANES, HY_COLS), prev),
            pl.BlockSpec((SUBLANES, HY_COLS), nxt),
            pl.BlockSpec((3, HY_COLS), lambda b, i: (0, 0)),
            pl.BlockSpec((1, HY_COLS), lambda b, i: (0, 0)),
        ],
        out_specs=pl.BlockSpec((HY_ORDER + 1, None, tm // LANES, HY_WIDTH, LANES), lambda b, i: (0, b, i, 0, 0)),
        out_shape=jax.ShapeDtypeStruct((HY_ORDER + 1, n_batch, a_rows, HY_WIDTH, LANES), F32),
        compiler_params=_cparams("arbitrary", "arbitrary"),
        name="short_conv_transposed",
    )(hy, hy, hy, conv_w, conv_b)


def _untranspose_kernel(z_ref, o_ref):
    for r in range(z_ref.shape[0]):
        o_ref[r * LANES:(r + 1) * LANES, :] = z_ref[r].T.astype(BF16)


def untranspose(zt, *, n_batch, seq):
    a_rows = seq // LANES
    ta = min(SUBLANES, a_rows)
    nt = a_rows // ta
    return pl.pallas_call(
        _untranspose_kernel,
        grid=(n_batch, nt),
        in_specs=[pl.BlockSpec((None, ta, HY_WIDTH, LANES), lambda b, i: (b, i, 0, 0))],
        out_specs=pl.BlockSpec((ta * LANES, HY_WIDTH), lambda b, i: (b * nt + i, 0)),
        out_shape=jax.ShapeDtypeStruct((n_batch * seq, HY_WIDTH), BF16),
        compiler_params=_cparams("arbitrary", "arbitrary"),
        name="untranspose",
    )(zt)


def hyena_latent(hy, conv_w, conv_b, hy_bias, filt_t, consts, *, n_batch, seq):
    assert n_batch == 2, "the two batches are packed as real / imaginary parts of one DFT"
    n1, a_rows = consts["n1"], consts["a_rows"]
    cols = HY_WIDTH * LANES
    ct = min(cols, 2048)
    kb = min(n1, 4)
    parts = short_conv_transposed(hy, conv_w, conv_b, n_batch=n_batch, seq=seq, tm=min(seq, 1024))
    parts = parts.reshape(HY_ORDER + 1, n_batch * a_rows, cols)
    yf = left_matmul(consts["m1_real"], filt_t.reshape(a_rows, 2 * HY_ORDER * cols), ct)
    hr, hi = filter_spectrum(yf, consts, 1)
    z = parts[HY_ORDER]
    for o in range(HY_ORDER):
        y = left_matmul(consts["m1_data"], z, ct)
        zr, zi = freq_multiply(y, hr, hi, o, consts, kb)
        bias_cols = jnp.repeat(hy_bias[o], LANES)[None, :]
        z = left_matmul_gate(consts["m3"], zr.reshape(n1, cols), zi.reshape(n1, cols), parts[o], z, bias_cols, ct)
    return untranspose(z.reshape(n_batch, a_rows, HY_WIDTH, LANES), n_batch=n_batch, seq=seq)


def _ctx_hyena_kernel(u_ref, w_ref, b_ref, h_ref, bias_ref, fc_ref, fs_ref, fct_ref, fst_ref, o_ref):
    u = u_ref[...]
    zero = jnp.zeros((1, u.shape[1]), F32)
    v = _short_conv_rows(u, zero, zero, w_ref, b_ref)
    z = v[:, HY_ORDER * HY_WIDTH:]
    mm = lambda a, b: jnp.dot(a, b, precision=HI, preferred_element_type=F32)
    fc, fs = fc_ref[...], fs_ref[...]
    inv_n = 1.0 / fc.shape[0]
    for o in range(HY_ORDER):
        h0 = h_ref[:, (2 * o) * HY_WIDTH:(2 * o + 1) * HY_WIDTH]
        h1 = h_ref[:, (2 * o + 1) * HY_WIDTH:(2 * o + 2) * HY_WIDTH]
        sr, si = mm(fc, h0 + h1), mm(fs, h1 - h0)
        zr, zi = mm(fc, z), -mm(fs, z)
        gr = zr * sr - zi * si
        gi = zr * si + zi * sr
        y = (mm(fct_ref[...], gr) - mm(fst_ref[...], gi)) * inv_n
        z = v[:, o * HY_WIDTH:(o + 1) * HY_WIDTH] * (y + bias_ref[o:o + 1, :] * z)
    o_ref[...] = z.astype(BF16)


def ctx_hyena(hy, conv_w, conv_b, filt, hy_bias, *, n_batch, seq, n_ctx):
    n = 2 * n_ctx
    k = np.arange(n, dtype=np.float64)[:, None]
    m = np.arange(n_ctx, dtype=np.float64)[None, :]
    ang = 2.0 * np.pi * k * m / n
    fc, fs = jnp.asarray(np.cos(ang), F32), jnp.asarray(np.sin(ang), F32)
    ctx0 = (n_batch * seq) // n_ctx
    const = lambda b: (0, 0)
    return pl.pallas_call(
        _ctx_hyena_kernel,
        grid=(n_batch,),
        in_specs=[
            pl.BlockSpec((n_ctx, HY_COLS), lambda b: (ctx0 + b, 0)),
            pl.BlockSpec((3, HY_COLS), const), pl.BlockSpec((1, HY_COLS), const),
            pl.BlockSpec((n_ctx, 2 * HY_ORDER * HY_WIDTH), const),
            pl.BlockSpec((HY_ORDER, HY_WIDTH), const),
            pl.BlockSpec((n, n_ctx), const), pl.BlockSpec((n, n_ctx), const),
            pl.BlockSpec((n_ctx, n), const), pl.BlockSpec((n_ctx, n), const),
        ],
        out_specs=pl.BlockSpec((n_ctx, HY_WIDTH), lambda b: (b, 0)),
        out_shape=jax.ShapeDtypeStruct((n_batch * n_ctx, HY_WIDTH), BF16),
        compiler_params=_cparams("arbitrary"),
        name="ctx_hyena",
    )(hy, conv_w, conv_b, filt, hy_bias, fc, fs, fc.T, fs.T)


def _outproj_kernel(x_ref, yh_ref, yw_ref, yd_ref, mod_ref, wo_ref, g_ref, wq_ref, xo_ref, h2_ref, qp_ref):
    y = jnp.dot(yh_ref[...], wo_ref[0:HY_WIDTH, :], preferred_element_type=F32)
    y = y + jnp.dot(yw_ref[...], wo_ref[HY_WIDTH:HY_WIDTH + WA_Q_COLS, :], preferred_element_type=F32)
    y = y + jnp.dot(yd_ref[...], wo_ref[HY_WIDTH + WA_Q_COLS:, :], preferred_element_type=F32)
    xn = x_ref[...] + mod_ref[2:3, :] * y
    xo_ref[...] = xn
    h2 = _rmsnorm_mod(xn, g_ref[...], mod_ref[3:4, :], mod_ref[4:5, :]).astype(BF16)
    h2_ref[...] = h2
    qp_ref[...] = jnp.dot(h2, wq_ref[...], preferred_element_type=F32)


def out_proj(x_all, y_hy, y_wa, y_da, mod, wo_bf, g2, wq_bf, *, n_rows, n_batch, seq, tm):
    d = x_all.shape[1]
    tpb = seq // tm
    grp = functools.partial(_group_of_tile, tiles_per_batch=tpb, n_batch=n_batch)
    row = lambda i: (i, 0)
    const = lambda i: (0, 0)
    mix = HY_WIDTH + WA_Q_COLS + DA_V_COLS
    return pl.pallas_call(
        _outproj_kernel,
        grid=(n_rows // tm,),
        in_specs=[
            pl.BlockSpec((tm, d), row), pl.BlockSpec((tm, HY_WIDTH), row), pl.BlockSpec((tm, WA_Q_COLS), row),
            pl.BlockSpec((tm, DA_V_COLS), row),
            pl.BlockSpec((None, N_MOD, d), lambda i: (grp(i), 0, 0)),
            pl.BlockSpec((mix, d), const), pl.BlockSpec((1, d), const), pl.BlockSpec((d, PEER_QCOLS), const),
        ],
        out_specs=[pl.BlockSpec((tm, d), row), pl.BlockSpec((tm, d), row), pl.BlockSpec((tm, PEER_QCOLS), row)],
        out_shape=[jax.ShapeDtypeStruct((n_rows, d), F32), jax.ShapeDtypeStruct((n_rows, d), BF16),
                   jax.ShapeDtypeStruct((n_rows, PEER_QCOLS), F32)],
        compiler_params=_cparams("arbitrary"),
        name="out_proj",
    )(x_all, y_hy, y_wa, y_da, mod, wo_bf, g2, wq_bf)


def _sort_pairs(n):
    pairs = []

    def merge(lo, cnt, r):
        step = r * 2
        if step < cnt:
            merge(lo, cnt, step)
            merge(lo + r, cnt, step)
            for i in range(lo + r, lo + cnt - r, step):
                pairs.append((i, i + r))
        else:
            pairs.append((lo, lo + r))

    def sort(lo, cnt):
        if cnt > 1:
            half = cnt // 2
            sort(lo, half)
            sort(lo + half, half)
            merge(lo, cnt, 1)

    sort(0, n)
    return pairs


def _sort_desc(vals):
    n = 1
    while n < len(vals):
        n *= 2
    v = list(vals) + [None] * (n - len(vals))
    for i, j in _sort_pairs(n):
        a, b = v[i], v[j]
        if b is None:
            continue
        if a is None:
            v[i], v[j] = b, None
        else:
            v[i], v[j] = jnp.maximum(a, b), jnp.minimum(a, b)
    return v[:len(vals)]


def _top16_rows(s):
    k = PEER_TOPK
    v = _sort_desc([s[SUBLANES * r:SUBLANES * (r + 1), :] for r in range(s.shape[0] // SUBLANES)])
    for shift in (4, 2, 1):
        other = [pltpu.roll(x, shift, 0) for x in v]
        v = [jnp.maximum(v[i], other[k - 1 - i]) for i in range(k)]
        d = k // 2
        while d >= 1:
            for i in range(k):
                if (i & d) == 0:
                    a, b = v[i], v[i + d]
                    v[i], v[i + d] = jnp.maximum(a, b), jnp.minimum(a, b)
            d //= 2
    return v


def _peer_topk_kernel(q_ref, keys_ref, cnt_ref, e1_ref, rank_ref, e2_ref):
    k = PEER_TOPK
    tmk = q_ref.shape[0]
    sub = lax.broadcasted_iota(jnp.int32, (SUBLANES, tmk), 0)
    scores, tops = [], []
    for hp in range(2 * PEER_HEADS):
        s = _mm3(keys_ref[hp], q_ref[:, hp * N_KEYS:(hp + 1) * N_KEYS], nt=True)
        scores.append(s)
        top = _top16_rows(s)
        top.append(jnp.max(jnp.where(s < top[k - 1][0:1, :], s, NEG_BIG), axis=0, keepdims=True))
        tops.append(top)
    packed = []
    for p in range(2):
        lst = []
        for r in range(k + 1):
            acc = jnp.broadcast_to(tops[p][r], (SUBLANES, tmk))
            for h in range(1, PEER_HEADS):
                acc = jnp.where(sub == h, tops[2 * h + p][r], acc)
            lst.append(acc)
        packed.append(lst)
    cand = [packed[0][i] + packed[1][j] for i in range(k) for j in range(k) if (i + 1) * (j + 1) <= k]
    cs = _sort_desc(cand)
    c17 = jnp.maximum(cs[k], jnp.maximum(packed[0][k] + packed[1][0], packed[0][0] + packed[1][k]))
    tau = 0.5 * (cs[k - 1] + c17)
    zsum = jnp.zeros_like(tau)
    for r in range(k):
        zsum = zsum + jnp.exp(cs[r] - cs[0])
    inv_z = 1.0 / zsum
    for h in range(PEER_HEADS):
        s1, s2 = scores[2 * h], scores[2 * h + 1]
        row = lambda x: x[h:h + 1, :]
        top2 = [t2[0:1, :] for t2 in tops[2 * h + 1][:k]]
        cnt_ref[h] = _count_sorted(top2, row(tau) - s1, strict=False)
        tiled = lambda x: x.astype(BF16).reshape(N_KEYS // BF16_ROWS, BF16_ROWS, tmk)
        rank_ref[h] = tiled(_count_sorted(top2, s2, strict=True))
        e1_ref[h] = 0.5 * jnp.exp(s1 - row(packed[0][0]))
        e2_ref[h] = tiled(jnp.exp(s2 - row(packed[1][0])) * row(inv_z))


def _count_sorted(tops, x, *, strict):
    ge = (lambda a: a > x) if strict else (lambda a: a >= x)
    b3 = ge(tops[7])
    b2 = ge(jnp.where(b3, tops[11], tops[3]))
    b1 = ge(jnp.where(b3, jnp.where(b2, tops[13], tops[9]), jnp.where(b2, tops[5], tops[1])))
    hi = jnp.where(b2, jnp.where(b1, tops[14], tops[12]), jnp.where(b1, tops[10], tops[8]))
    lo = jnp.where(b2, jnp.where(b1, tops[6], tops[4]), jnp.where(b1, tops[2], tops[0]))
    b0 = ge(jnp.where(b3, hi, lo))
    cnt = (jnp.where(b3, 8.0, 0.0) + jnp.where(b2, 4.0, 0.0)) + (jnp.where(b1, 2.0, 0.0) + jnp.where(b0, 1.0, 0.0))
    return jnp.where(ge(tops[15]), 16.0, cnt)


def peer_topk(qp, keys, *, tmk):
    t = qp.shape[0]
    tiles = N_KEYS // BF16_ROWS
    out = pl.BlockSpec((PEER_HEADS, N_KEYS, tmk), lambda i: (0, 0, i))
    out16 = pl.BlockSpec((PEER_HEADS, tiles, BF16_ROWS, tmk), lambda i: (0, 0, 0, i))
    f32 = jax.ShapeDtypeStruct((PEER_HEADS, N_KEYS, t), F32)
    b16 = jax.ShapeDtypeStruct((PEER_HEADS, tiles, BF16_ROWS, t), BF16)
    return pl.pallas_call(
        _peer_topk_kernel,
        grid=(t // tmk,),
        in_specs=[pl.BlockSpec((tmk, PEER_QCOLS), lambda i: (i, 0)),
                  pl.BlockSpec((2 * PEER_HEADS, N_KEYS, PEER_QDIM // 2), lambda i: (0, 0, 0))],
        out_specs=[out, out, out16, out16],
        out_shape=[f32, f32, b16, b16],
        compiler_params=_cparams("arbitrary"),
        name="peer_topk",
    )(qp, keys)


def _peer_dense_kernel(x_ref, h_ref, mod_ref, u_ref, vt_ref, cnt_ref, e1_ref, rank_ref, e2_ref, g_ref, o_ref,
                       a_scr, w_scr, acc_scr, *, final_norm, chunk, col_block):
    e = pl.program_id(1)

    @pl.when(e == 0)
    def _():
        acc_scr[...] = jnp.zeros(acc_scr.shape, F32)

    n_chunks = u_ref.shape[0] // chunk
    per_chunk = chunk // N_KEYS
    sqrt_half = math.sqrt(0.5)
    tm = h_ref.shape[0]
    tiles = N_KEYS // BF16_ROWS
    zero = jnp.zeros((), BF16)

    slots = a_scr.shape[0]
    ahead = slots - 1

    def pre_activations(p):
        a_scr[p % slots] = lax.dot_general(u_ref[p * chunk:(p + 1) * chunk, :], h_ref[...], (((1,), (1,)), ((), ())),
                                           preferred_element_type=F32)

    for p in range(min(ahead, n_chunks)):
        pre_activations(p)
    for p in range(n_chunks):
        if p + ahead < n_chunks:
            pre_activations(p + ahead)
        for ii, cb in [(ii, cb) for ii in range(p * per_chunk, (p + 1) * per_chunk) for cb in range(tm // col_block)]:
            rows = slice((ii - p * per_chunk) * N_KEYS, (ii - p * per_chunk + 1) * N_KEYS)
            cols = slice(cb * col_block, (cb + 1) * col_block)
            a = a_scr[p % slots, rows, cols]
            act = a * (1.0 + lax.erf(a * sqrt_half))
            act = act.astype(BF16).reshape(tiles, BF16_ROWS, col_block)
            gate = None
            for h in range(PEER_HEADS):
                cnt = jnp.broadcast_to(cnt_ref[h, ii:ii + 1, cols], (BF16_ROWS, col_block)).astype(BF16)
                e1 = jnp.broadcast_to(e1_ref[h, ii:ii + 1, cols], (BF16_ROWS, col_block)).astype(BF16)
                term = jnp.where(rank_ref[h, :, :, cols] < cnt[None], e2_ref[h, :, :, cols] * e1[None], zero)
                gate = term if gate is None else gate + term
            w_scr[p % 2, rows, cols] = (gate * act).reshape(N_KEYS, col_block)
        acc_scr[...] += jnp.dot(vt_ref[:, p * chunk:(p + 1) * chunk], w_scr[p % 2], preferred_element_type=F32)

    @pl.when(e == pl.num_programs(1) - 1)
    def _():
        xn = x_ref[...] + mod_ref[5:6, :] * acc_scr[...].T
        if final_norm:
            ms = jnp.mean(xn * xn, axis=-1, keepdims=True)
            xn = xn * lax.rsqrt(ms + NORM_EPS) * g_ref[...]
        o_ref[...] = xn


def peer_dense(x, h2, mod, u_bf, vt_bf, cnt, e1, rank, e2, g_final, *, n_batch, seq, tm, te, chunk, ahead, col_block,
               final_norm):
    n_rows, d = x.shape
    tpb = seq // tm
    grp = functools.partial(_group_of_tile, tiles_per_batch=tpb, n_batch=n_batch)
    n_i = te // N_KEYS
    tok = lambda i, e: (i, 0)
    rows_i = pl.BlockSpec((PEER_HEADS, n_i, tm), lambda i, e: (0, e, i))
    tiles = N_KEYS // BF16_ROWS
    full = pl.BlockSpec((PEER_HEADS, tiles, BF16_ROWS, tm), lambda i, e: (0, 0, 0, i))
    return pl.pallas_call(
        functools.partial(_peer_dense_kernel, final_norm=final_norm, chunk=chunk, col_block=col_block),
        grid=(n_rows // tm, N_EXPERTS // te),
        in_specs=[
            pl.BlockSpec((tm, d), tok), pl.BlockSpec((tm, d), tok),
            pl.BlockSpec((None, N_MOD, d), lambda i, e: (grp(i), 0, 0)),
            pl.BlockSpec((te, d), lambda i, e: (e, 0)),
            pl.BlockSpec((d, te), lambda i, e: (0, e)),
            rows_i, rows_i, full, full,
            pl.BlockSpec((1, d), lambda i, e: (0, 0)),
        ],
        out_specs=pl.BlockSpec((tm, d), tok),
        out_shape=jax.ShapeDtypeStruct((n_rows, d), F32),
        scratch_shapes=[pltpu.VMEM((ahead + 1, chunk, tm), F32), pltpu.VMEM((2, chunk, tm), BF16),
                        pltpu.VMEM((d, tm), F32)],
        compiler_params=_cparams("arbitrary", "arbitrary", vmem_limit_bytes=PEER_VMEM_LIMIT_BYTES),
        name="peer_dense",
    )(x, h2, mod, u_bf, vt_bf, cnt, e1, rank, e2, g_final)


def kernel(x, c, ctx, c_ctx, w_mod, b_mod, norm1_g, w_in, hy_conv_w, hy_conv_b, hy_w1, hy_b1, hy_freq, hy_w2,
           hy_b2, hy_w3, hy_decay, hy_bias, wa_sink, da_lq1, da_lk1, da_lq2, da_lk2, da_norm_g, w_out, norm2_g,
           peer_wq, peer_keys, peer_u, peer_v, final_g):
    n_batch, seq, d = x.shape
    n_ctx = ctx.shape[1]
    depth = w_in.shape[0]
    n_lat = n_batch * seq
    tm = 512 if seq % 512 == 0 else 256
    assert seq % tm == 0 and (n_batch * n_ctx) % tm == 0 and n_lat % n_ctx == 0 and n_batch + 1 <= SUBLANES

    cpad = jnp.zeros((SUBLANES, d), F32).at[:n_batch].set(c).at[n_batch].set(c_ctx)
    mod_all = mod_vectors(cpad, w_mod, b_mod).reshape(depth, SUBLANES, N_MOD, d)
    rope_wa = rope_tables(seq, HEAD_DIM, tm)
    rope_da = rope_tables(seq, DA_QK_DIM, tm)
    consts = dft_constants(seq)
    feat_lat = hyena_filter_features(seq)
    feat_ctx = hyena_filter_features(n_ctx)
    x_all = jnp.concatenate([x.reshape(n_lat, d), ctx.reshape(n_batch * n_ctx, d)], axis=0)

    out = None
    for li in range(depth):
        last = li == depth - 1
        lam_init = DA_LAMBDA_BASE - DA_LAMBDA_AMP * math.exp(-DA_LAMBDA_RATE * li)
        mod = mod_all[li]
        hy, qw, kw, vw, qd, kd, vd, vdt, vwt = proj_in(x_all, mod, norm1_g[li][None, :], w_in[li].astype(BF16), rope_wa,
                                             rope_da, n_batch=n_batch, seq=seq, tm=tm)
        w1p = jnp.pad(hy_w1[li], ((0, LANES - HY_EMB_DIM), (0, 0)))
        fargs = (w1p, hy_b1[li][None, :], hy_freq[li][None, :], hy_w2[li], hy_b2[li][None, :], hy_w3[li],
                 hy_decay[li].reshape(1, -1))
        conv_b = hy_conv_b[li][None, :]
        filt_t = hyena_filters(feat_lat, *fargs, transposed=True)
        y_hy = hyena_latent(hy, hy_conv_w[li], conv_b, hy_bias[li], filt_t, consts, n_batch=n_batch, seq=seq)
        y_wa = window_attention(wa_sink[li], qw, kw, vwt, n_batch=n_batch, seq=seq, n_ctx=n_ctx)
        lq = jnp.stack([da_lq1[li], da_lk1[li], da_lq2[li], da_lk2[li]], axis=0)
        g128 = jnp.tile(da_norm_g[li], LANES // DA_V_DIM)[None, :]
        tq = min(seq, 512)
        g_rows = jnp.broadcast_to(da_norm_g[li][:, None], (DA_V_DIM, tq))
        y_da = diff_attention(lq, g_rows, qd, kd, vdt, n_batch=n_batch, seq=seq, n_ctx=n_ctx, lam_init=lam_init,
                              tq=tq, tk=min(seq, 256), ahead=2, unroll=8 if seq % 2048 == 0 else 1)
        if not last:
            filt_c = hyena_filters(feat_ctx, *fargs, transposed=False)
            yc_hy = ctx_hyena(hy, hy_conv_w[li], conv_b, filt_c, hy_bias[li], n_batch=n_batch, seq=seq, n_ctx=n_ctx)
            yc_wa, yc_da = ctx_attention(wa_sink[li], lq, g128, qw, kw, vw, qd, kd, vd, n_batch=n_batch, seq=seq,
                                         n_ctx=n_ctx, lam_init=lam_init)
            y_hy = jnp.concatenate([y_hy, yc_hy], axis=0)
            y_wa = jnp.concatenate([y_wa, yc_wa], axis=0)
            y_da = jnp.concatenate([y_da, yc_da], axis=0)
        n_rows = n_lat if last else x_all.shape[0]
        x_mid, h2, qp = out_proj(x_all, y_hy, y_wa, y_da, mod, w_out[li].astype(BF16), norm2_g[li][None, :],
                                 peer_wq[li].astype(BF16), n_rows=n_rows, n_batch=n_batch, seq=seq, tm=tm)
        keys = peer_keys[li].reshape(2 * PEER_HEADS, N_KEYS, PEER_QDIM // 2)
        cnt, e1, rank, e2 = peer_topk(qp, keys, tmk=256)
        x_all = peer_dense(x_mid, h2, mod, peer_u[li].astype(BF16), peer_v[li].T.astype(BF16), cnt, e1, rank, e2,
                           final_g[None, :], n_batch=n_batch, seq=seq, tm=tm, te=4096, chunk=1024, ahead=3, col_block=tm,
                           final_norm=last)
        out = x_all
    return out.reshape(n_batch, seq, d)
```

```python
import functools
import math

import numpy as np
import jax
import jax.numpy as jnp
from jax import lax
from jax.experimental import pallas as pl
from jax.experimental.pallas import tpu as pltpu

F32 = jnp.float32
BF16 = jnp.bfloat16
HI = lax.Precision.HIGHEST

LANES = 128
SUBLANES = 8
BF16_ROWS = 16
VMEM_LIMIT_BYTES = 56 * 1024 * 1024
PEER_VMEM_LIMIT_BYTES = 62 * 1024 * 1024

D_MODEL = 1024
N_MOD = 6
NORM_EPS = 1e-6
ROPE_BASE = 10000.0
GRID_W = 64
BLOCK = 128
HEAD_DIM = 64
HY_WIDTH = 256
HY_ORDER = 2
HY_BANDS = 16
HY_EMB_DIM = 1 + 2 * HY_BANDS
HY_FILTER_WIDTH = 64
HY_COLS = 3 * HY_WIDTH
WA_Q_HEADS = 8
WA_KV_HEADS = 2
WA_GROUP = 4
WINDOW = 128
WA_Q_COLS = WA_Q_HEADS * HEAD_DIM
WA_KV_COLS = WA_KV_HEADS * HEAD_DIM
WA_COLS = WA_Q_COLS + 2 * WA_KV_COLS
DA_HEADS = 4
DA_QK_DIM = 32
DA_V_DIM = 64
DA_QK_COLS = DA_HEADS * 2 * DA_QK_DIM
DA_V_COLS = DA_HEADS * DA_V_DIM
DA_COLS = 2 * DA_QK_COLS + DA_V_COLS
DA_LAMBDA_BASE = 0.8
DA_LAMBDA_AMP = 0.6
DA_LAMBDA_RATE = 0.3
IN_COLS = HY_COLS + WA_COLS + DA_COLS
PEER_HEADS = 8
N_KEYS = 128
N_EXPERTS = N_KEYS * N_KEYS
PEER_TOPK = 16
PEER_QDIM = 256
PEER_QCOLS = PEER_HEADS * PEER_QDIM
NEG_BIG = -1e30
LOG2E = math.log2(math.e)


def _cparams(*sem, vmem_limit_bytes=VMEM_LIMIT_BYTES):
    return pltpu.CompilerParams(dimension_semantics=sem, vmem_limit_bytes=vmem_limit_bytes)


def _rmsnorm_mod(x, g, shift, scale):
    ms = jnp.mean(x * x, axis=-1, keepdims=True)
    return (x * lax.rsqrt(ms + NORM_EPS) * g) * (1.0 + scale) + shift


def _mod_kernel(c_ref, w_ref, b_ref, o_ref):
    c = c_ref[...]
    act = c * jax.nn.sigmoid(c)
    o_ref[...] = _mm3(act, w_ref[...]) + b_ref[...]


def mod_vectors(cpad, w_mod, b_mod):
    n_layers, d, n = w_mod.shape
    tn = 1536
    return pl.pallas_call(
        _mod_kernel,
        grid=(n_layers, n // tn),
        in_specs=[
            pl.BlockSpec((SUBLANES, d), lambda l, j: (0, 0)),
            pl.BlockSpec((None, d, tn), lambda l, j: (l, 0, j)),
            pl.BlockSpec((None, 1, tn), lambda l, j: (l, 0, j)),
        ],
        out_specs=pl.BlockSpec((None, SUBLANES, tn), lambda l, j: (l, 0, j)),
        out_shape=jax.ShapeDtypeStruct((n_layers, SUBLANES, n), F32),
        compiler_params=_cparams("arbitrary", "arbitrary"),
        name="mod_vectors",
    )(cpad, w_mod, b_mod.reshape(n_layers, 1, n))


def _rope128(x, cos, sin_a, sin_b, m):
    return x * cos + pltpu.roll(x, LANES - m, 1) * sin_a + pltpu.roll(x, m, 1) * sin_b


def _proj_in_kernel(x_ref, mod_ref, g_ref, w_ref, cw_ref, saw_ref, sbw_ref, cd_ref, sad_ref, sbd_ref,
                    hy_ref, qw_ref, kw_ref, vw_ref, qd_ref, kd_ref, vd_ref, vdt_ref, vwt_ref):
    h = _rmsnorm_mod(x_ref[...], g_ref[...], mod_ref[0:1, :], mod_ref[1:2, :]).astype(BF16)
    hy_ref[...] = jnp.dot(h, w_ref[:, 0:HY_COLS], preferred_element_type=F32)
    wa = jnp.dot(h, w_ref[:, HY_COLS:HY_COLS + WA_COLS], preferred_element_type=F32)
    da = jnp.dot(h, w_ref[:, HY_COLS + WA_COLS:IN_COLS], preferred_element_type=F32)
    cw, saw, sbw = cw_ref[...], saw_ref[...], sbw_ref[...]
    cd, sad, sbd = cd_ref[...], sad_ref[...], sbd_ref[...]
    wa_scale = HEAD_DIM ** -0.5
    da_scale = DA_QK_DIM ** -0.5 * LOG2E
    for kb in range(WA_Q_COLS // LANES):
        blk = wa[:, kb * LANES:(kb + 1) * LANES]
        qw_ref[:, kb * LANES:(kb + 1) * LANES] = (_rope128(blk, cw, saw, sbw, 16) * wa_scale).astype(BF16)
    kw_ref[...] = _rope128(wa[:, WA_Q_COLS:WA_Q_COLS + WA_KV_COLS], cw, saw, sbw, 16).astype(BF16)
    vw = wa[:, WA_Q_COLS + WA_KV_COLS:WA_COLS]
    vw_ref[...] = vw.astype(BF16)
    vwt_ref[...] = vw.T.astype(BF16)
    for kb in range(DA_QK_COLS // LANES):
        blk = da[:, kb * LANES:(kb + 1) * LANES]
        qd_ref[:, kb * LANES:(kb + 1) * LANES] = (_rope128(blk, cd, sad, sbd, 8) * da_scale).astype(BF16)
        blk = da[:, DA_QK_COLS + kb * LANES:DA_QK_COLS + (kb + 1) * LANES]
        kd_ref[:, kb * LANES:(kb + 1) * LANES] = _rope128(blk, cd, sad, sbd, 8).astype(BF16)
    vd = da[:, 2 * DA_QK_COLS:DA_COLS]
    vd_ref[...] = vd.astype(BF16)
    vdt_ref[...] = vd.T.astype(BF16)


def _group_of_tile(i, tiles_per_batch, n_batch):
    return jnp.minimum(i // tiles_per_batch, n_batch)


def proj_in(x_all, mod, g, w_bf, rope_wa, rope_da, *, n_batch, seq, tm):
    t_all, d = x_all.shape
    tpb = seq // tm
    grp = functools.partial(_group_of_tile, tiles_per_batch=tpb, n_batch=n_batch)
    pos = lambda i: (jnp.where(i < n_batch * tpb, i % tpb, tpb), 0)
    row = lambda i: (i, 0)
    const = lambda i: (0, 0)
    tab = pl.BlockSpec((tm, LANES), pos)
    outs = [(HY_COLS, F32), (WA_Q_COLS, BF16), (WA_KV_COLS, BF16), (WA_KV_COLS, BF16),
            (DA_QK_COLS, BF16), (DA_QK_COLS, BF16), (DA_V_COLS, BF16)]
    return pl.pallas_call(
        _proj_in_kernel,
        grid=(t_all // tm,),
        in_specs=[
            pl.BlockSpec((tm, d), row),
            pl.BlockSpec((None, N_MOD, d), lambda i: (grp(i), 0, 0)),
            pl.BlockSpec((1, d), const),
            pl.BlockSpec((d, IN_COLS), const),
            tab, tab, tab, tab, tab, tab,
        ],
        out_specs=[pl.BlockSpec((tm, n), row) for n, _ in outs]
        + [pl.BlockSpec((DA_V_COLS, tm), lambda i: (0, i)), pl.BlockSpec((WA_KV_COLS, tm), lambda i: (0, i))],
        out_shape=[jax.ShapeDtypeStruct((t_all, n), dt) for n, dt in outs]
        + [jax.ShapeDtypeStruct((DA_V_COLS, t_all), BF16), jax.ShapeDtypeStruct((WA_KV_COLS, t_all), BF16)],
        compiler_params=_cparams("arbitrary"),
        name="proj_in",
    )(x_all, mod, g, w_bf, *rope_wa, *rope_da)


def rope_tables(seq, head_dim, pad_rows):
    half = head_dim // 2
    axis_dim = half
    m = axis_dim // 2
    f32 = np.float32
    t = np.arange(seq)
    rowcol = np.stack([t // GRID_W, t % GRID_W], axis=0).astype(f32)
    inv_freq = f32(ROPE_BASE) ** (-np.arange(0, axis_dim, 2, dtype=f32) / f32(axis_dim))
    ang = rowcol[:, :, None] * inv_freq[None, None, :]
    cos, sin = np.cos(ang), np.sin(ang)
    lane = np.arange(LANES)
    dd = lane % head_dim
    axis = dd // half
    sub = dd % half
    idx = sub % m
    first = sub < m
    cos_t = cos[axis, :, idx].T
    sin_t = sin[axis, :, idx].T
    sin_a = np.where(first[None, :], -sin_t, f32(0))
    sin_b = np.where(first[None, :], f32(0), sin_t)
    pad1 = np.ones((pad_rows, LANES), f32)
    pad0 = np.zeros((pad_rows, LANES), f32)
    tables = (np.concatenate([cos_t, pad1], 0), np.concatenate([sin_a, pad0], 0), np.concatenate([sin_b, pad0], 0))
    return tuple(jnp.asarray(x, F32) for x in tables)


def _softmax_sink_attend(q, k, v, sink, valid):
    s = lax.dot_general(q, k, (((1,), (1,)), ((), ())), preferred_element_type=F32)
    if valid is not None:
        nm = valid.shape[1]
        s = jnp.concatenate([jnp.where(valid, s[:, :nm], NEG_BIG), s[:, nm:]], axis=1)
    m = jnp.maximum(jnp.max(s, axis=1, keepdims=True), sink)
    p = jnp.exp(s - m)
    den = jnp.sum(p, axis=1, keepdims=True) + jnp.exp(sink - m)
    return jnp.dot(p.astype(BF16), v, preferred_element_type=F32) / den


def _wattn_kernel(sink_ref, q_ref, kp_ref, kc_ref, kn_ref, kx_ref, vp_ref, vc_ref, vn_ref, vx_ref, o_ref, qm_scr,
                  *, n_pairs, ahead):
    m_idx = pl.program_id(1)
    nq = 2 * BLOCK
    n_loc = 4 * BLOCK
    kk = lax.broadcasted_iota(jnp.int32, (n_loc, nq), 0)
    qq = lax.broadcasted_iota(jnp.int32, (n_loc, nq), 1)
    lo = jnp.where(m_idx > 0, 0, BLOCK)
    hi = jnp.where(m_idx < n_pairs - 1, n_loc, n_loc - BLOCK)
    valid = (kk >= jnp.maximum(qq, lo)) & (kk <= qq + 2 * WINDOW) & (kk < hi)
    k_all = jnp.concatenate([kp_ref[...], kc_ref[...], kn_ref[...], kx_ref[...]], axis=0)
    vt_all = jnp.concatenate([vp_ref[...], vc_ref[...], vn_ref[...], vx_ref[...]], axis=1)
    lane = lax.broadcasted_iota(jnp.int32, (nq, LANES), 1)
    for h in range(WA_Q_HEADS):
        hk = h // WA_GROUP
        blk = q_ref[:, (h // 2) * LANES:(h // 2 + 1) * LANES]
        if h % 2 != hk:
            blk = jnp.concatenate([blk[:, HEAD_DIM:], blk[:, :HEAD_DIM]], axis=1)
        keep = (lane >= hk * HEAD_DIM) & (lane < (hk + 1) * HEAD_DIM)
        qm_scr[h] = jnp.where(keep, blk, jnp.zeros_like(blk))

    def scores(h):
        return lax.dot_general(k_all, qm_scr[h], (((1,), (1,)), ((), ())), preferred_element_type=F32)

    sts = [scores(h) for h in range(ahead)]
    outs = []
    for h in range(WA_Q_HEADS):
        hk = h // WA_GROUP
        if h + ahead < WA_Q_HEADS:
            sts.append(scores(h + ahead))
        st = sts[h]
        sts[h] = None
        st = jnp.concatenate([jnp.where(valid, st[:n_loc], NEG_BIG), st[n_loc:]], axis=0)
        sink = sink_ref[h]
        m = jnp.maximum(jnp.max(st, axis=0, keepdims=True), sink)
        p = jnp.exp(st - m)
        den = jnp.sum(p, axis=0, keepdims=True) + jnp.exp(sink - m)
        o = jnp.dot(vt_all[hk * HEAD_DIM:(hk + 1) * HEAD_DIM, :], p.astype(BF16), preferred_element_type=F32)
        outs.append(o / den)
    o_ref[...] = jnp.concatenate(outs, axis=0).T.astype(BF16)


def window_attention(sink, qw, kw, vwt, *, n_batch, seq, n_ctx):
    nb = seq // BLOCK
    n_pairs = nb // 2
    assert nb % 2 == 0
    ctx0 = (n_batch * seq) // n_ctx
    cur = lambda b, m: (b * n_pairs + m, 0)
    prev = lambda b, m: (b * nb + jnp.maximum(2 * m - 1, 0), 0)
    nxt = lambda b, m: (b * nb + jnp.minimum(2 * m + 2, nb - 1), 0)
    ctx = lambda b, m: (ctx0 + b, 0)
    swap = lambda f: (lambda b, m: f(b, m)[::-1])
    return pl.pallas_call(
        functools.partial(_wattn_kernel, n_pairs=n_pairs, ahead=3),
        grid=(n_batch, n_pairs),
        in_specs=[
            pl.BlockSpec(memory_space=pltpu.SMEM),
            pl.BlockSpec((2 * BLOCK, WA_Q_COLS), cur),
            pl.BlockSpec((BLOCK, WA_KV_COLS), prev), pl.BlockSpec((2 * BLOCK, WA_KV_COLS), cur),
            pl.BlockSpec((BLOCK, WA_KV_COLS), nxt), pl.BlockSpec((n_ctx, WA_KV_COLS), ctx),
            pl.BlockSpec((WA_KV_COLS, BLOCK), swap(prev)), pl.BlockSpec((WA_KV_COLS, 2 * BLOCK), swap(cur)),
            pl.BlockSpec((WA_KV_COLS, BLOCK), swap(nxt)), pl.BlockSpec((WA_KV_COLS, n_ctx), swap(ctx)),
        ],
        out_specs=pl.BlockSpec((2 * BLOCK, WA_Q_COLS), cur),
        out_shape=jax.ShapeDtypeStruct((n_batch * seq, WA_Q_COLS), BF16),
        scratch_shapes=[pltpu.VMEM((WA_Q_HEADS, 2 * BLOCK, LANES), BF16)],
        compiler_params=_cparams("arbitrary", "arbitrary"),
        name="window_attention",
    )(sink, qw, kw, kw, kw, kw, vwt, vwt, vwt, vwt)


def _diff_lambda(lq_ref, lam_init):
    a = jnp.sum(lq_ref[0:1, :] * lq_ref[1:2, :], axis=1, keepdims=True)
    b = jnp.sum(lq_ref[2:3, :] * lq_ref[3:4, :], axis=1, keepdims=True)
    return jnp.exp(a) - jnp.exp(b) + lam_init


def _diff_finish(acc1, l1, acc2, l2, lam, g128, lam_init):
    lane = lax.broadcasted_iota(jnp.int32, acc1[0].shape, 1)
    lo = lane < DA_V_DIM
    blocks = []
    for vb in range(DA_HEADS // 2):
        o = []
        for h in (2 * vb, 2 * vb + 1):
            o.append(acc1[h] / l1[h] - lam * (acc2[h] / l2[h]))
        blk = jnp.where(lo, o[0], o[1])
        sq = blk * blk
        s_lo = jnp.sum(jnp.where(lo, sq, 0.0), axis=1, keepdims=True)
        s_hi = jnp.sum(jnp.where(lo, 0.0, sq), axis=1, keepdims=True)
        ms = jnp.where(lo, s_lo, s_hi) * (1.0 / DA_V_DIM)
        blocks.append(blk * lax.rsqrt(ms + NORM_EPS) * g128 * (1.0 - lam_init))
    return jnp.concatenate(blocks, axis=1)


def _masked_q(q_ref, hm):
    kb, sub = divmod(hm, LANES // DA_QK_DIM)
    blk = q_ref[:, kb * LANES:(kb + 1) * LANES]
    lane = lax.broadcasted_iota(jnp.int32, blk.shape, 1)
    keep = (lane >= sub * DA_QK_DIM) & (lane < (sub + 1) * DA_QK_DIM)
    return jnp.where(keep, blk, jnp.zeros_like(blk))


def _dattn_kernel(lq_ref, g_ref, q_ref, k_ref, vt_ref, kx_ref, vxt_ref, o_ref, qm_scr, m_scr, l_scr, acc_scr,
                  *, lam_init, tk, ahead, unroll):
    n_hm = 2 * DA_HEADS
    per_blk = LANES // DA_QK_DIM
    for hm in range(n_hm):
        qm_scr[hm] = _masked_q(q_ref, hm)
    m_scr[...] = jnp.full(m_scr.shape, NEG_BIG, F32)
    l_scr[...] = jnp.zeros(l_scr.shape, F32)
    acc_scr[...] = jnp.zeros(acc_scr.shape, F32)

    def update(chunks):
        chains = [(c, hm) for c in range(len(chunks)) for hm in range(n_hm)]

        def scores(chain):
            c, hm = chain
            kb = hm // per_blk
            return lax.dot_general(chunks[c][0][:, kb * LANES:(kb + 1) * LANES], qm_scr[hm],
                                   (((1,), (1,)), ((), ())), preferred_element_type=F32)

        sts = [scores(ch) for ch in chains[:ahead]]
        for n, (c, hm) in enumerate(chains):
            h = hm // 2
            vtc = chunks[c][1]
            if n + ahead < len(chains):
                sts.append(scores(chains[n + ahead]))
            st = sts[n]
            sts[n] = None
            m_old = m_scr[hm]
            m_new = jnp.maximum(m_old, jnp.max(st, axis=0, keepdims=True))
            alpha = jnp.exp2(m_old - m_new)
            pt = jnp.exp2(st - m_new)
            l_scr[hm] = alpha * l_scr[hm] + jnp.sum(pt, axis=0, keepdims=True)
            acc_scr[hm] = alpha * acc_scr[hm] + jnp.dot(vtc[h * DA_V_DIM:(h + 1) * DA_V_DIM, :], pt.astype(BF16),
                                                        preferred_element_type=F32)
            m_scr[hm] = m_new

    def body(j, carry):
        chunks = []
        for c in range(unroll):
            start = pl.multiple_of((j * unroll + c) * tk, tk)
            chunks.append((k_ref[pl.ds(start, tk), :], vt_ref[:, pl.ds(start, tk)]))
        update(chunks)
        return carry

    lax.fori_loop(0, k_ref.shape[0] // (tk * unroll), body, 0)
    n_x = kx_ref.shape[0] // tk
    update([(kx_ref[c * tk:(c + 1) * tk, :], vxt_ref[:, c * tk:(c + 1) * tk]) for c in range(n_x)])
    lam = _diff_lambda(lq_ref, lam_init)
    heads = []
    for h in range(DA_HEADS):
        o = acc_scr[2 * h] / l_scr[2 * h] - lam * (acc_scr[2 * h + 1] / l_scr[2 * h + 1])
        ms = jnp.mean(o * o, axis=0, keepdims=True)
        heads.append(o * lax.rsqrt(ms + NORM_EPS) * g_ref[...] * (1.0 - lam_init))
    o_ref[...] = jnp.concatenate(heads, axis=0).T.astype(BF16)


def diff_attention(lq, g_rows, qd, kd, vdt, *, n_batch, seq, n_ctx, lam_init, tq, tk, ahead, unroll):
    nq = seq // tq
    ctx0 = (n_batch * seq) // n_ctx
    n_hm = 2 * DA_HEADS
    return pl.pallas_call(
        functools.partial(_dattn_kernel, lam_init=lam_init, tk=tk, ahead=ahead, unroll=unroll),
        grid=(n_batch, nq),
        in_specs=[
            pl.BlockSpec((4, DA_QK_DIM), lambda b, i: (0, 0)),
            pl.BlockSpec((DA_V_DIM, tq), lambda b, i: (0, 0)),
            pl.BlockSpec((tq, DA_QK_COLS), lambda b, i: (b * nq + i, 0)),
            pl.BlockSpec((seq, DA_QK_COLS), lambda b, i: (b, 0)),
            pl.BlockSpec((DA_V_COLS, seq), lambda b, i: (0, b)),
            pl.BlockSpec((n_ctx, DA_QK_COLS), lambda b, i: (ctx0 + b, 0)),
            pl.BlockSpec((DA_V_COLS, n_ctx), lambda b, i: (0, ctx0 + b)),
        ],
        out_specs=pl.BlockSpec((tq, DA_V_COLS), lambda b, i: (b * nq + i, 0)),
        out_shape=jax.ShapeDtypeStruct((n_batch * seq, DA_V_COLS), BF16),
        scratch_shapes=[
            pltpu.VMEM((n_hm, tq, LANES), BF16),
            pltpu.VMEM((n_hm, 1, tq), F32),
            pltpu.VMEM((n_hm, 1, tq), F32),
            pltpu.VMEM((n_hm, DA_V_DIM, tq), F32),
        ],
        compiler_params=_cparams("arbitrary", "arbitrary"),
        name="diff_attention",
    )(lq, g_rows, qd, kd, vdt, kd, vdt)


def _ctx_attn_kernel(sink_ref, lq_ref, g_ref, qw_ref, kw_ref, vw_ref, qd_ref, kd_ref, vd_ref, ow_ref, od_ref,
                     *, lam_init):
    outs = []
    for hk in range(WA_KV_HEADS):
        sl = slice(hk * HEAD_DIM, (hk + 1) * HEAD_DIM)
        k, v = kw_ref[:, sl], vw_ref[:, sl]
        for g in range(WA_GROUP):
            h = hk * WA_GROUP + g
            outs.append(_softmax_sink_attend(qw_ref[:, h * HEAD_DIM:(h + 1) * HEAD_DIM], k, v, sink_ref[h], None))
    ow_ref[...] = jnp.concatenate(outs, axis=1).astype(BF16)
    per_blk = LANES // DA_QK_DIM
    acc, den = [], []
    for hm in range(2 * DA_HEADS):
        kb, vb = hm // per_blk, hm // 4
        s = lax.dot_general(_masked_q(qd_ref, hm), kd_ref[:, kb * LANES:(kb + 1) * LANES],
                            (((1,), (1,)), ((), ())), preferred_element_type=F32)
        p = jnp.exp2(s - jnp.max(s, axis=1, keepdims=True))
        den.append(jnp.sum(p, axis=1, keepdims=True))
        acc.append(jnp.dot(p.astype(BF16), vd_ref[:, vb * LANES:(vb + 1) * LANES], preferred_element_type=F32))
    lam = _diff_lambda(lq_ref, lam_init)
    od_ref[...] = _diff_finish(acc[0::2], den[0::2], acc[1::2], den[1::2], lam, g_ref[...], lam_init).astype(BF16)


def ctx_attention(sink, lq, g128, qw, kw, vw, qd, kd, vd, *, n_batch, seq, n_ctx, lam_init):
    ctx0 = (n_batch * seq) // n_ctx
    ctx = lambda b: (ctx0 + b, 0)
    out = lambda b: (b, 0)
    spec = lambda n, f: pl.BlockSpec((n_ctx, n), f)
    return pl.pallas_call(
        functools.partial(_ctx_attn_kernel, lam_init=lam_init),
        grid=(n_batch,),
        in_specs=[
            pl.BlockSpec(memory_space=pltpu.SMEM),
            pl.BlockSpec((4, DA_QK_DIM), lambda b: (0, 0)),
            pl.BlockSpec((1, LANES), lambda b: (0, 0)),
            spec(WA_Q_COLS, ctx), spec(WA_KV_COLS, ctx), spec(WA_KV_COLS, ctx),
            spec(DA_QK_COLS, ctx), spec(DA_QK_COLS, ctx), spec(DA_V_COLS, ctx),
        ],
        out_specs=[spec(WA_Q_COLS, out), spec(DA_V_COLS, out)],
        out_shape=[jax.ShapeDtypeStruct((n_batch * n_ctx, WA_Q_COLS), BF16),
                   jax.ShapeDtypeStruct((n_batch * n_ctx, DA_V_COLS), BF16)],
        compiler_params=_cparams("arbitrary"),
        name="ctx_attention",
    )(sink, lq, g128, qw, kw, vw, qd, kd, vd)


def _filter_kernel(feat_ref, w1_ref, b1_ref, fr_ref, w2_ref, b2_ref, w3_ref, dec_ref, o_ref, *, transposed):
    feat = feat_ref[...]
    fr = fr_ref[...]
    z = jnp.sin(fr * (_mm3(feat, w1_ref[...]) + b1_ref[...]))
    z = jnp.sin(fr * (_mm3(z, w2_ref[...]) + b2_ref[...]))
    h = _mm3(z, w3_ref[...])
    h = h * jnp.exp(-feat[:, 0:1] * jnp.abs(dec_ref[...]))
    rows = lax.broadcasted_iota(jnp.int32, h.shape, 0) + pl.program_id(0) * h.shape[0]
    cols = lax.broadcasted_iota(jnp.int32, h.shape, 1)
    h = jnp.where((rows == 0) & (cols % (2 * HY_WIDTH) >= HY_WIDTH), 0.0, h)
    if transposed:
        for r in range(h.shape[0] // LANES):
            o_ref[r] = h[r * LANES:(r + 1) * LANES, :].T
    else:
        o_ref[...] = h


def hyena_filter_features(n):
    f32 = np.float32
    t = np.linspace(0.0, 1.0, n, dtype=f32)[:, None]
    bands = np.linspace(1e-4, HY_BANDS - 1, HY_BANDS, dtype=f32)
    phase = (f32(2.0 * math.pi / n) * np.arange(n, dtype=f32)[:, None]) * bands
    feat = np.concatenate([t, np.cos(phase), -np.sin(phase)], axis=-1).astype(f32)
    return jnp.asarray(np.pad(feat, ((0, 0), (0, LANES - HY_EMB_DIM))), F32)


def hyena_filters(feat, w1p, b1, freq, w2, b2, w3, decay, *, transposed):
    n = feat.shape[0]
    tr = min(n, 512)
    ncol = w3.shape[1]
    fw = w2.shape[0]
    const = lambda i: (0, 0)
    if transposed:
        out_spec = pl.BlockSpec((tr // LANES, ncol, LANES), lambda i: (i, 0, 0))
        out_shape = jax.ShapeDtypeStruct((n // LANES, ncol, LANES), F32)
    else:
        out_spec = pl.BlockSpec((tr, ncol), lambda i: (i, 0))
        out_shape = jax.ShapeDtypeStruct((n, ncol), F32)
    return pl.pallas_call(
        functools.partial(_filter_kernel, transposed=transposed),
        grid=(n // tr,),
        in_specs=[
            pl.BlockSpec((tr, LANES), lambda i: (i, 0)),
            pl.BlockSpec((LANES, fw), const), pl.BlockSpec((1, fw), const), pl.BlockSpec((1, fw), const),
            pl.BlockSpec((fw, fw), const), pl.BlockSpec((1, fw), const),
            pl.BlockSpec((fw, ncol), const), pl.BlockSpec((1, ncol), const),
        ],
        out_specs=out_spec,
        out_shape=out_shape,
        compiler_params=_cparams("arbitrary"),
        name="hyena_filters",
    )(feat, w1p, b1, freq, w2, b2, w3, decay)


def dft_constants(seq):
    n = 2 * seq
    n1 = n // LANES
    a_rows = n1 // 2
    k1 = np.arange(n1, dtype=np.float64)[:, None]
    a = np.arange(a_rows, dtype=np.float64)[None, :]
    th = 2.0 * np.pi * k1 * a / n1
    c1, s1 = np.cos(th), np.sin(th)
    m1_data = np.block([[c1, s1], [-s1, c1]])
    half_rows = -(-(n1 // 2 + 1) // SUBLANES) * SUBLANES
    pad = np.zeros((half_rows - (n1 // 2 + 1), a_rows))
    m1_real = np.concatenate([c1[:n1 // 2 + 1], pad, -s1[:n1 // 2 + 1], pad], axis=0)
    m3 = np.block([[c1.T, -s1.T], [s1.T, c1.T]]) / n
    b = np.arange(LANES, dtype=np.float64)[None, :]
    psi = 2.0 * np.pi * k1 * b / n
    twc, tws = np.cos(psi), np.sin(psi)
    bb = np.arange(LANES, dtype=np.float64)
    phi = 2.0 * np.pi * np.outer(bb, bb) / LANES
    c2, s2 = np.cos(phi), np.sin(phi)
    w2f = np.block([[c2, -s2], [s2, c2]])
    w2i = np.block([[c2, s2], [-s2, c2]])
    f = lambda x: jnp.asarray(x, dtype=F32)
    return dict(m1_data=f(m1_data), m1_real=f(m1_real), m3=f(m3), twc=f(twc[:, None, :]), tws=f(tws[:, None, :]),
                w2f=f(w2f), w2i=f(w2i), n1=n1, a_rows=a_rows, half_rows=half_rows)


def _split_bf16(x):
    hi = x.astype(BF16)
    return hi, (x - hi.astype(F32)).astype(BF16)


def _mm3(a, b, nt=False):
    ah, al = _split_bf16(a)
    bh, bl = _split_bf16(b)
    dims = (((1,), (1 if nt else 0,)), ((), ()))
    dot = lambda x, y: lax.dot_general(x, y, dims, preferred_element_type=F32)
    return dot(ah, bh) + (dot(ah, bl) + dot(al, bh))


def _leftmm_kernel(m_ref, x_ref, o_ref):
    o_ref[...] = _mm3(m_ref[...], x_ref[...])


def left_matmul(m, x, ct):
    r, k = m.shape
    cols = x.shape[1]
    return pl.pallas_call(
        _leftmm_kernel,
        grid=(cols // ct,),
        in_specs=[pl.BlockSpec((r, k), lambda j: (0, 0)), pl.BlockSpec((k, ct), lambda j: (0, j))],
        out_specs=pl.BlockSpec((r, ct), lambda j: (0, j)),
        out_shape=jax.ShapeDtypeStruct((r, cols), F32),
        compiler_params=_cparams("arbitrary"),
        name="dft_rows",
    )(m, x)


def _leftmm_gate_kernel(m_ref, zr_ref, zi_ref, x_ref, zold_ref, bias_ref, o_ref):
    n1 = zr_ref.shape[0]
    y = _mm3(m_ref[:, :n1], zr_ref[...]) + _mm3(m_ref[:, n1:], zi_ref[...])
    zold = zold_ref[...]
    o_ref[...] = x_ref[...] * (y + bias_ref[...] * zold)


def left_matmul_gate(m, zr, zi, xg, zold, bias_cols, ct):
    r, k = m.shape
    n1, cols = zr.shape
    col = lambda j: (0, j)
    return pl.pallas_call(
        _leftmm_gate_kernel,
        grid=(cols // ct,),
        in_specs=[pl.BlockSpec((r, k), lambda j: (0, 0)), pl.BlockSpec((n1, ct), col), pl.BlockSpec((n1, ct), col),
                  pl.BlockSpec((r, ct), col), pl.BlockSpec((r, ct), col), pl.BlockSpec((1, ct), col)],
        out_specs=pl.BlockSpec((r, ct), col),
        out_shape=jax.ShapeDtypeStruct((r, cols), F32),
        compiler_params=_cparams("arbitrary"),
        name="idft_rows_gate",
    )(m, zr, zi, xg, zold, bias_cols)


def _twiddle_fwd(yr, yi, c, s):
    return yr * c + yi * s, yi * c - yr * s


def _spectrum_kernel(yr_ref, yi_ref, twc_ref, tws_ref, w2f_ref, hr_ref, hi_ref, *, kb, n1):
    for t in range(kb):
        mirrored = pl.program_id(0) * kb + t > n1 // 2
        yi = yi_ref[t]
        ypr, ypi = _twiddle_fwd(yr_ref[t], jnp.where(mirrored, -yi, yi), twc_ref[t], tws_ref[t])
        x = _mm3(jnp.concatenate([ypr, ypi], axis=1), w2f_ref[...])
        xr, xi = x[:, :LANES], x[:, LANES:]
        for o in range(HY_ORDER):
            f0 = slice((2 * o) * HY_WIDTH, (2 * o + 1) * HY_WIDTH)
            f1 = slice((2 * o + 1) * HY_WIDTH, (2 * o + 2) * HY_WIDTH)
            hr_ref[o, t] = xr[f0] + xr[f1]
            hi_ref[o, t] = xi[f0] - xi[f1]


def filter_spectrum(y, consts, kb):
    assert kb == 1
    n1, half_rows = consts["n1"], consts["half_rows"]
    nc = 2 * HY_ORDER * HY_WIDTH
    y4 = y.reshape(2, half_rows, nc, LANES)
    tw = pl.BlockSpec((kb, 1, LANES), lambda i: (i, 0, 0))
    out = pl.BlockSpec((HY_ORDER, kb, HY_WIDTH, LANES), lambda i: (0, i, 0, 0))
    shp = jax.ShapeDtypeStruct((HY_ORDER, n1, HY_WIDTH, LANES), F32)
    src = lambda i: jnp.where(i > n1 // 2, n1 - i, i)
    return pl.pallas_call(
        functools.partial(_spectrum_kernel, kb=kb, n1=n1),
        grid=(n1 // kb,),
        in_specs=[
            pl.BlockSpec((None, kb, nc, LANES), lambda i: (0, src(i), 0, 0)),
            pl.BlockSpec((None, kb, nc, LANES), lambda i: (1, src(i), 0, 0)),
            tw, tw,
            pl.BlockSpec((2 * LANES, 2 * LANES), lambda i: (0, 0)),
        ],
        out_specs=[out, out],
        out_shape=[shp, shp],
        compiler_params=_cparams("arbitrary"),
        name="filter_spectrum",
    )(y4, y4, consts["twc"], consts["tws"], consts["w2f"])


def _freq_kernel(yr_ref, yi_ref, twc_ref, tws_ref, w2f_ref, w2i_ref, hr_ref, hi_ref, zr_ref, zi_ref, *, kb):
    for t in range(kb):
        c, s = twc_ref[t], tws_ref[t]
        ypr, ypi = _twiddle_fwd(yr_ref[t], yi_ref[t], c, s)
        x = _mm3(jnp.concatenate([ypr, ypi], axis=1), w2f_ref[...])
        xr, xi = x[:, :LANES], x[:, LANES:]
        hr, hi = hr_ref[t], hi_ref[t]
        gr = xr * hr - xi * hi
        gi = xr * hi + xi * hr
        z = _mm3(jnp.concatenate([gr, gi], axis=1), w2i_ref[...])
        zr, zi = z[:, :LANES], z[:, LANES:]
        zr_ref[t] = zr * c - zi * s
        zi_ref[t] = zi * c + zr * s


def freq_multiply(y, hr, hi, order, consts, kb):
    n1 = consts["n1"]
    y4 = y.reshape(2, n1, HY_WIDTH, LANES)
    tw = pl.BlockSpec((kb, 1, LANES), lambda i: (i, 0, 0))
    blk = lambda p: pl.BlockSpec((None, kb, HY_WIDTH, LANES), lambda i: (p, i, 0, 0))
    wspec = pl.BlockSpec((2 * LANES, 2 * LANES), lambda i: (0, 0))
    out = pl.BlockSpec((kb, HY_WIDTH, LANES), lambda i: (i, 0, 0))
    shp = jax.ShapeDtypeStruct((n1, HY_WIDTH, LANES), F32)
    return pl.pallas_call(
        functools.partial(_freq_kernel, kb=kb),
        grid=(n1 // kb,),
        in_specs=[blk(0), blk(1), tw, tw, wspec, wspec, blk(order), blk(order)],
        out_specs=[out, out],
        out_shape=[shp, shp],
        compiler_params=_cparams("arbitrary"),
        name="freq_multiply",
    )(y4, y4, consts["twc"], consts["tws"], consts["w2f"], consts["w2i"], hr, hi)


def _short_conv_rows(u, prev_row, next_row, w_ref, b_ref):
    rows = lax.broadcasted_iota(jnp.int32, u.shape, 0)
    um1 = jnp.where(rows == 0, prev_row, pltpu.roll(u, 1, 0))
    up1 = jnp.where(rows == u.shape[0] - 1, next_row, pltpu.roll(u, u.shape[0] - 1, 0))
    return b_ref[...] + um1 * w_ref[0:1, :] + u * w_ref[1:2, :] + up1 * w_ref[2:3, :]


def _shortconv_t_kernel(u_ref, up_ref, un_ref, w_ref, b_ref, o_ref, *, n_tiles):
    i = pl.program_id(1)
    prev_row = jnp.where(i > 0, up_ref[SUBLANES - 1:SUBLANES, :], 0.0)
    next_row = jnp.where(i < n_tiles - 1, un_ref[0:1, :], 0.0)
    v = _short_conv_rows(u_ref[...], prev_row, next_row, w_ref, b_ref)
    for part in range(HY_ORDER + 1):
        for r in range(v.shape[0] // LANES):
            o_ref[part, r] = v[r * LANES:(r + 1) * LANES, part * HY_WIDTH:(part + 1) * HY_WIDTH].T


def short_conv_transposed(hy, conv_w, conv_b, *, n_batch, seq, tm):
    nt = seq // tm
    per8 = tm // SUBLANES
    a_rows = seq // LANES
    prev = lambda b, i: (jnp.maximum((b * nt + i) * per8 - 1, 0), 0)
    nxt = lambda b, i: (jnp.minimum((b * nt + i + 1) * per8, n_batch * nt * per8 - 1), 0)
    return pl.pallas_call(
        functools.partial(_shortconv_t_kernel, n_tiles=nt),
        grid=(n_batch, nt),
        in_specs=[
            pl.BlockSpec((tm, HY_COLS), lambda b, i: (b * nt + i, 0)),
            pl.BlockSpec((SUBLANES, HY_COLS), prev),
            pl.BlockSpec((SUBLANES, HY_COLS), nxt),
            pl.BlockSpec((3, HY_COLS), lambda b, i: (0, 0)),
            pl.BlockSpec((1, HY_COLS), lambda b, i: (0, 0)),
        ],
        out_specs=pl.BlockSpec((HY_ORDER + 1, None, tm // LANES, HY_WIDTH, LANES), lambda b, i: (0, b, i, 0, 0)),
        out_shape=jax.ShapeDtypeStruct((HY_ORDER + 1, n_batch, a_rows, HY_WIDTH, LANES), F32),
        compiler_params=_cparams("arbitrary", "arbitrary"),
        name="short_conv_transposed",
    )(hy, hy, hy, conv_w, conv_b)


def _untranspose_kernel(z_ref, o_ref):
    for r in range(z_ref.shape[0]):
        o_ref[r * LANES:(r + 1) * LANES, :] = z_ref[r].T.astype(BF16)


def untranspose(zt, *, n_batch, seq):
    a_rows = seq // LANES
    ta = min(SUBLANES, a_rows)
    nt = a_rows // ta
    return pl.pallas_call(
        _untranspose_kernel,
        grid=(n_batch, nt),
        in_specs=[pl.BlockSpec((None, ta, HY_WIDTH, LANES), lambda b, i: (b, i, 0, 0))],
        out_specs=pl.BlockSpec((ta * LANES, HY_WIDTH), lambda b, i: (b * nt + i, 0)),
        out_shape=jax.ShapeDtypeStruct((n_batch * seq, HY_WIDTH), BF16),
        compiler_params=_cparams("arbitrary", "arbitrary"),
        name="untranspose",
    )(zt)


def hyena_latent(hy, conv_w, conv_b, hy_bias, filt_t, consts, *, n_batch, seq):
    assert n_batch == 2, "the two batches are packed as real / imaginary parts of one DFT"
    n1, a_rows = consts["n1"], consts["a_rows"]
    cols = HY_WIDTH * LANES
    ct = min(cols, 2048)
    kb = min(n1, 4)
    parts = short_conv_transposed(hy, conv_w, conv_b, n_batch=n_batch, seq=seq, tm=min(seq, 1024))
    parts = parts.reshape(HY_ORDER + 1, n_batch * a_rows, cols)
    yf = left_matmul(consts["m1_real"], filt_t.reshape(a_rows, 2 * HY_ORDER * cols), ct)
    hr, hi = filter_spectrum(yf, consts, 1)
    z = parts[HY_ORDER]
    for o in range(HY_ORDER):
        y = left_matmul(consts["m1_data"], z, ct)
        zr, zi = freq_multiply(y, hr, hi, o, consts, kb)
        bias_cols = jnp.repeat(hy_bias[o], LANES)[None, :]
        z = left_matmul_gate(consts["m3"], zr.reshape(n1, cols), zi.reshape(n1, cols), parts[o], z, bias_cols, ct)
    return untranspose(z.reshape(n_batch, a_rows, HY_WIDTH, LANES), n_batch=n_batch, seq=seq)


def _ctx_hyena_kernel(u_ref, w_ref, b_ref, h_ref, bias_ref, fc_ref, fs_ref, fct_ref, fst_ref, o_ref):
    u = u_ref[...]
    zero = jnp.zeros((1, u.shape[1]), F32)
    v = _short_conv_rows(u, zero, zero, w_ref, b_ref)
    z = v[:, HY_ORDER * HY_WIDTH:]
    mm = _mm3
    fc, fs = fc_ref[...], fs_ref[...]
    inv_n = 1.0 / fc.shape[0]
    for o in range(HY_ORDER):
        h0 = h_ref[:, (2 * o) * HY_WIDTH:(2 * o + 1) * HY_WIDTH]
        h1 = h_ref[:, (2 * o + 1) * HY_WIDTH:(2 * o + 2) * HY_WIDTH]
        sr, si = mm(fc, h0 + h1), mm(fs, h1 - h0)
        zr, zi = mm(fc, z), -mm(fs, z)
        gr = zr * sr - zi * si
        gi = zr * si + zi * sr
        y = (mm(fct_ref[...], gr) - mm(fst_ref[...], gi)) * inv_n
        z = v[:, o * HY_WIDTH:(o + 1) * HY_WIDTH] * (y + bias_ref[o:o + 1, :] * z)
    o_ref[...] = z.astype(BF16)


def ctx_hyena(hy, conv_w, conv_b, filt, hy_bias, *, n_batch, seq, n_ctx):
    n = 2 * n_ctx
    k = np.arange(n, dtype=np.float64)[:, None]
    m = np.arange(n_ctx, dtype=np.float64)[None, :]
    ang = 2.0 * np.pi * k * m / n
    fc, fs = jnp.asarray(np.cos(ang), F32), jnp.asarray(np.sin(ang), F32)
    ctx0 = (n_batch * seq) // n_ctx
    const = lambda b: (0, 0)
    return pl.pallas_call(
        _ctx_hyena_kernel,
        grid=(n_batch,),
        in_specs=[
            pl.BlockSpec((n_ctx, HY_COLS), lambda b: (ctx0 + b, 0)),
            pl.BlockSpec((3, HY_COLS), const), pl.BlockSpec((1, HY_COLS), const),
            pl.BlockSpec((n_ctx, 2 * HY_ORDER * HY_WIDTH), const),
            pl.BlockSpec((HY_ORDER, HY_WIDTH), const),
            pl.BlockSpec((n, n_ctx), const), pl.BlockSpec((n, n_ctx), const),
            pl.BlockSpec((n_ctx, n), const), pl.BlockSpec((n_ctx, n), const),
        ],
        out_specs=pl.BlockSpec((n_ctx, HY_WIDTH), lambda b: (b, 0)),
        out_shape=jax.ShapeDtypeStruct((n_batch * n_ctx, HY_WIDTH), BF16),
        compiler_params=_cparams("arbitrary"),
        name="ctx_hyena",
    )(hy, conv_w, conv_b, filt, hy_bias, fc, fs, fc.T, fs.T)


def _outproj_kernel(x_ref, yh_ref, yw_ref, yd_ref, mod_ref, wo_ref, g_ref, wq_ref, xo_ref, h2_ref, qp_ref):
    y = jnp.dot(yh_ref[...], wo_ref[0:HY_WIDTH, :], preferred_element_type=F32)
    y = y + jnp.dot(yw_ref[...], wo_ref[HY_WIDTH:HY_WIDTH + WA_Q_COLS, :], preferred_element_type=F32)
    y = y + jnp.dot(yd_ref[...], wo_ref[HY_WIDTH + WA_Q_COLS:, :], preferred_element_type=F32)
    xn = x_ref[...] + mod_ref[2:3, :] * y
    xo_ref[...] = xn
    h2 = _rmsnorm_mod(xn, g_ref[...], mod_ref[3:4, :], mod_ref[4:5, :]).astype(BF16)
    h2_ref[...] = h2
    qp_ref[...] = jnp.dot(h2, wq_ref[...], preferred_element_type=F32)


def out_proj(x_all, y_hy, y_wa, y_da, mod, wo_bf, g2, wq_bf, *, n_rows, n_batch, seq, tm):
    d = x_all.shape[1]
    tpb = seq // tm
    grp = functools.partial(_group_of_tile, tiles_per_batch=tpb, n_batch=n_batch)
    row = lambda i: (i, 0)
    const = lambda i: (0, 0)
    mix = HY_WIDTH + WA_Q_COLS + DA_V_COLS
    return pl.pallas_call(
        _outproj_kernel,
        grid=(n_rows // tm,),
        in_specs=[
            pl.BlockSpec((tm, d), row), pl.BlockSpec((tm, HY_WIDTH), row), pl.BlockSpec((tm, WA_Q_COLS), row),
            pl.BlockSpec((tm, DA_V_COLS), row),
            pl.BlockSpec((None, N_MOD, d), lambda i: (grp(i), 0, 0)),
            pl.BlockSpec((mix, d), const), pl.BlockSpec((1, d), const), pl.BlockSpec((d, PEER_QCOLS), const),
        ],
        out_specs=[pl.BlockSpec((tm, d), row), pl.BlockSpec((tm, d), row), pl.BlockSpec((tm, PEER_QCOLS), row)],
        out_shape=[jax.ShapeDtypeStruct((n_rows, d), F32), jax.ShapeDtypeStruct((n_rows, d), BF16),
                   jax.ShapeDtypeStruct((n_rows, PEER_QCOLS), F32)],
        compiler_params=_cparams("arbitrary"),
        name="out_proj",
    )(x_all, y_hy, y_wa, y_da, mod, wo_bf, g2, wq_bf)


def _sort_pairs(n):
    pairs = []

    def merge(lo, cnt, r):
        step = r * 2
        if step < cnt:
            merge(lo, cnt, step)
            merge(lo + r, cnt, step)
            for i in range(lo + r, lo + cnt - r, step):
                pairs.append((i, i + r))
        else:
            pairs.append((lo, lo + r))

    def sort(lo, cnt):
        if cnt > 1:
            half = cnt // 2
            sort(lo, half)
            sort(lo + half, half)
            merge(lo, cnt, 1)

    sort(0, n)
    return pairs


def _sort_desc(vals):
    n = 1
    while n < len(vals):
        n *= 2
    v = list(vals) + [None] * (n - len(vals))
    for i, j in _sort_pairs(n):
        a, b = v[i], v[j]
        if b is None:
            continue
        if a is None:
            v[i], v[j] = b, None
        else:
            v[i], v[j] = jnp.maximum(a, b), jnp.minimum(a, b)
    return v[:len(vals)]


def _top16_rows(s):
    k = PEER_TOPK
    v = _sort_desc([s[SUBLANES * r:SUBLANES * (r + 1), :] for r in range(s.shape[0] // SUBLANES)])
    for shift in (4, 2, 1):
        other = [pltpu.roll(x, shift, 0) for x in v]
        v = [jnp.maximum(v[i], other[k - 1 - i]) for i in range(k)]
        d = k // 2
        while d >= 1:
            for i in range(k):
                if (i & d) == 0:
                    a, b = v[i], v[i + d]
                    v[i], v[i + d] = jnp.maximum(a, b), jnp.minimum(a, b)
            d //= 2
    return v


def _peer_topk_kernel(q_ref, keys_ref, cnt_ref, e1_ref, rank_ref, e2_ref):
    k = PEER_TOPK
    tmk = q_ref.shape[0]
    sub = lax.broadcasted_iota(jnp.int32, (SUBLANES, tmk), 0)
    scores, tops = [], []
    for hp in range(2 * PEER_HEADS):
        s = _mm3(keys_ref[hp], q_ref[:, hp * N_KEYS:(hp + 1) * N_KEYS], nt=True)
        scores.append(s)
        top = _top16_rows(s)
        top.append(jnp.max(jnp.where(s < top[k - 1][0:1, :], s, NEG_BIG), axis=0, keepdims=True))
        tops.append(top)
    packed = []
    for p in range(2):
        lst = []
        for r in range(k + 1):
            acc = jnp.broadcast_to(tops[p][r], (SUBLANES, tmk))
            for h in range(1, PEER_HEADS):
                acc = jnp.where(sub == h, tops[2 * h + p][r], acc)
            lst.append(acc)
        packed.append(lst)
    cand = [packed[0][i] + packed[1][j] for i in range(k) for j in range(k) if (i + 1) * (j + 1) <= k]
    cs = _sort_desc(cand)
    c17 = jnp.maximum(cs[k], jnp.maximum(packed[0][k] + packed[1][0], packed[0][0] + packed[1][k]))
    tau = 0.5 * (cs[k - 1] + c17)
    zsum = jnp.zeros_like(tau)
    for r in range(k):
        zsum = zsum + jnp.exp(cs[r] - cs[0])
    inv_z = 1.0 / zsum
    for h in range(PEER_HEADS):
        s1, s2 = scores[2 * h], scores[2 * h + 1]
        row = lambda x: x[h:h + 1, :]
        top2 = [t2[0:1, :] for t2 in tops[2 * h + 1][:k]]
        cnt_ref[h] = _count_sorted(top2, row(tau) - s1, strict=False)
        tiled = lambda x: x.astype(BF16).reshape(N_KEYS // BF16_ROWS, BF16_ROWS, tmk)
        rank_ref[h] = tiled(_count_sorted(top2, s2, strict=True))
        e1_ref[h] = 0.5 * jnp.exp(s1 - row(packed[0][0]))
        e2_ref[h] = tiled(jnp.exp(s2 - row(packed[1][0])) * row(inv_z))


def _count_sorted(tops, x, *, strict):
    ge = (lambda a: a > x) if strict else (lambda a: a >= x)
    b3 = ge(tops[7])
    b2 = ge(jnp.where(b3, tops[11], tops[3]))
    b1 = ge(jnp.where(b3, jnp.where(b2, tops[13], tops[9]), jnp.where(b2, tops[5], tops[1])))
    hi = jnp.where(b2, jnp.where(b1, tops[14], tops[12]), jnp.where(b1, tops[10], tops[8]))
    lo = jnp.where(b2, jnp.where(b1, tops[6], tops[4]), jnp.where(b1, tops[2], tops[0]))
    b0 = ge(jnp.where(b3, hi, lo))
    cnt = (jnp.where(b3, 8.0, 0.0) + jnp.where(b2, 4.0, 0.0)) + (jnp.where(b1, 2.0, 0.0) + jnp.where(b0, 1.0, 0.0))
    return jnp.where(ge(tops[15]), 16.0, cnt)


def peer_topk(qp, keys, *, tmk):
    t = qp.shape[0]
    tiles = N_KEYS // BF16_ROWS
    out = pl.BlockSpec((PEER_HEADS, N_KEYS, tmk), lambda i: (0, 0, i))
    out16 = pl.BlockSpec((PEER_HEADS, tiles, BF16_ROWS, tmk), lambda i: (0, 0, 0, i))
    f32 = jax.ShapeDtypeStruct((PEER_HEADS, N_KEYS, t), F32)
    b16 = jax.ShapeDtypeStruct((PEER_HEADS, tiles, BF16_ROWS, t), BF16)
    return pl.pallas_call(
        _peer_topk_kernel,
        grid=(t // tmk,),
        in_specs=[pl.BlockSpec((tmk, PEER_QCOLS), lambda i: (i, 0)),
                  pl.BlockSpec((2 * PEER_HEADS, N_KEYS, PEER_QDIM // 2), lambda i: (0, 0, 0))],
        out_specs=[out, out, out16, out16],
        out_shape=[f32, f32, b16, b16],
        compiler_params=_cparams("arbitrary"),
        name="peer_topk",
    )(qp, keys)


def _peer_dense_kernel(x_ref, h_ref, mod_ref, u_ref, vt_ref, cnt_ref, e1_ref, rank_ref, e2_ref, g_ref, o_ref,
                       a_scr, w_scr, acc_scr, *, final_norm, chunk, col_block):
    e = pl.program_id(1)

    @pl.when(e == 0)
    def _():
        acc_scr[...] = jnp.zeros(acc_scr.shape, F32)

    n_chunks = u_ref.shape[0] // chunk
    per_chunk = chunk // N_KEYS
    sqrt_half = math.sqrt(0.5)
    tm = h_ref.shape[0]
    tiles = N_KEYS // BF16_ROWS
    zero = jnp.zeros((), BF16)

    slots = a_scr.shape[0]
    ahead = slots - 1

    def pre_activations(p):
        a_scr[p % slots] = lax.dot_general(u_ref[p * chunk:(p + 1) * chunk, :], h_ref[...], (((1,), (1,)), ((), ())),
                                           preferred_element_type=F32)

    for p in range(min(ahead, n_chunks)):
        pre_activations(p)
    for p in range(n_chunks):
        if p + ahead < n_chunks:
            pre_activations(p + ahead)
        for ii, cb in [(ii, cb) for ii in range(p * per_chunk, (p + 1) * per_chunk) for cb in range(tm // col_block)]:
            rows = slice((ii - p * per_chunk) * N_KEYS, (ii - p * per_chunk + 1) * N_KEYS)
            cols = slice(cb * col_block, (cb + 1) * col_block)
            a = a_scr[p % slots, rows, cols]
            act = a * (1.0 + lax.erf(a * sqrt_half))
            act = act.astype(BF16).reshape(tiles, BF16_ROWS, col_block)
            gate = None
            for h in range(PEER_HEADS):
                cnt = jnp.broadcast_to(cnt_ref[h, ii:ii + 1, cols], (BF16_ROWS, col_block)).astype(BF16)
                e1 = jnp.broadcast_to(e1_ref[h, ii:ii + 1, cols], (BF16_ROWS, col_block)).astype(BF16)
                term = jnp.where(rank_ref[h, :, :, cols] < cnt[None], e2_ref[h, :, :, cols] * e1[None], zero)
                gate = term if gate is None else gate + term
            w_scr[p % 2, rows, cols] = (gate * act).reshape(N_KEYS, col_block)
        acc_scr[...] += jnp.dot(vt_ref[:, p * chunk:(p + 1) * chunk], w_scr[p % 2], preferred_element_type=F32)

    @pl.when(e == pl.num_programs(1) - 1)
    def _():
        xn = x_ref[...] + mod_ref[5:6, :] * acc_scr[...].T
        if final_norm:
            ms = jnp.mean(xn * xn, axis=-1, keepdims=True)
            xn = xn * lax.rsqrt(ms + NORM_EPS) * g_ref[...]
        o_ref[...] = xn


def peer_dense(x, h2, mod, u_bf, vt_bf, cnt, e1, rank, e2, g_final, *, n_batch, seq, tm, te, chunk, ahead, col_block,
               final_norm):
    n_rows, d = x.shape
    tpb = seq // tm
    grp = functools.partial(_group_of_tile, tiles_per_batch=tpb, n_batch=n_batch)
    n_i = te // N_KEYS
    tok = lambda i, e: (i, 0)
    rows_i = pl.BlockSpec((PEER_HEADS, n_i, tm), lambda i, e: (0, e, i))
    tiles = N_KEYS // BF16_ROWS
    full = pl.BlockSpec((PEER_HEADS, tiles, BF16_ROWS, tm), lambda i, e: (0, 0, 0, i))
    return pl.pallas_call(
        functools.partial(_peer_dense_kernel, final_norm=final_norm, chunk=chunk, col_block=col_block),
        grid=(n_rows // tm, N_EXPERTS // te),
        in_specs=[
            pl.BlockSpec((tm, d), tok), pl.BlockSpec((tm, d), tok),
            pl.BlockSpec((None, N_MOD, d), lambda i, e: (grp(i), 0, 0)),
            pl.BlockSpec((te, d), lambda i, e: (e, 0)),
            pl.BlockSpec((d, te), lambda i, e: (0, e)),
            rows_i, rows_i, full, full,
            pl.BlockSpec((1, d), lambda i, e: (0, 0)),
        ],
        out_specs=pl.BlockSpec((tm, d), tok),
        out_shape=jax.ShapeDtypeStruct((n_rows, d), F32),
        scratch_shapes=[pltpu.VMEM((ahead + 1, chunk, tm), F32), pltpu.VMEM((2, chunk, tm), BF16),
                        pltpu.VMEM((d, tm), F32)],
        compiler_params=_cparams("arbitrary", "arbitrary", vmem_limit_bytes=PEER_VMEM_LIMIT_BYTES),
        name="peer_dense",
    )(x, h2, mod, u_bf, vt_bf, cnt, e1, rank, e2, g_final)


def kernel(x, c, ctx, c_ctx, w_mod, b_mod, norm1_g, w_in, hy_conv_w, hy_conv_b, hy_w1, hy_b1, hy_freq, hy_w2,
           hy_b2, hy_w3, hy_decay, hy_bias, wa_sink, da_lq1, da_lk1, da_lq2, da_lk2, da_norm_g, w_out, norm2_g,
           peer_wq, peer_keys, peer_u, peer_v, final_g):
    n_batch, seq, d = x.shape
    n_ctx = ctx.shape[1]
    depth = w_in.shape[0]
    n_lat = n_batch * seq
    tm = 512 if seq % 512 == 0 else 256
    assert seq % tm == 0 and (n_batch * n_ctx) % tm == 0 and n_lat % n_ctx == 0 and n_batch + 1 <= SUBLANES

    cpad = jnp.zeros((SUBLANES, d), F32).at[:n_batch].set(c).at[n_batch].set(c_ctx)
    mod_all = mod_vectors(cpad, w_mod, b_mod).reshape(depth, SUBLANES, N_MOD, d)
    rope_wa = rope_tables(seq, HEAD_DIM, tm)
    rope_da = rope_tables(seq, DA_QK_DIM, tm)
    consts = dft_constants(seq)
    feat_lat = hyena_filter_features(seq)
    feat_ctx = hyena_filter_features(n_ctx)
    x_all = jnp.concatenate([x.reshape(n_lat, d), ctx.reshape(n_batch * n_ctx, d)], axis=0)

    out = None
    for li in range(depth):
        last = li == depth - 1
        lam_init = DA_LAMBDA_BASE - DA_LAMBDA_AMP * math.exp(-DA_LAMBDA_RATE * li)
        mod = mod_all[li]
        hy, qw, kw, vw, qd, kd, vd, vdt, vwt = proj_in(x_all, mod, norm1_g[li][None, :], w_in[li].astype(BF16), rope_wa,
                                             rope_da, n_batch=n_batch, seq=seq, tm=tm)
        w1p = jnp.pad(hy_w1[li], ((0, LANES - HY_EMB_DIM), (0, 0)))
        fargs = (w1p, hy_b1[li][None, :], hy_freq[li][None, :], hy_w2[li], hy_b2[li][None, :], hy_w3[li],
                 hy_decay[li].reshape(1, -1))
        conv_b = hy_conv_b[li][None, :]
        filt_t = hyena_filters(feat_lat, *fargs, transposed=True)
        y_hy = hyena_latent(hy, hy_conv_w[li], conv_b, hy_bias[li], filt_t, consts, n_batch=n_batch, seq=seq)
        y_wa = window_attention(wa_sink[li], qw, kw, vwt, n_batch=n_batch, seq=seq, n_ctx=n_ctx)
        lq = jnp.stack([da_lq1[li], da_lk1[li], da_lq2[li], da_lk2[li]], axis=0)
        g128 = jnp.tile(da_norm_g[li], LANES // DA_V_DIM)[None, :]
        tq = min(seq, 512)
        g_rows = jnp.broadcast_to(da_norm_g[li][:, None], (DA_V_DIM, tq))
        y_da = diff_attention(lq, g_rows, qd, kd, vdt, n_batch=n_batch, seq=seq, n_ctx=n_ctx, lam_init=lam_init,
                              tq=tq, tk=min(seq, 256), ahead=2, unroll=8 if seq % 2048 == 0 else 1)
        if not last:
            filt_c = hyena_filters(feat_ctx, *fargs, transposed=False)
            yc_hy = ctx_hyena(hy, hy_conv_w[li], conv_b, filt_c, hy_bias[li], n_batch=n_batch, seq=seq, n_ctx=n_ctx)
            yc_wa, yc_da = ctx_attention(wa_sink[li], lq, g128, qw, kw, vw, qd, kd, vd, n_batch=n_batch, seq=seq,
                                         n_ctx=n_ctx, lam_init=lam_init)
            y_hy = jnp.concatenate([y_hy, yc_hy], axis=0)
            y_wa = jnp.concatenate([y_wa, yc_wa], axis=0)
            y_da = jnp.concatenate([y_da, yc_da], axis=0)
        n_rows = n_lat if last else x_all.shape[0]
        x_mid, h2, qp = out_proj(x_all, y_hy, y_wa, y_da, mod, w_out[li].astype(BF16), norm2_g[li][None, :],
                                 peer_wq[li].astype(BF16), n_rows=n_rows, n_batch=n_batch, seq=seq, tm=tm)
        keys = peer_keys[li].reshape(2 * PEER_HEADS, N_KEYS, PEER_QDIM // 2)
        cnt, e1, rank, e2 = peer_topk(qp, keys, tmk=256)
        x_all = peer_dense(x_mid, h2, mod, peer_u[li].astype(BF16), peer_v[li].T.astype(BF16), cnt, e1, rank, e2,
                           final_g[None, :], n_batch=n_batch, seq=seq, tm=tm, te=4096, chunk=1024, ahead=3, col_block=tm,
                           final_norm=last)
        out = x_all
    return out.reshape(n_batch, seq, d)
```
